```python
import math
import jax
import jax.numpy as jnp
from jax import lax
import numpy as np

D_MODEL = 1024
BATCH = 2
SEQ = 8192
DEPTH = 4
DEC_BATCH = 8
DEC_SEQ = 4096
PAST_LEN = 128

F32 = jnp.float32
N_MEM = 256
GRID_W = 64
Q_BLOCK = 128
D_FF = 4 * D_MODEL
NORM_EPS = 1e-6
ROPE_THETA = 500000.0
AXIAL_THETA = 10000.0
NEG_INF = -1e30
N_MIXERS = 4

A_PATTERNS = ((128, 1), (512, 4), (2048, 16))
A_GROUPS = len(A_PATTERNS)
A_HEADS = 8
A_HEAD_DIM = 64
A_ROT = A_HEAD_DIM // 4
A_IN = A_GROUPS * 3 * A_HEADS * A_HEAD_DIM
B_HEADS = 16
B_KV_HEADS = 4
B_HEAD_DIM = 64
B_IN = (B_HEADS + 2 * B_KV_HEADS) * B_HEAD_DIM
C_HEADS = 8
C_DIM = 64
C_ROT = C_DIM // 4
C_IN = 3 * C_HEADS * 2 * C_DIM
D_HEADS = 16
D_Q_RANK = 384
D_KV_RANK = 256
D_NOPE = 64
D_ROPE = 32
D_V = 64
D_IN = D_Q_RANK + D_KV_RANK + D_ROPE
X_HEADS = 4
X_HEAD_DIM = 128
N_A = (DEPTH + 3) // 4
N_B = (DEPTH + 2) // 4
N_C = (DEPTH + 1) // 4
N_D = DEPTH // 4

kernel_name = 'hybrid_bidir_encoder_interleaved'


def rmsnorm(x, g):
    xf = x.astype(F32)
    y = xf * lax.rsqrt(jnp.mean(xf * xf, axis=-1, keepdims=True) + NORM_EPS)
    return (y * g.astype(F32)).astype(x.dtype)


def rope(x, pos, theta, rot):
    half = rot // 2
    inv_freq = jnp.exp(jnp.arange(half, dtype=F32) * (-2.0 * math.log(theta) / rot))
    ang = pos.astype(F32)[:, None] * inv_freq[None, :]
    bshape = (1, pos.shape[0]) + (1,) * (x.ndim - 3) + (half,)
    cos = jnp.cos(ang).reshape(bshape).astype(x.dtype)
    sin = jnp.sin(ang).reshape(bshape).astype(x.dtype)
    x1 = x[..., :half]
    x2 = x[..., half:rot]
    return jnp.concatenate([x1 * cos - x2 * sin, x2 * cos + x1 * sin, x[..., rot:]], axis=-1)


def axial_rope(x, rows, cols):
    half = x.shape[-1] // 2
    return jnp.concatenate([rope(x[..., :half], rows, AXIAL_THETA, half),
                            rope(x[..., half:], cols, AXIAL_THETA, half)], axis=-1)


def to_blocks(t):
    b, s = t.shape[:2]
    return jnp.moveaxis(t.reshape((b, s // Q_BLOCK, Q_BLOCK) + t.shape[2:]), 1, 0)


def from_blocks(t):
    nb, b, qb = t.shape[:3]
    return jnp.moveaxis(t, 0, 1).reshape((b, nb * qb) + t.shape[3:])


def dilated_window_attention(q, k, v, window, dilation):
    b, s, h, dh = q.shape
    n = window // (2 * dilation)
    L = s // dilation
    c = n
    nb = -(-L // c)
    lp = nb * c

    def split(t):
        t = t.reshape(b, L, dilation, h, dh).transpose(0, 2, 1, 3, 4)
        t = jnp.pad(t, ((0, 0), (0, 0), (0, lp - L), (0, 0), (0, 0)))
        return t.reshape(b, dilation, nb, c, h, dh)

    def neighbours(t):
        tp = jnp.pad(t, ((0, 0), (0, 0), (1, 1), (0, 0), (0, 0), (0, 0)))
        return jnp.concatenate([tp[:, :, :-2], tp[:, :, 1:-1], tp[:, :, 2:]], axis=3)

    qb = split(q)
    kn = neighbours(split(k))
    vn = neighbours(split(v))
    iq = jnp.arange(nb)[:, None, None] * c + jnp.arange(c)[None, :, None]
    ik = (jnp.arange(nb)[:, None, None] - 1) * c + jnp.arange(3 * c)[None, None, :]
    valid = (jnp.abs(iq - ik) <= n) & (ik >= 0) & (ik < L)
    sc = jnp.einsum('brnqhd,brnkhd->brnhqk', qb, kn).astype(F32) * dh ** -0.5
    sc = jnp.where(valid[None, None, :, None], sc, NEG_INF)
    mx = jnp.max(sc, axis=-1, keepdims=True)
    e = jnp.exp(sc - mx)
    den = jnp.sum(e, axis=-1, keepdims=True)
    lse = (mx + jnp.log(den))[..., 0]
    o = jnp.einsum('brnhqk,brnkhd->brnqhd', (e / den).astype(v.dtype), vn)
    o = o.reshape(b, dilation, lp, h, dh)[:, :, :L].transpose(0, 2, 1, 3, 4).reshape(b, s, h, dh)
    lse = lse.transpose(0, 1, 2, 4, 3).reshape(b, dilation, lp, h)[:, :, :L]
    lse = lse.transpose(0, 2, 1, 3).reshape(b, s, h)
    return o, lse


def mixer_dilated(h, w_in, w_out, pos):
    b, s, _ = h.shape
    qkv = (h @ w_in).reshape(b, s, A_GROUPS, 3, A_HEADS, A_HEAD_DIM)
    outs = []
    lses = []
    for g, (window, dil) in enumerate(A_PATTERNS):
        q = rope(qkv[:, :, g, 0], pos, ROPE_THETA, A_ROT)
        k = rope(qkv[:, :, g, 1], pos, ROPE_THETA, A_ROT)
        o, lse = dilated_window_attention(q, k, qkv[:, :, g, 2], window, dil)
        outs.append(o)
        lses.append(lse)
    alpha = jax.nn.softmax(jnp.stack(lses), axis=0)
    o = jnp.einsum('gbsh,gbshd->bshd', alpha.astype(h.dtype), jnp.stack(outs))
    return o.reshape(b, s, A_HEADS * A_HEAD_DIM) @ w_out


def mixer_gqa_axial(h, w_in, q_gain, k_gain, w_out, rows, cols):
    b, s, _ = h.shape
    dh = B_HEAD_DIM
    grp = B_HEADS // B_KV_HEADS
    qkv = h @ w_in
    q = qkv[..., :B_HEADS * dh].reshape(b, s, B_HEADS, dh)
    k = qkv[..., B_HEADS * dh:(B_HEADS + B_KV_HEADS) * dh].reshape(b, s, B_KV_HEADS, dh)
    v = qkv[..., (B_HEADS + B_KV_HEADS) * dh:].reshape(b, s, B_KV_HEADS, dh)
    q = axial_rope(rmsnorm(q, q_gain), rows, cols).reshape(b, s, B_KV_HEADS, grp, dh)
    k = axial_rope(rmsnorm(k, k_gain), rows, cols)
    scale = dh ** -0.5

    def block(qb):
        sc = jnp.einsum('bqhgd,bkhd->bhgqk', qb, k).astype(F32) * scale
        p = jax.nn.softmax(sc, axis=-1).astype(v.dtype)
        return jnp.einsum('bhgqk,bkhd->bqhgd', p, v)

    o = from_blocks(lax.map(block, to_blocks(q)))
    return o.reshape(b, s, B_HEADS * dh) @ w_out


def mixer_diff(h, w_in, lq1, lk1, lq2, lk2, sub_gain, w_out, pos, lambda_init):
    b, s, _ = h.shape
    qkv = (h @ w_in).reshape(b, s, 3, C_HEADS, 2 * C_DIM)
    q = rope(qkv[:, :, 0].reshape(b, s, C_HEADS, 2, C_DIM), pos, ROPE_THETA, C_ROT)
    k = rope(qkv[:, :, 1].reshape(b, s, C_HEADS, 2, C_DIM), pos, ROPE_THETA, C_ROT)
    v = qkv[:, :, 2]
    lam = (jnp.exp(jnp.sum(lq1.astype(F32) * lk1.astype(F32)))
           - jnp.exp(jnp.sum(lq2.astype(F32) * lk2.astype(F32))) + lambda_init)
    scale = C_DIM ** -0.5

    def block(qb):
        sc = jnp.einsum('bqhcd,bkhcd->bchqk', qb, k).astype(F32) * scale
        p = jax.nn.softmax(sc, axis=-1)
        a = (p[:, 0] - lam * p[:, 1]).astype(v.dtype)
        return jnp.einsum('bhqk,bkhd->bqhd', a, v)

    o = from_blocks(lax.map(block, to_blocks(q)))
    o = rmsnorm(o, sub_gain) * (1.0 - lambda_init)
    return o.reshape(b, s, C_HEADS * 2 * C_DIM) @ w_out


def mixer_mla(h, w_in, q_gain, kv_gain, w_uq, w_ukv, w_out, pos):
    b, s, _ = h.shape
    cmb = h @ w_in
    c_q = rmsnorm(cmb[..., :D_Q_RANK], q_gain)
    c_kv = rmsnorm(cmb[..., D_Q_RANK:D_Q_RANK + D_KV_RANK], kv_gain)
    k_rope = rope(cmb[..., D_Q_RANK + D_KV_RANK:][:, :, None, :], pos, ROPE_THETA, D_ROPE)[:, :, 0]
    q = (c_q @ w_uq).reshape(b, s, D_HEADS, D_NOPE + D_ROPE)
    q_nope = q[..., :D_NOPE]
    q_rope = rope(q[..., D_NOPE:], pos, ROPE_THETA, D_ROPE)
    kv = (c_kv @ w_ukv).reshape(b, s, D_HEADS, D_NOPE + D_V)
    k_nope = kv[..., :D_NOPE]
    v = kv[..., D_NOPE:]
    scale = (D_NOPE + D_ROPE) ** -0.5

    def block(qs):
        qn, qr = qs
        sc = (jnp.einsum('bqhd,bkhd->bhqk', qn, k_nope)
              + jnp.einsum('bqhr,bkr->bhqk', qr, k_rope)).astype(F32) * scale
        p = jax.nn.softmax(sc, axis=-1).astype(v.dtype)
        return jnp.einsum('bhqk,bkhd->bqhd', p, v)

    o = from_blocks(lax.map(block, (to_blocks(q_nope), to_blocks(q_rope))))
    return o.reshape(b, s, D_HEADS * D_V) @ w_out


def memory_cross_attention(h, mem, mem_gain, w_q, w_kv, w_o):
    b, s, _ = h.shape
    n_mem = mem.shape[1]
    m = rmsnorm(mem, mem_gain)
    q = (h @ w_q).reshape(b, s, X_HEADS, X_HEAD_DIM)
    kv = (m @ w_kv).reshape(b, n_mem, 2, X_HEADS, X_HEAD_DIM)
    sc = jnp.einsum('bqhd,bmhd->bhqm', q, kv[:, :, 0]).astype(F32) * X_HEAD_DIM ** -0.5
    p = jax.nn.softmax(sc, axis=-1).astype(h.dtype)
    o = jnp.einsum('bhqm,bmhd->bqhd', p, kv[:, :, 1])
    return o.reshape(b, s, X_HEADS * X_HEAD_DIM) @ w_o


def sqrelu_mlp(h, w_in, w_out):
    a = jax.nn.relu(h @ w_in)
    return (a * a) @ w_out


def run_trunk(x, mem, p):
    s = x.shape[1]
    n_rows = s // GRID_W
    pos = jnp.arange(s, dtype=F32)
    rows = jnp.repeat(jnp.arange(n_rows, dtype=F32), GRID_W)
    cols = jnp.tile(jnp.arange(GRID_W, dtype=F32), n_rows)
    for i in range(DEPTH):
        m, j = i % N_MIXERS, i // N_MIXERS
        h = rmsnorm(x, p['norm_mix'][i])
        if m == 0:
            mix = mixer_dilated(h, p['a_w_in'][j], p['a_w_out'][j], pos)
        elif m == 1:
            mix = mixer_gqa_axial(h, p['b_w_in'][j], p['b_q_norm'][j], p['b_k_norm'][j],
                                  p['b_w_out'][j], rows, cols)
        elif m == 2:
            mix = mixer_diff(h, p['c_w_in'][j], p['c_lambda_q1'][j], p['c_lambda_k1'][j],
                             p['c_lambda_q2'][j], p['c_lambda_k2'][j], p['c_sub_norm'][j],
                             p['c_w_out'][j], pos, 0.8 - 0.6 * math.exp(-0.3 * i))
        else:
            mix = mixer_mla(h, p['d_w_in'][j], p['d_q_norm'][j], p['d_kv_norm'][j],
                            p['d_w_uq'][j], p['d_w_ukv'][j], p['d_w_out'][j], pos)
        x = x + mix
        x = x + memory_cross_attention(rmsnorm(x, p['norm_x'][i]), mem, p['norm_mem'][i],
                                       p['w_xq'][i], p['w_xkv'][i], p['w_xo'][i])
        x = x + sqrelu_mlp(rmsnorm(x, p['norm_mlp'][i]), p['w_mlp_in'][i], p['w_mlp_out'][i])
    return rmsnorm(x, p['final_norm'])


def setup_inputs(seed: int = 0) -> dict:
    key = jax.random.key(seed)
    ks = iter(jax.random.split(key, 40))

    def act(shape):
        return jax.random.normal(next(ks), shape, F32)

    def w(shape, fan_in):
        return jax.random.normal(next(ks), shape, F32) * fan_in ** -0.5

    def gain(shape):
        return 1.0 + 0.05 * jax.random.normal(next(ks), shape, F32)

    def small(shape):
        return 0.1 * jax.random.normal(next(ks), shape, F32)

    D = D_MODEL
    return {
        'x_prompt': act((BATCH, SEQ, D)),
        'x_sample': act((DEC_BATCH, DEC_SEQ, D)),
        'mem_prompt': act((BATCH, N_MEM, D)),
        'mem_sample': act((DEC_BATCH, N_MEM, D)),
        'norm_mix': gain((DEPTH, D)),
        'norm_x': gain((DEPTH, D)),
        'norm_mem': gain((DEPTH, D)),
        'w_xq': w((DEPTH, D, X_HEADS * X_HEAD_DIM), D),
        'w_xkv': w((DEPTH, D, 2 * X_HEADS * X_HEAD_DIM), D),
        'w_xo': w((DEPTH, X_HEADS * X_HEAD_DIM, D), X_HEADS * X_HEAD_DIM),
        'norm_mlp': gain((DEPTH, D)),
        'w_mlp_in': w((DEPTH, D, D_FF), D),
        'w_mlp_out': w((DEPTH, D_FF, D), D_FF),
        'a_w_in': w((N_A, D, A_IN), D),
        'a_w_out': w((N_A, A_HEADS * A_HEAD_DIM, D), A_HEADS * A_HEAD_DIM),
        'b_w_in': w((N_B, D, B_IN), D),
        'b_q_norm': gain((N_B, B_HEAD_DIM)),
        'b_k_norm': gain((N_B, B_HEAD_DIM)),
        'b_w_out': w((N_B, B_HEADS * B_HEAD_DIM, D), B_HEADS * B_HEAD_DIM),
        'c_w_in': w((N_C, D, C_IN), D),
        'c_lambda_q1': small((N_C, C_DIM)),
        'c_lambda_k1': small((N_C, C_DIM)),
        'c_lambda_q2': small((N_C, C_DIM)),
        'c_lambda_k2': small((N_C, C_DIM)),
        'c_sub_norm': gain((N_C, 2 * C_DIM)),
        'c_w_out': w((N_C, C_HEADS * 2 * C_DIM, D), C_HEADS * 2 * C_DIM),
        'd_w_in': w((N_D, D, D_IN), D),
        'd_q_norm': gain((N_D, D_Q_RANK)),
        'd_kv_norm': gain((N_D, D_KV_RANK)),
        'd_w_uq': w((N_D, D_Q_RANK, D_HEADS * (D_NOPE + D_ROPE)), D_Q_RANK),
        'd_w_ukv': w((N_D, D_KV_RANK, D_HEADS * (D_NOPE + D_V)), D_KV_RANK),
        'd_w_out': w((N_D, D_HEADS * D_V, D), D_HEADS * D_V),
        'final_norm': gain((D,)),
    }


def reference(x_prompt, x_sample, mem_prompt, mem_sample, norm_mix, norm_x, norm_mem, w_xq, w_xkv,
              w_xo, norm_mlp, w_mlp_in, w_mlp_out, a_w_in, a_w_out, b_w_in, b_q_norm, b_k_norm,
              b_w_out, c_w_in, c_lambda_q1, c_lambda_k1, c_lambda_q2, c_lambda_k2, c_sub_norm,
              c_w_out, d_w_in, d_q_norm, d_kv_norm, d_w_uq, d_w_ukv, d_w_out, final_norm):
    p = dict(norm_mix=norm_mix, norm_x=norm_x, norm_mem=norm_mem, w_xq=w_xq, w_xkv=w_xkv,
             w_xo=w_xo, norm_mlp=norm_mlp, w_mlp_in=w_mlp_in, w_mlp_out=w_mlp_out,
             a_w_in=a_w_in, a_w_out=a_w_out, b_w_in=b_w_in, b_q_norm=b_q_norm,
             b_k_norm=b_k_norm, b_w_out=b_w_out, c_w_in=c_w_in, c_lambda_q1=c_lambda_q1,
             c_lambda_k1=c_lambda_k1, c_lambda_q2=c_lambda_q2, c_lambda_k2=c_lambda_k2,
             c_sub_norm=c_sub_norm, c_w_out=c_w_out, d_w_in=d_w_in, d_q_norm=d_q_norm,
             d_kv_norm=d_kv_norm, d_w_uq=d_w_uq, d_w_ukv=d_w_ukv, d_w_out=d_w_out,
             final_norm=final_norm)
    y_prompt = run_trunk(x_prompt, mem_prompt, p)
    y_sample = run_trunk(x_sample, mem_sample, p)
    return (y_prompt, y_sample)
```

```python
import functools
import math

import numpy as np
import jax
import jax.numpy as jnp
from jax import lax
from jax.experimental import pallas as pl
from jax.experimental.pallas import tpu as pltpu

F32 = jnp.float32
BF16 = jnp.bfloat16

D_MODEL = 1024
DEPTH = 4
N_MEM = 256
GRID_W = 64
D_FF = 4 * D_MODEL
NORM_EPS = 1e-6
ROPE_THETA = 500000.0
AXIAL_THETA = 10000.0
NEG_INF = -1e30

A_PATTERNS = ((128, 1), (512, 4), (2048, 16))
A_GROUPS = 3
A_HEADS = 8
A_IN = A_GROUPS * 3 * A_HEADS * 64
A_HALF_WINDOW = 64
B_HEADS = 16
B_KV_HEADS = 4
C_HEADS = 8
D_HEADS = 16
D_Q_RANK = 384
D_KV_RANK = 256
D_NOPE = 64
D_ROPE = 32
X_HEADS = 4
X_HEAD_DIM = 128

LANES = 128
BLK = 2 * LANES
VMEM_LIMIT = 56 * 1024 * 1024


def _cparams(sem):
    return pltpu.CompilerParams(dimension_semantics=sem, vmem_limit_bytes=VMEM_LIMIT)


def _proj_kernel(tt_ref, src_ref, g_ref, w_ref, *rest, dnorm, rope, headnorm, add, nsub):
    del tt_ref
    rest = list(rest)
    c_ref = s_ref = hg_ref = bd_ref = add_ref = None
    if rope:
        c_ref, s_ref = rest[0], rest[1]
        rest = rest[2:]
    if headnorm:
        hg_ref, bd_ref = rest[0], rest[1]
        rest = rest[2:]
    if add:
        add_ref = rest[0]
        rest = rest[1:]
    o_ref, h_scr = rest

    @pl.when(pl.program_id(1) == 0)
    def _():
        xf = src_ref[...].astype(F32)
        ms = jnp.sum(xf * xf, axis=-1, keepdims=True) * (1.0 / dnorm)
        h_scr[...] = (xf * lax.rsqrt(ms + NORM_EPS) * g_ref[...]).astype(BF16)

    y = jnp.dot(h_scr[...], w_ref[...], preferred_element_type=F32)
    if add:
        y = y + add_ref[...]
    if headnorm:
        y2 = y * y
        hi = y2.astype(BF16)
        lo = (y2 - hi.astype(F32)).astype(BF16)
        parts = []
        for n in range(nsub):
            sl = slice(n * BLK, (n + 1) * BLK)
            ss = (jnp.dot(hi[:, sl], bd_ref[...], preferred_element_type=F32)
                  + jnp.dot(lo[:, sl], bd_ref[...], preferred_element_type=F32))
            parts.append(y[:, sl] * lax.rsqrt(ss * (1.0 / 64.0) + NORM_EPS))
        y = (parts[0] if nsub == 1 else jnp.concatenate(parts, axis=1)) * hg_ref[...]
    if rope:
        c = c_ref[0]
        s = s_ref[0]
        for n in range(nsub):
            y1 = y[:, n * BLK:n * BLK + LANES]
            y2 = y[:, n * BLK + LANES:(n + 1) * BLK]
            o_ref[:, n * BLK:n * BLK + LANES] = (y1 * c - y2 * s).astype(o_ref.dtype)
            o_ref[:, n * BLK + LANES:(n + 1) * BLK] = (y2 * c + y1 * s).astype(o_ref.dtype)
    else:
        o_ref[...] = y.astype(o_ref.dtype)


def _proj(src, src_cb, kdim, dnorm, gain, w, *, tn, out_dtype, seq, tm,
          tables=None, tt=None, headnorm=None, add=None):
    t = src.shape[0]
    n = w.shape[1]
    nj = n // tn
    nsub = tn // BLK if (tables is not None or headnorm is not None) else 1
    ns = seq // tm
    if tt is None:
        tt = np.zeros((nj,), np.int32)
    in_specs = [
        pl.BlockSpec((tm, kdim), lambda i, j, tt_: (i, src_cb)),
        pl.BlockSpec((1, kdim), lambda i, j, tt_: (0, 0)),
        pl.BlockSpec((kdim, tn), lambda i, j, tt_: (0, j)),
    ]
    args = [src, gain.reshape(1, kdim).astype(F32), w]
    if tables is not None:
        for tb in tables:
            in_specs.append(pl.BlockSpec((1, tm, LANES), lambda i, j, tt_: (tt_[j], i % ns, 0)))
            args.append(tb)
    if headnorm is not None:
        in_specs.append(pl.BlockSpec((1, tn), lambda i, j, tt_: (0, j)))
        in_specs.append(pl.BlockSpec((BLK, BLK), lambda i, j, tt_: (0, 0)))
        args += [headnorm[0], headnorm[1]]
    if add is not None:
        add_arr, add_cb = add
        in_specs.append(pl.BlockSpec((tm, tn), lambda i, j, tt_: (i, add_cb)))
        args.append(add_arr)
    kern = functools.partial(_proj_kernel, dnorm=dnorm, rope=tables is not None,
                             headnorm=headnorm is not None, add=add is not None, nsub=nsub)
    return pl.pallas_call(
        kern,
        out_shape=jax.ShapeDtypeStruct((t, n), out_dtype),
        grid_spec=pltpu.PrefetchScalarGridSpec(
            num_scalar_prefetch=1,
            grid=(t // tm, nj),
            in_specs=in_specs,
            out_specs=pl.BlockSpec((tm, tn), lambda i, j, tt_: (i, j)),
            scratch_shapes=[pltpu.VMEM((tm, kdim), BF16)],
        ),
        compiler_params=_cparams(("parallel", "arbitrary")),
    )(jnp.asarray(tt, jnp.int32), *args)


def _flash_kernel(*refs, nh, acc_id, vhead, tq, tk, nk, diff):
    if diff:
        (qmask_ref, vmask_ref, q_ref, k_ref, v_ref, lq1, lk1, lq2, lk2, sg_ref,
         o_ref, qm_scr, m_scr, l_scr, acc_scr) = refs
        lambda_init = diff
    else:
        qmask_ref, vmask_ref, q_ref, k_ref, v_ref, o_ref, qm_scr, m_scr, l_scr, acc_scr = refs
    nacc = max(acc_id) + 1
    heads_of = [[j for j in range(nh) if acc_id[j] == a] for a in range(nacc)]

    q = q_ref[...]
    for j in range(nh):
        qm_scr[j * tq:(j + 1) * tq, :] = q * qmask_ref[j:j + 1, :]
    m_scr[...] = jnp.full(m_scr.shape, NEG_INF, F32)
    l_scr[...] = jnp.zeros(l_scr.shape, F32)
    acc_scr[...] = jnp.zeros(acc_scr.shape, F32)
    vmask = vmask_ref[...]
    vmask_b = vmask.astype(BF16)

    def body(c, carry):
        ks = pl.multiple_of(c * tk, tk)
        kc = k_ref[pl.ds(ks, tk), :]
        vc = v_ref[pl.ds(ks, tk), :]
        s = lax.dot_general(qm_scr[...], kc, (((1,), (1,)), ((), ())),
                            preferred_element_type=F32)
        m_prev = m_scr[...]
        m_new = jnp.maximum(m_prev, jnp.max(s, axis=-1, keepdims=True))
        alpha = jnp.exp(m_prev - m_new)
        p = jnp.exp(s - m_new)
        l_scr[...] = alpha * l_scr[...] + jnp.sum(p, axis=-1, keepdims=True)
        m_scr[...] = m_new
        pb = p.astype(BF16)
        for a in range(nacc):
            hs = heads_of[a]
            lhs = jnp.concatenate([pb[j * tq:(j + 1) * tq] for j in hs], axis=1)
            rhs = jnp.concatenate([vc * vmask_b[j:j + 1, :] for j in hs], axis=0)
            pv = jnp.dot(lhs, rhs, preferred_element_type=F32)
            afull = alpha[hs[0] * tq:(hs[0] + 1) * tq] * vmask[hs[0]:hs[0] + 1, :]
            for j in hs[1:]:
                afull = afull + alpha[j * tq:(j + 1) * tq] * vmask[j:j + 1, :]
            acc_scr[a] = acc_scr[a] * afull + pv
        return carry

    lax.fori_loop(0, nk, body, 0)

    linv = 1.0 / l_scr[...]
    outs = []
    for a in range(nacc):
        hs = heads_of[a]
        inv = linv[hs[0] * tq:(hs[0] + 1) * tq] * vmask[hs[0]:hs[0] + 1, :]
        for j in hs[1:]:
            inv = inv + linv[j * tq:(j + 1) * tq] * vmask[j:j + 1, :]
        outs.append(acc_scr[a] * inv)
    if diff:
        lam = (jnp.exp(jnp.sum(lq1[...] * lk1[...], axis=-1, keepdims=True))
               - jnp.exp(jnp.sum(lq2[...] * lk2[...], axis=-1, keepdims=True)) + lambda_init)
        o = outs[0] - lam * outs[1]
        for hh in range(2):
            seg = o[:, hh * LANES:(hh + 1) * LANES]
            ms = jnp.mean(seg * seg, axis=-1, keepdims=True)
            seg = seg * lax.rsqrt(ms + NORM_EPS) * sg_ref[...] * (1.0 - lambda_init)
            o_ref[:, hh * LANES:(hh + 1) * LANES] = seg.astype(o_ref.dtype)
    else:
        o_ref[...] = outs[0].astype(o_ref.dtype)


def _flash(q, k, v, *, qcol0, kcol0, vcol0, ngroups, wq, wv, nh, acc_id, vhead, qmask, vmask,
           tq, tk, diff=None, diff_params=None):
    b, s = q.shape[0], q.shape[1]
    nacc = max(acc_id) + 1
    in_specs = [
        pl.BlockSpec((nh, wq), lambda bi, g, i: (0, 0)),
        pl.BlockSpec((nh, wv), lambda bi, g, i: (0, 0)),
        pl.BlockSpec((None, tq, wq), lambda bi, g, i: (bi, i, qcol0 + g)),
        pl.BlockSpec((None, s, wq), lambda bi, g, i: (bi, 0, kcol0 + g)),
        pl.BlockSpec((None, s, wv), lambda bi, g, i: (bi, 0, vcol0 + g)),
    ]
    args = [qmask, vmask, q, k, v]
    if diff is not None:
        for prm in diff_params[:4]:
            in_specs.append(pl.BlockSpec((1, 64), lambda bi, g, i: (0, 0)))
            args.append(prm.reshape(1, 64).astype(F32))
        in_specs.append(pl.BlockSpec((1, LANES), lambda bi, g, i: (0, 0)))
        args.append(diff_params[4].reshape(1, LANES).astype(F32))
    kern = functools.partial(_flash_kernel, nh=nh, acc_id=tuple(acc_id), vhead=tuple(vhead),
                             tq=tq, tk=tk, nk=s // tk, diff=diff)
    return pl.pallas_call(
        kern,
        out_shape=jax.ShapeDtypeStruct((b, s, ngroups * wv), BF16),
        grid=(b, ngroups, s // tq),
        in_specs=in_specs,
        out_specs=pl.BlockSpec((None, tq, wv), lambda bi, g, i: (bi, i, g)),
        scratch_shapes=[
            pltpu.VMEM((nh * tq, wq), BF16),
            pltpu.VMEM((nh * tq, 1), F32),
            pltpu.VMEM((nh * tq, 1), F32),
            pltpu.VMEM((nacc, tq, wv), F32),
        ],
        compiler_params=_cparams(("parallel", "parallel", "arbitrary")),
    )(*args)


def _band_kernel(qmask_ref, vmask_ref, q_ref, k_ref, v_ref, o_ref, lse_ref, *, tq, win, length):
    nh = 4
    i = pl.program_id(2)
    ks = jnp.clip(i * tq - A_HALF_WINDOW, 0, length - win)
    ks = pl.multiple_of(ks, A_HALF_WINDOW)
    kc = k_ref[pl.ds(ks, win), :]
    vc = v_ref[pl.ds(ks, win), :]
    q = q_ref[...]
    qm = jnp.concatenate([q * qmask_ref[j:j + 1, :] for j in range(nh)], axis=0)
    s = lax.dot_general(qm, kc, (((1,), (1,)), ((), ())), preferred_element_type=F32)
    qpos = i * tq + lax.broadcasted_iota(jnp.int32, (tq, win), 0)
    kpos = ks + lax.broadcasted_iota(jnp.int32, (tq, win), 1)
    valid = jnp.abs(qpos - kpos) <= A_HALF_WINDOW
    bias = jnp.where(valid, 0.0, NEG_INF).astype(F32)
    vmask = vmask_ref[...]
    vmask_b = vmask.astype(BF16)
    ps = []
    inv = None
    lse = None
    for j in range(nh):
        sj = jnp.where(valid, s[j * tq:(j + 1) * tq], bias)
        mj = jnp.max(sj, axis=-1, keepdims=True)
        pj = jnp.exp(sj - mj)
        lj = jnp.sum(pj, axis=-1, keepdims=True)
        ps.append(pj.astype(BF16))
        t_inv = (1.0 / lj) * vmask[j:j + 1, :]
        t_lse = (mj + jnp.log(lj)) * vmask[j:j + 1, :]
        inv = t_inv if inv is None else inv + t_inv
        lse = t_lse if lse is None else lse + t_lse
    lhs = jnp.concatenate(ps, axis=1)
    rhs = jnp.concatenate([vc * vmask_b[j:j + 1, :] for j in range(nh)], axis=0)
    pv = jnp.dot(lhs, rhs, preferred_element_type=F32)
    o_ref[...] = (pv * inv).astype(o_ref.dtype)
    lse_ref[...] = lse


def _band_attention(qkv, wg, dil, qmask, vmask):
    b, s, ncol = qkv.shape
    length = s // dil
    tq = min(256, length // 2)
    win = tq + 2 * A_HALF_WINDOW
    nblk = ncol // BLK
    view = qkv.reshape(b, length, dil * ncol)

    def col(which):
        return lambda bi, a, i: (bi, 0, (a // 2) * nblk + (wg * 3 + which) * 2 + a % 2)

    in_specs = [
        pl.BlockSpec((4, BLK), lambda bi, a, i: (0, 0)),
        pl.BlockSpec((4, BLK), lambda bi, a, i: (0, 0)),
        pl.BlockSpec((None, tq, BLK),
                     lambda bi, a, i: (bi, i, (a // 2) * nblk + (wg * 3) * 2 + a % 2)),
        pl.BlockSpec((None, length, BLK), col(1)),
        pl.BlockSpec((None, length, BLK), col(2)),
    ]
    kern = functools.partial(_band_kernel, tq=tq, win=win, length=length)
    o, lse = pl.pallas_call(
        kern,
        out_shape=(jax.ShapeDtypeStruct((b, length, dil * 2 * BLK), BF16),
                   jax.ShapeDtypeStruct((b, length, dil * 2 * BLK), F32)),
        grid=(b, dil * 2, length // tq),
        in_specs=in_specs,
        out_specs=(pl.BlockSpec((None, tq, BLK), lambda bi, a, i: (bi, i, a)),
                   pl.BlockSpec((None, tq, BLK), lambda bi, a, i: (bi, i, a))),
        compiler_params=_cparams(("parallel", "parallel", "arbitrary")),
    )(qmask, vmask, view, view, view)
    return o.reshape(b, s, 2 * BLK), lse.reshape(b, s, 2 * BLK)


def _a_out_kernel(o0, o1, o2, l0, l1, l2, w_ref, x_ref, out_ref):
    a0, a1, a2 = l0[...], l1[...], l2[...]
    mx = jnp.maximum(jnp.maximum(a0, a1), a2)
    e0, e1, e2 = jnp.exp(a0 - mx), jnp.exp(a1 - mx), jnp.exp(a2 - mx)
    inv = 1.0 / (e0 + e1 + e2)
    o = (e0 * o0[...].astype(F32) + e1 * o1[...].astype(F32) + e2 * o2[...].astype(F32)) * inv
    out_ref[...] = x_ref[...] + jnp.dot(o.astype(BF16), w_ref[...], preferred_element_type=F32)


def _a_out(os_, lses, w, x, tm):
    t = x.shape[0]
    kd = w.shape[0]
    row = lambda i: (i, 0)
    return pl.pallas_call(
        _a_out_kernel,
        out_shape=jax.ShapeDtypeStruct((t, D_MODEL), F32),
        grid=(t // tm,),
        in_specs=[pl.BlockSpec((tm, kd), row)] * 6
        + [pl.BlockSpec((kd, D_MODEL), lambda i: (0, 0)), pl.BlockSpec((tm, D_MODEL), row)],
        out_specs=pl.BlockSpec((tm, D_MODEL), row),
        compiler_params=_cparams(("parallel",)),
    )(*os_, *lses, w, x)


def _matres_kernel(o_ref, w_ref, x_ref, out_ref):
    out_ref[...] = x_ref[...] + jnp.dot(o_ref[...], w_ref[...], preferred_element_type=F32)


def _matmul_residual(o, w, x, tm):
    t = x.shape[0]
    kd = w.shape[0]
    return pl.pallas_call(
        _matres_kernel,
        out_shape=jax.ShapeDtypeStruct((t, D_MODEL), F32),
        grid=(t // tm,),
        in_specs=[pl.BlockSpec((tm, kd), lambda i: (i, 0)),
                  pl.BlockSpec((kd, D_MODEL), lambda i: (0, 0)),
                  pl.BlockSpec((tm, D_MODEL), lambda i: (i, 0))],
        out_specs=pl.BlockSpec((tm, D_MODEL), lambda i: (i, 0)),
        compiler_params=_cparams(("parallel",)),
    )(o, w, x)


def _xattn_kernel(x_ref, g_ref, wq_ref, kv_ref, wo_ref, out_ref):
    xf = x_ref[...]
    ms = jnp.mean(xf * xf, axis=-1, keepdims=True)
    h = (xf * lax.rsqrt(ms + NORM_EPS) * g_ref[...]).astype(BF16)
    q = jnp.dot(h, wq_ref[...], preferred_element_type=F32) * (X_HEAD_DIM ** -0.5)
    qb = q.astype(BF16)
    hd = X_HEADS * X_HEAD_DIM
    outs = []
    for hh in range(X_HEADS):
        qh = qb[:, hh * X_HEAD_DIM:(hh + 1) * X_HEAD_DIM]
        kh = kv_ref[:, hh * X_HEAD_DIM:(hh + 1) * X_HEAD_DIM]
        vh = kv_ref[:, hd + hh * X_HEAD_DIM:hd + (hh + 1) * X_HEAD_DIM]
        s = lax.dot_general(qh, kh, (((1,), (1,)), ((), ())), preferred_element_type=F32)
        m = jnp.max(s, axis=-1, keepdims=True)
        p = jnp.exp(s - m)
        l = jnp.sum(p, axis=-1, keepdims=True)
        oh = jnp.dot(p.astype(BF16), vh, preferred_element_type=F32) * (1.0 / l)
        outs.append(oh.astype(BF16))
    o = jnp.concatenate(outs, axis=1)
    out_ref[...] = xf + jnp.dot(o, wo_ref[...], preferred_element_type=F32)


def _xattn(x3, gain, wq, kv3, wo, tm):
    b, s, _ = x3.shape
    hd = X_HEADS * X_HEAD_DIM
    return pl.pallas_call(
        _xattn_kernel,
        out_shape=jax.ShapeDtypeStruct((b, s, D_MODEL), F32),
        grid=(b, s // tm),
        in_specs=[pl.BlockSpec((None, tm, D_MODEL), lambda bi, i: (bi, i, 0)),
                  pl.BlockSpec((1, D_MODEL), lambda bi, i: (0, 0)),
                  pl.BlockSpec((D_MODEL, hd), lambda bi, i: (0, 0)),
                  pl.BlockSpec((None, N_MEM, 2 * hd), lambda bi, i: (bi, 0, 0)),
                  pl.BlockSpec((hd, D_MODEL), lambda bi, i: (0, 0))],
        out_specs=pl.BlockSpec((None, tm, D_MODEL), lambda bi, i: (bi, i, 0)),
        compiler_params=_cparams(("parallel", "parallel")),
    )(x3, gain.reshape(1, D_MODEL), wq, kv3, wo)


def _mlp_kernel(x_ref, g_ref, w1_ref, w2_ref, fg_ref, out_ref, h_scr, acc_scr, *, final):
    f = pl.program_id(1)

    @pl.when(f == 0)
    def _():
        xf = x_ref[...]
        ms = jnp.mean(xf * xf, axis=-1, keepdims=True)
        h_scr[...] = (xf * lax.rsqrt(ms + NORM_EPS) * g_ref[...]).astype(BF16)
        acc_scr[...] = jnp.zeros(acc_scr.shape, F32)

    a = jnp.maximum(jnp.dot(h_scr[...], w1_ref[...], preferred_element_type=F32), 0.0)
    acc_scr[...] += jnp.dot((a * a).astype(BF16), w2_ref[...], preferred_element_type=F32)

    @pl.when(f == pl.num_programs(1) - 1)
    def _():
        y = x_ref[...] + acc_scr[...]
        if final:
            ms = jnp.mean(y * y, axis=-1, keepdims=True)
            y = y * lax.rsqrt(ms + NORM_EPS) * fg_ref[...]
        out_ref[...] = y


def _mlp(x, gain, w1, w2, final_gain, final, tm, tf):
    t = x.shape[0]
    return pl.pallas_call(
        functools.partial(_mlp_kernel, final=final),
        out_shape=jax.ShapeDtypeStruct((t, D_MODEL), F32),
        grid=(t // tm, D_FF // tf),
        in_specs=[pl.BlockSpec((tm, D_MODEL), lambda i, f: (i, 0)),
                  pl.BlockSpec((1, D_MODEL), lambda i, f: (0, 0)),
                  pl.BlockSpec((D_MODEL, tf), lambda i, f: (0, f)),
                  pl.BlockSpec((tf, D_MODEL), lambda i, f: (f, 0)),
                  pl.BlockSpec((1, D_MODEL), lambda i, f: (0, 0))],
        out_specs=pl.BlockSpec((tm, D_MODEL), lambda i, f: (i, 0)),
        scratch_shapes=[pltpu.VMEM((tm, D_MODEL), BF16), pltpu.VMEM((tm, D_MODEL), F32)],
        compiler_params=_cparams(("parallel", "arbitrary")),
    )(x, gain.reshape(1, D_MODEL), w1, w2, final_gain.reshape(1, D_MODEL))


_F_ROT16 = np.array(list(range(0, 8)) + list(range(16, 40)))
_P_ROT16 = np.array(list(range(8, 16)) + list(range(40, 64)))
_F_AXIAL = np.array(list(range(0, 16)) + list(range(32, 48)))
_P_AXIAL = np.array(list(range(16, 32)) + list(range(48, 64)))


def _block_dims(first, partner):
    lane = np.arange(BLK)
    half, slot, u = lane // LANES, (lane % LANES) // 32, lane % 32
    return np.where(half == 0, first[u], partner[u]), slot


def _slot_masks(nslot, slot_of_lane):
    return np.stack([(slot_of_lane == j) for j in range(nslot)]).astype(np.float32)


def _rope_tables(pos_list, theta, rot, scale_list, npad):
    half = rot // 2
    inv_freq = jnp.exp(jnp.arange(half, dtype=F32) * (-2.0 * math.log(theta) / rot))
    cs, ss = [], []
    for pos in pos_list:
        ang = pos.astype(F32)[:, None] * inv_freq[None, :]
        cs.append(jnp.cos(ang))
        ss.append(jnp.sin(ang))
    c = jnp.concatenate(cs, axis=1)
    s = jnp.concatenate(ss, axis=1)
    n = c.shape[0]
    if npad:
        c = jnp.concatenate([c, jnp.ones((n, npad), F32)], axis=1)
        s = jnp.concatenate([s, jnp.zeros((n, npad), F32)], axis=1)
    reps = LANES // c.shape[1]
    c = jnp.tile(c, (1, reps))
    s = jnp.tile(s, (1, reps))
    ctab = [c * sc for sc in scale_list] + [jnp.ones_like(c)]
    stab = [s * sc for sc in scale_list] + [jnp.zeros_like(s)]
    return jnp.stack(ctab), jnp.stack(stab)


def _mixer_a(x, b, s, gain, w_in, w_out, pos, tm):
    dims, slot = _block_dims(_F_ROT16, _P_ROT16)
    cols = []
    tt = []
    for wg in range(A_GROUPS):
        for which in range(3):
            base = (wg * 3 + which) * A_HEADS * 64
            for hg in range(2):
                if which < 2:
                    cols.append(base + hg * BLK + slot * 64 + dims)
                else:
                    cols.append(base + hg * BLK + np.arange(BLK))
            tt.append(which)
    cols = np.concatenate(cols)
    w = w_in[:, cols].astype(BF16)
    ctab, stab = _rope_tables([pos], ROPE_THETA, 16, [0.125, 1.0], 24)
    qkv = _proj(x, 0, D_MODEL, D_MODEL, gain, w, tn=2 * BLK, out_dtype=BF16, seq=s, tm=tm,
                tables=(ctab, stab), tt=np.array(tt, np.int32))
    qkv = qkv.reshape(b, s, A_IN)
    qmask = jnp.asarray(_slot_masks(4, slot), BF16)
    vmask = jnp.asarray(_slot_masks(4, np.arange(BLK) // 64), F32)
    outs, lses = [], []
    for wg, (window, dil) in enumerate(A_PATTERNS):
        assert window // (2 * dil) == A_HALF_WINDOW
        o, lse = _band_attention(qkv, wg, dil, qmask, vmask)
        outs.append(o.reshape(b * s, 2 * BLK))
        lses.append(lse.reshape(b * s, 2 * BLK))
    return _a_out(outs, lses, w_out.astype(BF16), x, tm)


def _mixer_b(x, b, s, gain, w_in, q_gain, k_gain, w_out, rows, cols_pos, tm, tq, tk):
    dims, slot = _block_dims(_F_AXIAL, _P_AXIAL)
    qcols = np.concatenate([g * BLK + slot * 64 + dims for g in range(B_KV_HEADS)])
    kcols = np.concatenate([B_HEADS * 64 + g * 64 + dims for g in range(B_KV_HEADS)])
    vcols = np.concatenate([(B_HEADS + B_KV_HEADS) * 64 + g * 64 + np.arange(BLK) % 64
                            for g in range(B_KV_HEADS)])
    wqk = w_in[:, np.concatenate([qcols, kcols])].astype(BF16)
    wv = w_in[:, vcols].astype(BF16)
    hgain = jnp.concatenate([jnp.tile(q_gain[dims] * 0.125, B_KV_HEADS),
                             jnp.tile(k_gain[dims], B_KV_HEADS)]).reshape(1, -1).astype(F32)
    bd = jnp.asarray(slot[:, None] == slot[None, :], BF16)
    ctab, stab = _rope_tables([rows, cols_pos], AXIAL_THETA, 32, [1.0], 0)
    n = B_KV_HEADS * BLK
    qk = _proj(x, 0, D_MODEL, D_MODEL, gain, wqk, tn=n, out_dtype=BF16, seq=s, tm=tm,
               tables=(ctab, stab), tt=np.zeros((2,), np.int32), headnorm=(hgain, bd))
    v = _proj(x, 0, D_MODEL, D_MODEL, gain, wv, tn=n, out_dtype=BF16, seq=s, tm=tm)
    qk = qk.reshape(b, s, 2 * n)
    v = v.reshape(b, s, n)
    qmask = jnp.asarray(_slot_masks(4, slot), BF16)
    vmask = jnp.asarray(_slot_masks(4, np.arange(BLK) // 64), F32)
    o = _flash(qk, qk, v, qcol0=0, kcol0=B_KV_HEADS, vcol0=0, ngroups=B_KV_HEADS, wq=BLK, wv=BLK,
               nh=4, acc_id=(0, 0, 0, 0), vhead=(0, 1, 2, 3), qmask=qmask, vmask=vmask, tq=tq, tk=tk)
    return _matmul_residual(o.reshape(b * s, B_HEADS * 64), w_out.astype(BF16), x, tm)


def _mixer_c(x, b, s, gain, w_in, lq1, lk1, lq2, lk2, sub_gain, w_out, pos, lambda_init, tm, tq, tk):
    dims, slot = _block_dims(_F_ROT16, _P_ROT16)
    ngr = C_HEADS // 2
    cols = []
    for which in range(3):
        for g in range(ngr):
            base = which * C_HEADS * 128 + g * BLK
            cols.append(base + (slot * 64 + dims if which < 2 else np.arange(BLK)))
    w = w_in[:, np.concatenate(cols)].astype(BF16)
    ctab, stab = _rope_tables([pos], ROPE_THETA, 16, [0.125, 1.0], 24)
    n = ngr * BLK
    qkv = _proj(x, 0, D_MODEL, D_MODEL, gain, w, tn=n, out_dtype=BF16, seq=s, tm=tm,
                tables=(ctab, stab), tt=np.array([0, 1, 2], np.int32))
    qkv = qkv.reshape(b, s, 3 * n)
    qmask = jnp.asarray(_slot_masks(4, slot), BF16)
    vmask = jnp.asarray(_slot_masks(2, np.arange(BLK) // LANES)[[0, 0, 1, 1]], F32)
    o = _flash(qkv, qkv, qkv, qcol0=0, kcol0=ngr, vcol0=2 * ngr, ngroups=ngr, wq=BLK, wv=BLK,
               nh=4, acc_id=(0, 1, 0, 1), vhead=(0, 0, 1, 1), qmask=qmask, vmask=vmask, tq=tq, tk=tk,
               diff=lambda_init, diff_params=(lq1, lk1, lq2, lk2, sub_gain))
    return _matmul_residual(o.reshape(b * s, C_HEADS * 128), w_out.astype(BF16), x, tm)


def _mixer_d(x, b, s, gain, w_in, q_gain, kv_gain, w_uq, w_ukv, w_out, pos, tm, tq, tk):
    lane = np.arange(LANES)
    slot_h = np.where(lane < 32, 0, np.where(lane < 64, 1, np.where(lane < 80, 0, np.where(lane < 96, 1, -1))))
    slot = np.concatenate([slot_h, slot_h])
    nope_lane = lane < 64
    rope_lane = (lane >= 64) & (lane < 96)
    ngr = D_HEADS // 2

    w1 = jnp.zeros((D_MODEL, 4 * BLK), F32)
    w1 = w1.at[:, :D_Q_RANK].set(w_in[:, :D_Q_RANK])
    w1 = w1.at[:, 2 * BLK:3 * BLK].set(w_in[:, D_Q_RANK:D_Q_RANK + D_KV_RANK])
    kr_src = np.zeros((BLK,), np.int64)
    kr_on = np.zeros((BLK,), bool)
    for hf in range(2):
        for l in range(LANES):
            if rope_lane[l]:
                kr_src[hf * LANES + l] = D_Q_RANK + D_KV_RANK + hf * 16 + (l - 64) % 16
                kr_on[hf * LANES + l] = True
    w1 = w1.at[:, 3 * BLK:].set(jnp.where(jnp.asarray(kr_on)[None, :], w_in[:, kr_src], 0.0))
    cmb = _proj(x, 0, D_MODEL, D_MODEL, gain, w1.astype(BF16), tn=4 * BLK, out_dtype=F32, seq=s, tm=tm)

    qsrc = np.zeros((ngr * BLK,), np.int64)
    qon = np.zeros((ngr * BLK,), bool)
    ksrc = np.zeros((ngr * BLK,), np.int64)
    kon = np.zeros((ngr * BLK,), bool)
    for g in range(ngr):
        for hf in range(2):
            for l in range(LANES):
                idx = g * BLK + hf * LANES + l
                if slot_h[l] < 0:
                    continue
                head = 2 * g + slot_h[l]
                if nope_lane[l]:
                    d = hf * 32 + l % 32
                    qsrc[idx], qon[idx] = head * 96 + d, True
                    ksrc[idx], kon[idx] = head * 128 + d, True
                else:
                    d = hf * 16 + (l - 64) % 16
                    qsrc[idx], qon[idx] = head * 96 + D_NOPE + d, True
    wq2 = jnp.where(jnp.asarray(qon)[None, :], w_uq[:, qsrc], 0.0)
    wq2 = jnp.concatenate([wq2, jnp.zeros((2 * BLK - D_Q_RANK, ngr * BLK), F32)], axis=0).astype(BF16)
    wk2 = jnp.where(jnp.asarray(kon)[None, :], w_ukv[:, ksrc], 0.0).astype(BF16)
    vsrc = np.concatenate([h * 128 + D_NOPE + np.arange(64) for h in range(D_HEADS)])
    wv2 = w_ukv[:, vsrc].astype(BF16)
    qg = jnp.concatenate([q_gain, jnp.zeros((2 * BLK - D_Q_RANK,), F32)])

    half = D_ROPE // 2
    inv_freq = jnp.exp(jnp.arange(half, dtype=F32) * (-2.0 * math.log(ROPE_THETA) / D_ROPE))
    ang = pos.astype(F32)[:, None] * inv_freq[None, :]
    ones64 = jnp.ones((s, 64), F32)
    pad32 = jnp.ones((s, 32), F32)
    c = jnp.concatenate([ones64, jnp.cos(ang), jnp.cos(ang), pad32], axis=1)
    sn = jnp.concatenate([0.0 * ones64, jnp.sin(ang), jnp.sin(ang), 0.0 * pad32], axis=1)
    qs = (D_NOPE + D_ROPE) ** -0.5
    ctab = jnp.stack([c * qs, c])
    stab = jnp.stack([sn * qs, sn])

    q = _proj(cmb, 0, 2 * BLK, D_Q_RANK, qg, wq2, tn=4 * BLK, out_dtype=BF16, seq=s, tm=tm,
              tables=(ctab, stab), tt=np.zeros((ngr // 4,), np.int32))
    k = _proj(cmb, 2, BLK, D_KV_RANK, kv_gain, wk2, tn=BLK, out_dtype=BF16, seq=s, tm=tm,
              tables=(ctab, stab), tt=np.ones((ngr,), np.int32), add=(cmb, 3))
    v = _proj(cmb, 2, BLK, D_KV_RANK, kv_gain, wv2, tn=4 * BLK, out_dtype=BF16, seq=s, tm=tm)
    q = q.reshape(b, s, ngr * BLK)
    k = k.reshape(b, s, ngr * BLK)
    v = v.reshape(b, s, D_HEADS * 64)
    qmask = jnp.asarray(_slot_masks(2, slot), BF16)
    vmask = jnp.asarray(_slot_masks(2, np.arange(LANES) // 64), F32)
    o = _flash(q, k, v, qcol0=0, kcol0=0, vcol0=0, ngroups=ngr, wq=BLK, wv=LANES,
               nh=2, acc_id=(0, 0), vhead=(0, 1), qmask=qmask, vmask=vmask, tq=tq, tk=tk)
    return _matmul_residual(o.reshape(b * s, D_HEADS * 64), w_out.astype(BF16), x, tm)


def _run_trunk(x3, mem3, p, tm=512, tq=256, tk=512):
    b, s, _ = x3.shape
    t = b * s
    tm = min(tm, s)
    x = x3.reshape(t, D_MODEL)
    pos = jnp.arange(s, dtype=F32)
    rows = jnp.repeat(jnp.arange(s // GRID_W, dtype=F32), GRID_W)
    cols_pos = jnp.tile(jnp.arange(GRID_W, dtype=F32), s // GRID_W)
    memf = mem3.reshape(b * N_MEM, D_MODEL)
    for i in range(DEPTH):
        m, j = i % 4, i // 4
        g = p['norm_mix'][i]
        if m == 0:
            x = _mixer_a(x, b, s, g, p['a_w_in'][j], p['a_w_out'][j], pos, tm)
        elif m == 1:
            x = _mixer_b(x, b, s, g, p['b_w_in'][j], p['b_q_norm'][j], p['b_k_norm'][j],
                         p['b_w_out'][j], rows, cols_pos, tm, tq, tk)
        elif m == 2:
            x = _mixer_c(x, b, s, g, p['c_w_in'][j], p['c_lambda_q1'][j], p['c_lambda_k1'][j],
                         p['c_lambda_q2'][j], p['c_lambda_k2'][j], p['c_sub_norm'][j],
                         p['c_w_out'][j], pos, 0.8 - 0.6 * math.exp(-0.3 * i), tm, tq, tk)
        else:
            x = _mixer_d(x, b, s, g, p['d_w_in'][j], p['d_q_norm'][j], p['d_kv_norm'][j],
                         p['d_w_uq'][j], p['d_w_ukv'][j], p['d_w_out'][j], pos, tm, tq, tk)
        kv = _proj(memf, 0, D_MODEL, D_MODEL, p['norm_mem'][i], p['w_xkv'][i].astype(BF16),
                   tn=2 * X_HEADS * X_HEAD_DIM, out_dtype=BF16, seq=N_MEM, tm=N_MEM)
        x = _xattn(x.reshape(b, s, D_MODEL), p['norm_x'][i], p['w_xq'][i].astype(BF16),
                   kv.reshape(b, N_MEM, 2 * X_HEADS * X_HEAD_DIM), p['w_xo'][i].astype(BF16),
                   min(tm, 512)).reshape(t, D_MODEL)
        x = _mlp(x, p['norm_mlp'][i], p['w_mlp_in'][i].astype(BF16), p['w_mlp_out'][i].astype(BF16),
                 p['final_norm'], i == DEPTH - 1, tm, 1024)
    return x.reshape(b, s, D_MODEL)


def kernel(x_prompt, x_sample, mem_prompt, mem_sample, norm_mix, norm_x, norm_mem, w_xq, w_xkv, w_xo, norm_mlp, w_mlp_in, w_mlp_out, a_w_in, a_w_out, b_w_in, b_q_norm, b_k_norm, b_w_out, c_w_in, c_lambda_q1, c_lambda_k1, c_lambda_q2, c_lambda_k2, c_sub_norm, c_w_out, d_w_in, d_q_norm, d_kv_norm, d_w_uq, d_w_ukv, d_w_out, final_norm):
    p = dict(norm_mix=norm_mix, norm_x=norm_x, norm_mem=norm_mem, w_xq=w_xq, w_xkv=w_xkv,
             w_xo=w_xo, norm_mlp=norm_mlp, w_mlp_in=w_mlp_in, w_mlp_out=w_mlp_out,
             a_w_in=a_w_in, a_w_out=a_w_out, b_w_in=b_w_in, b_q_norm=b_q_norm,
             b_k_norm=b_k_norm, b_w_out=b_w_out, c_w_in=c_w_in, c_lambda_q1=c_lambda_q1,
             c_lambda_k1=c_lambda_k1, c_lambda_q2=c_lambda_q2, c_lambda_k2=c_lambda_k2,
             c_sub_norm=c_sub_norm, c_w_out=c_w_out, d_w_in=d_w_in, d_q_norm=d_q_norm,
             d_kv_norm=d_kv_norm, d_w_uq=d_w_uq, d_w_ukv=d_w_ukv, d_w_out=d_w_out,
             final_norm=final_norm)
    return (_run_trunk(x_prompt, mem_prompt, p), _run_trunk(x_sample, mem_sample, p))
```

```python
import functools
import math

import numpy as np
import jax
import jax.numpy as jnp
from jax import lax
from jax.experimental import pallas as pl
from jax.experimental.pallas import tpu as pltpu

F32 = jnp.float32
BF16 = jnp.bfloat16

D_MODEL = 1024
DEPTH = 4
N_MEM = 256
GRID_W = 64
D_FF = 4 * D_MODEL
NORM_EPS = 1e-6
ROPE_THETA = 500000.0
AXIAL_THETA = 10000.0
NEG_INF = -1e30

A_PATTERNS = ((128, 1), (512, 4), (2048, 16))
A_GROUPS = 3
A_HEADS = 8
A_IN = A_GROUPS * 3 * A_HEADS * 64
A_HALF_WINDOW = 64
B_HEADS = 16
B_KV_HEADS = 4
C_HEADS = 8
D_HEADS = 16
D_Q_RANK = 384
D_KV_RANK = 256
D_NOPE = 64
D_ROPE = 32
X_HEADS = 4
X_HEAD_DIM = 128

LOG2E = 1.4426950408889634
ONES_ROWS = 16
LANES = 128
BLK = 2 * LANES
VMEM_LIMIT = 56 * 1024 * 1024


def _cparams(sem):
    return pltpu.CompilerParams(dimension_semantics=sem, vmem_limit_bytes=VMEM_LIMIT)


def _proj_kernel(tt_ref, src_ref, g_ref, w_ref, *rest, dnorm, rope, headnorm, add, nsub):
    del tt_ref
    rest = list(rest)
    c_ref = s_ref = hg_ref = bd_ref = add_ref = None
    if rope:
        c_ref, s_ref = rest[0], rest[1]
        rest = rest[2:]
    if headnorm:
        hg_ref, bd_ref = rest[0], rest[1]
        rest = rest[2:]
    if add:
        add_ref = rest[0]
        rest = rest[1:]
    o_ref, h_scr = rest

    @pl.when(pl.program_id(1) == 0)
    def _():
        xf = src_ref[...].astype(F32)
        ms = jnp.sum(xf * xf, axis=-1, keepdims=True) * (1.0 / dnorm)
        h_scr[...] = (xf * lax.rsqrt(ms + NORM_EPS) * g_ref[...]).astype(BF16)

    y = jnp.dot(h_scr[...], w_ref[...], preferred_element_type=F32)
    if add:
        y = y + add_ref[...]
    if headnorm:
        y2 = y * y
        hi = y2.astype(BF16)
        lo = (y2 - hi.astype(F32)).astype(BF16)
        parts = []
        for n in range(nsub):
            sl = slice(n * BLK, (n + 1) * BLK)
            ss = (jnp.dot(hi[:, sl], bd_ref[...], preferred_element_type=F32)
                  + jnp.dot(lo[:, sl], bd_ref[...], preferred_element_type=F32))
            parts.append(y[:, sl] * lax.rsqrt(ss * (1.0 / 64.0) + NORM_EPS))
        y = (parts[0] if nsub == 1 else jnp.concatenate(parts, axis=1)) * hg_ref[...]
    if rope:
        c = c_ref[0]
        s = s_ref[0]
        for n in range(nsub):
            y1 = y[:, n * BLK:n * BLK + LANES]
            y2 = y[:, n * BLK + LANES:(n + 1) * BLK]
            o_ref[:, n * BLK:n * BLK + LANES] = (y1 * c - y2 * s).astype(o_ref.dtype)
            o_ref[:, n * BLK + LANES:(n + 1) * BLK] = (y2 * c + y1 * s).astype(o_ref.dtype)
    else:
        o_ref[...] = y.astype(o_ref.dtype)


def _proj(src, src_cb, kdim, dnorm, gain, w, *, tn, out_dtype, seq, tm,
          tables=None, tt=None, headnorm=None, add=None):
    t = src.shape[0]
    n = w.shape[1]
    nj = n // tn
    nsub = tn // BLK if (tables is not None or headnorm is not None) else 1
    ns = seq // tm
    if tt is None:
        tt = np.zeros((nj,), np.int32)
    in_specs = [
        pl.BlockSpec((tm, kdim), lambda i, j, tt_: (i, src_cb)),
        pl.BlockSpec((1, kdim), lambda i, j, tt_: (0, 0)),
        pl.BlockSpec((kdim, tn), lambda i, j, tt_: (0, j)),
    ]
    args = [src, gain.reshape(1, kdim).astype(F32), w]
    if tables is not None:
        for tb in tables:
            in_specs.append(pl.BlockSpec((1, tm, LANES), lambda i, j, tt_: (tt_[j], i % ns, 0)))
            args.append(tb)
    if headnorm is not None:
        in_specs.append(pl.BlockSpec((1, tn), lambda i, j, tt_: (0, j)))
        in_specs.append(pl.BlockSpec((BLK, BLK), lambda i, j, tt_: (0, 0)))
        args += [headnorm[0], headnorm[1]]
    if add is not None:
        add_arr, add_cb = add
        in_specs.append(pl.BlockSpec((tm, tn), lambda i, j, tt_: (i, add_cb)))
        args.append(add_arr)
    kern = functools.partial(_proj_kernel, dnorm=dnorm, rope=tables is not None,
                             headnorm=headnorm is not None, add=add is not None, nsub=nsub)
    return pl.pallas_call(
        kern,
        out_shape=jax.ShapeDtypeStruct((t, n), out_dtype),
        grid_spec=pltpu.PrefetchScalarGridSpec(
            num_scalar_prefetch=1,
            grid=(t // tm, nj),
            in_specs=in_specs,
            out_specs=pl.BlockSpec((tm, tn), lambda i, j, tt_: (i, j)),
            scratch_shapes=[pltpu.VMEM((tm, kdim), BF16)],
        ),
        compiler_params=_cparams(("parallel", "arbitrary")),
    )(jnp.asarray(tt, jnp.int32), *args)


def _flash_kernel(*refs, nh, vgroups, tq, tk, nk, diff):
    if diff:
        (qmask_ref, q_ref, k_ref, vt_ref, lq1, lk1, lq2, lk2, sg_ref,
         o_ref, qm_scr, s_scr, mx_scr, m_scr, acc_scr) = refs
        lambda_init = diff
    else:
        qmask_ref, q_ref, k_ref, vt_ref, o_ref, qm_scr, s_scr, mx_scr, m_scr, acc_scr = refs

    q = q_ref[...]
    for j in range(nh):
        qm_scr[j] = q * qmask_ref[j:j + 1, :]
    m_scr[...] = jnp.full(m_scr.shape, NEG_INF, F32)
    acc_scr[...] = jnp.zeros(acc_scr.shape, F32)

    def scores(c, slot):
        ks = pl.multiple_of(c * tk, tk)
        kc = k_ref[pl.ds(ks, tk), :]
        for j in range(nh):
            st = lax.dot_general(kc, qm_scr[j], (((1,), (1,)), ((), ())),
                                 preferred_element_type=F32)
            s_scr[slot, j] = st
            mx_scr[slot, j] = jnp.max(st, axis=0, keepdims=True)

    def accumulate(c, slot):
        vtc = vt_ref[c]
        pts, alphas = [], []
        for j in range(nh):
            m_prev = m_scr[j]
            m_new = jnp.maximum(m_prev, mx_scr[slot, j])
            alphas.append(jnp.exp2(m_prev - m_new))
            m_scr[j] = m_new
            pts.append(jnp.exp2(s_scr[slot, j] - m_new).astype(BF16))
        for gi, (r0, nr, h0, hn) in enumerate(vgroups):
            rhs = pts[h0] if hn == 1 else jnp.concatenate(pts[h0:h0 + hn], axis=1)
            a = alphas[h0] if hn == 1 else jnp.concatenate(alphas[h0:h0 + hn], axis=1)
            acc_scr[gi] = acc_scr[gi] * a + jnp.dot(vtc[r0:r0 + nr + ONES_ROWS, :], rhs,
                                                    preferred_element_type=F32)

    scores(0, 0)

    def body(i, carry):
        c = 2 * i
        scores(c + 1, 1)
        accumulate(c, 0)
        scores(jnp.minimum(c + 2, nk - 1), 0)
        accumulate(c + 1, 1)
        return carry

    lax.fori_loop(0, nk // 2, body, 0)

    pieces = []
    if diff:
        lam = (jnp.exp(jnp.sum(lq1[...] * lk1[...], axis=-1, keepdims=True))
               - jnp.exp(jnp.sum(lq2[...] * lk2[...], axis=-1, keepdims=True)) + lambda_init)
        for gi, (r0, nr, h0, hn) in enumerate(vgroups):
            acc = acc_scr[gi]
            linv = 1.0 / acc[nr:nr + 1, :]
            oh = acc[:nr, :tq] * linv[:, :tq] - lam * (acc[:nr, tq:] * linv[:, tq:])
            ms = jnp.mean(oh * oh, axis=0, keepdims=True)
            pieces.append(oh * lax.rsqrt(ms + NORM_EPS))
    else:
        for gi, (r0, nr, h0, hn) in enumerate(vgroups):
            acc = acc_scr[gi]
            on = acc[:nr, :] * (1.0 / acc[nr:nr + 1, :])
            for jj in range(hn):
                pieces.append(on[:, jj * tq:(jj + 1) * tq])
    ot = pieces[0] if len(pieces) == 1 else jnp.concatenate(pieces, axis=0)
    o = ot.T
    if diff:
        o = o * sg_ref[...] * (1.0 - lambda_init)
    o_ref[...] = o.astype(o_ref.dtype)


def _flash(q, k, vt, *, qcol0, kcol0, ngroups, wq, wv, wo, nh, vgroups, qmask, tq, tk,
           diff=None, diff_params=None):
    b, s = q.shape[0], q.shape[1]
    nk = s // tk
    in_specs = [
        pl.BlockSpec((nh, wq), lambda bi, g, i: (0, 0)),
        pl.BlockSpec((None, tq, wq), lambda bi, g, i: (bi, i, qcol0 + g)),
        pl.BlockSpec((None, s, wq), lambda bi, g, i: (bi, 0, kcol0 + g)),
        pl.BlockSpec((None, None, nk, wv, tk), lambda bi, g, i: (bi, g, 0, 0, 0)),
    ]
    args = [qmask, q, k, vt]
    if diff is not None:
        for prm in diff_params[:4]:
            in_specs.append(pl.BlockSpec((1, 64), lambda bi, g, i: (0, 0)))
            args.append(prm.reshape(1, 64).astype(F32))
        in_specs.append(pl.BlockSpec((1, wo), lambda bi, g, i: (0, 0)))
        args.append(jnp.tile(diff_params[4].astype(F32), wo // LANES).reshape(1, wo))
    kern = functools.partial(_flash_kernel, nh=nh, vgroups=tuple(vgroups), tq=tq, tk=tk, nk=nk,
                             diff=diff)
    nr, hn = vgroups[0][1], vgroups[0][3]
    return pl.pallas_call(
        kern,
        out_shape=jax.ShapeDtypeStruct((b, s, ngroups * wo), BF16),
        grid=(b, ngroups, s // tq),
        in_specs=in_specs,
        out_specs=pl.BlockSpec((None, tq, wo), lambda bi, g, i: (bi, i, g)),
        scratch_shapes=[
            pltpu.VMEM((nh, tq, wq), BF16),
            pltpu.VMEM((2, nh, tk, tq), F32),
            pltpu.VMEM((2, nh, 1, tq), F32),
            pltpu.VMEM((nh, 1, tq), F32),
            pltpu.VMEM((len(vgroups), nr + ONES_ROWS, hn * tq), F32),
        ],
        compiler_params=_cparams(("parallel", "parallel", "arbitrary")),
    )(*args)


def _transpose_values(v, ngroups, nvg, nr, tk):
    b, s, _ = v.shape
    v = v.reshape(b, s, ngroups, nvg, nr)
    v = jnp.concatenate([v, jnp.ones((b, s, ngroups, nvg, ONES_ROWS), v.dtype)], axis=-1)
    v = v.reshape(b, s // tk, tk, ngroups, nvg * (nr + ONES_ROWS))
    return v.transpose(0, 3, 1, 4, 2)


def _band_kernel(qmask_ref, vmask_ref, q_ref, k_ref, v_ref, o_ref, lse_ref, *, tq, win, length):
    nh = 4
    i = pl.program_id(2)
    ks = jnp.clip(i * tq - A_HALF_WINDOW, 0, length - win)
    ks = pl.multiple_of(ks, A_HALF_WINDOW)
    kc = k_ref[pl.ds(ks, win), :]
    vc = v_ref[pl.ds(ks, win), :]
    q = q_ref[...]
    qm = jnp.concatenate([q * qmask_ref[j:j + 1, :] for j in range(nh)], axis=0)
    s = lax.dot_general(qm, kc, (((1,), (1,)), ((), ())), preferred_element_type=F32)
    qpos = i * tq + lax.broadcasted_iota(jnp.int32, (tq, win), 0)
    kpos = ks + lax.broadcasted_iota(jnp.int32, (tq, win), 1)
    valid = jnp.abs(qpos - kpos) <= A_HALF_WINDOW
    bias = jnp.where(valid, 0.0, NEG_INF).astype(F32)
    vmask = vmask_ref[...]
    vmask_b = vmask.astype(BF16)
    ps = []
    inv = None
    lse = None
    for j in range(nh):
        sj = jnp.where(valid, s[j * tq:(j + 1) * tq], bias)
        mj = jnp.max(sj, axis=-1, keepdims=True)
        pj = jnp.exp(sj - mj)
        lj = jnp.sum(pj, axis=-1, keepdims=True)
        ps.append(pj.astype(BF16))
        t_inv = (1.0 / lj) * vmask[j:j + 1, :]
        t_lse = (mj + jnp.log(lj)) * vmask[j:j + 1, :]
        inv = t_inv if inv is None else inv + t_inv
        lse = t_lse if lse is None else lse + t_lse
    lhs = jnp.concatenate(ps, axis=1)
    rhs = jnp.concatenate([vc * vmask_b[j:j + 1, :] for j in range(nh)], axis=0)
    pv = jnp.dot(lhs, rhs, preferred_element_type=F32)
    o_ref[...] = (pv * inv).astype(o_ref.dtype)
    lse_ref[...] = lse


def _band_attention(qkv, wg, dil, qmask, vmask):
    b, s, ncol = qkv.shape
    length = s // dil
    tq = min(256, length // 2)
    win = tq + 2 * A_HALF_WINDOW
    nblk = ncol // BLK
    view = qkv.reshape(b, length, dil * ncol)

    def col(which):
        return lambda bi, a, i: (bi, 0, (a // 2) * nblk + (wg * 3 + which) * 2 + a % 2)

    in_specs = [
        pl.BlockSpec((4, BLK), lambda bi, a, i: (0, 0)),
        pl.BlockSpec((4, BLK), lambda bi, a, i: (0, 0)),
        pl.BlockSpec((None, tq, BLK),
                     lambda bi, a, i: (bi, i, (a // 2) * nblk + (wg * 3) * 2 + a % 2)),
        pl.BlockSpec((None, length, BLK), col(1)),
        pl.BlockSpec((None, length, BLK), col(2)),
    ]
    kern = functools.partial(_band_kernel, tq=tq, win=win, length=length)
    o, lse = pl.pallas_call(
        kern,
        out_shape=(jax.ShapeDtypeStruct((b, length, dil * 2 * BLK), BF16),
                   jax.ShapeDtypeStruct((b, length, dil * 2 * BLK), F32)),
        grid=(b, dil * 2, length // tq),
        in_specs=in_specs,
        out_specs=(pl.BlockSpec((None, tq, BLK), lambda bi, a, i: (bi, i, a)),
                   pl.BlockSpec((None, tq, BLK), lambda bi, a, i: (bi, i, a))),
        compiler_params=_cparams(("parallel", "parallel", "arbitrary")),
    )(qmask, vmask, view, view, view)
    return o.reshape(b, s, 2 * BLK), lse.reshape(b, s, 2 * BLK)


def _a_out_kernel(o0, o1, o2, l0, l1, l2, w_ref, x_ref, out_ref):
    a0, a1, a2 = l0[...], l1[...], l2[...]
    mx = jnp.maximum(jnp.maximum(a0, a1), a2)
    e0, e1, e2 = jnp.exp(a0 - mx), jnp.exp(a1 - mx), jnp.exp(a2 - mx)
    inv = 1.0 / (e0 + e1 + e2)
    o = (e0 * o0[...].astype(F32) + e1 * o1[...].astype(F32) + e2 * o2[...].astype(F32)) * inv
    out_ref[...] = x_ref[...] + jnp.dot(o.astype(BF16), w_ref[...], preferred_element_type=F32)


def _a_out(os_, lses, w, x, tm):
    t = x.shape[0]
    kd = w.shape[0]
    row = lambda i: (i, 0)
    return pl.pallas_call(
        _a_out_kernel,
        out_shape=jax.ShapeDtypeStruct((t, D_MODEL), F32),
        grid=(t // tm,),
        in_specs=[pl.BlockSpec((tm, kd), row)] * 6
        + [pl.BlockSpec((kd, D_MODEL), lambda i: (0, 0)), pl.BlockSpec((tm, D_MODEL), row)],
        out_specs=pl.BlockSpec((tm, D_MODEL), row),
        compiler_params=_cparams(("parallel",)),
    )(*os_, *lses, w, x)


def _matres_kernel(o_ref, w_ref, x_ref, out_ref):
    out_ref[...] = x_ref[...] + jnp.dot(o_ref[...], w_ref[...], preferred_element_type=F32)


def _matmul_residual(o, w, x, tm):
    t = x.shape[0]
    kd = w.shape[0]
    return pl.pallas_call(
        _matres_kernel,
        out_shape=jax.ShapeDtypeStruct((t, D_MODEL), F32),
        grid=(t // tm,),
        in_specs=[pl.BlockSpec((tm, kd), lambda i: (i, 0)),
                  pl.BlockSpec((kd, D_MODEL), lambda i: (0, 0)),
                  pl.BlockSpec((tm, D_MODEL), lambda i: (i, 0))],
        out_specs=pl.BlockSpec((tm, D_MODEL), lambda i: (i, 0)),
        compiler_params=_cparams(("parallel",)),
    )(o, w, x)


def _xattn_kernel(x_ref, g_ref, wq_ref, kv_ref, wo_ref, out_ref):
    xf = x_ref[...]
    ms = jnp.mean(xf * xf, axis=-1, keepdims=True)
    h = (xf * lax.rsqrt(ms + NORM_EPS) * g_ref[...]).astype(BF16)
    q = jnp.dot(h, wq_ref[...], preferred_element_type=F32) * (X_HEAD_DIM ** -0.5)
    qb = q.astype(BF16)
    hd = X_HEADS * X_HEAD_DIM
    outs = []
    for hh in range(X_HEADS):
        qh = qb[:, hh * X_HEAD_DIM:(hh + 1) * X_HEAD_DIM]
        kh = kv_ref[:, hh * X_HEAD_DIM:(hh + 1) * X_HEAD_DIM]
        vh = kv_ref[:, hd + hh * X_HEAD_DIM:hd + (hh + 1) * X_HEAD_DIM]
        s = lax.dot_general(qh, kh, (((1,), (1,)), ((), ())), preferred_element_type=F32)
        m = jnp.max(s, axis=-1, keepdims=True)
        p = jnp.exp(s - m)
        l = jnp.sum(p, axis=-1, keepdims=True)
        oh = jnp.dot(p.astype(BF16), vh, preferred_element_type=F32) * (1.0 / l)
        outs.append(oh.astype(BF16))
    o = jnp.concatenate(outs, axis=1)
    out_ref[...] = xf + jnp.dot(o, wo_ref[...], preferred_element_type=F32)


def _xattn(x3, gain, wq, kv3, wo, tm):
    b, s, _ = x3.shape
    hd = X_HEADS * X_HEAD_DIM
    return pl.pallas_call(
        _xattn_kernel,
        out_shape=jax.ShapeDtypeStruct((b, s, D_MODEL), F32),
        grid=(b, s // tm),
        in_specs=[pl.BlockSpec((None, tm, D_MODEL), lambda bi, i: (bi, i, 0)),
                  pl.BlockSpec((1, D_MODEL), lambda bi, i: (0, 0)),
                  pl.BlockSpec((D_MODEL, hd), lambda bi, i: (0, 0)),
                  pl.BlockSpec((None, N_MEM, 2 * hd), lambda bi, i: (bi, 0, 0)),
                  pl.BlockSpec((hd, D_MODEL), lambda bi, i: (0, 0))],
        out_specs=pl.BlockSpec((None, tm, D_MODEL), lambda bi, i: (bi, i, 0)),
        compiler_params=_cparams(("parallel", "parallel")),
    )(x3, gain.reshape(1, D_MODEL), wq, kv3, wo)


def _mlp_kernel(x_ref, g_ref, w1_ref, w2_ref, fg_ref, out_ref, h_scr, acc_scr, *, final):
    f = pl.program_id(1)

    @pl.when(f == 0)
    def _():
        xf = x_ref[...]
        ms = jnp.mean(xf * xf, axis=-1, keepdims=True)
        h_scr[...] = (xf * lax.rsqrt(ms + NORM_EPS) * g_ref[...]).astype(BF16)
        acc_scr[...] = jnp.zeros(acc_scr.shape, F32)

    a = jnp.maximum(jnp.dot(h_scr[...], w1_ref[...], preferred_element_type=F32), 0.0)
    acc_scr[...] += jnp.dot((a * a).astype(BF16), w2_ref[...], preferred_element_type=F32)

    @pl.when(f == pl.num_programs(1) - 1)
    def _():
        y = x_ref[...] + acc_scr[...]
        if final:
            ms = jnp.mean(y * y, axis=-1, keepdims=True)
            y = y * lax.rsqrt(ms + NORM_EPS) * fg_ref[...]
        out_ref[...] = y


def _mlp(x, gain, w1, w2, final_gain, final, tm, tf):
    t = x.shape[0]
    return pl.pallas_call(
        functools.partial(_mlp_kernel, final=final),
        out_shape=jax.ShapeDtypeStruct((t, D_MODEL), F32),
        grid=(t // tm, D_FF // tf),
        in_specs=[pl.BlockSpec((tm, D_MODEL), lambda i, f: (i, 0)),
                  pl.BlockSpec((1, D_MODEL), lambda i, f: (0, 0)),
                  pl.BlockSpec((D_MODEL, tf), lambda i, f: (0, f)),
                  pl.BlockSpec((tf, D_MODEL), lambda i, f: (f, 0)),
                  pl.BlockSpec((1, D_MODEL), lambda i, f: (0, 0))],
        out_specs=pl.BlockSpec((tm, D_MODEL), lambda i, f: (i, 0)),
        scratch_shapes=[pltpu.VMEM((tm, D_MODEL), BF16), pltpu.VMEM((tm, D_MODEL), F32)],
        compiler_params=_cparams(("parallel", "arbitrary")),
    )(x, gain.reshape(1, D_MODEL), w1, w2, final_gain.reshape(1, D_MODEL))


_F_ROT16 = np.array(list(range(0, 8)) + list(range(16, 40)))
_P_ROT16 = np.array(list(range(8, 16)) + list(range(40, 64)))
_F_AXIAL = np.array(list(range(0, 16)) + list(range(32, 48)))
_P_AXIAL = np.array(list(range(16, 32)) + list(range(48, 64)))


def _block_dims(first, partner):
    lane = np.arange(BLK)
    half, slot, u = lane // LANES, (lane % LANES) // 32, lane % 32
    return np.where(half == 0, first[u], partner[u]), slot


def _slot_masks(nslot, slot_of_lane):
    return np.stack([(slot_of_lane == j) for j in range(nslot)]).astype(np.float32)


def _rope_tables(pos_list, theta, rot, scale_list, npad):
    half = rot // 2
    inv_freq = jnp.exp(jnp.arange(half, dtype=F32) * (-2.0 * math.log(theta) / rot))
    cs, ss = [], []
    for pos in pos_list:
        ang = pos.astype(F32)[:, None] * inv_freq[None, :]
        cs.append(jnp.cos(ang))
        ss.append(jnp.sin(ang))
    c = jnp.concatenate(cs, axis=1)
    s = jnp.concatenate(ss, axis=1)
    n = c.shape[0]
    if npad:
        c = jnp.concatenate([c, jnp.ones((n, npad), F32)], axis=1)
        s = jnp.concatenate([s, jnp.zeros((n, npad), F32)], axis=1)
    reps = LANES // c.shape[1]
    c = jnp.tile(c, (1, reps))
    s = jnp.tile(s, (1, reps))
    ctab = [c * sc for sc in scale_list] + [jnp.ones_like(c)]
    stab = [s * sc for sc in scale_list] + [jnp.zeros_like(s)]
    return jnp.stack(ctab), jnp.stack(stab)


def _mixer_a(x, b, s, gain, w_in, w_out, pos, tm):
    dims, slot = _block_dims(_F_ROT16, _P_ROT16)
    cols = []
    tt = []
    for wg in range(A_GROUPS):
        for which in range(3):
            base = (wg * 3 + which) * A_HEADS * 64
            for hg in range(2):
                if which < 2:
                    cols.append(base + hg * BLK + slot * 64 + dims)
                else:
                    cols.append(base + hg * BLK + np.arange(BLK))
            tt.append(which)
    cols = np.concatenate(cols)
    w = w_in[:, cols].astype(BF16)
    ctab, stab = _rope_tables([pos], ROPE_THETA, 16, [0.125, 1.0], 24)
    qkv = _proj(x, 0, D_MODEL, D_MODEL, gain, w, tn=2 * BLK, out_dtype=BF16, seq=s, tm=tm,
                tables=(ctab, stab), tt=np.array(tt, np.int32))
    qkv = qkv.reshape(b, s, A_IN)
    qmask = jnp.asarray(_slot_masks(4, slot), BF16)
    vmask = jnp.asarray(_slot_masks(4, np.arange(BLK) // 64), F32)
    outs, lses = [], []
    for wg, (window, dil) in enumerate(A_PATTERNS):
        assert window // (2 * dil) == A_HALF_WINDOW
        o, lse = _band_attention(qkv, wg, dil, qmask, vmask)
        outs.append(o.reshape(b * s, 2 * BLK))
        lses.append(lse.reshape(b * s, 2 * BLK))
    return _a_out(outs, lses, w_out.astype(BF16), x, tm)


def _mixer_b(x, b, s, gain, w_in, q_gain, k_gain, w_out, rows, cols_pos, tm, tq, tk):
    dims, slot = _block_dims(_F_AXIAL, _P_AXIAL)
    qcols = np.concatenate([g * BLK + slot * 64 + dims for g in range(B_KV_HEADS)])
    kcols = np.concatenate([B_HEADS * 64 + g * 64 + dims for g in range(B_KV_HEADS)])
    wqk = w_in[:, np.concatenate([qcols, kcols])].astype(BF16)
    wv = w_in[:, (B_HEADS + B_KV_HEADS) * 64:].astype(BF16)
    hgain = jnp.concatenate([jnp.tile(q_gain[dims] * (0.125 * LOG2E), B_KV_HEADS),
                             jnp.tile(k_gain[dims], B_KV_HEADS)]).reshape(1, -1).astype(F32)
    bd = jnp.asarray(slot[:, None] == slot[None, :], BF16)
    ctab, stab = _rope_tables([rows, cols_pos], AXIAL_THETA, 32, [1.0], 0)
    n = B_KV_HEADS * BLK
    qk = _proj(x, 0, D_MODEL, D_MODEL, gain, wqk, tn=n, out_dtype=BF16, seq=s, tm=tm,
               tables=(ctab, stab), tt=np.zeros((2,), np.int32), headnorm=(hgain, bd))
    v = _proj(x, 0, D_MODEL, D_MODEL, gain, wv, tn=B_KV_HEADS * 64, out_dtype=BF16, seq=s, tm=tm)
    qk = qk.reshape(b, s, 2 * n)
    vt = _transpose_values(v.reshape(b, s, B_KV_HEADS * 64), B_KV_HEADS, 1, 64, tk)
    qmask = jnp.asarray(_slot_masks(4, slot), BF16)
    o = _flash(qk, qk, vt, qcol0=0, kcol0=B_KV_HEADS, ngroups=B_KV_HEADS, wq=BLK,
               wv=64 + ONES_ROWS, wo=BLK, nh=4, vgroups=((0, 64, 0, 4),), qmask=qmask, tq=tq, tk=tk)
    return _matmul_residual(o.reshape(b * s, B_HEADS * 64), w_out.astype(BF16), x, tm)


def _mixer_c(x, b, s, gain, w_in, lq1, lk1, lq2, lk2, sub_gain, w_out, pos, lambda_init, tm, tq, tk):
    dims, slot = _block_dims(_F_ROT16, _P_ROT16)
    ngr = C_HEADS // 2
    cols = []
    for which in range(3):
        for g in range(ngr):
            base = which * C_HEADS * 128 + g * BLK
            cols.append(base + (slot * 64 + dims if which < 2 else np.arange(BLK)))
    w = w_in[:, np.concatenate(cols)].astype(BF16)
    ctab, stab = _rope_tables([pos], ROPE_THETA, 16, [0.125 * LOG2E, 1.0], 24)
    n = ngr * BLK
    qkv = _proj(x, 0, D_MODEL, D_MODEL, gain, w, tn=n, out_dtype=BF16, seq=s, tm=tm,
                tables=(ctab, stab), tt=np.array([0, 1, 2], np.int32))
    qkv = qkv.reshape(b, s, 3 * n)
    qmask = jnp.asarray(_slot_masks(4, slot), BF16)
    vt = _transpose_values(qkv[:, :, 2 * n:], ngr, 2, LANES, tk)
    o = _flash(qkv, qkv, vt, qcol0=0, kcol0=ngr, ngroups=ngr, wq=BLK, wv=2 * (LANES + ONES_ROWS),
               wo=BLK, nh=4, vgroups=((0, LANES, 0, 2), (LANES + ONES_ROWS, LANES, 2, 2)),
               qmask=qmask, tq=tq, tk=tk,
               diff=lambda_init, diff_params=(lq1, lk1, lq2, lk2, sub_gain))
    return _matmul_residual(o.reshape(b * s, C_HEADS * 128), w_out.astype(BF16), x, tm)


def _mixer_d(x, b, s, gain, w_in, q_gain, kv_gain, w_uq, w_ukv, w_out, pos, tm, tq, tk):
    lane = np.arange(LANES)
    slot_h = np.where(lane < 32, 0, np.where(lane < 64, 1, np.where(lane < 80, 0, np.where(lane < 96, 1, -1))))
    slot = np.concatenate([slot_h, slot_h])
    nope_lane = lane < 64
    rope_lane = (lane >= 64) & (lane < 96)
    ngr = D_HEADS // 2

    w1 = jnp.zeros((D_MODEL, 4 * BLK), F32)
    w1 = w1.at[:, :D_Q_RANK].set(w_in[:, :D_Q_RANK])
    w1 = w1.at[:, 2 * BLK:3 * BLK].set(w_in[:, D_Q_RANK:D_Q_RANK + D_KV_RANK])
    kr_src = np.zeros((BLK,), np.int64)
    kr_on = np.zeros((BLK,), bool)
    for hf in range(2):
        for l in range(LANES):
            if rope_lane[l]:
                kr_src[hf * LANES + l] = D_Q_RANK + D_KV_RANK + hf * 16 + (l - 64) % 16
                kr_on[hf * LANES + l] = True
    w1 = w1.at[:, 3 * BLK:].set(jnp.where(jnp.asarray(kr_on)[None, :], w_in[:, kr_src], 0.0))
    cmb = _proj(x, 0, D_MODEL, D_MODEL, gain, w1.astype(BF16), tn=4 * BLK, out_dtype=F32, seq=s, tm=tm)

    qsrc = np.zeros((ngr * BLK,), np.int64)
    qon = np.zeros((ngr * BLK,), bool)
    ksrc = np.zeros((ngr * BLK,), np.int64)
    kon = np.zeros((ngr * BLK,), bool)
    for g in range(ngr):
        for hf in range(2):
            for l in range(LANES):
                idx = g * BLK + hf * LANES + l
                if slot_h[l] < 0:
                    continue
                head = 2 * g + slot_h[l]
                if nope_lane[l]:
                    d = hf * 32 + l % 32
                    qsrc[idx], qon[idx] = head * 96 + d, True
                    ksrc[idx], kon[idx] = head * 128 + d, True
                else:
                    d = hf * 16 + (l - 64) % 16
                    qsrc[idx], qon[idx] = head * 96 + D_NOPE + d, True
    wq2 = jnp.where(jnp.asarray(qon)[None, :], w_uq[:, qsrc], 0.0)
    wq2 = jnp.concatenate([wq2, jnp.zeros((2 * BLK - D_Q_RANK, ngr * BLK), F32)], axis=0).astype(BF16)
    wk2 = jnp.where(jnp.asarray(kon)[None, :], w_ukv[:, ksrc], 0.0).astype(BF16)
    vsrc = np.concatenate([h * 128 + D_NOPE + np.arange(64) for h in range(D_HEADS)])
    wv2 = w_ukv[:, vsrc].astype(BF16)
    qg = jnp.concatenate([q_gain, jnp.zeros((2 * BLK - D_Q_RANK,), F32)])

    half = D_ROPE // 2
    inv_freq = jnp.exp(jnp.arange(half, dtype=F32) * (-2.0 * math.log(ROPE_THETA) / D_ROPE))
    ang = pos.astype(F32)[:, None] * inv_freq[None, :]
    ones64 = jnp.ones((s, 64), F32)
    pad32 = jnp.ones((s, 32), F32)
    c = jnp.concatenate([ones64, jnp.cos(ang), jnp.cos(ang), pad32], axis=1)
    sn = jnp.concatenate([0.0 * ones64, jnp.sin(ang), jnp.sin(ang), 0.0 * pad32], axis=1)
    qs = (D_NOPE + D_ROPE) ** -0.5 * LOG2E
    ctab = jnp.stack([c * qs, c])
    stab = jnp.stack([sn * qs, sn])

    q = _proj(cmb, 0, 2 * BLK, D_Q_RANK, qg, wq2, tn=4 * BLK, out_dtype=BF16, seq=s, tm=tm,
              tables=(ctab, stab), tt=np.zeros((ngr // 4,), np.int32))
    k = _proj(cmb, 2, BLK, D_KV_RANK, kv_gain, wk2, tn=BLK, out_dtype=BF16, seq=s, tm=tm,
              tables=(ctab, stab), tt=np.ones((ngr,), np.int32), add=(cmb, 3))
    v = _proj(cmb, 2, BLK, D_KV_RANK, kv_gain, wv2, tn=4 * BLK, out_dtype=BF16, seq=s, tm=tm)
    q = q.reshape(b, s, ngr * BLK)
    k = k.reshape(b, s, ngr * BLK)
    vt = _transpose_values(v.reshape(b, s, D_HEADS * 64), ngr, 2, 64, tk)
    qmask = jnp.asarray(_slot_masks(2, slot), BF16)
    o = _flash(q, k, vt, qcol0=0, kcol0=0, ngroups=ngr, wq=BLK, wv=2 * (64 + ONES_ROWS), wo=LANES,
               nh=2, vgroups=((0, 64, 0, 1), (64 + ONES_ROWS, 64, 1, 1)), qmask=qmask, tq=tq, tk=tk)
    return _matmul_residual(o.reshape(b * s, D_HEADS * 64), w_out.astype(BF16), x, tm)


def _run_trunk(x3, mem3, p, tm=512, tq=256, tk=512):
    b, s, _ = x3.shape
    t = b * s
    tm = min(tm, s)
    x = x3.reshape(t, D_MODEL)
    pos = jnp.arange(s, dtype=F32)
    rows = jnp.repeat(jnp.arange(s // GRID_W, dtype=F32), GRID_W)
    cols_pos = jnp.tile(jnp.arange(GRID_W, dtype=F32), s // GRID_W)
    memf = mem3.reshape(b * N_MEM, D_MODEL)
    for i in range(DEPTH):
        m, j = i % 4, i // 4
        g = p['norm_mix'][i]
        if m == 0:
            x = _mixer_a(x, b, s, g, p['a_w_in'][j], p['a_w_out'][j], pos, tm)
        elif m == 1:
            x = _mixer_b(x, b, s, g, p['b_w_in'][j], p['b_q_norm'][j], p['b_k_norm'][j],
                         p['b_w_out'][j], rows, cols_pos, tm, tq, tk)
        elif m == 2:
            x = _mixer_c(x, b, s, g, p['c_w_in'][j], p['c_lambda_q1'][j], p['c_lambda_k1'][j],
                         p['c_lambda_q2'][j], p['c_lambda_k2'][j], p['c_sub_norm'][j],
                         p['c_w_out'][j], pos, 0.8 - 0.6 * math.exp(-0.3 * i), tm, tq, tk)
        else:
            x = _mixer_d(x, b, s, g, p['d_w_in'][j], p['d_q_norm'][j], p['d_kv_norm'][j],
                         p['d_w_uq'][j], p['d_w_ukv'][j], p['d_w_out'][j], pos, tm, tq, tk)
        kv = _proj(memf, 0, D_MODEL, D_MODEL, p['norm_mem'][i], p['w_xkv'][i].astype(BF16),
                   tn=2 * X_HEADS * X_HEAD_DIM, out_dtype=BF16, seq=N_MEM, tm=N_MEM)
        x = _xattn(x.reshape(b, s, D_MODEL), p['norm_x'][i], p['w_xq'][i].astype(BF16),
                   kv.reshape(b, N_MEM, 2 * X_HEADS * X_HEAD_DIM), p['w_xo'][i].astype(BF16),
                   min(tm, 512)).reshape(t, D_MODEL)
        x = _mlp(x, p['norm_mlp'][i], p['w_mlp_in'][i].astype(BF16), p['w_mlp_out'][i].astype(BF16),
                 p['final_norm'], i == DEPTH - 1, tm, 1024)
    return x.reshape(b, s, D_MODEL)


def kernel(x_prompt, x_sample, mem_prompt, mem_sample, norm_mix, norm_x, norm_mem, w_xq, w_xkv, w_xo, norm_mlp, w_mlp_in, w_mlp_out, a_w_in, a_w_out, b_w_in, b_q_norm, b_k_norm, b_w_out, c_w_in, c_lambda_q1, c_lambda_k1, c_lambda_q2, c_lambda_k2, c_sub_norm, c_w_out, d_w_in, d_q_norm, d_kv_norm, d_w_uq, d_w_ukv, d_w_out, final_norm):
    p = dict(norm_mix=norm_mix, norm_x=norm_x, norm_mem=norm_mem, w_xq=w_xq, w_xkv=w_xkv,
             w_xo=w_xo, norm_mlp=norm_mlp, w_mlp_in=w_mlp_in, w_mlp_out=w_mlp_out,
             a_w_in=a_w_in, a_w_out=a_w_out, b_w_in=b_w_in, b_q_norm=b_q_norm,
             b_k_norm=b_k_norm, b_w_out=b_w_out, c_w_in=c_w_in, c_lambda_q1=c_lambda_q1,
             c_lambda_k1=c_lambda_k1, c_lambda_q2=c_lambda_q2, c_lambda_k2=c_lambda_k2,
             c_sub_norm=c_sub_norm, c_w_out=c_w_out, d_w_in=d_w_in, d_q_norm=d_q_norm,
             d_kv_norm=d_kv_norm, d_w_uq=d_w_uq, d_w_ukv=d_w_ukv, d_w_out=d_w_out,
             final_norm=final_norm)
    return (_run_trunk(x_prompt, mem_prompt, p), _run_trunk(x_sample, mem_sample, p))
```

```python
import functools
import math

import numpy as np
import jax
import jax.numpy as jnp
from jax import lax
from jax.experimental import pallas as pl
from jax.experimental.pallas import tpu as pltpu

F32 = jnp.float32
BF16 = jnp.bfloat16

D_MODEL = 1024
DEPTH = 4
N_MEM = 256
GRID_W = 64
D_FF = 4 * D_MODEL
NORM_EPS = 1e-6
ROPE_THETA = 500000.0
AXIAL_THETA = 10000.0
NEG_INF = -1e30

A_PATTERNS = ((128, 1), (512, 4), (2048, 16))
A_GROUPS = 3
A_HEADS = 8
A_IN = A_GROUPS * 3 * A_HEADS * 64
A_HALF_WINDOW = 64
B_HEADS = 16
B_KV_HEADS = 4
C_HEADS = 8
D_HEADS = 16
D_Q_RANK = 384
D_KV_RANK = 256
D_NOPE = 64
D_ROPE = 32
X_HEADS = 4
X_HEAD_DIM = 128

LOG2E = 1.4426950408889634
ONES_ROWS = 16
SCORE_SLOTS = 4
LANES = 128
BLK = 2 * LANES
VMEM_LIMIT = 56 * 1024 * 1024


def _cparams(sem):
    return pltpu.CompilerParams(dimension_semantics=sem, vmem_limit_bytes=VMEM_LIMIT)


def _proj_kernel(tt_ref, src_ref, g_ref, w_ref, *rest, dnorm, rope, headnorm, add, nsub):
    del tt_ref
    rest = list(rest)
    c_ref = s_ref = hg_ref = bd_ref = add_ref = None
    if rope:
        c_ref, s_ref = rest[0], rest[1]
        rest = rest[2:]
    if headnorm:
        hg_ref, bd_ref = rest[0], rest[1]
        rest = rest[2:]
    if add:
        add_ref = rest[0]
        rest = rest[1:]
    o_ref, h_scr = rest

    @pl.when(pl.program_id(1) == 0)
    def _():
        xf = src_ref[...].astype(F32)
        ms = jnp.sum(xf * xf, axis=-1, keepdims=True) * (1.0 / dnorm)
        h_scr[...] = (xf * lax.rsqrt(ms + NORM_EPS) * g_ref[...]).astype(BF16)

    y = jnp.dot(h_scr[...], w_ref[...], preferred_element_type=F32)
    if add:
        ad = add_ref[...]
        y = y + (ad if nsub == 1 else jnp.concatenate([ad] * nsub, axis=1))
    if headnorm:
        y2 = y * y
        hi = y2.astype(BF16)
        lo = (y2 - hi.astype(F32)).astype(BF16)
        parts = []
        for n in range(nsub):
            sl = slice(n * BLK, (n + 1) * BLK)
            ss = (jnp.dot(hi[:, sl], bd_ref[...], preferred_element_type=F32)
                  + jnp.dot(lo[:, sl], bd_ref[...], preferred_element_type=F32))
            parts.append(y[:, sl] * lax.rsqrt(ss * (1.0 / 64.0) + NORM_EPS))
        y = (parts[0] if nsub == 1 else jnp.concatenate(parts, axis=1)) * hg_ref[...]
    if rope:
        c = c_ref[0]
        s = s_ref[0]
        for n in range(nsub):
            y1 = y[:, n * BLK:n * BLK + LANES]
            y2 = y[:, n * BLK + LANES:(n + 1) * BLK]
            o_ref[:, n * BLK:n * BLK + LANES] = (y1 * c - y2 * s).astype(o_ref.dtype)
            o_ref[:, n * BLK + LANES:(n + 1) * BLK] = (y2 * c + y1 * s).astype(o_ref.dtype)
    else:
        o_ref[...] = y.astype(o_ref.dtype)


def _proj(src, src_cb, kdim, dnorm, gain, w, *, tn, out_dtype, seq, tm,
          tables=None, tt=None, headnorm=None, add=None):
    t = src.shape[0]
    n = w.shape[1]
    nj = n // tn
    nsub = tn // BLK if (tables is not None or headnorm is not None) else 1
    ns = seq // tm
    if tt is None:
        tt = np.zeros((nj,), np.int32)
    in_specs = [
        pl.BlockSpec((tm, kdim), lambda i, j, tt_: (i, src_cb)),
        pl.BlockSpec((1, kdim), lambda i, j, tt_: (0, 0)),
        pl.BlockSpec((kdim, tn), lambda i, j, tt_: (0, j)),
    ]
    args = [src, gain.reshape(1, kdim).astype(F32), w]
    if tables is not None:
        for tb in tables:
            in_specs.append(pl.BlockSpec((1, tm, LANES), lambda i, j, tt_: (tt_[j], i % ns, 0)))
            args.append(tb)
    if headnorm is not None:
        in_specs.append(pl.BlockSpec((1, tn), lambda i, j, tt_: (0, j)))
        in_specs.append(pl.BlockSpec((BLK, BLK), lambda i, j, tt_: (0, 0)))
        args += [headnorm[0], headnorm[1]]
    if add is not None:
        add_arr, add_cb = add
        in_specs.append(pl.BlockSpec((tm, BLK), lambda i, j, tt_: (i, add_cb)))
        args.append(add_arr)
    kern = functools.partial(_proj_kernel, dnorm=dnorm, rope=tables is not None,
                             headnorm=headnorm is not None, add=add is not None, nsub=nsub)
    return pl.pallas_call(
        kern,
        out_shape=jax.ShapeDtypeStruct((t, n), out_dtype),
        grid_spec=pltpu.PrefetchScalarGridSpec(
            num_scalar_prefetch=1,
            grid=(t // tm, nj),
            in_specs=in_specs,
            out_specs=pl.BlockSpec((tm, tn), lambda i, j, tt_: (i, j)),
            scratch_shapes=[pltpu.VMEM((tm, kdim), BF16)],
        ),
        compiler_params=_cparams(("parallel", "arbitrary")),
    )(jnp.asarray(tt, jnp.int32), *args)


def _flash_kernel(*refs, nh, vgroups, tq, tk, nk, diff):
    if diff:
        (qmask_ref, q_ref, k_ref, vt_ref, lq1, lk1, lq2, lk2, sg_ref,
         o_ref, qm_scr, s_scr, mx_scr, m_scr, acc_scr) = refs
        lambda_init = diff
    else:
        qmask_ref, q_ref, k_ref, vt_ref, o_ref, qm_scr, s_scr, mx_scr, m_scr, acc_scr = refs

    q = q_ref[...]
    for j in range(nh):
        qm_scr[j] = q * qmask_ref[j:j + 1, :]
    m_scr[...] = jnp.full(m_scr.shape, NEG_INF, F32)
    acc_scr[...] = jnp.zeros(acc_scr.shape, F32)

    def scores(c, slot):
        ks = pl.multiple_of(c * tk, tk)
        kc = k_ref[pl.ds(ks, tk), :]
        for j in range(nh):
            st = lax.dot_general(kc, qm_scr[j], (((1,), (1,)), ((), ())),
                                 preferred_element_type=F32)
            s_scr[slot, j] = st
            mx_scr[slot, j] = jnp.max(st, axis=0, keepdims=True)

    def accumulate(c, slot):
        vtc = vt_ref[c]
        pts, alphas = [], []
        for j in range(nh):
            m_prev = m_scr[j]
            m_new = jnp.maximum(m_prev, mx_scr[slot, j])
            alphas.append(jnp.exp2(m_prev - m_new))
            m_scr[j] = m_new
            pts.append(jnp.exp2(s_scr[slot, j] - m_new).astype(BF16))
        for gi, (r0, nr, h0, hn) in enumerate(vgroups):
            rhs = pts[h0] if hn == 1 else jnp.concatenate(pts[h0:h0 + hn], axis=1)
            a = alphas[h0] if hn == 1 else jnp.concatenate(alphas[h0:h0 + hn], axis=1)
            acc_scr[gi] = acc_scr[gi] * a + jnp.dot(vtc[r0:r0 + nr + ONES_ROWS, :], rhs,
                                                    preferred_element_type=F32)

    scores(0, 0)

    def body(i, carry):
        c = SCORE_SLOTS * i
        for u in range(SCORE_SLOTS):
            scores(jnp.minimum(c + u + 1, nk - 1), (u + 1) % SCORE_SLOTS)
            accumulate(c + u, u)
        return carry

    lax.fori_loop(0, nk // SCORE_SLOTS, body, 0)

    pieces = []
    if diff:
        lam = (jnp.exp(jnp.sum(lq1[...] * lk1[...], axis=-1, keepdims=True))
               - jnp.exp(jnp.sum(lq2[...] * lk2[...], axis=-1, keepdims=True)) + lambda_init)
        for gi, (r0, nr, h0, hn) in enumerate(vgroups):
            acc = acc_scr[gi]
            linv = 1.0 / acc[nr:nr + 1, :]
            oh = acc[:nr, :tq] * linv[:, :tq] - lam * (acc[:nr, tq:] * linv[:, tq:])
            ms = jnp.mean(oh * oh, axis=0, keepdims=True)
            pieces.append(oh * lax.rsqrt(ms + NORM_EPS))
    else:
        for gi, (r0, nr, h0, hn) in enumerate(vgroups):
            acc = acc_scr[gi]
            on = acc[:nr, :] * (1.0 / acc[nr:nr + 1, :])
            for jj in range(hn):
                pieces.append(on[:, jj * tq:(jj + 1) * tq])
    ot = pieces[0] if len(pieces) == 1 else jnp.concatenate(pieces, axis=0)
    o = ot.T
    if diff:
        o = o * sg_ref[...] * (1.0 - lambda_init)
    o_ref[...] = o.astype(o_ref.dtype)


def _flash(q, k, vt, *, qcol0, kcol0, ngroups, wq, wv, wo, nh, vgroups, qmask, tq, tk,
           diff=None, diff_params=None):
    b, s = q.shape[0], q.shape[1]
    nk = s // tk
    assert s % tk == 0 and nk % SCORE_SLOTS == 0 and s % tq == 0
    in_specs = [
        pl.BlockSpec((nh, wq), lambda bi, g, i: (0, 0)),
        pl.BlockSpec((None, tq, wq), lambda bi, g, i: (bi, i, qcol0 + g)),
        pl.BlockSpec((None, s, wq), lambda bi, g, i: (bi, 0, kcol0 + g)),
        pl.BlockSpec((None, None, nk, wv, tk), lambda bi, g, i: (bi, g, 0, 0, 0)),
    ]
    args = [qmask, q, k, vt]
    if diff is not None:
        for prm in diff_params[:4]:
            in_specs.append(pl.BlockSpec((1, 64), lambda bi, g, i: (0, 0)))
            args.append(prm.reshape(1, 64).astype(F32))
        in_specs.append(pl.BlockSpec((1, wo), lambda bi, g, i: (0, 0)))
        args.append(jnp.tile(diff_params[4].astype(F32), wo // LANES).reshape(1, wo))
    kern = functools.partial(_flash_kernel, nh=nh, vgroups=tuple(vgroups), tq=tq, tk=tk, nk=nk,
                             diff=diff)
    nr, hn = vgroups[0][1], vgroups[0][3]
    return pl.pallas_call(
        kern,
        out_shape=jax.ShapeDtypeStruct((b, s, ngroups * wo), BF16),
        grid=(b, ngroups, s // tq),
        in_specs=in_specs,
        out_specs=pl.BlockSpec((None, tq, wo), lambda bi, g, i: (bi, i, g)),
        scratch_shapes=[
            pltpu.VMEM((nh, tq, wq), BF16),
            pltpu.VMEM((SCORE_SLOTS, nh, tk, tq), F32),
            pltpu.VMEM((SCORE_SLOTS, nh, 1, tq), F32),
            pltpu.VMEM((nh, 1, tq), F32),
            pltpu.VMEM((len(vgroups), nr + ONES_ROWS, hn * tq), F32),
        ],
        compiler_params=_cparams(("parallel", "parallel", "arbitrary")),
    )(*args)


def _transpose_values(v, ngroups, nvg, nr, tk):
    b, s, _ = v.shape
    v = v.reshape(b, s, ngroups, nvg, nr)
    v = jnp.concatenate([v, jnp.ones((b, s, ngroups, nvg, ONES_ROWS), v.dtype)], axis=-1)
    v = v.reshape(b, s // tk, tk, ngroups, nvg * (nr + ONES_ROWS))
    return v.transpose(0, 3, 1, 4, 2)


def _band_kernel(qmask_ref, vmask_ref, q_ref, k_ref, v_ref, o_ref, lse_ref, *, tq, win, length):
    nh = 4
    i = pl.program_id(2)
    ks = jnp.clip(i * tq - A_HALF_WINDOW, 0, length - win)
    ks = pl.multiple_of(ks, A_HALF_WINDOW)
    kc = k_ref[pl.ds(ks, win), :]
    vc = v_ref[pl.ds(ks, win), :]
    q = q_ref[...]
    qm = jnp.concatenate([q * qmask_ref[j:j + 1, :] for j in range(nh)], axis=0)
    s = lax.dot_general(qm, kc, (((1,), (1,)), ((), ())), preferred_element_type=F32)
    qpos = i * tq + lax.broadcasted_iota(jnp.int32, (tq, win), 0)
    kpos = ks + lax.broadcasted_iota(jnp.int32, (tq, win), 1)
    valid = jnp.abs(qpos - kpos) <= A_HALF_WINDOW
    bias = jnp.where(valid, 0.0, NEG_INF).astype(F32)
    vmask = vmask_ref[...]
    vmask_b = vmask.astype(BF16)
    ps = []
    inv = None
    lse = None
    for j in range(nh):
        sj = jnp.where(valid, s[j * tq:(j + 1) * tq], bias)
        mj = jnp.max(sj, axis=-1, keepdims=True)
        pj = jnp.exp(sj - mj)
        lj = jnp.sum(pj, axis=-1, keepdims=True)
        ps.append(pj.astype(BF16))
        t_inv = (1.0 / lj) * vmask[j:j + 1, :]
        t_lse = (mj + jnp.log(lj)) * vmask[j:j + 1, :]
        inv = t_inv if inv is None else inv + t_inv
        lse = t_lse if lse is None else lse + t_lse
    lhs = jnp.concatenate(ps, axis=1)
    rhs = jnp.concatenate([vc * vmask_b[j:j + 1, :] for j in range(nh)], axis=0)
    pv = jnp.dot(lhs, rhs, preferred_element_type=F32)
    o_ref[...] = (pv * inv).astype(o_ref.dtype)
    lse_ref[...] = lse


def _band_attention(qkv, dil, qmask, vmask):
    b, s, ncol = qkv.shape
    length = s // dil
    tq = min(256, length // 2)
    win = tq + 2 * A_HALF_WINDOW
    nblk = ncol // BLK
    view = qkv.reshape(b, length, dil * ncol)

    def col(which):
        return lambda bi, a, i: (bi, 0, (a // 2) * nblk + which * 2 + a % 2)

    in_specs = [
        pl.BlockSpec((4, BLK), lambda bi, a, i: (0, 0)),
        pl.BlockSpec((4, BLK), lambda bi, a, i: (0, 0)),
        pl.BlockSpec((None, tq, BLK), lambda bi, a, i: (bi, i, (a // 2) * nblk + a % 2)),
        pl.BlockSpec((None, length, BLK), col(1)),
        pl.BlockSpec((None, length, BLK), col(2)),
    ]
    kern = functools.partial(_band_kernel, tq=tq, win=win, length=length)
    o, lse = pl.pallas_call(
        kern,
        out_shape=(jax.ShapeDtypeStruct((b, length, dil * 2 * BLK), BF16),
                   jax.ShapeDtypeStruct((b, length, dil * 2 * BLK), F32)),
        grid=(b, dil * 2, length // tq),
        in_specs=in_specs,
        out_specs=(pl.BlockSpec((None, tq, BLK), lambda bi, a, i: (bi, i, a)),
                   pl.BlockSpec((None, tq, BLK), lambda bi, a, i: (bi, i, a))),
        compiler_params=_cparams(("parallel", "parallel", "arbitrary")),
    )(qmask, vmask, view, view, view)
    return o.reshape(b, s, 2 * BLK), lse.reshape(b, s, 2 * BLK)


def _a_out_kernel(o0, o1, o2, l0, l1, l2, w_ref, x_ref, out_ref):
    a0, a1, a2 = l0[...], l1[...], l2[...]
    mx = jnp.maximum(jnp.maximum(a0, a1), a2)
    e0, e1, e2 = jnp.exp(a0 - mx), jnp.exp(a1 - mx), jnp.exp(a2 - mx)
    inv = 1.0 / (e0 + e1 + e2)
    o = (e0 * o0[...].astype(F32) + e1 * o1[...].astype(F32) + e2 * o2[...].astype(F32)) * inv
    out_ref[...] = x_ref[...] + jnp.dot(o.astype(BF16), w_ref[...], preferred_element_type=F32)


def _a_out(os_, lses, w, x, tm):
    t = x.shape[0]
    kd = w.shape[0]
    row = lambda i: (i, 0)
    return pl.pallas_call(
        _a_out_kernel,
        out_shape=jax.ShapeDtypeStruct((t, D_MODEL), F32),
        grid=(t // tm,),
        in_specs=[pl.BlockSpec((tm, kd), row)] * 6
        + [pl.BlockSpec((kd, D_MODEL), lambda i: (0, 0)), pl.BlockSpec((tm, D_MODEL), row)],
        out_specs=pl.BlockSpec((tm, D_MODEL), row),
        compiler_params=_cparams(("parallel",)),
    )(*os_, *lses, w, x)


def _matres_kernel(o_ref, w_ref, x_ref, out_ref):
    out_ref[...] = x_ref[...] + jnp.dot(o_ref[...], w_ref[...], preferred_element_type=F32)


def _matmul_residual(o, w, x, tm):
    t = x.shape[0]
    kd = w.shape[0]
    return pl.pallas_call(
        _matres_kernel,
        out_shape=jax.ShapeDtypeStruct((t, D_MODEL), F32),
        grid=(t // tm,),
        in_specs=[pl.BlockSpec((tm, kd), lambda i: (i, 0)),
                  pl.BlockSpec((kd, D_MODEL), lambda i: (0, 0)),
                  pl.BlockSpec((tm, D_MODEL), lambda i: (i, 0))],
        out_specs=pl.BlockSpec((tm, D_MODEL), lambda i: (i, 0)),
        compiler_params=_cparams(("parallel",)),
    )(o, w, x)


def _xattn_kernel(x_ref, g_ref, wq_ref, kv_ref, wo_ref, out_ref):
    xf = x_ref[...]
    ms = jnp.mean(xf * xf, axis=-1, keepdims=True)
    h = (xf * lax.rsqrt(ms + NORM_EPS) * g_ref[...]).astype(BF16)
    q = jnp.dot(h, wq_ref[...], preferred_element_type=F32) * (X_HEAD_DIM ** -0.5)
    qb = q.astype(BF16)
    hd = X_HEADS * X_HEAD_DIM
    outs = []
    for hh in range(X_HEADS):
        qh = qb[:, hh * X_HEAD_DIM:(hh + 1) * X_HEAD_DIM]
        kh = kv_ref[:, hh * X_HEAD_DIM:(hh + 1) * X_HEAD_DIM]
        vh = kv_ref[:, hd + hh * X_HEAD_DIM:hd + (hh + 1) * X_HEAD_DIM]
        s = lax.dot_general(qh, kh, (((1,), (1,)), ((), ())), preferred_element_type=F32)
        m = jnp.max(s, axis=-1, keepdims=True)
        p = jnp.exp(s - m)
        l = jnp.sum(p, axis=-1, keepdims=True)
        oh = jnp.dot(p.astype(BF16), vh, preferred_element_type=F32) * (1.0 / l)
        outs.append(oh.astype(BF16))
    o = jnp.concatenate(outs, axis=1)
    out_ref[...] = xf + jnp.dot(o, wo_ref[...], preferred_element_type=F32)


def _xattn(x3, gain, wq, kv3, wo, tm):
    b, s, _ = x3.shape
    hd = X_HEADS * X_HEAD_DIM
    return pl.pallas_call(
        _xattn_kernel,
        out_shape=jax.ShapeDtypeStruct((b, s, D_MODEL), F32),
        grid=(b, s // tm),
        in_specs=[pl.BlockSpec((None, tm, D_MODEL), lambda bi, i: (bi, i, 0)),
                  pl.BlockSpec((1, D_MODEL), lambda bi, i: (0, 0)),
                  pl.BlockSpec((D_MODEL, hd), lambda bi, i: (0, 0)),
                  pl.BlockSpec((None, N_MEM, 2 * hd), lambda bi, i: (bi, 0, 0)),
                  pl.BlockSpec((hd, D_MODEL), lambda bi, i: (0, 0))],
        out_specs=pl.BlockSpec((None, tm, D_MODEL), lambda bi, i: (bi, i, 0)),
        compiler_params=_cparams(("parallel", "parallel")),
    )(x3, gain.reshape(1, D_MODEL), wq, kv3, wo)


def _mlp_kernel(x_ref, g_ref, w1_ref, w2_ref, fg_ref, out_ref, h_scr, acc_scr, *, final):
    f = pl.program_id(1)

    @pl.when(f == 0)
    def _():
        xf = x_ref[...]
        ms = jnp.mean(xf * xf, axis=-1, keepdims=True)
        h_scr[...] = (xf * lax.rsqrt(ms + NORM_EPS) * g_ref[...]).astype(BF16)
        acc_scr[...] = jnp.zeros(acc_scr.shape, F32)

    a = jnp.maximum(jnp.dot(h_scr[...], w1_ref[...], preferred_element_type=F32), 0.0)
    acc_scr[...] += jnp.dot((a * a).astype(BF16), w2_ref[...], preferred_element_type=F32)

    @pl.when(f == pl.num_programs(1) - 1)
    def _():
        y = x_ref[...] + acc_scr[...]
        if final:
            ms = jnp.mean(y * y, axis=-1, keepdims=True)
            y = y * lax.rsqrt(ms + NORM_EPS) * fg_ref[...]
        out_ref[...] = y


def _mlp(x, gain, w1, w2, final_gain, final, tm, tf):
    t = x.shape[0]
    return pl.pallas_call(
        functools.partial(_mlp_kernel, final=final),
        out_shape=jax.ShapeDtypeStruct((t, D_MODEL), F32),
        grid=(t // tm, D_FF // tf),
        in_specs=[pl.BlockSpec((tm, D_MODEL), lambda i, f: (i, 0)),
                  pl.BlockSpec((1, D_MODEL), lambda i, f: (0, 0)),
                  pl.BlockSpec((D_MODEL, tf), lambda i, f: (0, f)),
                  pl.BlockSpec((tf, D_MODEL), lambda i, f: (f, 0)),
                  pl.BlockSpec((1, D_MODEL), lambda i, f: (0, 0))],
        out_specs=pl.BlockSpec((tm, D_MODEL), lambda i, f: (i, 0)),
        scratch_shapes=[pltpu.VMEM((tm, D_MODEL), BF16), pltpu.VMEM((tm, D_MODEL), F32)],
        compiler_params=_cparams(("parallel", "arbitrary")),
    )(x, gain.reshape(1, D_MODEL), w1, w2, final_gain.reshape(1, D_MODEL))


_F_ROT16 = np.array(list(range(0, 8)) + list(range(16, 40)))
_P_ROT16 = np.array(list(range(8, 16)) + list(range(40, 64)))
_F_AXIAL = np.array(list(range(0, 16)) + list(range(32, 48)))
_P_AXIAL = np.array(list(range(16, 32)) + list(range(48, 64)))


def _block_dims(first, partner):
    lane = np.arange(BLK)
    half, slot, u = lane // LANES, (lane % LANES) // 32, lane % 32
    return np.where(half == 0, first[u], partner[u]), slot


def _slot_masks(nslot, slot_of_lane):
    return np.stack([(slot_of_lane == j) for j in range(nslot)]).astype(np.float32)


def _rope_tables(pos_list, theta, rot, scale_list, npad):
    half = rot // 2
    inv_freq = jnp.exp(jnp.arange(half, dtype=F32) * (-2.0 * math.log(theta) / rot))
    cs, ss = [], []
    for pos in pos_list:
        ang = pos.astype(F32)[:, None] * inv_freq[None, :]
        cs.append(jnp.cos(ang))
        ss.append(jnp.sin(ang))
    c = jnp.concatenate(cs, axis=1)
    s = jnp.concatenate(ss, axis=1)
    n = c.shape[0]
    if npad:
        c = jnp.concatenate([c, jnp.ones((n, npad), F32)], axis=1)
        s = jnp.concatenate([s, jnp.zeros((n, npad), F32)], axis=1)
    reps = LANES // c.shape[1]
    c = jnp.tile(c, (1, reps))
    s = jnp.tile(s, (1, reps))
    ctab = [c * sc for sc in scale_list] + [jnp.ones_like(c)]
    stab = [s * sc for sc in scale_list] + [jnp.zeros_like(s)]
    return jnp.stack(ctab), jnp.stack(stab)


def _mixer_a(x, b, s, gain, w_in, w_out, pos, tm):
    dims, slot = _block_dims(_F_ROT16, _P_ROT16)
    ctab, stab = _rope_tables([pos], ROPE_THETA, 16, [0.125, 1.0], 24)
    qmask = jnp.asarray(_slot_masks(4, slot), BF16)
    vmask = jnp.asarray(_slot_masks(4, np.arange(BLK) // 64), F32)
    outs, lses = [], []
    for wg, (window, dil) in enumerate(A_PATTERNS):
        assert window // (2 * dil) == A_HALF_WINDOW
        cols = []
        for which in range(3):
            base = (wg * 3 + which) * A_HEADS * 64
            for hg in range(2):
                cols.append(base + hg * BLK + (slot * 64 + dims if which < 2 else np.arange(BLK)))
        w = w_in[:, np.concatenate(cols)].astype(BF16)
        qkv = _proj(x, 0, D_MODEL, D_MODEL, gain, w, tn=2 * BLK, out_dtype=BF16, seq=s, tm=tm,
                    tables=(ctab, stab), tt=np.array([0, 1, 2], np.int32))
        o, lse = _band_attention(qkv.reshape(b, s, 3 * 2 * BLK), dil, qmask, vmask)
        outs.append(o.reshape(b * s, 2 * BLK))
        lses.append(lse.reshape(b * s, 2 * BLK))
    return _a_out(outs, lses, w_out.astype(BF16), x, min(tm, 512))


def _mixer_b(x, b, s, gain, w_in, q_gain, k_gain, w_out, rows, cols_pos, tm, tq, tk):
    dims, slot = _block_dims(_F_AXIAL, _P_AXIAL)
    qcols = np.concatenate([g * BLK + slot * 64 + dims for g in range(B_KV_HEADS)])
    kcols = np.concatenate([B_HEADS * 64 + g * 64 + dims for g in range(B_KV_HEADS)])
    wqk = w_in[:, np.concatenate([qcols, kcols])].astype(BF16)
    wv = w_in[:, (B_HEADS + B_KV_HEADS) * 64:].astype(BF16)
    hgain = jnp.concatenate([jnp.tile(q_gain[dims] * (0.125 * LOG2E), B_KV_HEADS),
                             jnp.tile(k_gain[dims], B_KV_HEADS)]).reshape(1, -1).astype(F32)
    bd = jnp.asarray(slot[:, None] == slot[None, :], BF16)
    ctab, stab = _rope_tables([rows, cols_pos], AXIAL_THETA, 32, [1.0], 0)
    n = B_KV_HEADS * BLK
    qk = _proj(x, 0, D_MODEL, D_MODEL, gain, wqk, tn=2 * BLK, out_dtype=BF16, seq=s, tm=tm,
               tables=(ctab, stab), tt=np.zeros((n // BLK,), np.int32), headnorm=(hgain, bd))
    v = _proj(x, 0, D_MODEL, D_MODEL, gain, wv, tn=B_KV_HEADS * 64, out_dtype=BF16, seq=s, tm=tm)
    qk = qk.reshape(b, s, 2 * n)
    vt = _transpose_values(v.reshape(b, s, B_KV_HEADS * 64), B_KV_HEADS, 1, 64, tk)
    qmask = jnp.asarray(_slot_masks(4, slot), BF16)
    o = _flash(qk, qk, vt, qcol0=0, kcol0=B_KV_HEADS, ngroups=B_KV_HEADS, wq=BLK,
               wv=64 + ONES_ROWS, wo=BLK, nh=4, vgroups=((0, 64, 0, 4),), qmask=qmask, tq=tq, tk=tk)
    return _matmul_residual(o.reshape(b * s, B_HEADS * 64), w_out.astype(BF16), x, tm)


def _mixer_c(x, b, s, gain, w_in, lq1, lk1, lq2, lk2, sub_gain, w_out, pos, lambda_init, tm, tq, tk):
    dims, slot = _block_dims(_F_ROT16, _P_ROT16)
    ngr = C_HEADS // 2
    cols = []
    for which in range(3):
        for g in range(ngr):
            base = which * C_HEADS * 128 + g * BLK
            cols.append(base + (slot * 64 + dims if which < 2 else np.arange(BLK)))
    w = w_in[:, np.concatenate(cols)].astype(BF16)
    ctab, stab = _rope_tables([pos], ROPE_THETA, 16, [0.125 * LOG2E, 1.0], 24)
    n = ngr * BLK
    qkv = _proj(x, 0, D_MODEL, D_MODEL, gain, w, tn=2 * BLK, out_dtype=BF16, seq=s, tm=tm,
                tables=(ctab, stab), tt=np.array([0, 0, 1, 1, 2, 2], np.int32))
    qkv = qkv.reshape(b, s, 3 * n)
    qmask = jnp.asarray(_slot_masks(4, slot), BF16)
    vt = _transpose_values(qkv[:, :, 2 * n:], ngr, 2, LANES, tk)
    o = _flash(qkv, qkv, vt, qcol0=0, kcol0=ngr, ngroups=ngr, wq=BLK, wv=2 * (LANES + ONES_ROWS),
               wo=BLK, nh=4, vgroups=((0, LANES, 0, 2), (LANES + ONES_ROWS, LANES, 2, 2)),
               qmask=qmask, tq=tq, tk=tk,
               diff=lambda_init, diff_params=(lq1, lk1, lq2, lk2, sub_gain))
    return _matmul_residual(o.reshape(b * s, C_HEADS * 128), w_out.astype(BF16), x, tm)


def _mixer_d(x, b, s, gain, w_in, q_gain, kv_gain, w_uq, w_ukv, w_out, pos, tm, tq, tk):
    lane = np.arange(LANES)
    slot_h = np.where(lane < 32, 0, np.where(lane < 64, 1, np.where(lane < 80, 0, np.where(lane < 96, 1, -1))))
    slot = np.concatenate([slot_h, slot_h])
    nope_lane = lane < 64
    rope_lane = (lane >= 64) & (lane < 96)
    ngr = D_HEADS // 2

    w1 = jnp.zeros((D_MODEL, 4 * BLK), F32)
    w1 = w1.at[:, :D_Q_RANK].set(w_in[:, :D_Q_RANK])
    w1 = w1.at[:, 2 * BLK:3 * BLK].set(w_in[:, D_Q_RANK:D_Q_RANK + D_KV_RANK])
    kr_src = np.zeros((BLK,), np.int64)
    kr_on = np.zeros((BLK,), bool)
    for hf in range(2):
        for l in range(LANES):
            if rope_lane[l]:
                kr_src[hf * LANES + l] = D_Q_RANK + D_KV_RANK + hf * 16 + (l - 64) % 16
                kr_on[hf * LANES + l] = True
    w1 = w1.at[:, 3 * BLK:].set(jnp.where(jnp.asarray(kr_on)[None, :], w_in[:, kr_src], 0.0))
    cmb = _proj(x, 0, D_MODEL, D_MODEL, gain, w1.astype(BF16), tn=4 * BLK, out_dtype=F32, seq=s, tm=tm)

    qsrc = np.zeros((ngr * BLK,), np.int64)
    qon = np.zeros((ngr * BLK,), bool)
    ksrc = np.zeros((ngr * BLK,), np.int64)
    kon = np.zeros((ngr * BLK,), bool)
    for g in range(ngr):
        for hf in range(2):
            for l in range(LANES):
                idx = g * BLK + hf * LANES + l
                if slot_h[l] < 0:
                    continue
                head = 2 * g + slot_h[l]
                if nope_lane[l]:
                    d = hf * 32 + l % 32
                    qsrc[idx], qon[idx] = head * 96 + d, True
                    ksrc[idx], kon[idx] = head * 128 + d, True
                else:
                    d = hf * 16 + (l - 64) % 16
                    qsrc[idx], qon[idx] = head * 96 + D_NOPE + d, True
    wq2 = jnp.where(jnp.asarray(qon)[None, :], w_uq[:, qsrc], 0.0)
    wq2 = jnp.concatenate([wq2, jnp.zeros((2 * BLK - D_Q_RANK, ngr * BLK), F32)], axis=0).astype(BF16)
    wk2 = jnp.where(jnp.asarray(kon)[None, :], w_ukv[:, ksrc], 0.0).astype(BF16)
    vsrc = np.concatenate([h * 128 + D_NOPE + np.arange(64) for h in range(D_HEADS)])
    wv2 = w_ukv[:, vsrc].astype(BF16)
    qg = jnp.concatenate([q_gain, jnp.zeros((2 * BLK - D_Q_RANK,), F32)])

    half = D_ROPE // 2
    inv_freq = jnp.exp(jnp.arange(half, dtype=F32) * (-2.0 * math.log(ROPE_THETA) / D_ROPE))
    ang = pos.astype(F32)[:, None] * inv_freq[None, :]
    ones64 = jnp.ones((s, 64), F32)
    pad32 = jnp.ones((s, 32), F32)
    c = jnp.concatenate([ones64, jnp.cos(ang), jnp.cos(ang), pad32], axis=1)
    sn = jnp.concatenate([0.0 * ones64, jnp.sin(ang), jnp.sin(ang), 0.0 * pad32], axis=1)
    qs = (D_NOPE + D_ROPE) ** -0.5 * LOG2E
    ctab = jnp.stack([c * qs, c])
    stab = jnp.stack([sn * qs, sn])

    q = _proj(cmb, 0, 2 * BLK, D_Q_RANK, qg, wq2, tn=2 * BLK, out_dtype=BF16, seq=s, tm=tm,
              tables=(ctab, stab), tt=np.zeros((ngr // 2,), np.int32))
    k = _proj(cmb, 2, BLK, D_KV_RANK, kv_gain, wk2, tn=4 * BLK, out_dtype=BF16, seq=s, tm=tm,
              tables=(ctab, stab), tt=np.ones((ngr // 4,), np.int32), add=(cmb, 3))
    v = _proj(cmb, 2, BLK, D_KV_RANK, kv_gain, wv2, tn=4 * BLK, out_dtype=BF16, seq=s, tm=tm)
    q = q.reshape(b, s, ngr * BLK)
    k = k.reshape(b, s, ngr * BLK)
    vt = _transpose_values(v.reshape(b, s, D_HEADS * 64), ngr, 2, 64, tk)
    qmask = jnp.asarray(_slot_masks(2, slot), BF16)
    o = _flash(q, k, vt, qcol0=0, kcol0=0, ngroups=ngr, wq=BLK, wv=2 * (64 + ONES_ROWS), wo=LANES,
               nh=2, vgroups=((0, 64, 0, 1), (64 + ONES_ROWS, 64, 1, 1)), qmask=qmask, tq=tq, tk=tk)
    return _matmul_residual(o.reshape(b * s, D_HEADS * 64), w_out.astype(BF16), x, tm)


def _run_trunk(x3, mem3, p):
    b, s, _ = x3.shape
    t = b * s
    tm = 1024
    tm_mlp = 512
    tq, tq_mla, tk = 256, 512, 512
    x = x3.reshape(t, D_MODEL)
    pos = jnp.arange(s, dtype=F32)
    rows = jnp.repeat(jnp.arange(s // GRID_W, dtype=F32), GRID_W)
    cols_pos = jnp.tile(jnp.arange(GRID_W, dtype=F32), s // GRID_W)
    memf = mem3.reshape(b * N_MEM, D_MODEL)
    for i in range(DEPTH):
        m, j = i % 4, i // 4
        g = p['norm_mix'][i]
        if m == 0:
            x = _mixer_a(x, b, s, g, p['a_w_in'][j], p['a_w_out'][j], pos, tm)
        elif m == 1:
            x = _mixer_b(x, b, s, g, p['b_w_in'][j], p['b_q_norm'][j], p['b_k_norm'][j],
                         p['b_w_out'][j], rows, cols_pos, tm, tq, tk)
        elif m == 2:
            x = _mixer_c(x, b, s, g, p['c_w_in'][j], p['c_lambda_q1'][j], p['c_lambda_k1'][j],
                         p['c_lambda_q2'][j], p['c_lambda_k2'][j], p['c_sub_norm'][j],
                         p['c_w_out'][j], pos, 0.8 - 0.6 * math.exp(-0.3 * i), tm, tq, tk)
        else:
            x = _mixer_d(x, b, s, g, p['d_w_in'][j], p['d_q_norm'][j], p['d_kv_norm'][j],
                         p['d_w_uq'][j], p['d_w_ukv'][j], p['d_w_out'][j], pos, tm, tq_mla, tk)
        kv = _proj(memf, 0, D_MODEL, D_MODEL, p['norm_mem'][i], p['w_xkv'][i].astype(BF16),
                   tn=2 * X_HEADS * X_HEAD_DIM, out_dtype=BF16, seq=N_MEM, tm=N_MEM)
        x = _xattn(x.reshape(b, s, D_MODEL), p['norm_x'][i], p['w_xq'][i].astype(BF16),
                   kv.reshape(b, N_MEM, 2 * X_HEADS * X_HEAD_DIM), p['w_xo'][i].astype(BF16),
                   tm_mlp).reshape(t, D_MODEL)
        x = _mlp(x, p['norm_mlp'][i], p['w_mlp_in'][i].astype(BF16), p['w_mlp_out'][i].astype(BF16),
                 p['final_norm'], i == DEPTH - 1, tm_mlp, 1024)
    return x.reshape(b, s, D_MODEL)


def kernel(x_prompt, x_sample, mem_prompt, mem_sample, norm_mix, norm_x, norm_mem, w_xq, w_xkv, w_xo, norm_mlp, w_mlp_in, w_mlp_out, a_w_in, a_w_out, b_w_in, b_q_norm, b_k_norm, b_w_out, c_w_in, c_lambda_q1, c_lambda_k1, c_lambda_q2, c_lambda_k2, c_sub_norm, c_w_out, d_w_in, d_q_norm, d_kv_norm, d_w_uq, d_w_ukv, d_w_out, final_norm):
    p = dict(norm_mix=norm_mix, norm_x=norm_x, norm_mem=norm_mem, w_xq=w_xq, w_xkv=w_xkv,
             w_xo=w_xo, norm_mlp=norm_mlp, w_mlp_in=w_mlp_in, w_mlp_out=w_mlp_out,
             a_w_in=a_w_in, a_w_out=a_w_out, b_w_in=b_w_in, b_q_norm=b_q_norm,
             b_k_norm=b_k_norm, b_w_out=b_w_out, c_w_in=c_w_in, c_lambda_q1=c_lambda_q1,
             c_lambda_k1=c_lambda_k1, c_lambda_q2=c_lambda_q2, c_lambda_k2=c_lambda_k2,
             c_sub_norm=c_sub_norm, c_w_out=c_w_out, d_w_in=d_w_in, d_q_norm=d_q_norm,
             d_kv_norm=d_kv_norm, d_w_uq=d_w_uq, d_w_ukv=d_w_ukv, d_w_out=d_w_out,
             final_norm=final_norm)
    return (_run_trunk(x_prompt, mem_prompt, p), _run_trunk(x_sample, mem_sample, p))
```

```python
import functools
import math

import numpy as np
import jax
import jax.numpy as jnp
from jax import lax
from jax.experimental import pallas as pl
from jax.experimental.pallas import tpu as pltpu

F32 = jnp.float32
BF16 = jnp.bfloat16

D_MODEL = 1024
DEPTH = 4
N_MEM = 256
GRID_W = 64
D_FF = 4 * D_MODEL
NORM_EPS = 1e-6
ROPE_THETA = 500000.0
AXIAL_THETA = 10000.0
NEG_INF = -1e30

A_PATTERNS = ((128, 1), (512, 4), (2048, 16))
A_GROUPS = 3
A_HEADS = 8
A_IN = A_GROUPS * 3 * A_HEADS * 64
A_HALF_WINDOW = 64
B_HEADS = 16
B_KV_HEADS = 4
C_HEADS = 8
D_HEADS = 16
D_Q_RANK = 384
D_KV_RANK = 256
D_NOPE = 64
D_ROPE = 32
X_HEADS = 4
X_HEAD_DIM = 128

LOG2E = 1.4426950408889634
ONES_ROWS = 16
SCORE_SLOTS = 4
LANES = 128
BLK = 2 * LANES
VMEM_LIMIT = 56 * 1024 * 1024


def _cparams(sem):
    return pltpu.CompilerParams(dimension_semantics=sem, vmem_limit_bytes=VMEM_LIMIT)


def _proj_kernel(tt_ref, src_ref, g_ref, w_ref, *rest, dnorm, rope, headnorm, add, nsub):
    del tt_ref
    rest = list(rest)
    c_ref = s_ref = hg_ref = bd_ref = add_ref = None
    if rope:
        c_ref, s_ref = rest[0], rest[1]
        rest = rest[2:]
    if headnorm:
        hg_ref, bd_ref = rest[0], rest[1]
        rest = rest[2:]
    if add:
        add_ref = rest[0]
        rest = rest[1:]
    o_ref, h_scr = rest

    @pl.when(pl.program_id(1) == 0)
    def _():
        xf = src_ref[...].astype(F32)
        ms = jnp.sum(xf * xf, axis=-1, keepdims=True) * (1.0 / dnorm)
        h_scr[...] = (xf * lax.rsqrt(ms + NORM_EPS) * g_ref[...]).astype(BF16)

    y = jnp.dot(h_scr[...], w_ref[...], preferred_element_type=F32)
    if add:
        ad = add_ref[...]
        y = y + (ad if nsub == 1 else jnp.concatenate([ad] * nsub, axis=1))
    if headnorm:
        y2 = y * y
        hi = y2.astype(BF16)
        lo = (y2 - hi.astype(F32)).astype(BF16)
        parts = []
        for n in range(nsub):
            sl = slice(n * BLK, (n + 1) * BLK)
            ss = (jnp.dot(hi[:, sl], bd_ref[...], preferred_element_type=F32)
                  + jnp.dot(lo[:, sl], bd_ref[...], preferred_element_type=F32))
            parts.append(y[:, sl] * lax.rsqrt(ss * (1.0 / 64.0) + NORM_EPS))
        y = (parts[0] if nsub == 1 else jnp.concatenate(parts, axis=1)) * hg_ref[...]
    if rope:
        c = c_ref[0]
        s = s_ref[0]
        for n in range(nsub):
            y1 = y[:, n * BLK:n * BLK + LANES]
            y2 = y[:, n * BLK + LANES:(n + 1) * BLK]
            o_ref[:, n * BLK:n * BLK + LANES] = (y1 * c - y2 * s).astype(o_ref.dtype)
            o_ref[:, n * BLK + LANES:(n + 1) * BLK] = (y2 * c + y1 * s).astype(o_ref.dtype)
    else:
        o_ref[...] = y.astype(o_ref.dtype)


def _proj(src, src_cb, kdim, dnorm, gain, w, *, tn, out_dtype, seq, tm,
          tables=None, tt=None, headnorm=None, add=None):
    t = src.shape[0]
    n = w.shape[1]
    nj = n // tn
    nsub = tn // BLK if (tables is not None or headnorm is not None) else 1
    ns = seq // tm
    if tt is None:
        tt = np.zeros((nj,), np.int32)
    in_specs = [
        pl.BlockSpec((tm, kdim), lambda i, j, tt_: (i, src_cb)),
        pl.BlockSpec((1, kdim), lambda i, j, tt_: (0, 0)),
        pl.BlockSpec((kdim, tn), lambda i, j, tt_: (0, j)),
    ]
    args = [src, gain.reshape(1, kdim).astype(F32), w]
    if tables is not None:
        for tb in tables:
            in_specs.append(pl.BlockSpec((1, tm, LANES), lambda i, j, tt_: (tt_[j], i % ns, 0)))
            args.append(tb)
    if headnorm is not None:
        in_specs.append(pl.BlockSpec((1, tn), lambda i, j, tt_: (0, j)))
        in_specs.append(pl.BlockSpec((BLK, BLK), lambda i, j, tt_: (0, 0)))
        args += [headnorm[0], headnorm[1]]
    if add is not None:
        add_arr, add_cb = add
        in_specs.append(pl.BlockSpec((tm, BLK), lambda i, j, tt_: (i, add_cb)))
        args.append(add_arr)
    kern = functools.partial(_proj_kernel, dnorm=dnorm, rope=tables is not None,
                             headnorm=headnorm is not None, add=add is not None, nsub=nsub)
    return pl.pallas_call(
        kern,
        out_shape=jax.ShapeDtypeStruct((t, n), out_dtype),
        grid_spec=pltpu.PrefetchScalarGridSpec(
            num_scalar_prefetch=1,
            grid=(t // tm, nj),
            in_specs=in_specs,
            out_specs=pl.BlockSpec((tm, tn), lambda i, j, tt_: (i, j)),
            scratch_shapes=[pltpu.VMEM((tm, kdim), BF16)],
        ),
        compiler_params=_cparams(("parallel", "arbitrary")),
    )(jnp.asarray(tt, jnp.int32), *args)


def _flash_kernel(*refs, nh, vgroups, tq, tk, nk, diff):
    if diff:
        (qmask_ref, q_ref, k_ref, vt_ref, lq1, lk1, lq2, lk2, sg_ref,
         o_ref, qm_scr, s_scr, mx_scr, m_scr, acc_scr) = refs
        lambda_init = diff
    else:
        qmask_ref, q_ref, k_ref, vt_ref, o_ref, qm_scr, s_scr, mx_scr, m_scr, acc_scr = refs

    q = q_ref[...].astype(F32)
    for j in range(nh):
        qm_scr[j] = (q * qmask_ref[j:j + 1, :]).T.astype(BF16)
    m_scr[...] = jnp.full(m_scr.shape, NEG_INF, F32)
    acc_scr[...] = jnp.zeros(acc_scr.shape, F32)

    def scores(c, slot):
        ks = pl.multiple_of(c * tk, tk)
        kc = k_ref[pl.ds(ks, tk), :]
        for j in range(nh):
            st = jnp.dot(kc, qm_scr[j], preferred_element_type=F32)
            s_scr[slot, j] = st
            mx_scr[slot, j] = jnp.max(st, axis=0, keepdims=True)

    def accumulate(c, slot):
        vtc = vt_ref[c]
        pts, alphas = [], []
        for j in range(nh):
            m_prev = m_scr[j]
            m_new = jnp.maximum(m_prev, mx_scr[slot, j])
            alphas.append(jnp.exp2(m_prev - m_new))
            m_scr[j] = m_new
            pts.append(jnp.exp2(s_scr[slot, j] - m_new).astype(BF16))
        for gi, (r0, nr, h0, hn) in enumerate(vgroups):
            rhs = pts[h0] if hn == 1 else jnp.concatenate(pts[h0:h0 + hn], axis=1)
            a = alphas[h0] if hn == 1 else jnp.concatenate(alphas[h0:h0 + hn], axis=1)
            acc_scr[gi] = acc_scr[gi] * a + jnp.dot(vtc[r0:r0 + nr + ONES_ROWS, :], rhs,
                                                    preferred_element_type=F32)

    scores(0, 0)

    def body(i, carry):
        c = SCORE_SLOTS * i
        for u in range(SCORE_SLOTS):
            scores(jnp.minimum(c + u + 1, nk - 1), (u + 1) % SCORE_SLOTS)
            accumulate(c + u, u)
        return carry

    lax.fori_loop(0, nk // SCORE_SLOTS, body, 0)

    pieces = []
    if diff:
        lam = (jnp.exp(jnp.sum(lq1[...] * lk1[...], axis=-1, keepdims=True))
               - jnp.exp(jnp.sum(lq2[...] * lk2[...], axis=-1, keepdims=True)) + lambda_init)
        for gi, (r0, nr, h0, hn) in enumerate(vgroups):
            acc = acc_scr[gi]
            linv = 1.0 / acc[nr:nr + 1, :]
            oh = acc[:nr, :tq] * linv[:, :tq] - lam * (acc[:nr, tq:] * linv[:, tq:])
            ms = jnp.mean(oh * oh, axis=0, keepdims=True)
            pieces.append(oh * lax.rsqrt(ms + NORM_EPS))
    else:
        for gi, (r0, nr, h0, hn) in enumerate(vgroups):
            acc = acc_scr[gi]
            on = acc[:nr, :] * (1.0 / acc[nr:nr + 1, :])
            for jj in range(hn):
                pieces.append(on[:, jj * tq:(jj + 1) * tq])
    ot = pieces[0] if len(pieces) == 1 else jnp.concatenate(pieces, axis=0)
    o = ot.T
    if diff:
        o = o * sg_ref[...] * (1.0 - lambda_init)
    o_ref[...] = o.astype(o_ref.dtype)


def _flash(q, k, vt, *, qcol0, kcol0, ngroups, wq, wv, wo, nh, vgroups, qmask, tq, tk,
           diff=None, diff_params=None):
    b, s = q.shape[0], q.shape[1]
    nk = s // tk
    assert s % tk == 0 and nk % SCORE_SLOTS == 0 and s % tq == 0
    in_specs = [
        pl.BlockSpec((nh, wq), lambda bi, g, i: (0, 0)),
        pl.BlockSpec((None, tq, wq), lambda bi, g, i: (bi, i, qcol0 + g)),
        pl.BlockSpec((None, s, wq), lambda bi, g, i: (bi, 0, kcol0 + g)),
        pl.BlockSpec((None, None, nk, wv, tk), lambda bi, g, i: (bi, g, 0, 0, 0)),
    ]
    args = [qmask, q, k, vt]
    if diff is not None:
        for prm in diff_params[:4]:
            in_specs.append(pl.BlockSpec((1, 64), lambda bi, g, i: (0, 0)))
            args.append(prm.reshape(1, 64).astype(F32))
        in_specs.append(pl.BlockSpec((1, wo), lambda bi, g, i: (0, 0)))
        args.append(jnp.tile(diff_params[4].astype(F32), wo // LANES).reshape(1, wo))
    kern = functools.partial(_flash_kernel, nh=nh, vgroups=tuple(vgroups), tq=tq, tk=tk, nk=nk,
                             diff=diff)
    nr, hn = vgroups[0][1], vgroups[0][3]
    return pl.pallas_call(
        kern,
        out_shape=jax.ShapeDtypeStruct((b, s, ngroups * wo), BF16),
        grid=(b, ngroups, s // tq),
        in_specs=in_specs,
        out_specs=pl.BlockSpec((None, tq, wo), lambda bi, g, i: (bi, i, g)),
        scratch_shapes=[
            pltpu.VMEM((nh, wq, tq), BF16),
            pltpu.VMEM((SCORE_SLOTS, nh, tk, tq), F32),
            pltpu.VMEM((SCORE_SLOTS, nh, 1, tq), F32),
            pltpu.VMEM((nh, 1, tq), F32),
            pltpu.VMEM((len(vgroups), nr + ONES_ROWS, hn * tq), F32),
        ],
        compiler_params=_cparams(("parallel", "parallel", "arbitrary")),
    )(*args)


def _transpose_values(v, ngroups, nvg, nr, tk):
    b, s, _ = v.shape
    v = v.reshape(b, s, ngroups, nvg, nr)
    v = jnp.concatenate([v, jnp.ones((b, s, ngroups, nvg, ONES_ROWS), v.dtype)], axis=-1)
    v = v.reshape(b, s // tk, tk, ngroups, nvg * (nr + ONES_ROWS))
    return v.transpose(0, 3, 1, 4, 2)


def _band_kernel(qmask_ref, vmask_ref, q_ref, k_ref, v_ref, o_ref, lse_ref, *, tq, win, length):
    nh = 4
    i = pl.program_id(2)
    ks = jnp.clip(i * tq - A_HALF_WINDOW, 0, length - win)
    ks = pl.multiple_of(ks, A_HALF_WINDOW)
    kc = k_ref[pl.ds(ks, win), :]
    vc = v_ref[pl.ds(ks, win), :]
    q = q_ref[...]
    qm = jnp.concatenate([q * qmask_ref[j:j + 1, :] for j in range(nh)], axis=0)
    s = lax.dot_general(qm, kc, (((1,), (1,)), ((), ())), preferred_element_type=F32)
    qpos = i * tq + lax.broadcasted_iota(jnp.int32, (tq, win), 0)
    kpos = ks + lax.broadcasted_iota(jnp.int32, (tq, win), 1)
    valid = jnp.abs(qpos - kpos) <= A_HALF_WINDOW
    vmask = vmask_ref[...]
    vmask_b = vmask.astype(BF16)
    ps = []
    inv = None
    lse = None
    for j in range(nh):
        sj = jnp.where(valid, s[j * tq:(j + 1) * tq], NEG_INF)
        mj = jnp.max(sj, axis=-1, keepdims=True)
        pj = jnp.exp(sj - mj)
        lj = jnp.sum(pj, axis=-1, keepdims=True)
        ps.append(pj.astype(BF16))
        t_inv = (1.0 / lj) * vmask[j:j + 1, :]
        t_lse = (mj + jnp.log(lj)) * vmask[j:j + 1, :]
        inv = t_inv if inv is None else inv + t_inv
        lse = t_lse if lse is None else lse + t_lse
    lhs = jnp.concatenate(ps, axis=1)
    rhs = jnp.concatenate([vc * vmask_b[j:j + 1, :] for j in range(nh)], axis=0)
    pv = jnp.dot(lhs, rhs, preferred_element_type=F32)
    o_ref[...] = (pv * inv).astype(o_ref.dtype)
    lse_ref[...] = lse


def _band_attention(qkv, dil, qmask, vmask):
    b, s, ncol = qkv.shape
    length = s // dil
    tq = min(256, length // 2)
    win = tq + 2 * A_HALF_WINDOW
    nblk = ncol // BLK
    view = qkv.reshape(b, length, dil * ncol)

    def col(which):
        return lambda bi, a, i: (bi, 0, (a // 2) * nblk + which * 2 + a % 2)

    in_specs = [
        pl.BlockSpec((4, BLK), lambda bi, a, i: (0, 0)),
        pl.BlockSpec((4, BLK), lambda bi, a, i: (0, 0)),
        pl.BlockSpec((None, tq, BLK), lambda bi, a, i: (bi, i, (a // 2) * nblk + a % 2)),
        pl.BlockSpec((None, length, BLK), col(1)),
        pl.BlockSpec((None, length, BLK), col(2)),
    ]
    kern = functools.partial(_band_kernel, tq=tq, win=win, length=length)
    o, lse = pl.pallas_call(
        kern,
        out_shape=(jax.ShapeDtypeStruct((b, length, dil * 2 * BLK), BF16),
                   jax.ShapeDtypeStruct((b, length, dil * 2 * BLK), F32)),
        grid=(b, dil * 2, length // tq),
        in_specs=in_specs,
        out_specs=(pl.BlockSpec((None, tq, BLK), lambda bi, a, i: (bi, i, a)),
                   pl.BlockSpec((None, tq, BLK), lambda bi, a, i: (bi, i, a))),
        compiler_params=_cparams(("parallel", "parallel", "arbitrary")),
    )(qmask, vmask, view, view, view)
    return o.reshape(b, s, 2 * BLK), lse.reshape(b, s, 2 * BLK)


def _a_out_kernel(o0, o1, o2, l0, l1, l2, w_ref, x_ref, out_ref):
    a0, a1, a2 = l0[...], l1[...], l2[...]
    mx = jnp.maximum(jnp.maximum(a0, a1), a2)
    e0, e1, e2 = jnp.exp(a0 - mx), jnp.exp(a1 - mx), jnp.exp(a2 - mx)
    inv = 1.0 / (e0 + e1 + e2)
    o = (e0 * o0[...].astype(F32) + e1 * o1[...].astype(F32) + e2 * o2[...].astype(F32)) * inv
    out_ref[...] = x_ref[...] + jnp.dot(o.astype(BF16), w_ref[...], preferred_element_type=F32)


def _a_out(os_, lses, w, x, tm):
    t = x.shape[0]
    kd = w.shape[0]
    row = lambda i: (i, 0)
    return pl.pallas_call(
        _a_out_kernel,
        out_shape=jax.ShapeDtypeStruct((t, D_MODEL), F32),
        grid=(t // tm,),
        in_specs=[pl.BlockSpec((tm, kd), row)] * 6
        + [pl.BlockSpec((kd, D_MODEL), lambda i: (0, 0)), pl.BlockSpec((tm, D_MODEL), row)],
        out_specs=pl.BlockSpec((tm, D_MODEL), row),
        compiler_params=_cparams(("parallel",)),
    )(*os_, *lses, w, x)


def _matres_kernel(o_ref, w_ref, x_ref, out_ref):
    out_ref[...] = x_ref[...] + jnp.dot(o_ref[...], w_ref[...], preferred_element_type=F32)


def _matmul_residual(o, w, x, tm):
    t = x.shape[0]
    kd = w.shape[0]
    return pl.pallas_call(
        _matres_kernel,
        out_shape=jax.ShapeDtypeStruct((t, D_MODEL), F32),
        grid=(t // tm,),
        in_specs=[pl.BlockSpec((tm, kd), lambda i: (i, 0)),
                  pl.BlockSpec((kd, D_MODEL), lambda i: (0, 0)),
                  pl.BlockSpec((tm, D_MODEL), lambda i: (i, 0))],
        out_specs=pl.BlockSpec((tm, D_MODEL), lambda i: (i, 0)),
        compiler_params=_cparams(("parallel",)),
    )(o, w, x)


def _xattn_kernel(x_ref, g_ref, wq_ref, kv_ref, wo_ref, out_ref):
    xf = x_ref[...]
    ms = jnp.mean(xf * xf, axis=-1, keepdims=True)
    h = (xf * lax.rsqrt(ms + NORM_EPS) * g_ref[...]).astype(BF16)
    q = jnp.dot(h, wq_ref[...], preferred_element_type=F32) * (X_HEAD_DIM ** -0.5)
    qb = q.astype(BF16)
    hd = X_HEADS * X_HEAD_DIM
    outs = []
    for hh in range(X_HEADS):
        qh = qb[:, hh * X_HEAD_DIM:(hh + 1) * X_HEAD_DIM]
        kh = kv_ref[:, hh * X_HEAD_DIM:(hh + 1) * X_HEAD_DIM]
        vh = kv_ref[:, hd + hh * X_HEAD_DIM:hd + (hh + 1) * X_HEAD_DIM]
        s = lax.dot_general(qh, kh, (((1,), (1,)), ((), ())), preferred_element_type=F32)
        m = jnp.max(s, axis=-1, keepdims=True)
        p = jnp.exp(s - m)
        l = jnp.sum(p, axis=-1, keepdims=True)
        oh = jnp.dot(p.astype(BF16), vh, preferred_element_type=F32) * (1.0 / l)
        outs.append(oh.astype(BF16))
    o = jnp.concatenate(outs, axis=1)
    out_ref[...] = xf + jnp.dot(o, wo_ref[...], preferred_element_type=F32)


def _xattn(x3, gain, wq, kv3, wo, tm):
    b, s, _ = x3.shape
    hd = X_HEADS * X_HEAD_DIM
    return pl.pallas_call(
        _xattn_kernel,
        out_shape=jax.ShapeDtypeStruct((b, s, D_MODEL), F32),
        grid=(b, s // tm),
        in_specs=[pl.BlockSpec((None, tm, D_MODEL), lambda bi, i: (bi, i, 0)),
                  pl.BlockSpec((1, D_MODEL), lambda bi, i: (0, 0)),
                  pl.BlockSpec((D_MODEL, hd), lambda bi, i: (0, 0)),
                  pl.BlockSpec((None, N_MEM, 2 * hd), lambda bi, i: (bi, 0, 0)),
                  pl.BlockSpec((hd, D_MODEL), lambda bi, i: (0, 0))],
        out_specs=pl.BlockSpec((None, tm, D_MODEL), lambda bi, i: (bi, i, 0)),
        compiler_params=_cparams(("parallel", "parallel")),
    )(x3, gain.reshape(1, D_MODEL), wq, kv3, wo)


def _mlp_kernel(x_ref, g_ref, w1_ref, w2_ref, fg_ref, out_ref, h_scr, acc_scr, *, final):
    f = pl.program_id(1)

    @pl.when(f == 0)
    def _():
        xf = x_ref[...]
        ms = jnp.mean(xf * xf, axis=-1, keepdims=True)
        h_scr[...] = (xf * lax.rsqrt(ms + NORM_EPS) * g_ref[...]).astype(BF16)
        acc_scr[...] = jnp.zeros(acc_scr.shape, F32)

    a = jnp.maximum(jnp.dot(h_scr[...], w1_ref[...], preferred_element_type=F32), 0.0)
    acc_scr[...] += jnp.dot((a * a).astype(BF16), w2_ref[...], preferred_element_type=F32)

    @pl.when(f == pl.num_programs(1) - 1)
    def _():
        y = x_ref[...] + acc_scr[...]
        if final:
            ms = jnp.mean(y * y, axis=-1, keepdims=True)
            y = y * lax.rsqrt(ms + NORM_EPS) * fg_ref[...]
        out_ref[...] = y


def _mlp(x, gain, w1, w2, final_gain, final, tm, tf):
    t = x.shape[0]
    return pl.pallas_call(
        functools.partial(_mlp_kernel, final=final),
        out_shape=jax.ShapeDtypeStruct((t, D_MODEL), F32),
        grid=(t // tm, D_FF // tf),
        in_specs=[pl.BlockSpec((tm, D_MODEL), lambda i, f: (i, 0)),
                  pl.BlockSpec((1, D_MODEL), lambda i, f: (0, 0)),
                  pl.BlockSpec((D_MODEL, tf), lambda i, f: (0, f)),
                  pl.BlockSpec((tf, D_MODEL), lambda i, f: (f, 0)),
                  pl.BlockSpec((1, D_MODEL), lambda i, f: (0, 0))],
        out_specs=pl.BlockSpec((tm, D_MODEL), lambda i, f: (i, 0)),
        scratch_shapes=[pltpu.VMEM((tm, D_MODEL), BF16), pltpu.VMEM((tm, D_MODEL), F32)],
        compiler_params=_cparams(("parallel", "arbitrary")),
    )(x, gain.reshape(1, D_MODEL), w1, w2, final_gain.reshape(1, D_MODEL))


_F_ROT16 = np.array(list(range(0, 8)) + list(range(16, 40)))
_P_ROT16 = np.array(list(range(8, 16)) + list(range(40, 64)))
_F_AXIAL = np.array(list(range(0, 16)) + list(range(32, 48)))
_P_AXIAL = np.array(list(range(16, 32)) + list(range(48, 64)))


def _block_dims(first, partner):
    lane = np.arange(BLK)
    half, slot, u = lane // LANES, (lane % LANES) // 32, lane % 32
    return np.where(half == 0, first[u], partner[u]), slot


def _slot_masks(nslot, slot_of_lane):
    return np.stack([(slot_of_lane == j) for j in range(nslot)]).astype(np.float32)


def _rope_tables(pos_list, theta, rot, scale_list, npad):
    half = rot // 2
    inv_freq = jnp.exp(jnp.arange(half, dtype=F32) * (-2.0 * math.log(theta) / rot))
    cs, ss = [], []
    for pos in pos_list:
        ang = pos.astype(F32)[:, None] * inv_freq[None, :]
        cs.append(jnp.cos(ang))
        ss.append(jnp.sin(ang))
    c = jnp.concatenate(cs, axis=1)
    s = jnp.concatenate(ss, axis=1)
    n = c.shape[0]
    if npad:
        c = jnp.concatenate([c, jnp.ones((n, npad), F32)], axis=1)
        s = jnp.concatenate([s, jnp.zeros((n, npad), F32)], axis=1)
    reps = LANES // c.shape[1]
    c = jnp.tile(c, (1, reps))
    s = jnp.tile(s, (1, reps))
    ctab = [c * sc for sc in scale_list] + [jnp.ones_like(c)]
    stab = [s * sc for sc in scale_list] + [jnp.zeros_like(s)]
    return jnp.stack(ctab), jnp.stack(stab)


def _mixer_a(x, b, s, gain, w_in, w_out, pos, tm):
    dims, slot = _block_dims(_F_ROT16, _P_ROT16)
    ctab, stab = _rope_tables([pos], ROPE_THETA, 16, [0.125, 1.0], 24)
    qmask = jnp.asarray(_slot_masks(4, slot), BF16)
    vmask = jnp.asarray(_slot_masks(4, np.arange(BLK) // 64), F32)
    outs, lses = [], []
    for wg, (window, dil) in enumerate(A_PATTERNS):
        assert window // (2 * dil) == A_HALF_WINDOW
        cols = []
        for which in range(3):
            base = (wg * 3 + which) * A_HEADS * 64
            for hg in range(2):
                cols.append(base + hg * BLK + (slot * 64 + dims if which < 2 else np.arange(BLK)))
        w = w_in[:, np.concatenate(cols)].astype(BF16)
        qkv = _proj(x, 0, D_MODEL, D_MODEL, gain, w, tn=2 * BLK, out_dtype=BF16, seq=s, tm=tm,
                    tables=(ctab, stab), tt=np.array([0, 1, 2], np.int32))
        o, lse = _band_attention(qkv.reshape(b, s, 3 * 2 * BLK), dil, qmask, vmask)
        outs.append(o.reshape(b * s, 2 * BLK))
        lses.append(lse.reshape(b * s, 2 * BLK))
    return _a_out(outs, lses, w_out.astype(BF16), x, min(tm, 512))


def _mixer_b(x, b, s, gain, w_in, q_gain, k_gain, w_out, rows, cols_pos, tm, tq, tk):
    dims, slot = _block_dims(_F_AXIAL, _P_AXIAL)
    qcols = np.concatenate([g * BLK + slot * 64 + dims for g in range(B_KV_HEADS)])
    kcols = np.concatenate([B_HEADS * 64 + g * 64 + dims for g in range(B_KV_HEADS)])
    wqk = w_in[:, np.concatenate([qcols, kcols])].astype(BF16)
    wv = w_in[:, (B_HEADS + B_KV_HEADS) * 64:].astype(BF16)
    hgain = jnp.concatenate([jnp.tile(q_gain[dims] * (0.125 * LOG2E), B_KV_HEADS),
                             jnp.tile(k_gain[dims], B_KV_HEADS)]).reshape(1, -1).astype(F32)
    bd = jnp.asarray(slot[:, None] == slot[None, :], BF16)
    ctab, stab = _rope_tables([rows, cols_pos], AXIAL_THETA, 32, [1.0], 0)
    n = B_KV_HEADS * BLK
    qk = _proj(x, 0, D_MODEL, D_MODEL, gain, wqk, tn=2 * BLK, out_dtype=BF16, seq=s, tm=tm,
               tables=(ctab, stab), tt=np.zeros((n // BLK,), np.int32), headnorm=(hgain, bd))
    v = _proj(x, 0, D_MODEL, D_MODEL, gain, wv, tn=B_KV_HEADS * 64, out_dtype=BF16, seq=s, tm=tm)
    qk = qk.reshape(b, s, 2 * n)
    vt = _transpose_values(v.reshape(b, s, B_KV_HEADS * 64), B_KV_HEADS, 1, 64, tk)
    qmask = jnp.asarray(_slot_masks(4, slot), BF16)
    o = _flash(qk, qk, vt, qcol0=0, kcol0=B_KV_HEADS, ngroups=B_KV_HEADS, wq=BLK,
               wv=64 + ONES_ROWS, wo=BLK, nh=4, vgroups=((0, 64, 0, 4),), qmask=qmask, tq=tq, tk=tk)
    return _matmul_residual(o.reshape(b * s, B_HEADS * 64), w_out.astype(BF16), x, tm)


def _mixer_c(x, b, s, gain, w_in, lq1, lk1, lq2, lk2, sub_gain, w_out, pos, lambda_init, tm, tq, tk):
    dims, slot = _block_dims(_F_ROT16, _P_ROT16)
    ngr = C_HEADS // 2
    cols = []
    for which in range(3):
        for g in range(ngr):
            base = which * C_HEADS * 128 + g * BLK
            cols.append(base + (slot * 64 + dims if which < 2 else np.arange(BLK)))
    w = w_in[:, np.concatenate(cols)].astype(BF16)
    ctab, stab = _rope_tables([pos], ROPE_THETA, 16, [0.125 * LOG2E, 1.0], 24)
    n = ngr * BLK
    qkv = _proj(x, 0, D_MODEL, D_MODEL, gain, w, tn=2 * BLK, out_dtype=BF16, seq=s, tm=tm,
                tables=(ctab, stab), tt=np.array([0, 0, 1, 1, 2, 2], np.int32))
    qkv = qkv.reshape(b, s, 3 * n)
    qmask = jnp.asarray(_slot_masks(4, slot), BF16)
    vt = _transpose_values(qkv[:, :, 2 * n:], ngr, 2, LANES, tk)
    o = _flash(qkv, qkv, vt, qcol0=0, kcol0=ngr, ngroups=ngr, wq=BLK, wv=2 * (LANES + ONES_ROWS),
               wo=BLK, nh=4, vgroups=((0, LANES, 0, 2), (LANES + ONES_ROWS, LANES, 2, 2)),
               qmask=qmask, tq=tq, tk=tk,
               diff=lambda_init, diff_params=(lq1, lk1, lq2, lk2, sub_gain))
    return _matmul_residual(o.reshape(b * s, C_HEADS * 128), w_out.astype(BF16), x, tm)


def _mixer_d(x, b, s, gain, w_in, q_gain, kv_gain, w_uq, w_ukv, w_out, pos, tm, tq, tk):
    lane = np.arange(LANES)
    slot_h = np.where(lane < 32, 0, np.where(lane < 64, 1, np.where(lane < 80, 0, np.where(lane < 96, 1, -1))))
    slot = np.concatenate([slot_h, slot_h])
    nope_lane = lane < 64
    rope_lane = (lane >= 64) & (lane < 96)
    ngr = D_HEADS // 2

    w1 = jnp.zeros((D_MODEL, 4 * BLK), F32)
    w1 = w1.at[:, :D_Q_RANK].set(w_in[:, :D_Q_RANK])
    w1 = w1.at[:, 2 * BLK:3 * BLK].set(w_in[:, D_Q_RANK:D_Q_RANK + D_KV_RANK])
    kr_src = np.zeros((BLK,), np.int64)
    kr_on = np.zeros((BLK,), bool)
    for hf in range(2):
        for l in range(LANES):
            if rope_lane[l]:
                kr_src[hf * LANES + l] = D_Q_RANK + D_KV_RANK + hf * 16 + (l - 64) % 16
                kr_on[hf * LANES + l] = True
    w1 = w1.at[:, 3 * BLK:].set(jnp.where(jnp.asarray(kr_on)[None, :], w_in[:, kr_src], 0.0))
    cmb = _proj(x, 0, D_MODEL, D_MODEL, gain, w1.astype(BF16), tn=4 * BLK, out_dtype=F32, seq=s, tm=tm)

    qsrc = np.zeros((ngr * BLK,), np.int64)
    qon = np.zeros((ngr * BLK,), bool)
    ksrc = np.zeros((ngr * BLK,), np.int64)
    kon = np.zeros((ngr * BLK,), bool)
    for g in range(ngr):
        for hf in range(2):
            for l in range(LANES):
                idx = g * BLK + hf * LANES + l
                if slot_h[l] < 0:
                    continue
                head = 2 * g + slot_h[l]
                if nope_lane[l]:
                    d = hf * 32 + l % 32
                    qsrc[idx], qon[idx] = head * 96 + d, True
                    ksrc[idx], kon[idx] = head * 128 + d, True
                else:
                    d = hf * 16 + (l - 64) % 16
                    qsrc[idx], qon[idx] = head * 96 + D_NOPE + d, True
    wq2 = jnp.where(jnp.asarray(qon)[None, :], w_uq[:, qsrc], 0.0)
    wq2 = jnp.concatenate([wq2, jnp.zeros((2 * BLK - D_Q_RANK, ngr * BLK), F32)], axis=0).astype(BF16)
    wk2 = jnp.where(jnp.asarray(kon)[None, :], w_ukv[:, ksrc], 0.0).astype(BF16)
    vsrc = np.concatenate([h * 128 + D_NOPE + np.arange(64) for h in range(D_HEADS)])
    wv2 = w_ukv[:, vsrc].astype(BF16)
    qg = jnp.concatenate([q_gain, jnp.zeros((2 * BLK - D_Q_RANK,), F32)])

    half = D_ROPE // 2
    inv_freq = jnp.exp(jnp.arange(half, dtype=F32) * (-2.0 * math.log(ROPE_THETA) / D_ROPE))
    ang = pos.astype(F32)[:, None] * inv_freq[None, :]
    ones64 = jnp.ones((s, 64), F32)
    pad32 = jnp.ones((s, 32), F32)
    c = jnp.concatenate([ones64, jnp.cos(ang), jnp.cos(ang), pad32], axis=1)
    sn = jnp.concatenate([0.0 * ones64, jnp.sin(ang), jnp.sin(ang), 0.0 * pad32], axis=1)
    qs = (D_NOPE + D_ROPE) ** -0.5 * LOG2E
    ctab = jnp.stack([c * qs, c])
    stab = jnp.stack([sn * qs, sn])

    q = _proj(cmb, 0, 2 * BLK, D_Q_RANK, qg, wq2, tn=2 * BLK, out_dtype=BF16, seq=s, tm=tm,
              tables=(ctab, stab), tt=np.zeros((ngr // 2,), np.int32))
    k = _proj(cmb, 2, BLK, D_KV_RANK, kv_gain, wk2, tn=4 * BLK, out_dtype=BF16, seq=s, tm=tm,
              tables=(ctab, stab), tt=np.ones((ngr // 4,), np.int32), add=(cmb, 3))
    v = _proj(cmb, 2, BLK, D_KV_RANK, kv_gain, wv2, tn=4 * BLK, out_dtype=BF16, seq=s, tm=tm)
    q = q.reshape(b, s, ngr * BLK)
    k = k.reshape(b, s, ngr * BLK)
    vt = _transpose_values(v.reshape(b, s, D_HEADS * 64), ngr, 2, 64, tk)
    qmask = jnp.asarray(_slot_masks(2, slot), BF16)
    o = _flash(q, k, vt, qcol0=0, kcol0=0, ngroups=ngr, wq=BLK, wv=2 * (64 + ONES_ROWS), wo=LANES,
               nh=2, vgroups=((0, 64, 0, 1), (64 + ONES_ROWS, 64, 1, 1)), qmask=qmask, tq=tq, tk=tk)
    return _matmul_residual(o.reshape(b * s, D_HEADS * 64), w_out.astype(BF16), x, tm)


def _run_trunk(x3, mem3, p):
    b, s, _ = x3.shape
    t = b * s
    tm = 1024
    tm_mlp = 512
    tq, tq_mla, tk = 256, 512, 512
    x = x3.reshape(t, D_MODEL)
    pos = jnp.arange(s, dtype=F32)
    rows = jnp.repeat(jnp.arange(s // GRID_W, dtype=F32), GRID_W)
    cols_pos = jnp.tile(jnp.arange(GRID_W, dtype=F32), s // GRID_W)
    memf = mem3.reshape(b * N_MEM, D_MODEL)
    for i in range(DEPTH):
        m, j = i % 4, i // 4
        g = p['norm_mix'][i]
        if m == 0:
            x = _mixer_a(x, b, s, g, p['a_w_in'][j], p['a_w_out'][j], pos, tm)
        elif m == 1:
            x = _mixer_b(x, b, s, g, p['b_w_in'][j], p['b_q_norm'][j], p['b_k_norm'][j],
                         p['b_w_out'][j], rows, cols_pos, tm, tq, tk)
        elif m == 2:
            x = _mixer_c(x, b, s, g, p['c_w_in'][j], p['c_lambda_q1'][j], p['c_lambda_k1'][j],
                         p['c_lambda_q2'][j], p['c_lambda_k2'][j], p['c_sub_norm'][j],
                         p['c_w_out'][j], pos, 0.8 - 0.6 * math.exp(-0.3 * i), tm, tq, tk)
        else:
            x = _mixer_d(x, b, s, g, p['d_w_in'][j], p['d_q_norm'][j], p['d_kv_norm'][j],
                         p['d_w_uq'][j], p['d_w_ukv'][j], p['d_w_out'][j], pos, tm, tq_mla, tk)
        kv = _proj(memf, 0, D_MODEL, D_MODEL, p['norm_mem'][i], p['w_xkv'][i].astype(BF16),
                   tn=2 * X_HEADS * X_HEAD_DIM, out_dtype=BF16, seq=N_MEM, tm=N_MEM)
        x = _xattn(x.reshape(b, s, D_MODEL), p['norm_x'][i], p['w_xq'][i].astype(BF16),
                   kv.reshape(b, N_MEM, 2 * X_HEADS * X_HEAD_DIM), p['w_xo'][i].astype(BF16),
                   tm_mlp).reshape(t, D_MODEL)
        x = _mlp(x, p['norm_mlp'][i], p['w_mlp_in'][i].astype(BF16), p['w_mlp_out'][i].astype(BF16),
                 p['final_norm'], i == DEPTH - 1, tm, 512)
    return x.reshape(b, s, D_MODEL)


def kernel(x_prompt, x_sample, mem_prompt, mem_sample, norm_mix, norm_x, norm_mem, w_xq, w_xkv, w_xo, norm_mlp, w_mlp_in, w_mlp_out, a_w_in, a_w_out, b_w_in, b_q_norm, b_k_norm, b_w_out, c_w_in, c_lambda_q1, c_lambda_k1, c_lambda_q2, c_lambda_k2, c_sub_norm, c_w_out, d_w_in, d_q_norm, d_kv_norm, d_w_uq, d_w_ukv, d_w_out, final_norm):
    p = dict(norm_mix=norm_mix, norm_x=norm_x, norm_mem=norm_mem, w_xq=w_xq, w_xkv=w_xkv,
             w_xo=w_xo, norm_mlp=norm_mlp, w_mlp_in=w_mlp_in, w_mlp_out=w_mlp_out,
             a_w_in=a_w_in, a_w_out=a_w_out, b_w_in=b_w_in, b_q_norm=b_q_norm,
             b_k_norm=b_k_norm, b_w_out=b_w_out, c_w_in=c_w_in, c_lambda_q1=c_lambda_q1,
             c_lambda_k1=c_lambda_k1, c_lambda_q2=c_lambda_q2, c_lambda_k2=c_lambda_k2,
             c_sub_norm=c_sub_norm, c_w_out=c_w_out, d_w_in=d_w_in, d_q_norm=d_q_norm,
             d_kv_norm=d_kv_norm, d_w_uq=d_w_uq, d_w_ukv=d_w_ukv, d_w_out=d_w_out,
             final_norm=final_norm)
    return (_run_trunk(x_prompt, mem_prompt, p), _run_trunk(x_sample, mem_sample, p))
```

```python
import functools
import math

import numpy as np
import jax
import jax.numpy as jnp
from jax import lax
from jax.experimental import pallas as pl
from jax.experimental.pallas import tpu as pltpu

F32 = jnp.float32
BF16 = jnp.bfloat16

D_MODEL = 1024
DEPTH = 4
N_MEM = 256
GRID_W = 64
D_FF = 4 * D_MODEL
NORM_EPS = 1e-6
ROPE_THETA = 500000.0
AXIAL_THETA = 10000.0
NEG_INF = -1e30

A_PATTERNS = ((128, 1), (512, 4), (2048, 16))
A_GROUPS = 3
A_HEADS = 8
A_IN = A_GROUPS * 3 * A_HEADS * 64
A_HALF_WINDOW = 64
B_HEADS = 16
B_KV_HEADS = 4
C_HEADS = 8
D_HEADS = 16
D_Q_RANK = 384
D_KV_RANK = 256
D_NOPE = 64
D_ROPE = 32
X_HEADS = 4
X_HEAD_DIM = 128

LOG2E = 1.4426950408889634
ONES_ROWS = 16
SCORE_SLOTS = 4
LANES = 128
BLK = 2 * LANES
VMEM_LIMIT = 56 * 1024 * 1024


def _cparams(sem):
    return pltpu.CompilerParams(dimension_semantics=sem, vmem_limit_bytes=VMEM_LIMIT)


def _proj_kernel(tt_ref, src_ref, g_ref, w_ref, *rest, dnorm, rope, headnorm, add, nsub):
    del tt_ref
    rest = list(rest)
    c_ref = s_ref = hg_ref = bd_ref = add_ref = None
    if rope:
        c_ref, s_ref = rest[0], rest[1]
        rest = rest[2:]
    if headnorm:
        hg_ref, bd_ref = rest[0], rest[1]
        rest = rest[2:]
    if add:
        add_ref = rest[0]
        rest = rest[1:]
    o_ref, h_scr = rest

    @pl.when(pl.program_id(1) == 0)
    def _():
        xf = src_ref[...].astype(F32)
        ms = jnp.sum(xf * xf, axis=-1, keepdims=True) * (1.0 / dnorm)
        h_scr[...] = (xf * lax.rsqrt(ms + NORM_EPS) * g_ref[...]).astype(BF16)

    y = jnp.dot(h_scr[...], w_ref[...], preferred_element_type=F32)
    if add:
        ad = add_ref[...]
        y = y + (ad if nsub == 1 else jnp.concatenate([ad] * nsub, axis=1))
    if headnorm:
        y2 = y * y
        hi = y2.astype(BF16)
        lo = (y2 - hi.astype(F32)).astype(BF16)
        parts = []
        for n in range(nsub):
            sl = slice(n * BLK, (n + 1) * BLK)
            ss = (jnp.dot(hi[:, sl], bd_ref[...], preferred_element_type=F32)
                  + jnp.dot(lo[:, sl], bd_ref[...], preferred_element_type=F32))
            parts.append(y[:, sl] * lax.rsqrt(ss * (1.0 / 64.0) + NORM_EPS))
        y = (parts[0] if nsub == 1 else jnp.concatenate(parts, axis=1)) * hg_ref[...]
    if rope:
        c = c_ref[0]
        s = s_ref[0]
        for n in range(nsub):
            y1 = y[:, n * BLK:n * BLK + LANES]
            y2 = y[:, n * BLK + LANES:(n + 1) * BLK]
            o_ref[:, n * BLK:n * BLK + LANES] = (y1 * c - y2 * s).astype(o_ref.dtype)
            o_ref[:, n * BLK + LANES:(n + 1) * BLK] = (y2 * c + y1 * s).astype(o_ref.dtype)
    else:
        o_ref[...] = y.astype(o_ref.dtype)


def _proj(src, src_cb, kdim, dnorm, gain, w, *, tn, out_dtype, seq, tm,
          tables=None, tt=None, headnorm=None, add=None):
    t = src.shape[0]
    n = w.shape[1]
    nj = n // tn
    nsub = tn // BLK if (tables is not None or headnorm is not None) else 1
    ns = seq // tm
    if tt is None:
        tt = np.zeros((nj,), np.int32)
    in_specs = [
        pl.BlockSpec((tm, kdim), lambda i, j, tt_: (i, src_cb)),
        pl.BlockSpec((1, kdim), lambda i, j, tt_: (0, 0)),
        pl.BlockSpec((kdim, tn), lambda i, j, tt_: (0, j)),
    ]
    args = [src, gain.reshape(1, kdim).astype(F32), w]
    if tables is not None:
        for tb in tables:
            in_specs.append(pl.BlockSpec((1, tm, LANES), lambda i, j, tt_: (tt_[j], i % ns, 0)))
            args.append(tb)
    if headnorm is not None:
        in_specs.append(pl.BlockSpec((1, tn), lambda i, j, tt_: (0, j)))
        in_specs.append(pl.BlockSpec((BLK, BLK), lambda i, j, tt_: (0, 0)))
        args += [headnorm[0], headnorm[1]]
    if add is not None:
        add_arr, add_cb = add
        in_specs.append(pl.BlockSpec((tm, BLK), lambda i, j, tt_: (i, add_cb)))
        args.append(add_arr)
    kern = functools.partial(_proj_kernel, dnorm=dnorm, rope=tables is not None,
                             headnorm=headnorm is not None, add=add is not None, nsub=nsub)
    return pl.pallas_call(
        kern,
        out_shape=jax.ShapeDtypeStruct((t, n), out_dtype),
        grid_spec=pltpu.PrefetchScalarGridSpec(
            num_scalar_prefetch=1,
            grid=(t // tm, nj),
            in_specs=in_specs,
            out_specs=pl.BlockSpec((tm, tn), lambda i, j, tt_: (i, j)),
            scratch_shapes=[pltpu.VMEM((tm, kdim), BF16)],
        ),
        compiler_params=_cparams(("parallel", "arbitrary")),
    )(jnp.asarray(tt, jnp.int32), *args)


def _flash_kernel(*refs, nh, vgroups, tq, tk, nk, diff):
    if diff:
        (qmask_ref, q_ref, k_ref, vt_ref, lq1, lk1, lq2, lk2, sg_ref,
         o_ref, qm_scr, s_scr, mx_scr, m_scr, acc_scr) = refs
        lambda_init = diff
    else:
        qmask_ref, q_ref, k_ref, vt_ref, o_ref, qm_scr, s_scr, mx_scr, m_scr, acc_scr = refs

    q = q_ref[...].astype(F32)
    for j in range(nh):
        qm_scr[:, j * tq:(j + 1) * tq] = (q * qmask_ref[j:j + 1, :]).T.astype(BF16)
    m_scr[...] = jnp.full(m_scr.shape, NEG_INF, F32)
    acc_scr[...] = jnp.zeros(acc_scr.shape, F32)

    def scores(c, slot):
        ks = pl.multiple_of(c * tk, tk)
        st = jnp.dot(k_ref[pl.ds(ks, tk), :], qm_scr[...], preferred_element_type=F32)
        s_scr[slot] = st
        mx_scr[slot] = jnp.max(st, axis=0, keepdims=True)

    def accumulate(c, slot):
        vtc = vt_ref[c]
        m_prev = m_scr[...]
        m_new = jnp.maximum(m_prev, mx_scr[slot])
        alpha = jnp.exp2(m_prev - m_new)
        m_scr[...] = m_new
        pt = jnp.exp2(s_scr[slot] - m_new).astype(BF16)
        for gi, (r0, nr, h0, hn) in enumerate(vgroups):
            cols = slice(h0 * tq, (h0 + hn) * tq)
            acc_scr[gi] = acc_scr[gi] * alpha[:, cols] + jnp.dot(
                vtc[r0:r0 + nr + ONES_ROWS, :], pt[:, cols], preferred_element_type=F32)

    scores(0, 0)

    def body(i, carry):
        c = SCORE_SLOTS * i
        for u in range(SCORE_SLOTS):
            scores(jnp.minimum(c + u + 1, nk - 1), (u + 1) % SCORE_SLOTS)
            accumulate(c + u, u)
        return carry

    lax.fori_loop(0, nk // SCORE_SLOTS, body, 0)

    pieces = []
    if diff:
        lam = (jnp.exp(jnp.sum(lq1[...] * lk1[...], axis=-1, keepdims=True))
               - jnp.exp(jnp.sum(lq2[...] * lk2[...], axis=-1, keepdims=True)) + lambda_init)
        for gi, (r0, nr, h0, hn) in enumerate(vgroups):
            acc = acc_scr[gi]
            linv = 1.0 / acc[nr:nr + 1, :]
            oh = acc[:nr, :tq] * linv[:, :tq] - lam * (acc[:nr, tq:] * linv[:, tq:])
            ms = jnp.mean(oh * oh, axis=0, keepdims=True)
            pieces.append(oh * lax.rsqrt(ms + NORM_EPS))
    else:
        for gi, (r0, nr, h0, hn) in enumerate(vgroups):
            acc = acc_scr[gi]
            on = acc[:nr, :] * (1.0 / acc[nr:nr + 1, :])
            for jj in range(hn):
                pieces.append(on[:, jj * tq:(jj + 1) * tq])
    ot = pieces[0] if len(pieces) == 1 else jnp.concatenate(pieces, axis=0)
    o = ot.T
    if diff:
        o = o * sg_ref[...] * (1.0 - lambda_init)
    o_ref[...] = o.astype(o_ref.dtype)


def _flash(q, k, vt, *, qcol0, kcol0, ngroups, wq, wv, wo, nh, vgroups, qmask, tq, tk,
           diff=None, diff_params=None):
    b, s = q.shape[0], q.shape[1]
    nk = s // tk
    assert s % tk == 0 and nk % SCORE_SLOTS == 0 and s % tq == 0
    in_specs = [
        pl.BlockSpec((nh, wq), lambda bi, g, i: (0, 0)),
        pl.BlockSpec((None, tq, wq), lambda bi, g, i: (bi, i, qcol0 + g)),
        pl.BlockSpec((None, s, wq), lambda bi, g, i: (bi, 0, kcol0 + g)),
        pl.BlockSpec((None, None, nk, wv, tk), lambda bi, g, i: (bi, g, 0, 0, 0)),
    ]
    args = [qmask, q, k, vt]
    if diff is not None:
        for prm in diff_params[:4]:
            in_specs.append(pl.BlockSpec((1, 64), lambda bi, g, i: (0, 0)))
            args.append(prm.reshape(1, 64).astype(F32))
        in_specs.append(pl.BlockSpec((1, wo), lambda bi, g, i: (0, 0)))
        args.append(jnp.tile(diff_params[4].astype(F32), wo // LANES).reshape(1, wo))
    kern = functools.partial(_flash_kernel, nh=nh, vgroups=tuple(vgroups), tq=tq, tk=tk, nk=nk,
                             diff=diff)
    nr, hn = vgroups[0][1], vgroups[0][3]
    return pl.pallas_call(
        kern,
        out_shape=jax.ShapeDtypeStruct((b, s, ngroups * wo), BF16),
        grid=(b, ngroups, s // tq),
        in_specs=in_specs,
        out_specs=pl.BlockSpec((None, tq, wo), lambda bi, g, i: (bi, i, g)),
        scratch_shapes=[
            pltpu.VMEM((wq, nh * tq), BF16),
            pltpu.VMEM((SCORE_SLOTS, tk, nh * tq), F32),
            pltpu.VMEM((SCORE_SLOTS, 1, nh * tq), F32),
            pltpu.VMEM((1, nh * tq), F32),
            pltpu.VMEM((len(vgroups), nr + ONES_ROWS, hn * tq), F32),
        ],
        compiler_params=_cparams(("parallel", "parallel", "arbitrary")),
    )(*args)


def _transpose_values(v, ngroups, nvg, nr, tk):
    b, s, _ = v.shape
    v = v.reshape(b, s, ngroups, nvg, nr)
    v = jnp.concatenate([v, jnp.ones((b, s, ngroups, nvg, ONES_ROWS), v.dtype)], axis=-1)
    v = v.reshape(b, s // tk, tk, ngroups, nvg * (nr + ONES_ROWS))
    return v.transpose(0, 3, 1, 4, 2)


def _band_kernel(qmask_ref, vmask_ref, q_ref, k_ref, v_ref, o_ref, lse_ref, *, tq, win, length):
    nh = 4
    i = pl.program_id(2)
    ks = jnp.clip(i * tq - A_HALF_WINDOW, 0, length - win)
    ks = pl.multiple_of(ks, A_HALF_WINDOW)
    kc = k_ref[pl.ds(ks, win), :]
    vc = v_ref[pl.ds(ks, win), :]
    q = q_ref[...]
    qm = jnp.concatenate([q * qmask_ref[j:j + 1, :] for j in range(nh)], axis=0)
    s = lax.dot_general(qm, kc, (((1,), (1,)), ((), ())), preferred_element_type=F32)
    qpos = i * tq + lax.broadcasted_iota(jnp.int32, (tq, win), 0)
    kpos = ks + lax.broadcasted_iota(jnp.int32, (tq, win), 1)
    valid = jnp.abs(qpos - kpos) <= A_HALF_WINDOW
    vmask = vmask_ref[...]
    vmask_b = vmask.astype(BF16)
    ps = []
    inv = None
    lse = None
    for j in range(nh):
        sj = jnp.where(valid, s[j * tq:(j + 1) * tq], NEG_INF)
        mj = jnp.max(sj, axis=-1, keepdims=True)
        pj = jnp.exp(sj - mj)
        lj = jnp.sum(pj, axis=-1, keepdims=True)
        ps.append(pj.astype(BF16))
        t_inv = (1.0 / lj) * vmask[j:j + 1, :]
        t_lse = (mj + jnp.log(lj)) * vmask[j:j + 1, :]
        inv = t_inv if inv is None else inv + t_inv
        lse = t_lse if lse is None else lse + t_lse
    lhs = jnp.concatenate(ps, axis=1)
    rhs = jnp.concatenate([vc * vmask_b[j:j + 1, :] for j in range(nh)], axis=0)
    pv = jnp.dot(lhs, rhs, preferred_element_type=F32)
    o_ref[...] = (pv * inv).astype(o_ref.dtype)
    lse_ref[...] = lse


def _band_attention(qkv, dil, qmask, vmask):
    b, s, ncol = qkv.shape
    length = s // dil
    tq = min(256, length // 2)
    win = tq + 2 * A_HALF_WINDOW
    nblk = ncol // BLK
    view = qkv.reshape(b, length, dil * ncol)

    def col(which):
        return lambda bi, a, i: (bi, 0, (a // 2) * nblk + which * 2 + a % 2)

    in_specs = [
        pl.BlockSpec((4, BLK), lambda bi, a, i: (0, 0)),
        pl.BlockSpec((4, BLK), lambda bi, a, i: (0, 0)),
        pl.BlockSpec((None, tq, BLK), lambda bi, a, i: (bi, i, (a // 2) * nblk + a % 2)),
        pl.BlockSpec((None, length, BLK), col(1)),
        pl.BlockSpec((None, length, BLK), col(2)),
    ]
    kern = functools.partial(_band_kernel, tq=tq, win=win, length=length)
    o, lse = pl.pallas_call(
        kern,
        out_shape=(jax.ShapeDtypeStruct((b, length, dil * 2 * BLK), BF16),
                   jax.ShapeDtypeStruct((b, length, dil * 2 * BLK), F32)),
        grid=(b, dil * 2, length // tq),
        in_specs=in_specs,
        out_specs=(pl.BlockSpec((None, tq, BLK), lambda bi, a, i: (bi, i, a)),
                   pl.BlockSpec((None, tq, BLK), lambda bi, a, i: (bi, i, a))),
        compiler_params=_cparams(("parallel", "parallel", "arbitrary")),
    )(qmask, vmask, view, view, view)
    return o.reshape(b, s, 2 * BLK), lse.reshape(b, s, 2 * BLK)


def _a_out_kernel(o0, o1, o2, l0, l1, l2, w_ref, x_ref, out_ref):
    a0, a1, a2 = l0[...], l1[...], l2[...]
    mx = jnp.maximum(jnp.maximum(a0, a1), a2)
    e0, e1, e2 = jnp.exp(a0 - mx), jnp.exp(a1 - mx), jnp.exp(a2 - mx)
    inv = 1.0 / (e0 + e1 + e2)
    o = (e0 * o0[...].astype(F32) + e1 * o1[...].astype(F32) + e2 * o2[...].astype(F32)) * inv
    out_ref[...] = x_ref[...] + jnp.dot(o.astype(BF16), w_ref[...], preferred_element_type=F32)


def _a_out(os_, lses, w, x, tm):
    t = x.shape[0]
    kd = w.shape[0]
    row = lambda i: (i, 0)
    return pl.pallas_call(
        _a_out_kernel,
        out_shape=jax.ShapeDtypeStruct((t, D_MODEL), F32),
        grid=(t // tm,),
        in_specs=[pl.BlockSpec((tm, kd), row)] * 6
        + [pl.BlockSpec((kd, D_MODEL), lambda i: (0, 0)), pl.BlockSpec((tm, D_MODEL), row)],
        out_specs=pl.BlockSpec((tm, D_MODEL), row),
        compiler_params=_cparams(("parallel",)),
    )(*os_, *lses, w, x)


def _post_kernel(*refs, has_proj, final):
    if has_proj:
        (x_ref, o_ref, wout_ref, gx_ref, wq_ref, kv_ref, wo_ref, gm_ref, w1_ref, w2_ref, fg_ref,
         out_ref, x2_scr, h_scr, acc_scr) = refs
    else:
        (x_ref, gx_ref, wq_ref, kv_ref, wo_ref, gm_ref, w1_ref, w2_ref, fg_ref,
         out_ref, x2_scr, h_scr, acc_scr) = refs
    f = pl.program_id(2)

    @pl.when(f == 0)
    def _():
        x1 = x_ref[...]
        if has_proj:
            x1 = x1 + jnp.dot(o_ref[...], wout_ref[...], preferred_element_type=F32)
        ms = jnp.mean(x1 * x1, axis=-1, keepdims=True)
        h = (x1 * lax.rsqrt(ms + NORM_EPS) * gx_ref[...]).astype(BF16)
        qb = (jnp.dot(h, wq_ref[...], preferred_element_type=F32) * (X_HEAD_DIM ** -0.5)).astype(BF16)
        hd = X_HEADS * X_HEAD_DIM
        outs = []
        for hh in range(X_HEADS):
            qh = qb[:, hh * X_HEAD_DIM:(hh + 1) * X_HEAD_DIM]
            kh = kv_ref[:, hh * X_HEAD_DIM:(hh + 1) * X_HEAD_DIM]
            vh = kv_ref[:, hd + hh * X_HEAD_DIM:hd + (hh + 1) * X_HEAD_DIM]
            s = lax.dot_general(qh, kh, (((1,), (1,)), ((), ())), preferred_element_type=F32)
            m = jnp.max(s, axis=-1, keepdims=True)
            p = jnp.exp(s - m)
            l = jnp.sum(p, axis=-1, keepdims=True)
            oh = jnp.dot(p.astype(BF16), vh, preferred_element_type=F32) * (1.0 / l)
            outs.append(oh.astype(BF16))
        x2 = x1 + jnp.dot(jnp.concatenate(outs, axis=1), wo_ref[...], preferred_element_type=F32)
        x2_scr[...] = x2
        ms2 = jnp.mean(x2 * x2, axis=-1, keepdims=True)
        h_scr[...] = (x2 * lax.rsqrt(ms2 + NORM_EPS) * gm_ref[...]).astype(BF16)
        acc_scr[...] = jnp.zeros(acc_scr.shape, F32)

    a = jnp.maximum(jnp.dot(h_scr[...], w1_ref[...], preferred_element_type=F32), 0.0)
    acc_scr[...] += jnp.dot((a * a).astype(BF16), w2_ref[...], preferred_element_type=F32)

    @pl.when(f == pl.num_programs(2) - 1)
    def _():
        y = x2_scr[...] + acc_scr[...]
        if final:
            ms = jnp.mean(y * y, axis=-1, keepdims=True)
            y = y * lax.rsqrt(ms + NORM_EPS) * fg_ref[...]
        out_ref[...] = y


def _post_mixer(x3, o3, w_out, gx, wq, kv3, wo, gm, w1, w2, final_gain, final, tm, tf):
    b, s, _ = x3.shape
    hd = X_HEADS * X_HEAD_DIM
    has_proj = o3 is not None
    const = lambda bi, i, f: (0, 0)
    row = lambda bi, i, f: (bi, i, 0)
    in_specs = [pl.BlockSpec((None, tm, D_MODEL), row)]
    args = [x3]
    if has_proj:
        kd = o3.shape[-1]
        in_specs += [pl.BlockSpec((None, tm, kd), row), pl.BlockSpec((kd, D_MODEL), const)]
        args += [o3, w_out]
    in_specs += [pl.BlockSpec((1, D_MODEL), const),
                 pl.BlockSpec((D_MODEL, hd), const),
                 pl.BlockSpec((None, N_MEM, 2 * hd), lambda bi, i, f: (bi, 0, 0)),
                 pl.BlockSpec((hd, D_MODEL), const),
                 pl.BlockSpec((1, D_MODEL), const),
                 pl.BlockSpec((D_MODEL, tf), lambda bi, i, f: (0, f)),
                 pl.BlockSpec((tf, D_MODEL), lambda bi, i, f: (f, 0)),
                 pl.BlockSpec((1, D_MODEL), const)]
    args += [gx.reshape(1, D_MODEL), wq, kv3, wo, gm.reshape(1, D_MODEL), w1, w2,
             final_gain.reshape(1, D_MODEL)]
    return pl.pallas_call(
        functools.partial(_post_kernel, has_proj=has_proj, final=final),
        out_shape=jax.ShapeDtypeStruct((b, s, D_MODEL), F32),
        grid=(b, s // tm, D_FF // tf),
        in_specs=in_specs,
        out_specs=pl.BlockSpec((None, tm, D_MODEL), row),
        scratch_shapes=[pltpu.VMEM((tm, D_MODEL), F32), pltpu.VMEM((tm, D_MODEL), BF16),
                        pltpu.VMEM((tm, D_MODEL), F32)],
        compiler_params=_cparams(("parallel", "parallel", "arbitrary")),
    )(*args)


_F_ROT16 = np.array(list(range(0, 8)) + list(range(16, 40)))
_P_ROT16 = np.array(list(range(8, 16)) + list(range(40, 64)))
_F_AXIAL = np.array(list(range(0, 16)) + list(range(32, 48)))
_P_AXIAL = np.array(list(range(16, 32)) + list(range(48, 64)))


def _block_dims(first, partner):
    lane = np.arange(BLK)
    half, slot, u = lane // LANES, (lane % LANES) // 32, lane % 32
    return np.where(half == 0, first[u], partner[u]), slot


def _slot_masks(nslot, slot_of_lane):
    return np.stack([(slot_of_lane == j) for j in range(nslot)]).astype(np.float32)


def _rope_tables(pos_list, theta, rot, scale_list, npad):
    half = rot // 2
    inv_freq = jnp.exp(jnp.arange(half, dtype=F32) * (-2.0 * math.log(theta) / rot))
    cs, ss = [], []
    for pos in pos_list:
        ang = pos.astype(F32)[:, None] * inv_freq[None, :]
        cs.append(jnp.cos(ang))
        ss.append(jnp.sin(ang))
    c = jnp.concatenate(cs, axis=1)
    s = jnp.concatenate(ss, axis=1)
    n = c.shape[0]
    if npad:
        c = jnp.concatenate([c, jnp.ones((n, npad), F32)], axis=1)
        s = jnp.concatenate([s, jnp.zeros((n, npad), F32)], axis=1)
    reps = LANES // c.shape[1]
    c = jnp.tile(c, (1, reps))
    s = jnp.tile(s, (1, reps))
    ctab = [c * sc for sc in scale_list] + [jnp.ones_like(c)]
    stab = [s * sc for sc in scale_list] + [jnp.zeros_like(s)]
    return jnp.stack(ctab), jnp.stack(stab)


def _mixer_a(x, b, s, gain, w_in, w_out, pos, tm):
    dims, slot = _block_dims(_F_ROT16, _P_ROT16)
    ctab, stab = _rope_tables([pos], ROPE_THETA, 16, [0.125, 1.0], 24)
    qmask = jnp.asarray(_slot_masks(4, slot), BF16)
    vmask = jnp.asarray(_slot_masks(4, np.arange(BLK) // 64), F32)
    outs, lses = [], []
    for wg, (window, dil) in enumerate(A_PATTERNS):
        assert window // (2 * dil) == A_HALF_WINDOW
        cols = []
        for which in range(3):
            base = (wg * 3 + which) * A_HEADS * 64
            for hg in range(2):
                cols.append(base + hg * BLK + (slot * 64 + dims if which < 2 else np.arange(BLK)))
        w = w_in[:, np.concatenate(cols)].astype(BF16)
        qkv = _proj(x, 0, D_MODEL, D_MODEL, gain, w, tn=2 * BLK, out_dtype=BF16, seq=s, tm=tm,
                    tables=(ctab, stab), tt=np.array([0, 1, 2], np.int32))
        o, lse = _band_attention(qkv.reshape(b, s, 3 * 2 * BLK), dil, qmask, vmask)
        outs.append(o.reshape(b * s, 2 * BLK))
        lses.append(lse.reshape(b * s, 2 * BLK))
    return _a_out(outs, lses, w_out.astype(BF16), x, min(tm, 512)), None, None


def _mixer_b(x, b, s, gain, w_in, q_gain, k_gain, w_out, rows, cols_pos, tm, tq, tk):
    dims, slot = _block_dims(_F_AXIAL, _P_AXIAL)
    qcols = np.concatenate([g * BLK + slot * 64 + dims for g in range(B_KV_HEADS)])
    kcols = np.concatenate([B_HEADS * 64 + g * 64 + dims for g in range(B_KV_HEADS)])
    wqk = w_in[:, np.concatenate([qcols, kcols])].astype(BF16)
    wv = w_in[:, (B_HEADS + B_KV_HEADS) * 64:].astype(BF16)
    hgain = jnp.concatenate([jnp.tile(q_gain[dims] * (0.125 * LOG2E), B_KV_HEADS),
                             jnp.tile(k_gain[dims], B_KV_HEADS)]).reshape(1, -1).astype(F32)
    bd = jnp.asarray(slot[:, None] == slot[None, :], BF16)
    ctab, stab = _rope_tables([rows, cols_pos], AXIAL_THETA, 32, [1.0], 0)
    n = B_KV_HEADS * BLK
    qk = _proj(x, 0, D_MODEL, D_MODEL, gain, wqk, tn=2 * BLK, out_dtype=BF16, seq=s, tm=tm,
               tables=(ctab, stab), tt=np.zeros((n // BLK,), np.int32), headnorm=(hgain, bd))
    v = _proj(x, 0, D_MODEL, D_MODEL, gain, wv, tn=B_KV_HEADS * 64, out_dtype=BF16, seq=s, tm=tm)
    qk = qk.reshape(b, s, 2 * n)
    vt = _transpose_values(v.reshape(b, s, B_KV_HEADS * 64), B_KV_HEADS, 1, 64, tk)
    qmask = jnp.asarray(_slot_masks(4, slot), BF16)
    o = _flash(qk, qk, vt, qcol0=0, kcol0=B_KV_HEADS, ngroups=B_KV_HEADS, wq=BLK,
               wv=64 + ONES_ROWS, wo=BLK, nh=4, vgroups=((0, 64, 0, 4),), qmask=qmask, tq=tq, tk=tk)
    return x, o, w_out.astype(BF16)


def _mixer_c(x, b, s, gain, w_in, lq1, lk1, lq2, lk2, sub_gain, w_out, pos, lambda_init, tm, tq, tk):
    dims, slot = _block_dims(_F_ROT16, _P_ROT16)
    ngr = C_HEADS // 2
    cols = []
    for which in range(3):
        for g in range(ngr):
            base = which * C_HEADS * 128 + g * BLK
            cols.append(base + (slot * 64 + dims if which < 2 else np.arange(BLK)))
    w = w_in[:, np.concatenate(cols)].astype(BF16)
    ctab, stab = _rope_tables([pos], ROPE_THETA, 16, [0.125 * LOG2E, 1.0], 24)
    n = ngr * BLK
    qkv = _proj(x, 0, D_MODEL, D_MODEL, gain, w, tn=2 * BLK, out_dtype=BF16, seq=s, tm=tm,
                tables=(ctab, stab), tt=np.array([0, 0, 1, 1, 2, 2], np.int32))
    qkv = qkv.reshape(b, s, 3 * n)
    qmask = jnp.asarray(_slot_masks(4, slot), BF16)
    vt = _transpose_values(qkv[:, :, 2 * n:], ngr, 2, LANES, tk)
    o = _flash(qkv, qkv, vt, qcol0=0, kcol0=ngr, ngroups=ngr, wq=BLK, wv=2 * (LANES + ONES_ROWS),
               wo=BLK, nh=4, vgroups=((0, LANES, 0, 2), (LANES + ONES_ROWS, LANES, 2, 2)),
               qmask=qmask, tq=tq, tk=tk,
               diff=lambda_init, diff_params=(lq1, lk1, lq2, lk2, sub_gain))
    return x, o, w_out.astype(BF16)


def _mixer_d(x, b, s, gain, w_in, q_gain, kv_gain, w_uq, w_ukv, w_out, pos, tm, tq, tk):
    lane = np.arange(LANES)
    slot_h = np.where(lane < 32, 0, np.where(lane < 64, 1, np.where(lane < 80, 0, np.where(lane < 96, 1, -1))))
    slot = np.concatenate([slot_h, slot_h])
    nope_lane = lane < 64
    rope_lane = (lane >= 64) & (lane < 96)
    ngr = D_HEADS // 2

    w1 = jnp.zeros((D_MODEL, 4 * BLK), F32)
    w1 = w1.at[:, :D_Q_RANK].set(w_in[:, :D_Q_RANK])
    w1 = w1.at[:, 2 * BLK:3 * BLK].set(w_in[:, D_Q_RANK:D_Q_RANK + D_KV_RANK])
    kr_src = np.zeros((BLK,), np.int64)
    kr_on = np.zeros((BLK,), bool)
    for hf in range(2):
        for l in range(LANES):
            if rope_lane[l]:
                kr_src[hf * LANES + l] = D_Q_RANK + D_KV_RANK + hf * 16 + (l - 64) % 16
                kr_on[hf * LANES + l] = True
    w1 = w1.at[:, 3 * BLK:].set(jnp.where(jnp.asarray(kr_on)[None, :], w_in[:, kr_src], 0.0))
    cmb = _proj(x, 0, D_MODEL, D_MODEL, gain, w1.astype(BF16), tn=4 * BLK, out_dtype=F32, seq=s, tm=tm)

    qsrc = np.zeros((ngr * BLK,), np.int64)
    qon = np.zeros((ngr * BLK,), bool)
    ksrc = np.zeros((ngr * BLK,), np.int64)
    kon = np.zeros((ngr * BLK,), bool)
    for g in range(ngr):
        for hf in range(2):
            for l in range(LANES):
                idx = g * BLK + hf * LANES + l
                if slot_h[l] < 0:
                    continue
                head = 2 * g + slot_h[l]
                if nope_lane[l]:
                    d = hf * 32 + l % 32
                    qsrc[idx], qon[idx] = head * 96 + d, True
                    ksrc[idx], kon[idx] = head * 128 + d, True
                else:
                    d = hf * 16 + (l - 64) % 16
                    qsrc[idx], qon[idx] = head * 96 + D_NOPE + d, True
    wq2 = jnp.where(jnp.asarray(qon)[None, :], w_uq[:, qsrc], 0.0)
    wq2 = jnp.concatenate([wq2, jnp.zeros((2 * BLK - D_Q_RANK, ngr * BLK), F32)], axis=0).astype(BF16)
    wk2 = jnp.where(jnp.asarray(kon)[None, :], w_ukv[:, ksrc], 0.0).astype(BF16)
    vsrc = np.concatenate([h * 128 + D_NOPE + np.arange(64) for h in range(D_HEADS)])
    wv2 = w_ukv[:, vsrc].astype(BF16)
    qg = jnp.concatenate([q_gain, jnp.zeros((2 * BLK - D_Q_RANK,), F32)])

    half = D_ROPE // 2
    inv_freq = jnp.exp(jnp.arange(half, dtype=F32) * (-2.0 * math.log(ROPE_THETA) / D_ROPE))
    ang = pos.astype(F32)[:, None] * inv_freq[None, :]
    ones64 = jnp.ones((s, 64), F32)
    pad32 = jnp.ones((s, 32), F32)
    c = jnp.concatenate([ones64, jnp.cos(ang), jnp.cos(ang), pad32], axis=1)
    sn = jnp.concatenate([0.0 * ones64, jnp.sin(ang), jnp.sin(ang), 0.0 * pad32], axis=1)
    qs = (D_NOPE + D_ROPE) ** -0.5 * LOG2E
    ctab = jnp.stack([c * qs, c])
    stab = jnp.stack([sn * qs, sn])

    q = _proj(cmb, 0, 2 * BLK, D_Q_RANK, qg, wq2, tn=2 * BLK, out_dtype=BF16, seq=s, tm=tm,
              tables=(ctab, stab), tt=np.zeros((ngr // 2,), np.int32))
    k = _proj(cmb, 2, BLK, D_KV_RANK, kv_gain, wk2, tn=4 * BLK, out_dtype=BF16, seq=s, tm=tm,
              tables=(ctab, stab), tt=np.ones((ngr // 4,), np.int32), add=(cmb, 3))
    v = _proj(cmb, 2, BLK, D_KV_RANK, kv_gain, wv2, tn=4 * BLK, out_dtype=BF16, seq=s, tm=tm)
    q = q.reshape(b, s, ngr * BLK)
    k = k.reshape(b, s, ngr * BLK)
    vt = _transpose_values(v.reshape(b, s, D_HEADS * 64), ngr, 2, 64, tk)
    qmask = jnp.asarray(_slot_masks(2, slot), BF16)
    o = _flash(q, k, vt, qcol0=0, kcol0=0, ngroups=ngr, wq=BLK, wv=2 * (64 + ONES_ROWS), wo=LANES,
               nh=2, vgroups=((0, 64, 0, 1), (64 + ONES_ROWS, 64, 1, 1)), qmask=qmask, tq=tq, tk=tk)
    return x, o, w_out.astype(BF16)


def _run_trunk(x3, mem3, p):
    b, s, _ = x3.shape
    t = b * s
    tm = 1024
    tm_mlp = 512
    tq, tq_mla, tk = 256, 512, 512
    x = x3.reshape(t, D_MODEL)
    pos = jnp.arange(s, dtype=F32)
    rows = jnp.repeat(jnp.arange(s // GRID_W, dtype=F32), GRID_W)
    cols_pos = jnp.tile(jnp.arange(GRID_W, dtype=F32), s // GRID_W)
    memf = mem3.reshape(b * N_MEM, D_MODEL)
    for i in range(DEPTH):
        m, j = i % 4, i // 4
        g = p['norm_mix'][i]
        if m == 0:
            x, o, w_out = _mixer_a(x, b, s, g, p['a_w_in'][j], p['a_w_out'][j], pos, tm)
        elif m == 1:
            x, o, w_out = _mixer_b(x, b, s, g, p['b_w_in'][j], p['b_q_norm'][j], p['b_k_norm'][j],
                                   p['b_w_out'][j], rows, cols_pos, tm, tq, tk)
        elif m == 2:
            x, o, w_out = _mixer_c(x, b, s, g, p['c_w_in'][j], p['c_lambda_q1'][j],
                                   p['c_lambda_k1'][j], p['c_lambda_q2'][j], p['c_lambda_k2'][j],
                                   p['c_sub_norm'][j], p['c_w_out'][j], pos,
                                   0.8 - 0.6 * math.exp(-0.3 * i), tm, tq, tk)
        else:
            x, o, w_out = _mixer_d(x, b, s, g, p['d_w_in'][j], p['d_q_norm'][j], p['d_kv_norm'][j],
                                   p['d_w_uq'][j], p['d_w_ukv'][j], p['d_w_out'][j], pos, tm,
                                   tq_mla, tk)
        kv = _proj(memf, 0, D_MODEL, D_MODEL, p['norm_mem'][i], p['w_xkv'][i].astype(BF16),
                   tn=2 * X_HEADS * X_HEAD_DIM, out_dtype=BF16, seq=N_MEM, tm=N_MEM)
        x = _post_mixer(x.reshape(b, s, D_MODEL), o, w_out, p['norm_x'][i],
                        p['w_xq'][i].astype(BF16), kv.reshape(b, N_MEM, 2 * X_HEADS * X_HEAD_DIM),
                        p['w_xo'][i].astype(BF16), p['norm_mlp'][i], p['w_mlp_in'][i].astype(BF16),
                        p['w_mlp_out'][i].astype(BF16), p['final_norm'], i == DEPTH - 1,
                        tm_mlp, 1024).reshape(t, D_MODEL)
    return x.reshape(b, s, D_MODEL)


def kernel(x_prompt, x_sample, mem_prompt, mem_sample, norm_mix, norm_x, norm_mem, w_xq, w_xkv, w_xo, norm_mlp, w_mlp_in, w_mlp_out, a_w_in, a_w_out, b_w_in, b_q_norm, b_k_norm, b_w_out, c_w_in, c_lambda_q1, c_lambda_k1, c_lambda_q2, c_lambda_k2, c_sub_norm, c_w_out, d_w_in, d_q_norm, d_kv_norm, d_w_uq, d_w_ukv, d_w_out, final_norm):
    p = dict(norm_mix=norm_mix, norm_x=norm_x, norm_mem=norm_mem, w_xq=w_xq, w_xkv=w_xkv,
             w_xo=w_xo, norm_mlp=norm_mlp, w_mlp_in=w_mlp_in, w_mlp_out=w_mlp_out,
             a_w_in=a_w_in, a_w_out=a_w_out, b_w_in=b_w_in, b_q_norm=b_q_norm,
             b_k_norm=b_k_norm, b_w_out=b_w_out, c_w_in=c_w_in, c_lambda_q1=c_lambda_q1,
             c_lambda_k1=c_lambda_k1, c_lambda_q2=c_lambda_q2, c_lambda_k2=c_lambda_k2,
             c_sub_norm=c_sub_norm, c_w_out=c_w_out, d_w_in=d_w_in, d_q_norm=d_q_norm,
             d_kv_norm=d_kv_norm, d_w_uq=d_w_uq, d_w_ukv=d_w_ukv, d_w_out=d_w_out,
             final_norm=final_norm)
    return (_run_trunk(x_prompt, mem_prompt, p), _run_trunk(x_sample, mem_sample, p))
```

```python
import functools
import math

import numpy as np
import jax
import jax.numpy as jnp
from jax import lax
from jax.experimental import pallas as pl
from jax.experimental.pallas import tpu as pltpu

F32 = jnp.float32
BF16 = jnp.bfloat16

D_MODEL = 1024
DEPTH = 4
N_MEM = 256
GRID_W = 64
D_FF = 4 * D_MODEL
NORM_EPS = 1e-6
ROPE_THETA = 500000.0
AXIAL_THETA = 10000.0
NEG_INF = -1e30

A_PATTERNS = ((128, 1), (512, 4), (2048, 16))
A_GROUPS = 3
A_HEADS = 8
A_IN = A_GROUPS * 3 * A_HEADS * 64
A_HALF_WINDOW = 64
B_HEADS = 16
B_KV_HEADS = 4
C_HEADS = 8
D_HEADS = 16
D_Q_RANK = 384
D_KV_RANK = 256
D_NOPE = 64
D_ROPE = 32
X_HEADS = 4
X_HEAD_DIM = 128

LOG2E = 1.4426950408889634
ONES_ROWS = 16
SCORE_SLOTS = 4
CHUNKS_PER_TRIP = 4
LANES = 128
BLK = 2 * LANES
VMEM_LIMIT = 56 * 1024 * 1024


def _cparams(sem):
    return pltpu.CompilerParams(dimension_semantics=sem, vmem_limit_bytes=VMEM_LIMIT)


def _proj_kernel(tt_ref, src_ref, g_ref, w_ref, *rest, dnorm, rope, headnorm, add, nsub, vt, dil):
    del tt_ref
    rest = list(rest)
    c_ref = s_ref = hg_ref = bd_ref = add_ref = y_scr = None
    if dil > 1:
        y_scr = rest.pop()
    if rope:
        c_ref, s_ref = rest[0], rest[1]
        rest = rest[2:]
    if headnorm:
        hg_ref, bd_ref = rest[0], rest[1]
        rest = rest[2:]
    if add:
        add_ref = rest[0]
        rest = rest[1:]
    o_ref, h_scr = rest

    @pl.when(pl.program_id(1) == 0)
    def _():
        xf = src_ref[...].astype(F32)
        ms = jnp.sum(xf * xf, axis=-1, keepdims=True) * (1.0 / dnorm)
        h_scr[...] = (xf * lax.rsqrt(ms + NORM_EPS) * g_ref[...]).astype(BF16)

    y = jnp.dot(h_scr[...], w_ref[...], preferred_element_type=F32)
    if add:
        ad = add_ref[...]
        y = y + (ad if nsub == 1 else jnp.concatenate([ad] * nsub, axis=1))
    if headnorm:
        y2 = y * y
        hi = y2.astype(BF16)
        lo = (y2 - hi.astype(F32)).astype(BF16)
        parts = []
        for n in range(nsub):
            sl = slice(n * BLK, (n + 1) * BLK)
            ss = (jnp.dot(hi[:, sl], bd_ref[...], preferred_element_type=F32)
                  + jnp.dot(lo[:, sl], bd_ref[...], preferred_element_type=F32))
            parts.append(y[:, sl] * lax.rsqrt(ss * (1.0 / 64.0) + NORM_EPS))
        y = (parts[0] if nsub == 1 else jnp.concatenate(parts, axis=1)) * hg_ref[...]
    if rope:
        c = c_ref[0]
        s = s_ref[0]
        for n in range(nsub):
            y1 = y[:, n * BLK:n * BLK + LANES]
            y2 = y[:, n * BLK + LANES:(n + 1) * BLK]
            r1 = y1 * c - y2 * s
            r2 = y2 * c + y1 * s
            if dil > 1:
                y_scr[2 * n] = r1
                y_scr[2 * n + 1] = r2
            else:
                o_ref[:, n * BLK:n * BLK + LANES] = r1.astype(o_ref.dtype)
                o_ref[:, n * BLK + LANES:(n + 1) * BLK] = r2.astype(o_ref.dtype)
        if dil > 1:
            rows = y.shape[0] // dil
            for r in range(dil):
                for cb in range(2 * nsub):
                    o_ref[r, :, cb * LANES:(cb + 1) * LANES] = (
                        y_scr[cb, pl.ds(r, rows, stride=dil), :].astype(o_ref.dtype))
    elif vt:
        ngroups, nvg, nr, tk = vt
        ones = jnp.ones((ONES_ROWS, tk), o_ref.dtype)
        for cc in range(y.shape[0] // tk):
            yt = y[cc * tk:(cc + 1) * tk, :].T
            for g in range(ngroups):
                for vg in range(nvg):
                    src0 = (g * nvg + vg) * nr
                    r0 = vg * (nr + ONES_ROWS)
                    o_ref[g, cc, r0:r0 + nr, :] = yt[src0:src0 + nr, :].astype(o_ref.dtype)
                    o_ref[g, cc, r0 + nr:r0 + nr + ONES_ROWS, :] = ones
    else:
        o_ref[...] = y.astype(o_ref.dtype)


def _proj(src, src_cb, kdim, dnorm, gain, w, *, tn, out_dtype, seq, tm,
          tables=None, tt=None, headnorm=None, add=None, vt=None, dil=1):
    t = src.shape[0]
    n = w.shape[1]
    nj = n // tn
    out_shape = jax.ShapeDtypeStruct((t, n), out_dtype)
    out_spec = pl.BlockSpec((tm, tn), lambda i, j, tt_: (i, j))
    if vt is not None:
        ngroups, nvg, nr, tk = vt
        assert nj == 1 and n == ngroups * nvg * nr and tm % tk == 0
        wv = nvg * (nr + ONES_ROWS)
        out_shape = jax.ShapeDtypeStruct((t // seq, ngroups, seq // tk, wv, tk), out_dtype)
        out_spec = pl.BlockSpec((None, ngroups, tm // tk, wv, tk),
                                lambda i, j, tt_: (i // (seq // tm), 0, i % (seq // tm), 0, 0))
    scratch = [pltpu.VMEM((tm, kdim), BF16)]
    if dil > 1:
        assert tables is not None and tm % (16 * dil) == 0
        out_shape = jax.ShapeDtypeStruct((t // seq, dil, seq // dil, n), out_dtype)
        out_spec = pl.BlockSpec((None, dil, tm // dil, tn),
                                lambda i, j, tt_: (i // (seq // tm), 0, i % (seq // tm), j))
        scratch.append(pltpu.VMEM((tn // LANES, tm, LANES), F32))
    nsub = tn // BLK if (tables is not None or headnorm is not None) else 1
    ns = seq // tm
    if tt is None:
        tt = np.zeros((nj,), np.int32)
    in_specs = [
        pl.BlockSpec((tm, kdim), lambda i, j, tt_: (i, src_cb)),
        pl.BlockSpec((1, kdim), lambda i, j, tt_: (0, 0)),
        pl.BlockSpec((kdim, tn), lambda i, j, tt_: (0, j)),
    ]
    args = [src, gain.reshape(1, kdim).astype(F32), w]
    if tables is not None:
        for tb in tables:
            in_specs.append(pl.BlockSpec((1, tm, LANES), lambda i, j, tt_: (tt_[j], i % ns, 0)))
            args.append(tb)
    if headnorm is not None:
        in_specs.append(pl.BlockSpec((1, tn), lambda i, j, tt_: (0, j)))
        in_specs.append(pl.BlockSpec((BLK, BLK), lambda i, j, tt_: (0, 0)))
        args += [headnorm[0], headnorm[1]]
    if add is not None:
        add_arr, add_cb = add
        in_specs.append(pl.BlockSpec((tm, BLK), lambda i, j, tt_: (i, add_cb)))
        args.append(add_arr)
    kern = functools.partial(_proj_kernel, dnorm=dnorm, rope=tables is not None,
                             headnorm=headnorm is not None, add=add is not None, nsub=nsub, vt=vt,
                             dil=dil)
    return pl.pallas_call(
        kern,
        out_shape=out_shape,
        grid_spec=pltpu.PrefetchScalarGridSpec(
            num_scalar_prefetch=1,
            grid=(t // tm, nj),
            in_specs=in_specs,
            out_specs=out_spec,
            scratch_shapes=scratch,
        ),
        compiler_params=_cparams(("parallel", "arbitrary")),
    )(jnp.asarray(tt, jnp.int32), *args)


def _flash_kernel(*refs, nh, vgroups, tq, tk, nk, diff):
    if diff:
        (qmask_ref, q_ref, k_ref, vt_ref, lq1, lk1, lq2, lk2, sg_ref,
         o_ref, qm_scr, s_scr, mx_scr, m_scr, acc_scr) = refs
        lambda_init = diff
    else:
        qmask_ref, q_ref, k_ref, vt_ref, o_ref, qm_scr, s_scr, mx_scr, m_scr, acc_scr = refs

    q = q_ref[...].astype(F32)
    for j in range(nh):
        qm_scr[:, j * tq:(j + 1) * tq] = (q * qmask_ref[j:j + 1, :]).T.astype(BF16)
    m_scr[...] = jnp.full(m_scr.shape, NEG_INF, F32)
    acc_scr[...] = jnp.zeros(acc_scr.shape, F32)

    def scores(c, slot):
        ks = pl.multiple_of(c * tk, tk)
        st = jnp.dot(k_ref[pl.ds(ks, tk), :], qm_scr[...], preferred_element_type=F32)
        s_scr[slot] = st
        mx_scr[slot] = jnp.max(st, axis=0, keepdims=True)

    def accumulate(c, slot):
        vtc = vt_ref[c]
        m_prev = m_scr[...]
        m_new = jnp.maximum(m_prev, mx_scr[slot])
        alpha = jnp.exp2(m_prev - m_new)
        m_scr[...] = m_new
        pt = jnp.exp2(s_scr[slot] - m_new).astype(BF16)
        for gi, (r0, nr, h0, hn) in enumerate(vgroups):
            cols = slice(h0 * tq, (h0 + hn) * tq)
            acc_scr[gi] = acc_scr[gi] * alpha[:, cols] + jnp.dot(
                vtc[r0:r0 + nr + ONES_ROWS, :], pt[:, cols], preferred_element_type=F32)

    scores(0, 0)

    def body(i, carry):
        c = CHUNKS_PER_TRIP * i
        for u in range(CHUNKS_PER_TRIP):
            scores(jnp.minimum(c + u + 1, nk - 1), (u + 1) % SCORE_SLOTS)
            accumulate(c + u, u % SCORE_SLOTS)
        return carry

    lax.fori_loop(0, nk // CHUNKS_PER_TRIP, body, 0)

    pieces = []
    if diff:
        lam = (jnp.exp(jnp.sum(lq1[...] * lk1[...], axis=-1, keepdims=True))
               - jnp.exp(jnp.sum(lq2[...] * lk2[...], axis=-1, keepdims=True)) + lambda_init)
        for gi, (r0, nr, h0, hn) in enumerate(vgroups):
            acc = acc_scr[gi]
            linv = 1.0 / acc[nr:nr + 1, :]
            oh = acc[:nr, :tq] * linv[:, :tq] - lam * (acc[:nr, tq:] * linv[:, tq:])
            ms = jnp.mean(oh * oh, axis=0, keepdims=True)
            pieces.append(oh * lax.rsqrt(ms + NORM_EPS))
    else:
        for gi, (r0, nr, h0, hn) in enumerate(vgroups):
            acc = acc_scr[gi]
            on = acc[:nr, :] * (1.0 / acc[nr:nr + 1, :])
            for jj in range(hn):
                pieces.append(on[:, jj * tq:(jj + 1) * tq])
    ot = pieces[0] if len(pieces) == 1 else jnp.concatenate(pieces, axis=0)
    o = ot.T
    if diff:
        o = o * sg_ref[...] * (1.0 - lambda_init)
    o_ref[...] = o.astype(o_ref.dtype)


def _flash(q, k, vt, *, qcol0, kcol0, ngroups, wq, wv, wo, nh, vgroups, qmask, tq, tk,
           diff=None, diff_params=None):
    b, s = q.shape[0], q.shape[1]
    nk = s // tk
    assert s % tk == 0 and nk % CHUNKS_PER_TRIP == 0 and s % tq == 0
    in_specs = [
        pl.BlockSpec((nh, wq), lambda bi, g, i: (0, 0)),
        pl.BlockSpec((None, tq, wq), lambda bi, g, i: (bi, i, qcol0 + g)),
        pl.BlockSpec((None, s, wq), lambda bi, g, i: (bi, 0, kcol0 + g)),
        pl.BlockSpec((None, None, nk, wv, tk), lambda bi, g, i: (bi, g, 0, 0, 0)),
    ]
    args = [qmask, q, k, vt]
    if diff is not None:
        for prm in diff_params[:4]:
            in_specs.append(pl.BlockSpec((1, 64), lambda bi, g, i: (0, 0)))
            args.append(prm.reshape(1, 64).astype(F32))
        in_specs.append(pl.BlockSpec((1, wo), lambda bi, g, i: (0, 0)))
        args.append(jnp.tile(diff_params[4].astype(F32), wo // LANES).reshape(1, wo))
    kern = functools.partial(_flash_kernel, nh=nh, vgroups=tuple(vgroups), tq=tq, tk=tk, nk=nk,
                             diff=diff)
    nr, hn = vgroups[0][1], vgroups[0][3]
    return pl.pallas_call(
        kern,
        out_shape=jax.ShapeDtypeStruct((b, s, ngroups * wo), BF16),
        grid=(b, ngroups, s // tq),
        in_specs=in_specs,
        out_specs=pl.BlockSpec((None, tq, wo), lambda bi, g, i: (bi, i, g)),
        scratch_shapes=[
            pltpu.VMEM((wq, nh * tq), BF16),
            pltpu.VMEM((SCORE_SLOTS, tk, nh * tq), F32),
            pltpu.VMEM((SCORE_SLOTS, 1, nh * tq), F32),
            pltpu.VMEM((1, nh * tq), F32),
            pltpu.VMEM((len(vgroups), nr + ONES_ROWS, hn * tq), F32),
        ],
        compiler_params=_cparams(("parallel", "parallel", "arbitrary")),
    )(*args)


def _band_kernel(qmask_ref, vmask_ref, q_ref, k_ref, v_ref, o_ref, lse_ref, *, tq, win, length):
    nh = 4
    i = pl.program_id(2)
    ks = jnp.clip(i * tq - A_HALF_WINDOW, 0, length - win)
    ks = pl.multiple_of(ks, A_HALF_WINDOW)
    kc = k_ref[pl.ds(ks, win), :]
    vc = v_ref[pl.ds(ks, win), :]
    q = q_ref[...]
    qm = jnp.concatenate([q * qmask_ref[j:j + 1, :] for j in range(nh)], axis=0)
    s = lax.dot_general(qm, kc, (((1,), (1,)), ((), ())), preferred_element_type=F32)
    qpos = i * tq + lax.broadcasted_iota(jnp.int32, (tq, win), 0)
    kpos = ks + lax.broadcasted_iota(jnp.int32, (tq, win), 1)
    valid = jnp.abs(qpos - kpos) <= A_HALF_WINDOW
    vmask = vmask_ref[...]
    vmask_b = vmask.astype(BF16)
    ps = []
    inv = None
    lse = None
    for j in range(nh):
        sj = jnp.where(valid, s[j * tq:(j + 1) * tq], NEG_INF)
        mj = jnp.max(sj, axis=-1, keepdims=True)
        pj = jnp.exp(sj - mj)
        lj = jnp.sum(pj, axis=-1, keepdims=True)
        ps.append(pj.astype(BF16))
        t_inv = (1.0 / lj) * vmask[j:j + 1, :]
        t_lse = (mj + jnp.log(lj)) * vmask[j:j + 1, :]
        inv = t_inv if inv is None else inv + t_inv
        lse = t_lse if lse is None else lse + t_lse
    lhs = jnp.concatenate(ps, axis=1)
    rhs = jnp.concatenate([vc * vmask_b[j:j + 1, :] for j in range(nh)], axis=0)
    pv = jnp.dot(lhs, rhs, preferred_element_type=F32)
    o_ref[...] = (pv * inv).astype(o_ref.dtype)
    lse_ref[...] = lse


def _band_attention(qkv, qmask, vmask):
    b, dil, length, _ = qkv.shape
    tq = min(256, length // 2)
    win = tq + 2 * A_HALF_WINDOW

    def col(which):
        return lambda bi, a, i: (bi, a // 2, 0, which * 2 + a % 2)

    in_specs = [
        pl.BlockSpec((4, BLK), lambda bi, a, i: (0, 0)),
        pl.BlockSpec((4, BLK), lambda bi, a, i: (0, 0)),
        pl.BlockSpec((None, None, tq, BLK), lambda bi, a, i: (bi, a // 2, i, a % 2)),
        pl.BlockSpec((None, None, length, BLK), col(1)),
        pl.BlockSpec((None, None, length, BLK), col(2)),
    ]
    kern = functools.partial(_band_kernel, tq=tq, win=win, length=length)
    out_block = pl.BlockSpec((None, None, tq, BLK), lambda bi, a, i: (bi, a // 2, i, a % 2))
    return pl.pallas_call(
        kern,
        out_shape=(jax.ShapeDtypeStruct((b, dil, length, 2 * BLK), BF16),
                   jax.ShapeDtypeStruct((b, dil, length, 2 * BLK), F32)),
        grid=(b, dil * 2, length // tq),
        in_specs=in_specs,
        out_specs=(out_block, out_block),
        compiler_params=_cparams(("parallel", "parallel", "arbitrary")),
    )(qmask, vmask, qkv, qkv, qkv)


def _a_out_kernel(*refs, dils, tm):
    ng = len(dils)
    o_refs, l_refs = refs[:ng], refs[ng:2 * ng]
    w_ref, x_ref, out_ref = refs[2 * ng:2 * ng + 3]
    scr = list(refs[2 * ng + 3:])

    def token_order(ref, dil):
        if dil == 1:
            return ref[0].astype(F32)
        buf = scr.pop(0)
        ncb = buf.shape[0]
        for r in range(dil):
            v = ref[r].astype(F32)
            for cb in range(ncb):
                buf[cb, pl.ds(r, tm // dil, stride=dil), :] = v[:, cb * LANES:(cb + 1) * LANES]
        return jnp.concatenate([buf[cb] for cb in range(ncb)], axis=1)

    ls = [token_order(l_refs[g], dils[g]) for g in range(ng)]
    os_ = [token_order(o_refs[g], dils[g]) for g in range(ng)]
    mx = functools.reduce(jnp.maximum, ls)
    es = [jnp.exp(l - mx) for l in ls]
    inv = 1.0 / functools.reduce(jnp.add, es)
    o = functools.reduce(jnp.add, [e * og for e, og in zip(es, os_)]) * inv
    out_ref[...] = x_ref[...] + jnp.dot(o.astype(BF16), w_ref[...], preferred_element_type=F32)


def _a_out(os_, lses, w, x3, tm):
    b, s, _ = x3.shape
    kd = w.shape[0]
    dils = tuple(o.shape[1] for o in os_)
    grp = [pl.BlockSpec((None, d, tm // d, kd), lambda bi, i: (bi, 0, i, 0)) for d in dils]
    row = pl.BlockSpec((None, tm, D_MODEL), lambda bi, i: (bi, i, 0))
    nscr = 2 * sum(1 for d in dils if d > 1)
    return pl.pallas_call(
        functools.partial(_a_out_kernel, dils=dils, tm=tm),
        out_shape=jax.ShapeDtypeStruct((b, s, D_MODEL), F32),
        grid=(b, s // tm),
        in_specs=grp + grp + [pl.BlockSpec((kd, D_MODEL), lambda bi, i: (0, 0)), row],
        out_specs=row,
        scratch_shapes=[pltpu.VMEM((kd // LANES, tm, LANES), F32)] * nscr,
        compiler_params=_cparams(("parallel", "parallel")),
    )(*os_, *lses, w, x3)


def _post_kernel(*refs, has_proj, final):
    if has_proj:
        (x_ref, o_ref, wout_ref, gx_ref, wq_ref, kv_ref, wo_ref, gm_ref, w1_ref, w2_ref, fg_ref,
         out_ref, x2_scr, h_scr, acc_scr) = refs
    else:
        (x_ref, gx_ref, wq_ref, kv_ref, wo_ref, gm_ref, w1_ref, w2_ref, fg_ref,
         out_ref, x2_scr, h_scr, acc_scr) = refs
    f = pl.program_id(2)

    @pl.when(f == 0)
    def _():
        x1 = x_ref[...]
        if has_proj:
            x1 = x1 + jnp.dot(o_ref[...], wout_ref[...], preferred_element_type=F32)
        ms = jnp.mean(x1 * x1, axis=-1, keepdims=True)
        h = (x1 * lax.rsqrt(ms + NORM_EPS) * gx_ref[...]).astype(BF16)
        qb = (jnp.dot(h, wq_ref[...], preferred_element_type=F32) * (X_HEAD_DIM ** -0.5)).astype(BF16)
        hd = X_HEADS * X_HEAD_DIM
        outs = []
        for hh in range(X_HEADS):
            qh = qb[:, hh * X_HEAD_DIM:(hh + 1) * X_HEAD_DIM]
            kh = kv_ref[:, hh * X_HEAD_DIM:(hh + 1) * X_HEAD_DIM]
            vh = kv_ref[:, hd + hh * X_HEAD_DIM:hd + (hh + 1) * X_HEAD_DIM]
            s = lax.dot_general(qh, kh, (((1,), (1,)), ((), ())), preferred_element_type=F32)
            m = jnp.max(s, axis=-1, keepdims=True)
            p = jnp.exp(s - m)
            l = jnp.sum(p, axis=-1, keepdims=True)
            oh = jnp.dot(p.astype(BF16), vh, preferred_element_type=F32) * (1.0 / l)
            outs.append(oh.astype(BF16))
        x2 = x1 + jnp.dot(jnp.concatenate(outs, axis=1), wo_ref[...], preferred_element_type=F32)
        x2_scr[...] = x2
        ms2 = jnp.mean(x2 * x2, axis=-1, keepdims=True)
        h_scr[...] = (x2 * lax.rsqrt(ms2 + NORM_EPS) * gm_ref[...]).astype(BF16)
        acc_scr[...] = jnp.zeros(acc_scr.shape, F32)

    a = jnp.maximum(jnp.dot(h_scr[...], w1_ref[...], preferred_element_type=F32), 0.0)
    acc_scr[...] += jnp.dot((a * a).astype(BF16), w2_ref[...], preferred_element_type=F32)

    @pl.when(f == pl.num_programs(2) - 1)
    def _():
        y = x2_scr[...] + acc_scr[...]
        if final:
            ms = jnp.mean(y * y, axis=-1, keepdims=True)
            y = y * lax.rsqrt(ms + NORM_EPS) * fg_ref[...]
        out_ref[...] = y


def _post_mixer(x3, o3, w_out, gx, wq, kv3, wo, gm, w1, w2, final_gain, final, tm, tf):
    b, s, _ = x3.shape
    hd = X_HEADS * X_HEAD_DIM
    has_proj = o3 is not None
    const = lambda bi, i, f: (0, 0)
    row = lambda bi, i, f: (bi, i, 0)
    in_specs = [pl.BlockSpec((None, tm, D_MODEL), row)]
    args = [x3]
    if has_proj:
        kd = o3.shape[-1]
        in_specs += [pl.BlockSpec((None, tm, kd), row), pl.BlockSpec((kd, D_MODEL), const)]
        args += [o3, w_out]
    in_specs += [pl.BlockSpec((1, D_MODEL), const),
                 pl.BlockSpec((D_MODEL, hd), const),
                 pl.BlockSpec((None, N_MEM, 2 * hd), lambda bi, i, f: (bi, 0, 0)),
                 pl.BlockSpec((hd, D_MODEL), const),
                 pl.BlockSpec((1, D_MODEL), const),
                 pl.BlockSpec((D_MODEL, tf), lambda bi, i, f: (0, f)),
                 pl.BlockSpec((tf, D_MODEL), lambda bi, i, f: (f, 0)),
                 pl.BlockSpec((1, D_MODEL), const)]
    args += [gx.reshape(1, D_MODEL), wq, kv3, wo, gm.reshape(1, D_MODEL), w1, w2,
             final_gain.reshape(1, D_MODEL)]
    return pl.pallas_call(
        functools.partial(_post_kernel, has_proj=has_proj, final=final),
        out_shape=jax.ShapeDtypeStruct((b, s, D_MODEL), F32),
        grid=(b, s // tm, D_FF // tf),
        in_specs=in_specs,
        out_specs=pl.BlockSpec((None, tm, D_MODEL), row),
        scratch_shapes=[pltpu.VMEM((tm, D_MODEL), F32), pltpu.VMEM((tm, D_MODEL), BF16),
                        pltpu.VMEM((tm, D_MODEL), F32)],
        compiler_params=_cparams(("parallel", "parallel", "arbitrary")),
    )(*args)


_F_ROT16 = np.array(list(range(0, 8)) + list(range(16, 40)))
_P_ROT16 = np.array(list(range(8, 16)) + list(range(40, 64)))
_F_AXIAL = np.array(list(range(0, 16)) + list(range(32, 48)))
_P_AXIAL = np.array(list(range(16, 32)) + list(range(48, 64)))


def _block_dims(first, partner):
    lane = np.arange(BLK)
    half, slot, u = lane // LANES, (lane % LANES) // 32, lane % 32
    return np.where(half == 0, first[u], partner[u]), slot


def _slot_masks(nslot, slot_of_lane):
    return np.stack([(slot_of_lane == j) for j in range(nslot)]).astype(np.float32)


def _rope_tables(pos_list, theta, rot, scale_list, npad):
    half = rot // 2
    inv_freq = jnp.exp(jnp.arange(half, dtype=F32) * (-2.0 * math.log(theta) / rot))
    cs, ss = [], []
    for pos in pos_list:
        ang = pos.astype(F32)[:, None] * inv_freq[None, :]
        cs.append(jnp.cos(ang))
        ss.append(jnp.sin(ang))
    c = jnp.concatenate(cs, axis=1)
    s = jnp.concatenate(ss, axis=1)
    n = c.shape[0]
    if npad:
        c = jnp.concatenate([c, jnp.ones((n, npad), F32)], axis=1)
        s = jnp.concatenate([s, jnp.zeros((n, npad), F32)], axis=1)
    reps = LANES // c.shape[1]
    c = jnp.tile(c, (1, reps))
    s = jnp.tile(s, (1, reps))
    ctab = [c * sc for sc in scale_list] + [jnp.ones_like(c)]
    stab = [s * sc for sc in scale_list] + [jnp.zeros_like(s)]
    return jnp.stack(ctab), jnp.stack(stab)


def _mixer_a(x, b, s, gain, w_in, w_out, pos, tm):
    dims, slot = _block_dims(_F_ROT16, _P_ROT16)
    ctab, stab = _rope_tables([pos], ROPE_THETA, 16, [0.125, 1.0], 24)
    qmask = jnp.asarray(_slot_masks(4, slot), BF16)
    vmask = jnp.asarray(_slot_masks(4, np.arange(BLK) // 64), F32)
    w_bf = w_in.astype(BF16)
    outs, lses = [], []
    for wg, (window, dil) in enumerate(A_PATTERNS):
        assert window // (2 * dil) == A_HALF_WINDOW
        cols = []
        for which in range(3):
            base = (wg * 3 + which) * A_HEADS * 64
            for hg in range(2):
                cols.append(base + hg * BLK + (slot * 64 + dims if which < 2 else np.arange(BLK)))
        w = w_bf[:, np.concatenate(cols)]
        qkv = _proj(x, 0, D_MODEL, D_MODEL, gain, w, tn=2 * BLK, out_dtype=BF16, seq=s, tm=tm,
                    tables=(ctab, stab), tt=np.array([0, 1, 2], np.int32), dil=dil)
        o, lse = _band_attention(qkv.reshape(b, dil, s // dil, 3 * 2 * BLK), qmask, vmask)
        outs.append(o)
        lses.append(lse)
    x3 = _a_out(outs, lses, w_out.astype(BF16), x.reshape(b, s, D_MODEL), min(tm, 512))
    return x3.reshape(b * s, D_MODEL), None, None


def _mixer_b(x, b, s, gain, w_in, q_gain, k_gain, w_out, rows, cols_pos, tm, tq, tk):
    dims, slot = _block_dims(_F_AXIAL, _P_AXIAL)
    qcols = np.concatenate([g * BLK + slot * 64 + dims for g in range(B_KV_HEADS)])
    kcols = np.concatenate([B_HEADS * 64 + g * 64 + dims for g in range(B_KV_HEADS)])
    w_bf = w_in.astype(BF16)
    wqk = w_bf[:, np.concatenate([qcols, kcols])]
    wv = w_bf[:, (B_HEADS + B_KV_HEADS) * 64:]
    hgain = jnp.concatenate([jnp.tile(q_gain[dims] * (0.125 * LOG2E), B_KV_HEADS),
                             jnp.tile(k_gain[dims], B_KV_HEADS)]).reshape(1, -1).astype(F32)
    bd = jnp.asarray(slot[:, None] == slot[None, :], BF16)
    ctab, stab = _rope_tables([rows, cols_pos], AXIAL_THETA, 32, [1.0], 0)
    n = B_KV_HEADS * BLK
    qk = _proj(x, 0, D_MODEL, D_MODEL, gain, wqk, tn=2 * BLK, out_dtype=BF16, seq=s, tm=tm,
               tables=(ctab, stab), tt=np.zeros((n // BLK,), np.int32), headnorm=(hgain, bd))
    vt = _proj(x, 0, D_MODEL, D_MODEL, gain, wv, tn=B_KV_HEADS * 64, out_dtype=BF16, seq=s, tm=tm,
               vt=(B_KV_HEADS, 1, 64, tk))
    qk = qk.reshape(b, s, 2 * n)
    qmask = jnp.asarray(_slot_masks(4, slot), BF16)
    o = _flash(qk, qk, vt, qcol0=0, kcol0=B_KV_HEADS, ngroups=B_KV_HEADS, wq=BLK,
               wv=64 + ONES_ROWS, wo=BLK, nh=4, vgroups=((0, 64, 0, 4),), qmask=qmask, tq=tq, tk=tk)
    return x, o, w_out.astype(BF16)


def _mixer_c(x, b, s, gain, w_in, lq1, lk1, lq2, lk2, sub_gain, w_out, pos, lambda_init, tm, tq, tk):
    dims, slot = _block_dims(_F_ROT16, _P_ROT16)
    ngr = C_HEADS // 2
    cols = []
    for which in range(2):
        for g in range(ngr):
            cols.append(which * C_HEADS * 128 + g * BLK + slot * 64 + dims)
    w_bf = w_in.astype(BF16)
    ctab, stab = _rope_tables([pos], ROPE_THETA, 16, [0.125 * LOG2E, 1.0], 24)
    n = ngr * BLK
    qk = _proj(x, 0, D_MODEL, D_MODEL, gain, w_bf[:, np.concatenate(cols)], tn=2 * BLK, out_dtype=BF16,
               seq=s, tm=tm, tables=(ctab, stab), tt=np.array([0, 0, 1, 1], np.int32))
    qkv = qk.reshape(b, s, 2 * n)
    vt = _proj(x, 0, D_MODEL, D_MODEL, gain, w_bf[:, 2 * C_HEADS * 128:], tn=n, out_dtype=BF16,
               seq=s, tm=tm, vt=(ngr, 2, LANES, tk))
    qmask = jnp.asarray(_slot_masks(4, slot), BF16)
    o = _flash(qkv, qkv, vt, qcol0=0, kcol0=ngr, ngroups=ngr, wq=BLK, wv=2 * (LANES + ONES_ROWS),
               wo=BLK, nh=4, vgroups=((0, LANES, 0, 2), (LANES + ONES_ROWS, LANES, 2, 2)),
               qmask=qmask, tq=tq, tk=tk,
               diff=lambda_init, diff_params=(lq1, lk1, lq2, lk2, sub_gain))
    return x, o, w_out.astype(BF16)


def _mixer_d(x, b, s, gain, w_in, q_gain, kv_gain, w_uq, w_ukv, w_out, pos, tm, tq, tk):
    lane = np.arange(LANES)
    slot_h = np.where(lane < 32, 0, np.where(lane < 64, 1, np.where(lane < 80, 0, np.where(lane < 96, 1, -1))))
    slot = np.concatenate([slot_h, slot_h])
    nope_lane = lane < 64
    rope_lane = (lane >= 64) & (lane < 96)
    ngr = D_HEADS // 2

    w1 = jnp.zeros((D_MODEL, 4 * BLK), F32)
    w1 = w1.at[:, :D_Q_RANK].set(w_in[:, :D_Q_RANK])
    w1 = w1.at[:, 2 * BLK:3 * BLK].set(w_in[:, D_Q_RANK:D_Q_RANK + D_KV_RANK])
    kr_src = np.zeros((BLK,), np.int64)
    kr_on = np.zeros((BLK,), bool)
    for hf in range(2):
        for l in range(LANES):
            if rope_lane[l]:
                kr_src[hf * LANES + l] = D_Q_RANK + D_KV_RANK + hf * 16 + (l - 64) % 16
                kr_on[hf * LANES + l] = True
    w1 = w1.at[:, 3 * BLK:].set(jnp.where(jnp.asarray(kr_on)[None, :], w_in[:, kr_src], 0.0))
    cmb = _proj(x, 0, D_MODEL, D_MODEL, gain, w1.astype(BF16), tn=4 * BLK, out_dtype=F32, seq=s, tm=tm)

    qsrc = np.zeros((ngr * BLK,), np.int64)
    qon = np.zeros((ngr * BLK,), bool)
    ksrc = np.zeros((ngr * BLK,), np.int64)
    kon = np.zeros((ngr * BLK,), bool)
    for g in range(ngr):
        for hf in range(2):
            for l in range(LANES):
                idx = g * BLK + hf * LANES + l
                if slot_h[l] < 0:
                    continue
                head = 2 * g + slot_h[l]
                if nope_lane[l]:
                    d = hf * 32 + l % 32
                    qsrc[idx], qon[idx] = head * 96 + d, True
                    ksrc[idx], kon[idx] = head * 128 + d, True
                else:
                    d = hf * 16 + (l - 64) % 16
                    qsrc[idx], qon[idx] = head * 96 + D_NOPE + d, True
    wq2 = jnp.where(jnp.asarray(qon)[None, :], w_uq[:, qsrc], 0.0)
    wq2 = jnp.concatenate([wq2, jnp.zeros((2 * BLK - D_Q_RANK, ngr * BLK), F32)], axis=0).astype(BF16)
    wk2 = jnp.where(jnp.asarray(kon)[None, :], w_ukv[:, ksrc], 0.0).astype(BF16)
    vsrc = np.concatenate([h * 128 + D_NOPE + np.arange(64) for h in range(D_HEADS)])
    wv2 = w_ukv[:, vsrc].astype(BF16)
    qg = jnp.concatenate([q_gain, jnp.zeros((2 * BLK - D_Q_RANK,), F32)])

    half = D_ROPE // 2
    inv_freq = jnp.exp(jnp.arange(half, dtype=F32) * (-2.0 * math.log(ROPE_THETA) / D_ROPE))
    ang = pos.astype(F32)[:, None] * inv_freq[None, :]
    ones64 = jnp.ones((s, 64), F32)
    pad32 = jnp.ones((s, 32), F32)
    c = jnp.concatenate([ones64, jnp.cos(ang), jnp.cos(ang), pad32], axis=1)
    sn = jnp.concatenate([0.0 * ones64, jnp.sin(ang), jnp.sin(ang), 0.0 * pad32], axis=1)
    qs = (D_NOPE + D_ROPE) ** -0.5 * LOG2E
    ctab = jnp.stack([c * qs, c])
    stab = jnp.stack([sn * qs, sn])

    q = _proj(cmb, 0, 2 * BLK, D_Q_RANK, qg, wq2, tn=2 * BLK, out_dtype=BF16, seq=s, tm=tm,
              tables=(ctab, stab), tt=np.zeros((ngr // 2,), np.int32))
    k = _proj(cmb, 2, BLK, D_KV_RANK, kv_gain, wk2, tn=4 * BLK, out_dtype=BF16, seq=s, tm=tm,
              tables=(ctab, stab), tt=np.ones((ngr // 4,), np.int32), add=(cmb, 3))
    vt = _proj(cmb, 2, BLK, D_KV_RANK, kv_gain, wv2, tn=4 * BLK, out_dtype=BF16, seq=s, tm=tm,
               vt=(ngr, 2, 64, tk))
    q = q.reshape(b, s, ngr * BLK)
    k = k.reshape(b, s, ngr * BLK)
    qmask = jnp.asarray(_slot_masks(2, slot), BF16)
    o = _flash(q, k, vt, qcol0=0, kcol0=0, ngroups=ngr, wq=BLK, wv=2 * (64 + ONES_ROWS), wo=LANES,
               nh=2, vgroups=((0, 64, 0, 1), (64 + ONES_ROWS, 64, 1, 1)), qmask=qmask, tq=tq, tk=tk)
    return x, o, w_out.astype(BF16)


def _run_trunk(x3, mem3, p):
    b, s, _ = x3.shape
    t = b * s
    tm = 1024
    tm_mlp = 512
    tq, tq_mla, tk = 256, 512, 512
    x = x3.reshape(t, D_MODEL)
    pos = jnp.arange(s, dtype=F32)
    rows = jnp.repeat(jnp.arange(s // GRID_W, dtype=F32), GRID_W)
    cols_pos = jnp.tile(jnp.arange(GRID_W, dtype=F32), s // GRID_W)
    memf = mem3.reshape(b * N_MEM, D_MODEL)
    for i in range(DEPTH):
        m, j = i % 4, i // 4
        g = p['norm_mix'][i]
        if m == 0:
            x, o, w_out = _mixer_a(x, b, s, g, p['a_w_in'][j], p['a_w_out'][j], pos, tm)
        elif m == 1:
            x, o, w_out = _mixer_b(x, b, s, g, p['b_w_in'][j], p['b_q_norm'][j], p['b_k_norm'][j],
                                   p['b_w_out'][j], rows, cols_pos, tm, tq, tk)
        elif m == 2:
            x, o, w_out = _mixer_c(x, b, s, g, p['c_w_in'][j], p['c_lambda_q1'][j],
                                   p['c_lambda_k1'][j], p['c_lambda_q2'][j], p['c_lambda_k2'][j],
                                   p['c_sub_norm'][j], p['c_w_out'][j], pos,
                                   0.8 - 0.6 * math.exp(-0.3 * i), tm, tq, tk)
        else:
            x, o, w_out = _mixer_d(x, b, s, g, p['d_w_in'][j], p['d_q_norm'][j], p['d_kv_norm'][j],
                                   p['d_w_uq'][j], p['d_w_ukv'][j], p['d_w_out'][j], pos, tm,
                                   tq_mla, tk)
        kv = _proj(memf, 0, D_MODEL, D_MODEL, p['norm_mem'][i], p['w_xkv'][i].astype(BF16),
                   tn=2 * X_HEADS * X_HEAD_DIM, out_dtype=BF16, seq=N_MEM, tm=N_MEM)
        x = _post_mixer(x.reshape(b, s, D_MODEL), o, w_out, p['norm_x'][i],
                        p['w_xq'][i].astype(BF16), kv.reshape(b, N_MEM, 2 * X_HEADS * X_HEAD_DIM),
                        p['w_xo'][i].astype(BF16), p['norm_mlp'][i], p['w_mlp_in'][i].astype(BF16),
                        p['w_mlp_out'][i].astype(BF16), p['final_norm'], i == DEPTH - 1,
                        tm_mlp, 1024).reshape(t, D_MODEL)
    return x.reshape(b, s, D_MODEL)


def kernel(x_prompt, x_sample, mem_prompt, mem_sample, norm_mix, norm_x, norm_mem, w_xq, w_xkv, w_xo, norm_mlp, w_mlp_in, w_mlp_out, a_w_in, a_w_out, b_w_in, b_q_norm, b_k_norm, b_w_out, c_w_in, c_lambda_q1, c_lambda_k1, c_lambda_q2, c_lambda_k2, c_sub_norm, c_w_out, d_w_in, d_q_norm, d_kv_norm, d_w_uq, d_w_ukv, d_w_out, final_norm):
    p = dict(norm_mix=norm_mix, norm_x=norm_x, norm_mem=norm_mem, w_xq=w_xq, w_xkv=w_xkv,
             w_xo=w_xo, norm_mlp=norm_mlp, w_mlp_in=w_mlp_in, w_mlp_out=w_mlp_out,
             a_w_in=a_w_in, a_w_out=a_w_out, b_w_in=b_w_in, b_q_norm=b_q_norm,
             b_k_norm=b_k_norm, b_w_out=b_w_out, c_w_in=c_w_in, c_lambda_q1=c_lambda_q1,
             c_lambda_k1=c_lambda_k1, c_lambda_q2=c_lambda_q2, c_lambda_k2=c_lambda_k2,
             c_sub_norm=c_sub_norm, c_w_out=c_w_out, d_w_in=d_w_in, d_q_norm=d_q_norm,
             d_kv_norm=d_kv_norm, d_w_uq=d_w_uq, d_w_ukv=d_w_ukv, d_w_out=d_w_out,
             final_norm=final_norm)
    return (_run_trunk(x_prompt, mem_prompt, p), _run_trunk(x_sample, mem_sample, p))
```

```python
import functools
import math

import numpy as np
import jax
import jax.numpy as jnp
from jax import lax
from jax.experimental import pallas as pl
from jax.experimental.pallas import tpu as pltpu

F32 = jnp.float32
BF16 = jnp.bfloat16

D_MODEL = 1024
DEPTH = 4
N_MEM = 256
GRID_W = 64
D_FF = 4 * D_MODEL
NORM_EPS = 1e-6
ROPE_THETA = 500000.0
AXIAL_THETA = 10000.0
NEG_INF = -1e30

A_PATTERNS = ((128, 1), (512, 4), (2048, 16))
A_GROUPS = 3
A_HEADS = 8
A_IN = A_GROUPS * 3 * A_HEADS * 64
A_HALF_WINDOW = 64
B_HEADS = 16
B_KV_HEADS = 4
C_HEADS = 8
D_HEADS = 16
D_Q_RANK = 384
D_KV_RANK = 256
D_NOPE = 64
D_ROPE = 32
X_HEADS = 4
X_HEAD_DIM = 128

LOG2E = 1.4426950408889634
BF16_ROWS = 16
SCORE_SLOTS = 4
CHUNKS_PER_TRIP = 4
LANES = 128
BLK = 2 * LANES
VMEM_LIMIT = 56 * 1024 * 1024


def _ones_rows(nr):
    return LANES - nr if nr < LANES else BF16_ROWS


def _cparams(sem):
    return pltpu.CompilerParams(dimension_semantics=sem, vmem_limit_bytes=VMEM_LIMIT)


def _proj_kernel(tt_ref, src_ref, g_ref, w_ref, *rest, dnorm, rope, headnorm, add, nsub, vt, dil):
    del tt_ref
    rest = list(rest)
    c_ref = s_ref = hg_ref = bd_ref = add_ref = y_scr = None
    if dil > 1:
        y_scr = rest.pop()
    if rope:
        c_ref, s_ref = rest[0], rest[1]
        rest = rest[2:]
    if headnorm:
        hg_ref, bd_ref = rest[0], rest[1]
        rest = rest[2:]
    if add:
        add_ref = rest[0]
        rest = rest[1:]
    o_ref, h_scr = rest

    @pl.when(pl.program_id(1) == 0)
    def _():
        xf = src_ref[...].astype(F32)
        ms = jnp.sum(xf * xf, axis=-1, keepdims=True) * (1.0 / dnorm)
        h_scr[...] = (xf * lax.rsqrt(ms + NORM_EPS) * g_ref[...]).astype(BF16)

    y = jnp.dot(h_scr[...], w_ref[...], preferred_element_type=F32)
    if add:
        ad = add_ref[...]
        y = y + (ad if nsub == 1 else jnp.concatenate([ad] * nsub, axis=1))
    if headnorm:
        y2 = y * y
        hi = y2.astype(BF16)
        lo = (y2 - hi.astype(F32)).astype(BF16)
        parts = []
        for n in range(nsub):
            sl = slice(n * BLK, (n + 1) * BLK)
            ss = (jnp.dot(hi[:, sl], bd_ref[...], preferred_element_type=F32)
                  + jnp.dot(lo[:, sl], bd_ref[...], preferred_element_type=F32))
            parts.append(y[:, sl] * lax.rsqrt(ss * (1.0 / 64.0) + NORM_EPS))
        y = (parts[0] if nsub == 1 else jnp.concatenate(parts, axis=1)) * hg_ref[...]
    if rope:
        c = c_ref[0]
        s = s_ref[0]
        for n in range(nsub):
            y1 = y[:, n * BLK:n * BLK + LANES]
            y2 = y[:, n * BLK + LANES:(n + 1) * BLK]
            r1 = y1 * c - y2 * s
            r2 = y2 * c + y1 * s
            if dil > 1:
                y_scr[2 * n] = r1
                y_scr[2 * n + 1] = r2
            else:
                o_ref[:, n * BLK:n * BLK + LANES] = r1.astype(o_ref.dtype)
                o_ref[:, n * BLK + LANES:(n + 1) * BLK] = r2.astype(o_ref.dtype)
        if dil > 1:
            rows = y.shape[0] // dil
            for r in range(dil):
                for cb in range(2 * nsub):
                    o_ref[r, :, cb * LANES:(cb + 1) * LANES] = (
                        y_scr[cb, pl.ds(r, rows, stride=dil), :].astype(o_ref.dtype))
    elif vt:
        ngroups, nvg, nr, tk = vt
        nones = _ones_rows(nr)
        ones = jnp.ones((nones, tk), o_ref.dtype)
        for cc in range(y.shape[0] // tk):
            yt = y[cc * tk:(cc + 1) * tk, :].T
            for g in range(ngroups):
                for vg in range(nvg):
                    src0 = (g * nvg + vg) * nr
                    r0 = vg * (nr + nones)
                    o_ref[g, cc, r0:r0 + nr, :] = yt[src0:src0 + nr, :].astype(o_ref.dtype)
                    o_ref[g, cc, r0 + nr:r0 + nr + nones, :] = ones
    else:
        o_ref[...] = y.astype(o_ref.dtype)


def _proj(src, src_cb, kdim, dnorm, gain, w, *, tn, out_dtype, seq, tm,
          tables=None, tt=None, headnorm=None, add=None, vt=None, dil=1):
    t = src.shape[0]
    n = w.shape[1]
    nj = n // tn
    out_shape = jax.ShapeDtypeStruct((t, n), out_dtype)
    out_spec = pl.BlockSpec((tm, tn), lambda i, j, tt_: (i, j))
    if vt is not None:
        ngroups, nvg, nr, tk = vt
        assert nj == 1 and n == ngroups * nvg * nr and tm % tk == 0
        wv = nvg * (nr + _ones_rows(nr))
        out_shape = jax.ShapeDtypeStruct((t // seq, ngroups, seq // tk, wv, tk), out_dtype)
        out_spec = pl.BlockSpec((None, ngroups, tm // tk, wv, tk),
                                lambda i, j, tt_: (i // (seq // tm), 0, i % (seq // tm), 0, 0))
    scratch = [pltpu.VMEM((tm, kdim), BF16)]
    if dil > 1:
        assert tables is not None and tm % (16 * dil) == 0
        out_shape = jax.ShapeDtypeStruct((t // seq, dil, seq // dil, n), out_dtype)
        out_spec = pl.BlockSpec((None, dil, tm // dil, tn),
                                lambda i, j, tt_: (i // (seq // tm), 0, i % (seq // tm), j))
        scratch.append(pltpu.VMEM((tn // LANES, tm, LANES), F32))
    nsub = tn // BLK if (tables is not None or headnorm is not None) else 1
    ns = seq // tm
    if tt is None:
        tt = np.zeros((nj,), np.int32)
    in_specs = [
        pl.BlockSpec((tm, kdim), lambda i, j, tt_: (i, src_cb)),
        pl.BlockSpec((1, kdim), lambda i, j, tt_: (0, 0)),
        pl.BlockSpec((kdim, tn), lambda i, j, tt_: (0, j)),
    ]
    args = [src, gain.reshape(1, kdim).astype(F32), w]
    if tables is not None:
        for tb in tables:
            in_specs.append(pl.BlockSpec((1, tm, LANES), lambda i, j, tt_: (tt_[j], i % ns, 0)))
            args.append(tb)
    if headnorm is not None:
        in_specs.append(pl.BlockSpec((1, tn), lambda i, j, tt_: (0, j)))
        in_specs.append(pl.BlockSpec((BLK, BLK), lambda i, j, tt_: (0, 0)))
        args += [headnorm[0], headnorm[1]]
    if add is not None:
        add_arr, add_cb = add
        in_specs.append(pl.BlockSpec((tm, BLK), lambda i, j, tt_: (i, add_cb)))
        args.append(add_arr)
    kern = functools.partial(_proj_kernel, dnorm=dnorm, rope=tables is not None,
                             headnorm=headnorm is not None, add=add is not None, nsub=nsub, vt=vt,
                             dil=dil)
    return pl.pallas_call(
        kern,
        out_shape=out_shape,
        grid_spec=pltpu.PrefetchScalarGridSpec(
            num_scalar_prefetch=1,
            grid=(t // tm, nj),
            in_specs=in_specs,
            out_specs=out_spec,
            scratch_shapes=scratch,
        ),
        compiler_params=_cparams(("parallel", "arbitrary")),
    )(jnp.asarray(tt, jnp.int32), *args)


def _flash_kernel(*refs, nh, vgroups, tq, tk, nk, diff):
    if diff:
        (qmask_ref, q_ref, k_ref, vt_ref, lq1, lk1, lq2, lk2, sg_ref,
         o_ref, qm_scr, s_scr, mx_scr, m_scr, acc_scr) = refs
        lambda_init = diff
    else:
        qmask_ref, q_ref, k_ref, vt_ref, o_ref, qm_scr, s_scr, mx_scr, m_scr, acc_scr = refs

    q = q_ref[...].astype(F32)
    for j in range(nh):
        qm_scr[:, j * tq:(j + 1) * tq] = (q * qmask_ref[j:j + 1, :]).T.astype(BF16)
    m_scr[...] = jnp.full(m_scr.shape, NEG_INF, F32)
    acc_scr[...] = jnp.zeros(acc_scr.shape, F32)

    def scores(c, slot):
        ks = pl.multiple_of(c * tk, tk)
        st = jnp.dot(k_ref[pl.ds(ks, tk), :], qm_scr[...], preferred_element_type=F32)
        s_scr[slot] = st
        mx_scr[slot] = jnp.max(st, axis=0, keepdims=True)

    def accumulate(c, slot):
        vtc = vt_ref[c]
        m_prev = m_scr[...]
        m_new = jnp.maximum(m_prev, mx_scr[slot])
        alpha = jnp.exp2(m_prev - m_new)
        m_scr[...] = m_new
        pt = jnp.exp2(s_scr[slot] - m_new).astype(BF16)
        for gi, (r0, nr, h0, hn) in enumerate(vgroups):
            cols = slice(h0 * tq, (h0 + hn) * tq)
            acc_scr[gi] = acc_scr[gi] * alpha[:, cols] + jnp.dot(
                vtc[r0:r0 + nr + _ones_rows(nr), :], pt[:, cols], preferred_element_type=F32)

    scores(0, 0)

    def body(i, carry):
        c = CHUNKS_PER_TRIP * i
        for u in range(CHUNKS_PER_TRIP):
            scores(jnp.minimum(c + u + 1, nk - 1), (u + 1) % SCORE_SLOTS)
            accumulate(c + u, u % SCORE_SLOTS)
        return carry

    lax.fori_loop(0, nk // CHUNKS_PER_TRIP, body, 0)

    pieces = []
    if diff:
        lam = (jnp.exp(jnp.sum(lq1[...] * lk1[...], axis=-1, keepdims=True))
               - jnp.exp(jnp.sum(lq2[...] * lk2[...], axis=-1, keepdims=True)) + lambda_init)
        for gi, (r0, nr, h0, hn) in enumerate(vgroups):
            acc = acc_scr[gi]
            linv = 1.0 / acc[nr:nr + 1, :]
            oh = acc[:nr, :tq] * linv[:, :tq] - lam * (acc[:nr, tq:] * linv[:, tq:])
            ms = jnp.mean(oh * oh, axis=0, keepdims=True)
            pieces.append(oh * lax.rsqrt(ms + NORM_EPS))
    else:
        for gi, (r0, nr, h0, hn) in enumerate(vgroups):
            acc = acc_scr[gi]
            on = acc[:nr, :] * (1.0 / acc[nr:nr + 1, :])
            for jj in range(hn):
                pieces.append(on[:, jj * tq:(jj + 1) * tq])
    ot = pieces[0] if len(pieces) == 1 else jnp.concatenate(pieces, axis=0)
    o = ot.T
    if diff:
        o = o * sg_ref[...] * (1.0 - lambda_init)
    o_ref[...] = o.astype(o_ref.dtype)


def _flash(q, k, vt, *, qcol0, kcol0, ngroups, wq, wv, wo, nh, vgroups, qmask, tq, tk,
           diff=None, diff_params=None):
    b, s = q.shape[0], q.shape[1]
    nk = s // tk
    assert s % tk == 0 and nk % CHUNKS_PER_TRIP == 0 and s % tq == 0
    in_specs = [
        pl.BlockSpec((nh, wq), lambda bi, g, i: (0, 0)),
        pl.BlockSpec((None, tq, wq), lambda bi, g, i: (bi, i, qcol0 + g)),
        pl.BlockSpec((None, s, wq), lambda bi, g, i: (bi, 0, kcol0 + g)),
        pl.BlockSpec((None, None, nk, wv, tk), lambda bi, g, i: (bi, g, 0, 0, 0)),
    ]
    args = [qmask, q, k, vt]
    if diff is not None:
        for prm in diff_params[:4]:
            in_specs.append(pl.BlockSpec((1, 64), lambda bi, g, i: (0, 0)))
            args.append(prm.reshape(1, 64).astype(F32))
        in_specs.append(pl.BlockSpec((1, wo), lambda bi, g, i: (0, 0)))
        args.append(jnp.tile(diff_params[4].astype(F32), wo // LANES).reshape(1, wo))
    kern = functools.partial(_flash_kernel, nh=nh, vgroups=tuple(vgroups), tq=tq, tk=tk, nk=nk,
                             diff=diff)
    nr, hn = vgroups[0][1], vgroups[0][3]
    return pl.pallas_call(
        kern,
        out_shape=jax.ShapeDtypeStruct((b, s, ngroups * wo), BF16),
        grid=(b, ngroups, s // tq),
        in_specs=in_specs,
        out_specs=pl.BlockSpec((None, tq, wo), lambda bi, g, i: (bi, i, g)),
        scratch_shapes=[
            pltpu.VMEM((wq, nh * tq), BF16),
            pltpu.VMEM((SCORE_SLOTS, tk, nh * tq), F32),
            pltpu.VMEM((SCORE_SLOTS, 1, nh * tq), F32),
            pltpu.VMEM((1, nh * tq), F32),
            pltpu.VMEM((len(vgroups), nr + _ones_rows(nr), hn * tq), F32),
        ],
        compiler_params=_cparams(("parallel", "parallel", "arbitrary")),
    )(*args)


def _band_kernel(qmask_ref, vmask_ref, q_ref, k_ref, v_ref, o_ref, lse_ref, *, tq, win, length):
    nh = 4
    i = pl.program_id(2)
    ks = jnp.clip(i * tq - A_HALF_WINDOW, 0, length - win)
    ks = pl.multiple_of(ks, A_HALF_WINDOW)
    kc = k_ref[pl.ds(ks, win), :]
    vc = v_ref[pl.ds(ks, win), :]
    q = q_ref[...]
    qm = jnp.concatenate([q * qmask_ref[j:j + 1, :] for j in range(nh)], axis=0)
    s = lax.dot_general(qm, kc, (((1,), (1,)), ((), ())), preferred_element_type=F32)
    qpos = i * tq + lax.broadcasted_iota(jnp.int32, (tq, win), 0)
    kpos = ks + lax.broadcasted_iota(jnp.int32, (tq, win), 1)
    valid = jnp.abs(qpos - kpos) <= A_HALF_WINDOW
    vmask = vmask_ref[...]
    vmask_b = vmask.astype(BF16)
    ps = []
    inv = None
    lse = None
    for j in range(nh):
        sj = jnp.where(valid, s[j * tq:(j + 1) * tq], NEG_INF)
        mj = jnp.max(sj, axis=-1, keepdims=True)
        pj = jnp.exp2(sj - mj)
        lj = jnp.sum(pj, axis=-1, keepdims=True)
        ps.append(pj.astype(BF16))
        t_inv = (1.0 / lj) * vmask[j:j + 1, :]
        t_lse = ((mj + jnp.log2(lj)) * (1.0 / LOG2E)) * vmask[j:j + 1, :]
        inv = t_inv if inv is None else inv + t_inv
        lse = t_lse if lse is None else lse + t_lse
    lhs = jnp.concatenate(ps, axis=1)
    rhs = jnp.concatenate([vc * vmask_b[j:j + 1, :] for j in range(nh)], axis=0)
    pv = jnp.dot(lhs, rhs, preferred_element_type=F32)
    o_ref[...] = (pv * inv).astype(o_ref.dtype)
    lse_ref[...] = lse


def _band_attention(qkv, qmask, vmask):
    b, dil, length, _ = qkv.shape
    tq = min(256, length // 2)
    win = tq + 2 * A_HALF_WINDOW

    def col(which):
        return lambda bi, a, i: (bi, a // 2, 0, which * 2 + a % 2)

    in_specs = [
        pl.BlockSpec((4, BLK), lambda bi, a, i: (0, 0)),
        pl.BlockSpec((4, BLK), lambda bi, a, i: (0, 0)),
        pl.BlockSpec((None, None, tq, BLK), lambda bi, a, i: (bi, a // 2, i, a % 2)),
        pl.BlockSpec((None, None, length, BLK), col(1)),
        pl.BlockSpec((None, None, length, BLK), col(2)),
    ]
    kern = functools.partial(_band_kernel, tq=tq, win=win, length=length)
    out_block = pl.BlockSpec((None, None, tq, BLK), lambda bi, a, i: (bi, a // 2, i, a % 2))
    return pl.pallas_call(
        kern,
        out_shape=(jax.ShapeDtypeStruct((b, dil, length, 2 * BLK), BF16),
                   jax.ShapeDtypeStruct((b, dil, length, 2 * BLK), F32)),
        grid=(b, dil * 2, length // tq),
        in_specs=in_specs,
        out_specs=(out_block, out_block),
        compiler_params=_cparams(("parallel", "parallel", "arbitrary")),
    )(qmask, vmask, qkv, qkv, qkv)


def _a_out_kernel(*refs, dils, tm):
    ng = len(dils)
    o_refs, l_refs = refs[:ng], refs[ng:2 * ng]
    w_ref, x_ref, out_ref = refs[2 * ng:2 * ng + 3]
    scr = list(refs[2 * ng + 3:])

    def token_order(ref, dil):
        if dil == 1:
            return ref[0].astype(F32)
        buf = scr.pop(0)
        ncb = buf.shape[0]
        for r in range(dil):
            v = ref[r].astype(F32)
            for cb in range(ncb):
                buf[cb, pl.ds(r, tm // dil, stride=dil), :] = v[:, cb * LANES:(cb + 1) * LANES]
        return jnp.concatenate([buf[cb] for cb in range(ncb)], axis=1)

    ls = [token_order(l_refs[g], dils[g]) for g in range(ng)]
    os_ = [token_order(o_refs[g], dils[g]) for g in range(ng)]
    mx = functools.reduce(jnp.maximum, ls)
    es = [jnp.exp(l - mx) for l in ls]
    inv = 1.0 / functools.reduce(jnp.add, es)
    o = functools.reduce(jnp.add, [e * og for e, og in zip(es, os_)]) * inv
    out_ref[...] = x_ref[...] + jnp.dot(o.astype(BF16), w_ref[...], preferred_element_type=F32)


def _a_out(os_, lses, w, x3, tm):
    b, s, _ = x3.shape
    kd = w.shape[0]
    dils = tuple(o.shape[1] for o in os_)
    grp = [pl.BlockSpec((None, d, tm // d, kd), lambda bi, i: (bi, 0, i, 0)) for d in dils]
    row = pl.BlockSpec((None, tm, D_MODEL), lambda bi, i: (bi, i, 0))
    nscr = 2 * sum(1 for d in dils if d > 1)
    return pl.pallas_call(
        functools.partial(_a_out_kernel, dils=dils, tm=tm),
        out_shape=jax.ShapeDtypeStruct((b, s, D_MODEL), F32),
        grid=(b, s // tm),
        in_specs=grp + grp + [pl.BlockSpec((kd, D_MODEL), lambda bi, i: (0, 0)), row],
        out_specs=row,
        scratch_shapes=[pltpu.VMEM((kd // LANES, tm, LANES), F32)] * nscr,
        compiler_params=_cparams(("parallel", "parallel")),
    )(*os_, *lses, w, x3)


def _post_kernel(*refs, has_proj, final):
    if has_proj:
        (x_ref, o_ref, wout_ref, gx_ref, wq_ref, kv_ref, wo_ref, gm_ref, w1_ref, w2_ref, fg_ref,
         out_ref, x2_scr, h_scr, acc_scr) = refs
    else:
        (x_ref, gx_ref, wq_ref, kv_ref, wo_ref, gm_ref, w1_ref, w2_ref, fg_ref,
         out_ref, x2_scr, h_scr, acc_scr) = refs
    f = pl.program_id(2)

    @pl.when(f == 0)
    def _():
        x1 = x_ref[...]
        if has_proj:
            x1 = x1 + jnp.dot(o_ref[...], wout_ref[...], preferred_element_type=F32)
        ms = jnp.mean(x1 * x1, axis=-1, keepdims=True)
        h = (x1 * lax.rsqrt(ms + NORM_EPS) * gx_ref[...]).astype(BF16)
        qb = (jnp.dot(h, wq_ref[...], preferred_element_type=F32) * (X_HEAD_DIM ** -0.5)).astype(BF16)
        hd = X_HEADS * X_HEAD_DIM
        outs = []
        for hh in range(X_HEADS):
            qh = qb[:, hh * X_HEAD_DIM:(hh + 1) * X_HEAD_DIM]
            kh = kv_ref[:, hh * X_HEAD_DIM:(hh + 1) * X_HEAD_DIM]
            vh = kv_ref[:, hd + hh * X_HEAD_DIM:hd + (hh + 1) * X_HEAD_DIM]
            s = lax.dot_general(qh, kh, (((1,), (1,)), ((), ())), preferred_element_type=F32)
            m = jnp.max(s, axis=-1, keepdims=True)
            p = jnp.exp(s - m)
            l = jnp.sum(p, axis=-1, keepdims=True)
            oh = jnp.dot(p.astype(BF16), vh, preferred_element_type=F32) * (1.0 / l)
            outs.append(oh.astype(BF16))
        x2 = x1 + jnp.dot(jnp.concatenate(outs, axis=1), wo_ref[...], preferred_element_type=F32)
        x2_scr[...] = x2
        ms2 = jnp.mean(x2 * x2, axis=-1, keepdims=True)
        h_scr[...] = (x2 * lax.rsqrt(ms2 + NORM_EPS) * gm_ref[...]).astype(BF16)
        acc_scr[...] = jnp.zeros(acc_scr.shape, F32)

    a = jnp.maximum(jnp.dot(h_scr[...], w1_ref[...], preferred_element_type=F32), 0.0)
    acc_scr[...] += jnp.dot((a * a).astype(BF16), w2_ref[...], preferred_element_type=F32)

    @pl.when(f == pl.num_programs(2) - 1)
    def _():
        y = x2_scr[...] + acc_scr[...]
        if final:
            ms = jnp.mean(y * y, axis=-1, keepdims=True)
            y = y * lax.rsqrt(ms + NORM_EPS) * fg_ref[...]
        out_ref[...] = y


def _post_mixer(x3, o3, w_out, gx, wq, kv3, wo, gm, w1, w2, final_gain, final, tm, tf):
    b, s, _ = x3.shape
    hd = X_HEADS * X_HEAD_DIM
    has_proj = o3 is not None
    const = lambda bi, i, f: (0, 0)
    row = lambda bi, i, f: (bi, i, 0)
    in_specs = [pl.BlockSpec((None, tm, D_MODEL), row)]
    args = [x3]
    if has_proj:
        kd = o3.shape[-1]
        in_specs += [pl.BlockSpec((None, tm, kd), row), pl.BlockSpec((kd, D_MODEL), const)]
        args += [o3, w_out]
    in_specs += [pl.BlockSpec((1, D_MODEL), const),
                 pl.BlockSpec((D_MODEL, hd), const),
                 pl.BlockSpec((None, N_MEM, 2 * hd), lambda bi, i, f: (bi, 0, 0)),
                 pl.BlockSpec((hd, D_MODEL), const),
                 pl.BlockSpec((1, D_MODEL), const),
                 pl.BlockSpec((D_MODEL, tf), lambda bi, i, f: (0, f)),
                 pl.BlockSpec((tf, D_MODEL), lambda bi, i, f: (f, 0)),
                 pl.BlockSpec((1, D_MODEL), const)]
    args += [gx.reshape(1, D_MODEL), wq, kv3, wo, gm.reshape(1, D_MODEL), w1, w2,
             final_gain.reshape(1, D_MODEL)]
    return pl.pallas_call(
        functools.partial(_post_kernel, has_proj=has_proj, final=final),
        out_shape=jax.ShapeDtypeStruct((b, s, D_MODEL), F32),
        grid=(b, s // tm, D_FF // tf),
        in_specs=in_specs,
        out_specs=pl.BlockSpec((None, tm, D_MODEL), row),
        scratch_shapes=[pltpu.VMEM((tm, D_MODEL), F32), pltpu.VMEM((tm, D_MODEL), BF16),
                        pltpu.VMEM((tm, D_MODEL), F32)],
        compiler_params=_cparams(("parallel", "parallel", "arbitrary")),
    )(*args)


_F_ROT16 = np.array(list(range(0, 8)) + list(range(16, 40)))
_P_ROT16 = np.array(list(range(8, 16)) + list(range(40, 64)))
_F_AXIAL = np.array(list(range(0, 16)) + list(range(32, 48)))
_P_AXIAL = np.array(list(range(16, 32)) + list(range(48, 64)))


def _block_dims(first, partner):
    lane = np.arange(BLK)
    half, slot, u = lane // LANES, (lane % LANES) // 32, lane % 32
    return np.where(half == 0, first[u], partner[u]), slot


def _slot_masks(nslot, slot_of_lane):
    return np.stack([(slot_of_lane == j) for j in range(nslot)]).astype(np.float32)


def _rope_tables(pos_list, theta, rot, scale_list, npad):
    half = rot // 2
    inv_freq = jnp.exp(jnp.arange(half, dtype=F32) * (-2.0 * math.log(theta) / rot))
    cs, ss = [], []
    for pos in pos_list:
        ang = pos.astype(F32)[:, None] * inv_freq[None, :]
        cs.append(jnp.cos(ang))
        ss.append(jnp.sin(ang))
    c = jnp.concatenate(cs, axis=1)
    s = jnp.concatenate(ss, axis=1)
    n = c.shape[0]
    if npad:
        c = jnp.concatenate([c, jnp.ones((n, npad), F32)], axis=1)
        s = jnp.concatenate([s, jnp.zeros((n, npad), F32)], axis=1)
    reps = LANES // c.shape[1]
    c = jnp.tile(c, (1, reps))
    s = jnp.tile(s, (1, reps))
    ctab = [c * sc for sc in scale_list] + [jnp.ones_like(c)]
    stab = [s * sc for sc in scale_list] + [jnp.zeros_like(s)]
    return jnp.stack(ctab), jnp.stack(stab)


def _mixer_a(x, b, s, gain, w_in, w_out, pos, tm):
    dims, slot = _block_dims(_F_ROT16, _P_ROT16)
    ctab, stab = _rope_tables([pos], ROPE_THETA, 16, [0.125 * LOG2E, 1.0], 24)
    qmask = jnp.asarray(_slot_masks(4, slot), BF16)
    vmask = jnp.asarray(_slot_masks(4, np.arange(BLK) // 64), F32)
    w_bf = w_in.astype(BF16)
    outs, lses = [], []
    for wg, (window, dil) in enumerate(A_PATTERNS):
        assert window // (2 * dil) == A_HALF_WINDOW
        cols = []
        for which in range(3):
            base = (wg * 3 + which) * A_HEADS * 64
            for hg in range(2):
                cols.append(base + hg * BLK + (slot * 64 + dims if which < 2 else np.arange(BLK)))
        w = w_bf[:, np.concatenate(cols)]
        qkv = _proj(x, 0, D_MODEL, D_MODEL, gain, w, tn=2 * BLK, out_dtype=BF16, seq=s, tm=tm,
                    tables=(ctab, stab), tt=np.array([0, 1, 2], np.int32), dil=dil)
        o, lse = _band_attention(qkv.reshape(b, dil, s // dil, 3 * 2 * BLK), qmask, vmask)
        outs.append(o)
        lses.append(lse)
    x3 = _a_out(outs, lses, w_out.astype(BF16), x.reshape(b, s, D_MODEL), min(tm, 512))
    return x3.reshape(b * s, D_MODEL), None, None


def _mixer_b(x, b, s, gain, w_in, q_gain, k_gain, w_out, rows, cols_pos, tm, tq, tk):
    dims, slot = _block_dims(_F_AXIAL, _P_AXIAL)
    qcols = np.concatenate([g * BLK + slot * 64 + dims for g in range(B_KV_HEADS)])
    kcols = np.concatenate([B_HEADS * 64 + g * 64 + dims for g in range(B_KV_HEADS)])
    w_bf = w_in.astype(BF16)
    wqk = w_bf[:, np.concatenate([qcols, kcols])]
    wv = w_bf[:, (B_HEADS + B_KV_HEADS) * 64:]
    hgain = jnp.concatenate([jnp.tile(q_gain[dims] * (0.125 * LOG2E), B_KV_HEADS),
                             jnp.tile(k_gain[dims], B_KV_HEADS)]).reshape(1, -1).astype(F32)
    bd = jnp.asarray(slot[:, None] == slot[None, :], BF16)
    ctab, stab = _rope_tables([rows, cols_pos], AXIAL_THETA, 32, [1.0], 0)
    n = B_KV_HEADS * BLK
    qk = _proj(x, 0, D_MODEL, D_MODEL, gain, wqk, tn=2 * BLK, out_dtype=BF16, seq=s, tm=tm,
               tables=(ctab, stab), tt=np.zeros((n // BLK,), np.int32), headnorm=(hgain, bd))
    vt = _proj(x, 0, D_MODEL, D_MODEL, gain, wv, tn=B_KV_HEADS * 64, out_dtype=BF16, seq=s, tm=tm,
               vt=(B_KV_HEADS, 1, 64, tk))
    qk = qk.reshape(b, s, 2 * n)
    qmask = jnp.asarray(_slot_masks(4, slot), BF16)
    o = _flash(qk, qk, vt, qcol0=0, kcol0=B_KV_HEADS, ngroups=B_KV_HEADS, wq=BLK,
               wv=64 + _ones_rows(64), wo=BLK, nh=4, vgroups=((0, 64, 0, 4),), qmask=qmask,
               tq=tq, tk=tk)
    return x, o, w_out.astype(BF16)


def _mixer_c(x, b, s, gain, w_in, lq1, lk1, lq2, lk2, sub_gain, w_out, pos, lambda_init, tm, tq, tk):
    dims, slot = _block_dims(_F_ROT16, _P_ROT16)
    ngr = C_HEADS // 2
    cols = []
    for which in range(2):
        for g in range(ngr):
            cols.append(which * C_HEADS * 128 + g * BLK + slot * 64 + dims)
    w_bf = w_in.astype(BF16)
    ctab, stab = _rope_tables([pos], ROPE_THETA, 16, [0.125 * LOG2E, 1.0], 24)
    n = ngr * BLK
    qk = _proj(x, 0, D_MODEL, D_MODEL, gain, w_bf[:, np.concatenate(cols)], tn=2 * BLK, out_dtype=BF16,
               seq=s, tm=tm, tables=(ctab, stab), tt=np.array([0, 0, 1, 1], np.int32))
    qkv = qk.reshape(b, s, 2 * n)
    vt = _proj(x, 0, D_MODEL, D_MODEL, gain, w_bf[:, 2 * C_HEADS * 128:], tn=n, out_dtype=BF16,
               seq=s, tm=tm, vt=(ngr, 2, LANES, tk))
    qmask = jnp.asarray(_slot_masks(4, slot), BF16)
    hrows = LANES + _ones_rows(LANES)
    o = _flash(qkv, qkv, vt, qcol0=0, kcol0=ngr, ngroups=ngr, wq=BLK, wv=2 * hrows,
               wo=BLK, nh=4, vgroups=((0, LANES, 0, 2), (hrows, LANES, 2, 2)),
               qmask=qmask, tq=tq, tk=tk,
               diff=lambda_init, diff_params=(lq1, lk1, lq2, lk2, sub_gain))
    return x, o, w_out.astype(BF16)


def _mixer_d(x, b, s, gain, w_in, q_gain, kv_gain, w_uq, w_ukv, w_out, pos, tm, tq, tk):
    lane = np.arange(LANES)
    slot_h = np.where(lane < 32, 0, np.where(lane < 64, 1, np.where(lane < 80, 0, np.where(lane < 96, 1, -1))))
    slot = np.concatenate([slot_h, slot_h])
    nope_lane = lane < 64
    rope_lane = (lane >= 64) & (lane < 96)
    ngr = D_HEADS // 2

    w1 = jnp.zeros((D_MODEL, 4 * BLK), F32)
    w1 = w1.at[:, :D_Q_RANK].set(w_in[:, :D_Q_RANK])
    w1 = w1.at[:, 2 * BLK:3 * BLK].set(w_in[:, D_Q_RANK:D_Q_RANK + D_KV_RANK])
    kr_src = np.zeros((BLK,), np.int64)
    kr_on = np.zeros((BLK,), bool)
    for hf in range(2):
        for l in range(LANES):
            if rope_lane[l]:
                kr_src[hf * LANES + l] = D_Q_RANK + D_KV_RANK + hf * 16 + (l - 64) % 16
                kr_on[hf * LANES + l] = True
    w1 = w1.at[:, 3 * BLK:].set(jnp.where(jnp.asarray(kr_on)[None, :], w_in[:, kr_src], 0.0))
    cmb = _proj(x, 0, D_MODEL, D_MODEL, gain, w1.astype(BF16), tn=4 * BLK, out_dtype=F32, seq=s, tm=tm)

    qsrc = np.zeros((ngr * BLK,), np.int64)
    qon = np.zeros((ngr * BLK,), bool)
    ksrc = np.zeros((ngr * BLK,), np.int64)
    kon = np.zeros((ngr * BLK,), bool)
    for g in range(ngr):
        for hf in range(2):
            for l in range(LANES):
                idx = g * BLK + hf * LANES + l
                if slot_h[l] < 0:
                    continue
                head = 2 * g + slot_h[l]
                if nope_lane[l]:
                    d = hf * 32 + l % 32
                    qsrc[idx], qon[idx] = head * 96 + d, True
                    ksrc[idx], kon[idx] = head * 128 + d, True
                else:
                    d = hf * 16 + (l - 64) % 16
                    qsrc[idx], qon[idx] = head * 96 + D_NOPE + d, True
    wq2 = jnp.where(jnp.asarray(qon)[None, :], w_uq[:, qsrc], 0.0)
    wq2 = jnp.concatenate([wq2, jnp.zeros((2 * BLK - D_Q_RANK, ngr * BLK), F32)], axis=0).astype(BF16)
    wk2 = jnp.where(jnp.asarray(kon)[None, :], w_ukv[:, ksrc], 0.0).astype(BF16)
    vsrc = np.concatenate([h * 128 + D_NOPE + np.arange(64) for h in range(D_HEADS)])
    wv2 = w_ukv[:, vsrc].astype(BF16)
    qg = jnp.concatenate([q_gain, jnp.zeros((2 * BLK - D_Q_RANK,), F32)])

    half = D_ROPE // 2
    inv_freq = jnp.exp(jnp.arange(half, dtype=F32) * (-2.0 * math.log(ROPE_THETA) / D_ROPE))
    ang = pos.astype(F32)[:, None] * inv_freq[None, :]
    ones64 = jnp.ones((s, 64), F32)
    pad32 = jnp.ones((s, 32), F32)
    c = jnp.concatenate([ones64, jnp.cos(ang), jnp.cos(ang), pad32], axis=1)
    sn = jnp.concatenate([0.0 * ones64, jnp.sin(ang), jnp.sin(ang), 0.0 * pad32], axis=1)
    qs = (D_NOPE + D_ROPE) ** -0.5 * LOG2E
    ctab = jnp.stack([c * qs, c])
    stab = jnp.stack([sn * qs, sn])

    q = _proj(cmb, 0, 2 * BLK, D_Q_RANK, qg, wq2, tn=2 * BLK, out_dtype=BF16, seq=s, tm=tm,
              tables=(ctab, stab), tt=np.zeros((ngr // 2,), np.int32))
    k = _proj(cmb, 2, BLK, D_KV_RANK, kv_gain, wk2, tn=4 * BLK, out_dtype=BF16, seq=s, tm=tm,
              tables=(ctab, stab), tt=np.ones((ngr // 4,), np.int32), add=(cmb, 3))
    vt = _proj(cmb, 2, BLK, D_KV_RANK, kv_gain, wv2, tn=4 * BLK, out_dtype=BF16, seq=s, tm=tm,
               vt=(ngr, 2, 64, tk))
    q = q.reshape(b, s, ngr * BLK)
    k = k.reshape(b, s, ngr * BLK)
    qmask = jnp.asarray(_slot_masks(2, slot), BF16)
    hrows = 64 + _ones_rows(64)
    o = _flash(q, k, vt, qcol0=0, kcol0=0, ngroups=ngr, wq=BLK, wv=2 * hrows, wo=LANES,
               nh=2, vgroups=((0, 64, 0, 1), (hrows, 64, 1, 1)), qmask=qmask, tq=tq, tk=tk)
    return x, o, w_out.astype(BF16)


def _run_trunk(x3, mem3, p):
    b, s, _ = x3.shape
    t = b * s
    tm = 1024
    tm_mlp = 512
    tq, tq_mla, tk = 256, 512, 512
    x = x3.reshape(t, D_MODEL)
    pos = jnp.arange(s, dtype=F32)
    rows = jnp.repeat(jnp.arange(s // GRID_W, dtype=F32), GRID_W)
    cols_pos = jnp.tile(jnp.arange(GRID_W, dtype=F32), s // GRID_W)
    memf = mem3.reshape(b * N_MEM, D_MODEL)
    for i in range(DEPTH):
        m, j = i % 4, i // 4
        g = p['norm_mix'][i]
        if m == 0:
            x, o, w_out = _mixer_a(x, b, s, g, p['a_w_in'][j], p['a_w_out'][j], pos, tm)
        elif m == 1:
            x, o, w_out = _mixer_b(x, b, s, g, p['b_w_in'][j], p['b_q_norm'][j], p['b_k_norm'][j],
                                   p['b_w_out'][j], rows, cols_pos, tm, tq, tk)
        elif m == 2:
            x, o, w_out = _mixer_c(x, b, s, g, p['c_w_in'][j], p['c_lambda_q1'][j],
                                   p['c_lambda_k1'][j], p['c_lambda_q2'][j], p['c_lambda_k2'][j],
                                   p['c_sub_norm'][j], p['c_w_out'][j], pos,
                                   0.8 - 0.6 * math.exp(-0.3 * i), tm, tq, tk)
        else:
            x, o, w_out = _mixer_d(x, b, s, g, p['d_w_in'][j], p['d_q_norm'][j], p['d_kv_norm'][j],
                                   p['d_w_uq'][j], p['d_w_ukv'][j], p['d_w_out'][j], pos, tm,
                                   tq_mla, tk)
        kv = _proj(memf, 0, D_MODEL, D_MODEL, p['norm_mem'][i], p['w_xkv'][i].astype(BF16),
                   tn=2 * X_HEADS * X_HEAD_DIM, out_dtype=BF16, seq=N_MEM, tm=N_MEM)
        x = _post_mixer(x.reshape(b, s, D_MODEL), o, w_out, p['norm_x'][i],
                        p['w_xq'][i].astype(BF16), kv.reshape(b, N_MEM, 2 * X_HEADS * X_HEAD_DIM),
                        p['w_xo'][i].astype(BF16), p['norm_mlp'][i], p['w_mlp_in'][i].astype(BF16),
                        p['w_mlp_out'][i].astype(BF16), p['final_norm'], i == DEPTH - 1,
                        tm_mlp, 1024).reshape(t, D_MODEL)
    return x.reshape(b, s, D_MODEL)


def kernel(x_prompt, x_sample, mem_prompt, mem_sample, norm_mix, norm_x, norm_mem, w_xq, w_xkv, w_xo, norm_mlp, w_mlp_in, w_mlp_out, a_w_in, a_w_out, b_w_in, b_q_norm, b_k_norm, b_w_out, c_w_in, c_lambda_q1, c_lambda_k1, c_lambda_q2, c_lambda_k2, c_sub_norm, c_w_out, d_w_in, d_q_norm, d_kv_norm, d_w_uq, d_w_ukv, d_w_out, final_norm):
    p = dict(norm_mix=norm_mix, norm_x=norm_x, norm_mem=norm_mem, w_xq=w_xq, w_xkv=w_xkv,
             w_xo=w_xo, norm_mlp=norm_mlp, w_mlp_in=w_mlp_in, w_mlp_out=w_mlp_out,
             a_w_in=a_w_in, a_w_out=a_w_out, b_w_in=b_w_in, b_q_norm=b_q_norm,
             b_k_norm=b_k_norm, b_w_out=b_w_out, c_w_in=c_w_in, c_lambda_q1=c_lambda_q1,
             c_lambda_k1=c_lambda_k1, c_lambda_q2=c_lambda_q2, c_lambda_k2=c_lambda_k2,
             c_sub_norm=c_sub_norm, c_w_out=c_w_out, d_w_in=d_w_in, d_q_norm=d_q_norm,
             d_kv_norm=d_kv_norm, d_w_uq=d_w_uq, d_w_ukv=d_w_ukv, d_w_out=d_w_out,
             final_norm=final_norm)
    return (_run_trunk(x_prompt, mem_prompt, p), _run_trunk(x_sample, mem_sample, p))
```

```python
import functools
import math

import numpy as np
import jax
import jax.numpy as jnp
from jax import lax
from jax.experimental import pallas as pl
from jax.experimental.pallas import tpu as pltpu

F32 = jnp.float32
BF16 = jnp.bfloat16

D_MODEL = 1024
DEPTH = 4
N_MEM = 256
GRID_W = 64
D_FF = 4 * D_MODEL
NORM_EPS = 1e-6
ROPE_THETA = 500000.0
AXIAL_THETA = 10000.0
NEG_INF = -1e30

A_PATTERNS = ((128, 1), (512, 4), (2048, 16))
A_GROUPS = 3
A_HEADS = 8
A_IN = A_GROUPS * 3 * A_HEADS * 64
A_HALF_WINDOW = 64
B_HEADS = 16
B_KV_HEADS = 4
C_HEADS = 8
D_HEADS = 16
D_Q_RANK = 384
D_KV_RANK = 256
D_NOPE = 64
D_ROPE = 32
X_HEADS = 4
X_HEAD_DIM = 128

LOG2E = 1.4426950408889634
BF16_ROWS = 16
SCORE_SLOTS = 4
KEY_LOOP_TRIPS = 2
LANES = 128
BLK = 2 * LANES
VMEM_LIMIT = 56 * 1024 * 1024


def _ones_rows(nr):
    return LANES - nr if nr < LANES else BF16_ROWS


def _cparams(sem):
    return pltpu.CompilerParams(dimension_semantics=sem, vmem_limit_bytes=VMEM_LIMIT)


def _proj_kernel(tt_ref, src_ref, g_ref, w_ref, *rest, dnorm, rope, headnorm, add, nsub, vt, dil):
    del tt_ref
    rest = list(rest)
    c_ref = s_ref = hg_ref = bd_ref = add_ref = y_scr = None
    if dil > 1:
        y_scr = rest.pop()
    if rope:
        c_ref, s_ref = rest[0], rest[1]
        rest = rest[2:]
    if headnorm:
        hg_ref, bd_ref = rest[0], rest[1]
        rest = rest[2:]
    if add:
        add_ref = rest[0]
        rest = rest[1:]
    o_ref, h_scr = rest

    @pl.when(pl.program_id(1) == 0)
    def _():
        xf = src_ref[...].astype(F32)
        ms = jnp.sum(xf * xf, axis=-1, keepdims=True) * (1.0 / dnorm)
        h_scr[...] = (xf * lax.rsqrt(ms + NORM_EPS) * g_ref[...]).astype(BF16)

    y = jnp.dot(h_scr[...], w_ref[...], preferred_element_type=F32)
    if add:
        ad = add_ref[...]
        y = y + (ad if nsub == 1 else jnp.concatenate([ad] * nsub, axis=1))
    if headnorm:
        y2 = y * y
        hi = y2.astype(BF16)
        lo = (y2 - hi.astype(F32)).astype(BF16)
        parts = []
        for n in range(nsub):
            sl = slice(n * BLK, (n + 1) * BLK)
            ss = (jnp.dot(hi[:, sl], bd_ref[...], preferred_element_type=F32)
                  + jnp.dot(lo[:, sl], bd_ref[...], preferred_element_type=F32))
            parts.append(y[:, sl] * lax.rsqrt(ss * (1.0 / 64.0) + NORM_EPS))
        y = (parts[0] if nsub == 1 else jnp.concatenate(parts, axis=1)) * hg_ref[...]
    if rope:
        c = c_ref[0]
        s = s_ref[0]
        for n in range(nsub):
            y1 = y[:, n * BLK:n * BLK + LANES]
            y2 = y[:, n * BLK + LANES:(n + 1) * BLK]
            r1 = y1 * c - y2 * s
            r2 = y2 * c + y1 * s
            if dil > 1:
                y_scr[2 * n] = r1
                y_scr[2 * n + 1] = r2
            else:
                o_ref[:, n * BLK:n * BLK + LANES] = r1.astype(o_ref.dtype)
                o_ref[:, n * BLK + LANES:(n + 1) * BLK] = r2.astype(o_ref.dtype)
        if dil > 1:
            rows = y.shape[0] // dil
            for r in range(dil):
                for cb in range(2 * nsub):
                    o_ref[r, :, cb * LANES:(cb + 1) * LANES] = (
                        y_scr[cb, pl.ds(r, rows, stride=dil), :].astype(o_ref.dtype))
    elif vt:
        ngroups, nvg, nr, tk = vt
        nones = _ones_rows(nr)
        ones = jnp.ones((nones, tk), o_ref.dtype)
        for cc in range(y.shape[0] // tk):
            yt = y[cc * tk:(cc + 1) * tk, :].T
            for g in range(ngroups):
                for vg in range(nvg):
                    src0 = (g * nvg + vg) * nr
                    r0 = vg * (nr + nones)
                    o_ref[g, cc, r0:r0 + nr, :] = yt[src0:src0 + nr, :].astype(o_ref.dtype)
                    o_ref[g, cc, r0 + nr:r0 + nr + nones, :] = ones
    else:
        o_ref[...] = y.astype(o_ref.dtype)


def _proj(src, src_cb, kdim, dnorm, gain, w, *, tn, out_dtype, seq, tm,
          tables=None, tt=None, headnorm=None, add=None, vt=None, dil=1):
    t = src.shape[0]
    n = w.shape[1]
    nj = n // tn
    out_shape = jax.ShapeDtypeStruct((t, n), out_dtype)
    out_spec = pl.BlockSpec((tm, tn), lambda i, j, tt_: (i, j))
    if vt is not None:
        ngroups, nvg, nr, tk = vt
        assert nj == 1 and n == ngroups * nvg * nr and tm % tk == 0
        wv = nvg * (nr + _ones_rows(nr))
        out_shape = jax.ShapeDtypeStruct((t // seq, ngroups, seq // tk, wv, tk), out_dtype)
        out_spec = pl.BlockSpec((None, ngroups, tm // tk, wv, tk),
                                lambda i, j, tt_: (i // (seq // tm), 0, i % (seq // tm), 0, 0))
    scratch = [pltpu.VMEM((tm, kdim), BF16)]
    if dil > 1:
        assert tables is not None and tm % (16 * dil) == 0
        out_shape = jax.ShapeDtypeStruct((t // seq, dil, seq // dil, n), out_dtype)
        out_spec = pl.BlockSpec((None, dil, tm // dil, tn),
                                lambda i, j, tt_: (i // (seq // tm), 0, i % (seq // tm), j))
        scratch.append(pltpu.VMEM((tn // LANES, tm, LANES), F32))
    nsub = tn // BLK if (tables is not None or headnorm is not None) else 1
    ns = seq // tm
    if tt is None:
        tt = np.zeros((nj,), np.int32)
    in_specs = [
        pl.BlockSpec((tm, kdim), lambda i, j, tt_: (i, src_cb)),
        pl.BlockSpec((1, kdim), lambda i, j, tt_: (0, 0)),
        pl.BlockSpec((kdim, tn), lambda i, j, tt_: (0, j)),
    ]
    args = [src, gain.reshape(1, kdim).astype(F32), w]
    if tables is not None:
        for tb in tables:
            in_specs.append(pl.BlockSpec((1, tm, LANES), lambda i, j, tt_: (tt_[j], i % ns, 0)))
            args.append(tb)
    if headnorm is not None:
        in_specs.append(pl.BlockSpec((1, tn), lambda i, j, tt_: (0, j)))
        in_specs.append(pl.BlockSpec((BLK, BLK), lambda i, j, tt_: (0, 0)))
        args += [headnorm[0], headnorm[1]]
    if add is not None:
        add_arr, add_cb = add
        in_specs.append(pl.BlockSpec((tm, BLK), lambda i, j, tt_: (i, add_cb)))
        args.append(add_arr)
    kern = functools.partial(_proj_kernel, dnorm=dnorm, rope=tables is not None,
                             headnorm=headnorm is not None, add=add is not None, nsub=nsub, vt=vt,
                             dil=dil)
    return pl.pallas_call(
        kern,
        out_shape=out_shape,
        grid_spec=pltpu.PrefetchScalarGridSpec(
            num_scalar_prefetch=1,
            grid=(t // tm, nj),
            in_specs=in_specs,
            out_specs=out_spec,
            scratch_shapes=scratch,
        ),
        compiler_params=_cparams(("parallel", "arbitrary")),
    )(jnp.asarray(tt, jnp.int32), *args)


def _flash_kernel(*refs, nh, vgroups, tq, tk, nk, diff):
    if diff:
        (qmask_ref, q_ref, k_ref, vt_ref, lq1, lk1, lq2, lk2, sg_ref,
         o_ref, qm_scr, s_scr, mx_scr, m_scr, acc_scr) = refs
        lambda_init = diff
    else:
        qmask_ref, q_ref, k_ref, vt_ref, o_ref, qm_scr, s_scr, mx_scr, m_scr, acc_scr = refs

    qt = q_ref[...].astype(F32).T
    for j in range(nh):
        rowmask = jnp.concatenate([qmask_ref[j]] * (tq // LANES), axis=1)
        qm_scr[:, j * tq:(j + 1) * tq] = (qt * rowmask).astype(BF16)
    m_scr[...] = jnp.full(m_scr.shape, NEG_INF, F32)
    acc_scr[...] = jnp.zeros(acc_scr.shape, F32)

    ucols = min(2 * BLK, vgroups[0][3] * tq)
    units = []
    for gi, (r0, nr, h0, hn) in enumerate(vgroups):
        for off in range(0, hn * tq, ucols):
            units.append((gi, r0, nr, h0 * tq + off, off))

    def scores(c, slot, col0):
        cols = slice(col0, col0 + ucols)
        ks = pl.multiple_of(c * tk, tk)
        st = jnp.dot(k_ref[pl.ds(ks, tk), :], qm_scr[:, cols], preferred_element_type=F32)
        s_scr[slot, :, cols] = st
        mx_scr[slot, :, cols] = jnp.max(st, axis=0, keepdims=True)

    def step(c, slot, c_next, slot_next):
        vtc = vt_ref[c]
        m_prev = m_scr[...]
        m_new = jnp.maximum(m_prev, mx_scr[slot])
        alpha = jnp.exp2(m_prev - m_new)
        m_scr[...] = m_new
        for gi, r0, nr, col0, off in units:
            cols = slice(col0, col0 + ucols)
            scores(c_next, slot_next, col0)
            pt = jnp.exp2(s_scr[slot, :, cols] - m_new[:, cols]).astype(BF16)
            acols = slice(off, off + ucols)
            acc_scr[gi, :, acols] = acc_scr[gi, :, acols] * alpha[:, cols] + jnp.dot(
                vtc[r0:r0 + nr + _ones_rows(nr), :], pt, preferred_element_type=F32)

    for unit in units:
        scores(0, 0, unit[3])

    per_trip = nk // KEY_LOOP_TRIPS

    def body(i, carry):
        c = per_trip * i
        for u in range(per_trip):
            step(c + u, u % SCORE_SLOTS, jnp.minimum(c + u + 1, nk - 1), (u + 1) % SCORE_SLOTS)
        return carry

    lax.fori_loop(0, KEY_LOOP_TRIPS, body, 0)

    pieces = []
    if diff:
        lam = (jnp.exp(jnp.sum(lq1[...] * lk1[...], axis=-1, keepdims=True))
               - jnp.exp(jnp.sum(lq2[...] * lk2[...], axis=-1, keepdims=True)) + lambda_init)
        for gi, (r0, nr, h0, hn) in enumerate(vgroups):
            acc = acc_scr[gi]
            linv = 1.0 / acc[nr:nr + 1, :]
            oh = acc[:nr, :tq] * linv[:, :tq] - lam * (acc[:nr, tq:] * linv[:, tq:])
            ms = jnp.mean(oh * oh, axis=0, keepdims=True)
            pieces.append(oh * lax.rsqrt(ms + NORM_EPS))
    else:
        for gi, (r0, nr, h0, hn) in enumerate(vgroups):
            acc = acc_scr[gi]
            on = acc[:nr, :] * (1.0 / acc[nr:nr + 1, :])
            for jj in range(hn):
                pieces.append(on[:, jj * tq:(jj + 1) * tq])
    ot = pieces[0] if len(pieces) == 1 else jnp.concatenate(pieces, axis=0)
    o = ot.T
    if diff:
        o = o * sg_ref[...] * (1.0 - lambda_init)
    o_ref[...] = o.astype(o_ref.dtype)


def _flash(q, k, vt, *, qcol0, kcol0, ngroups, wq, wv, wo, nh, vgroups, qmask, tq, tk,
           diff=None, diff_params=None):
    b, s = q.shape[0], q.shape[1]
    nk = s // tk
    assert s % tk == 0 and nk % (KEY_LOOP_TRIPS * SCORE_SLOTS) == 0 and s % tq == 0
    qmask = jnp.broadcast_to(qmask.astype(F32)[:, :, None], (nh, wq, LANES))
    in_specs = [
        pl.BlockSpec((nh, wq, LANES), lambda bi, g, i: (0, 0, 0)),
        pl.BlockSpec((None, tq, wq), lambda bi, g, i: (bi, i, qcol0 + g)),
        pl.BlockSpec((None, s, wq), lambda bi, g, i: (bi, 0, kcol0 + g)),
        pl.BlockSpec((None, None, nk, wv, tk), lambda bi, g, i: (bi, g, 0, 0, 0)),
    ]
    args = [qmask, q, k, vt]
    if diff is not None:
        for prm in diff_params[:4]:
            in_specs.append(pl.BlockSpec((1, 64), lambda bi, g, i: (0, 0)))
            args.append(prm.reshape(1, 64).astype(F32))
        in_specs.append(pl.BlockSpec((1, wo), lambda bi, g, i: (0, 0)))
        args.append(jnp.tile(diff_params[4].astype(F32), wo // LANES).reshape(1, wo))
    kern = functools.partial(_flash_kernel, nh=nh, vgroups=tuple(vgroups), tq=tq, tk=tk, nk=nk,
                             diff=diff)
    nr, hn = vgroups[0][1], vgroups[0][3]
    return pl.pallas_call(
        kern,
        out_shape=jax.ShapeDtypeStruct((b, s, ngroups * wo), BF16),
        grid=(b, ngroups, s // tq),
        in_specs=in_specs,
        out_specs=pl.BlockSpec((None, tq, wo), lambda bi, g, i: (bi, i, g)),
        scratch_shapes=[
            pltpu.VMEM((wq, nh * tq), BF16),
            pltpu.VMEM((SCORE_SLOTS, tk, nh * tq), F32),
            pltpu.VMEM((SCORE_SLOTS, 1, nh * tq), F32),
            pltpu.VMEM((1, nh * tq), F32),
            pltpu.VMEM((len(vgroups), nr + _ones_rows(nr), hn * tq), F32),
        ],
        compiler_params=_cparams(("parallel", "parallel", "arbitrary")),
    )(*args)


def _band_kernel(qmask_ref, vmask_ref, q_ref, k_ref, v_ref, o_ref, lse_ref, *, tq, win, length):
    nh = 4
    i = pl.program_id(2)
    ks = jnp.clip(i * tq - A_HALF_WINDOW, 0, length - win)
    ks = pl.multiple_of(ks, A_HALF_WINDOW)
    kc = k_ref[pl.ds(ks, win), :]
    vc = v_ref[pl.ds(ks, win), :]
    q = q_ref[...]
    qm = jnp.concatenate([q * qmask_ref[j:j + 1, :] for j in range(nh)], axis=0)
    s = lax.dot_general(qm, kc, (((1,), (1,)), ((), ())), preferred_element_type=F32)
    qpos = i * tq + lax.broadcasted_iota(jnp.int32, (tq, win), 0)
    kpos = ks + lax.broadcasted_iota(jnp.int32, (tq, win), 1)
    valid = jnp.abs(qpos - kpos) <= A_HALF_WINDOW
    vmask = vmask_ref[...]
    vmask_b = vmask.astype(BF16)
    ps = []
    inv = None
    lse = None
    for j in range(nh):
        sj = jnp.where(valid, s[j * tq:(j + 1) * tq], NEG_INF)
        mj = jnp.max(sj, axis=-1, keepdims=True)
        pj = jnp.exp2(sj - mj)
        lj = jnp.sum(pj, axis=-1, keepdims=True)
        ps.append(pj.astype(BF16))
        t_inv = (1.0 / lj) * vmask[j:j + 1, :]
        t_lse = ((mj + jnp.log2(lj)) * (1.0 / LOG2E)) * vmask[j:j + 1, :]
        inv = t_inv if inv is None else inv + t_inv
        lse = t_lse if lse is None else lse + t_lse
    lhs = jnp.concatenate(ps, axis=1)
    rhs = jnp.concatenate([vc * vmask_b[j:j + 1, :] for j in range(nh)], axis=0)
    pv = jnp.dot(lhs, rhs, preferred_element_type=F32)
    o_ref[...] = (pv * inv).astype(o_ref.dtype)
    lse_ref[...] = lse


def _band_attention(qkv, qmask, vmask):
    b, dil, length, _ = qkv.shape
    tq = min(256, length)
    win = min(tq + 2 * A_HALF_WINDOW, length)

    def col(which):
        return lambda bi, a, i: (bi, a // 2, 0, which * 2 + a % 2)

    in_specs = [
        pl.BlockSpec((4, BLK), lambda bi, a, i: (0, 0)),
        pl.BlockSpec((4, BLK), lambda bi, a, i: (0, 0)),
        pl.BlockSpec((None, None, tq, BLK), lambda bi, a, i: (bi, a // 2, i, a % 2)),
        pl.BlockSpec((None, None, length, BLK), col(1)),
        pl.BlockSpec((None, None, length, BLK), col(2)),
    ]
    kern = functools.partial(_band_kernel, tq=tq, win=win, length=length)
    out_block = pl.BlockSpec((None, None, tq, BLK), lambda bi, a, i: (bi, a // 2, i, a % 2))
    return pl.pallas_call(
        kern,
        out_shape=(jax.ShapeDtypeStruct((b, dil, length, 2 * BLK), BF16),
                   jax.ShapeDtypeStruct((b, dil, length, 2 * BLK), F32)),
        grid=(b, dil * 2, length // tq),
        in_specs=in_specs,
        out_specs=(out_block, out_block),
        compiler_params=_cparams(("parallel", "parallel", "arbitrary")),
    )(qmask, vmask, qkv, qkv, qkv)


def _a_out_kernel(*refs, dils, tm):
    ng = len(dils)
    o_refs, l_refs = refs[:ng], refs[ng:2 * ng]
    w_ref, x_ref, out_ref = refs[2 * ng:2 * ng + 3]
    scr = list(refs[2 * ng + 3:])

    def token_order(ref, dil):
        if dil == 1:
            return ref[0].astype(F32)
        buf = scr.pop(0)
        ncb = buf.shape[0]
        for r in range(dil):
            v = ref[r].astype(F32)
            for cb in range(ncb):
                buf[cb, pl.ds(r, tm // dil, stride=dil), :] = v[:, cb * LANES:(cb + 1) * LANES]
        return jnp.concatenate([buf[cb] for cb in range(ncb)], axis=1)

    ls = [token_order(l_refs[g], dils[g]) for g in range(ng)]
    os_ = [token_order(o_refs[g], dils[g]) for g in range(ng)]
    mx = functools.reduce(jnp.maximum, ls)
    es = [jnp.exp(l - mx) for l in ls]
    inv = 1.0 / functools.reduce(jnp.add, es)
    o = functools.reduce(jnp.add, [e * og for e, og in zip(es, os_)]) * inv
    out_ref[...] = x_ref[...] + jnp.dot(o.astype(BF16), w_ref[...], preferred_element_type=F32)


def _a_out(os_, lses, w, x3, tm):
    b, s, _ = x3.shape
    kd = w.shape[0]
    dils = tuple(o.shape[1] for o in os_)
    grp = [pl.BlockSpec((None, d, tm // d, kd), lambda bi, i: (bi, 0, i, 0)) for d in dils]
    row = pl.BlockSpec((None, tm, D_MODEL), lambda bi, i: (bi, i, 0))
    nscr = 2 * sum(1 for d in dils if d > 1)
    return pl.pallas_call(
        functools.partial(_a_out_kernel, dils=dils, tm=tm),
        out_shape=jax.ShapeDtypeStruct((b, s, D_MODEL), F32),
        grid=(b, s // tm),
        in_specs=grp + grp + [pl.BlockSpec((kd, D_MODEL), lambda bi, i: (0, 0)), row],
        out_specs=row,
        scratch_shapes=[pltpu.VMEM((kd // LANES, tm, LANES), F32)] * nscr,
        compiler_params=_cparams(("parallel", "parallel")),
    )(*os_, *lses, w, x3)


def _post_kernel(*refs, has_proj, final):
    if has_proj:
        (x_ref, o_ref, wout_ref, gx_ref, wq_ref, kv_ref, wo_ref, gm_ref, w1_ref, w2_ref, fg_ref,
         out_ref, x2_scr, h_scr, acc_scr) = refs
    else:
        (x_ref, gx_ref, wq_ref, kv_ref, wo_ref, gm_ref, w1_ref, w2_ref, fg_ref,
         out_ref, x2_scr, h_scr, acc_scr) = refs
    f = pl.program_id(2)

    @pl.when(f == 0)
    def _():
        x1 = x_ref[...]
        if has_proj:
            x1 = x1 + jnp.dot(o_ref[...], wout_ref[...], preferred_element_type=F32)
        ms = jnp.mean(x1 * x1, axis=-1, keepdims=True)
        h = (x1 * lax.rsqrt(ms + NORM_EPS) * gx_ref[...]).astype(BF16)
        qb = (jnp.dot(h, wq_ref[...], preferred_element_type=F32)
              * (X_HEAD_DIM ** -0.5 * LOG2E)).astype(BF16)
        hd = X_HEADS * X_HEAD_DIM
        outs = []
        for hh in range(X_HEADS):
            qh = qb[:, hh * X_HEAD_DIM:(hh + 1) * X_HEAD_DIM]
            kh = kv_ref[:, hh * X_HEAD_DIM:(hh + 1) * X_HEAD_DIM]
            vh = kv_ref[:, hd + hh * X_HEAD_DIM:hd + (hh + 1) * X_HEAD_DIM]
            s = lax.dot_general(qh, kh, (((1,), (1,)), ((), ())), preferred_element_type=F32)
            m = jnp.max(s, axis=-1, keepdims=True)
            p = jnp.exp2(s - m)
            l = jnp.sum(p, axis=-1, keepdims=True)
            oh = jnp.dot(p.astype(BF16), vh, preferred_element_type=F32) * (1.0 / l)
            outs.append(oh.astype(BF16))
        x2 = x1 + jnp.dot(jnp.concatenate(outs, axis=1), wo_ref[...], preferred_element_type=F32)
        x2_scr[...] = x2
        ms2 = jnp.mean(x2 * x2, axis=-1, keepdims=True)
        h_scr[...] = (x2 * lax.rsqrt(ms2 + NORM_EPS) * gm_ref[...]).astype(BF16)
        acc_scr[...] = jnp.zeros(acc_scr.shape, F32)

    a = jnp.maximum(jnp.dot(h_scr[...], w1_ref[...], preferred_element_type=F32), 0.0)
    acc_scr[...] += jnp.dot((a * a).astype(BF16), w2_ref[...], preferred_element_type=F32)

    @pl.when(f == pl.num_programs(2) - 1)
    def _():
        y = x2_scr[...] + acc_scr[...]
        if final:
            ms = jnp.mean(y * y, axis=-1, keepdims=True)
            y = y * lax.rsqrt(ms + NORM_EPS) * fg_ref[...]
        out_ref[...] = y


def _post_mixer(x3, o3, w_out, gx, wq, kv3, wo, gm, w1, w2, final_gain, final, tm, tf):
    b, s, _ = x3.shape
    hd = X_HEADS * X_HEAD_DIM
    has_proj = o3 is not None
    const = lambda bi, i, f: (0, 0)
    row = lambda bi, i, f: (bi, i, 0)
    in_specs = [pl.BlockSpec((None, tm, D_MODEL), row)]
    args = [x3]
    if has_proj:
        kd = o3.shape[-1]
        in_specs += [pl.BlockSpec((None, tm, kd), row), pl.BlockSpec((kd, D_MODEL), const)]
        args += [o3, w_out]
    in_specs += [pl.BlockSpec((1, D_MODEL), const),
                 pl.BlockSpec((D_MODEL, hd), const),
                 pl.BlockSpec((None, N_MEM, 2 * hd), lambda bi, i, f: (bi, 0, 0)),
                 pl.BlockSpec((hd, D_MODEL), const),
                 pl.BlockSpec((1, D_MODEL), const),
                 pl.BlockSpec((D_MODEL, tf), lambda bi, i, f: (0, f)),
                 pl.BlockSpec((tf, D_MODEL), lambda bi, i, f: (f, 0)),
                 pl.BlockSpec((1, D_MODEL), const)]
    args += [gx.reshape(1, D_MODEL), wq, kv3, wo, gm.reshape(1, D_MODEL), w1, w2,
             final_gain.reshape(1, D_MODEL)]
    return pl.pallas_call(
        functools.partial(_post_kernel, has_proj=has_proj, final=final),
        out_shape=jax.ShapeDtypeStruct((b, s, D_MODEL), F32),
        grid=(b, s // tm, D_FF // tf),
        in_specs=in_specs,
        out_specs=pl.BlockSpec((None, tm, D_MODEL), row),
        scratch_shapes=[pltpu.VMEM((tm, D_MODEL), F32), pltpu.VMEM((tm, D_MODEL), BF16),
                        pltpu.VMEM((tm, D_MODEL), F32)],
        compiler_params=_cparams(("parallel", "parallel", "arbitrary")),
    )(*args)


_F_ROT16 = np.array(list(range(0, 8)) + list(range(16, 40)))
_P_ROT16 = np.array(list(range(8, 16)) + list(range(40, 64)))
_F_AXIAL = np.array(list(range(0, 16)) + list(range(32, 48)))
_P_AXIAL = np.array(list(range(16, 32)) + list(range(48, 64)))


def _block_dims(first, partner):
    lane = np.arange(BLK)
    half, slot, u = lane // LANES, (lane % LANES) // 32, lane % 32
    return np.where(half == 0, first[u], partner[u]), slot


def _slot_masks(nslot, slot_of_lane):
    return np.stack([(slot_of_lane == j) for j in range(nslot)]).astype(np.float32)


def _rope_tables(pos_list, theta, rot, scale_list, npad):
    half = rot // 2
    inv_freq = jnp.exp(jnp.arange(half, dtype=F32) * (-2.0 * math.log(theta) / rot))
    cs, ss = [], []
    for pos in pos_list:
        ang = pos.astype(F32)[:, None] * inv_freq[None, :]
        cs.append(jnp.cos(ang))
        ss.append(jnp.sin(ang))
    c = jnp.concatenate(cs, axis=1)
    s = jnp.concatenate(ss, axis=1)
    n = c.shape[0]
    if npad:
        c = jnp.concatenate([c, jnp.ones((n, npad), F32)], axis=1)
        s = jnp.concatenate([s, jnp.zeros((n, npad), F32)], axis=1)
    reps = LANES // c.shape[1]
    c = jnp.tile(c, (1, reps))
    s = jnp.tile(s, (1, reps))
    ctab = [c * sc for sc in scale_list] + [jnp.ones_like(c)]
    stab = [s * sc for sc in scale_list] + [jnp.zeros_like(s)]
    return jnp.stack(ctab), jnp.stack(stab)


def _mixer_a(x, b, s, gain, w_in, w_out, pos, tm):
    dims, slot = _block_dims(_F_ROT16, _P_ROT16)
    ctab, stab = _rope_tables([pos], ROPE_THETA, 16, [0.125 * LOG2E, 1.0], 24)
    qmask = jnp.asarray(_slot_masks(4, slot), BF16)
    vmask = jnp.asarray(_slot_masks(4, np.arange(BLK) // 64), F32)
    w_bf = w_in.astype(BF16)
    outs, lses = [], []
    for wg, (window, dil) in enumerate(A_PATTERNS):
        assert window // (2 * dil) == A_HALF_WINDOW
        cols = []
        for which in range(3):
            base = (wg * 3 + which) * A_HEADS * 64
            for hg in range(2):
                cols.append(base + hg * BLK + (slot * 64 + dims if which < 2 else np.arange(BLK)))
        w = w_bf[:, np.concatenate(cols)]
        qkv = _proj(x, 0, D_MODEL, D_MODEL, gain, w, tn=2 * BLK, out_dtype=BF16, seq=s, tm=tm,
                    tables=(ctab, stab), tt=np.array([0, 1, 2], np.int32), dil=dil)
        o, lse = _band_attention(qkv.reshape(b, dil, s // dil, 3 * 2 * BLK), qmask, vmask)
        outs.append(o)
        lses.append(lse)
    x3 = _a_out(outs, lses, w_out.astype(BF16), x.reshape(b, s, D_MODEL), min(tm, 512))
    return x3.reshape(b * s, D_MODEL), None, None


def _mixer_b(x, b, s, gain, w_in, q_gain, k_gain, w_out, rows, cols_pos, tm, tq, tk):
    dims, slot = _block_dims(_F_AXIAL, _P_AXIAL)
    qcols = np.concatenate([g * BLK + slot * 64 + dims for g in range(B_KV_HEADS)])
    kcols = np.concatenate([B_HEADS * 64 + g * 64 + dims for g in range(B_KV_HEADS)])
    w_bf = w_in.astype(BF16)
    wqk = w_bf[:, np.concatenate([qcols, kcols])]
    wv = w_bf[:, (B_HEADS + B_KV_HEADS) * 64:]
    hgain = jnp.concatenate([jnp.tile(q_gain[dims] * (0.125 * LOG2E), B_KV_HEADS),
                             jnp.tile(k_gain[dims], B_KV_HEADS)]).reshape(1, -1).astype(F32)
    bd = jnp.asarray(slot[:, None] == slot[None, :], BF16)
    ctab, stab = _rope_tables([rows, cols_pos], AXIAL_THETA, 32, [1.0], 0)
    n = B_KV_HEADS * BLK
    qk = _proj(x, 0, D_MODEL, D_MODEL, gain, wqk, tn=2 * BLK, out_dtype=BF16, seq=s, tm=tm,
               tables=(ctab, stab), tt=np.zeros((n // BLK,), np.int32), headnorm=(hgain, bd))
    vt = _proj(x, 0, D_MODEL, D_MODEL, gain, wv, tn=B_KV_HEADS * 64, out_dtype=BF16, seq=s, tm=tm,
               vt=(B_KV_HEADS, 1, 64, tk))
    qk = qk.reshape(b, s, 2 * n)
    qmask = jnp.asarray(_slot_masks(4, slot), BF16)
    o = _flash(qk, qk, vt, qcol0=0, kcol0=B_KV_HEADS, ngroups=B_KV_HEADS, wq=BLK,
               wv=64 + _ones_rows(64), wo=BLK, nh=4, vgroups=((0, 64, 0, 4),), qmask=qmask,
               tq=tq, tk=tk)
    return x, o, w_out.astype(BF16)


def _mixer_c(x, b, s, gain, w_in, lq1, lk1, lq2, lk2, sub_gain, w_out, pos, lambda_init, tm, tq, tk):
    dims, slot = _block_dims(_F_ROT16, _P_ROT16)
    ngr = C_HEADS // 2
    cols = []
    for which in range(2):
        for g in range(ngr):
            cols.append(which * C_HEADS * 128 + g * BLK + slot * 64 + dims)
    w_bf = w_in.astype(BF16)
    ctab, stab = _rope_tables([pos], ROPE_THETA, 16, [0.125 * LOG2E, 1.0], 24)
    n = ngr * BLK
    qk = _proj(x, 0, D_MODEL, D_MODEL, gain, w_bf[:, np.concatenate(cols)], tn=2 * BLK, out_dtype=BF16,
               seq=s, tm=tm, tables=(ctab, stab), tt=np.array([0, 0, 1, 1], np.int32))
    qkv = qk.reshape(b, s, 2 * n)
    vt = _proj(x, 0, D_MODEL, D_MODEL, gain, w_bf[:, 2 * C_HEADS * 128:], tn=n, out_dtype=BF16,
               seq=s, tm=tm, vt=(ngr, 2, LANES, tk))
    qmask = jnp.asarray(_slot_masks(4, slot), BF16)
    hrows = LANES + _ones_rows(LANES)
    o = _flash(qkv, qkv, vt, qcol0=0, kcol0=ngr, ngroups=ngr, wq=BLK, wv=2 * hrows,
               wo=BLK, nh=4, vgroups=((0, LANES, 0, 2), (hrows, LANES, 2, 2)),
               qmask=qmask, tq=tq, tk=tk,
               diff=lambda_init, diff_params=(lq1, lk1, lq2, lk2, sub_gain))
    return x, o, w_out.astype(BF16)


def _mixer_d(x, b, s, gain, w_in, q_gain, kv_gain, w_uq, w_ukv, w_out, pos, tm, tq, tk):
    lane = np.arange(LANES)
    slot_h = np.where(lane < 32, 0, np.where(lane < 64, 1, np.where(lane < 80, 0, np.where(lane < 96, 1, -1))))
    slot = np.concatenate([slot_h, slot_h])
    nope_lane = lane < 64
    rope_lane = (lane >= 64) & (lane < 96)
    ngr = D_HEADS // 2

    w1 = jnp.zeros((D_MODEL, 4 * BLK), F32)
    w1 = w1.at[:, :D_Q_RANK].set(w_in[:, :D_Q_RANK])
    w1 = w1.at[:, 2 * BLK:3 * BLK].set(w_in[:, D_Q_RANK:D_Q_RANK + D_KV_RANK])
    kr_src = np.zeros((BLK,), np.int64)
    kr_on = np.zeros((BLK,), bool)
    for hf in range(2):
        for l in range(LANES):
            if rope_lane[l]:
                kr_src[hf * LANES + l] = D_Q_RANK + D_KV_RANK + hf * 16 + (l - 64) % 16
                kr_on[hf * LANES + l] = True
    w1 = w1.at[:, 3 * BLK:].set(jnp.where(jnp.asarray(kr_on)[None, :], w_in[:, kr_src], 0.0))
    cmb = _proj(x, 0, D_MODEL, D_MODEL, gain, w1.astype(BF16), tn=4 * BLK, out_dtype=F32, seq=s, tm=tm)

    qsrc = np.zeros((ngr * BLK,), np.int64)
    qon = np.zeros((ngr * BLK,), bool)
    ksrc = np.zeros((ngr * BLK,), np.int64)
    kon = np.zeros((ngr * BLK,), bool)
    for g in range(ngr):
        for hf in range(2):
            for l in range(LANES):
                idx = g * BLK + hf * LANES + l
                if slot_h[l] < 0:
                    continue
                head = 2 * g + slot_h[l]
                if nope_lane[l]:
                    d = hf * 32 + l % 32
                    qsrc[idx], qon[idx] = head * 96 + d, True
                    ksrc[idx], kon[idx] = head * 128 + d, True
                else:
                    d = hf * 16 + (l - 64) % 16
                    qsrc[idx], qon[idx] = head * 96 + D_NOPE + d, True
    wq2 = jnp.where(jnp.asarray(qon)[None, :], w_uq[:, qsrc], 0.0)
    wq2 = jnp.concatenate([wq2, jnp.zeros((2 * BLK - D_Q_RANK, ngr * BLK), F32)], axis=0).astype(BF16)
    wk2 = jnp.where(jnp.asarray(kon)[None, :], w_ukv[:, ksrc], 0.0).astype(BF16)
    vsrc = np.concatenate([h * 128 + D_NOPE + np.arange(64) for h in range(D_HEADS)])
    wv2 = w_ukv[:, vsrc].astype(BF16)
    qg = jnp.concatenate([q_gain, jnp.zeros((2 * BLK - D_Q_RANK,), F32)])

    half = D_ROPE // 2
    inv_freq = jnp.exp(jnp.arange(half, dtype=F32) * (-2.0 * math.log(ROPE_THETA) / D_ROPE))
    ang = pos.astype(F32)[:, None] * inv_freq[None, :]
    ones64 = jnp.ones((s, 64), F32)
    pad32 = jnp.ones((s, 32), F32)
    c = jnp.concatenate([ones64, jnp.cos(ang), jnp.cos(ang), pad32], axis=1)
    sn = jnp.concatenate([0.0 * ones64, jnp.sin(ang), jnp.sin(ang), 0.0 * pad32], axis=1)
    qs = (D_NOPE + D_ROPE) ** -0.5 * LOG2E
    ctab = jnp.stack([c * qs, c])
    stab = jnp.stack([sn * qs, sn])

    q = _proj(cmb, 0, 2 * BLK, D_Q_RANK, qg, wq2, tn=2 * BLK, out_dtype=BF16, seq=s, tm=tm,
              tables=(ctab, stab), tt=np.zeros((ngr // 2,), np.int32))
    k = _proj(cmb, 2, BLK, D_KV_RANK, kv_gain, wk2, tn=4 * BLK, out_dtype=BF16, seq=s, tm=tm,
              tables=(ctab, stab), tt=np.ones((ngr // 4,), np.int32), add=(cmb, 3))
    vt = _proj(cmb, 2, BLK, D_KV_RANK, kv_gain, wv2, tn=4 * BLK, out_dtype=BF16, seq=s, tm=tm,
               vt=(ngr, 2, 64, tk))
    q = q.reshape(b, s, ngr * BLK)
    k = k.reshape(b, s, ngr * BLK)
    qmask = jnp.asarray(_slot_masks(2, slot), BF16)
    hrows = 64 + _ones_rows(64)
    o = _flash(q, k, vt, qcol0=0, kcol0=0, ngroups=ngr, wq=BLK, wv=2 * hrows, wo=LANES,
               nh=2, vgroups=((0, 64, 0, 1), (hrows, 64, 1, 1)), qmask=qmask, tq=tq, tk=tk)
    return x, o, w_out.astype(BF16)


def _run_trunk(x3, mem3, p):
    b, s, _ = x3.shape
    t = b * s
    tm = 1024
    tm_mlp = 512
    tq, tq_mla, tk = 256, 512, 512
    x = x3.reshape(t, D_MODEL)
    pos = jnp.arange(s, dtype=F32)
    rows = jnp.repeat(jnp.arange(s // GRID_W, dtype=F32), GRID_W)
    cols_pos = jnp.tile(jnp.arange(GRID_W, dtype=F32), s // GRID_W)
    memf = mem3.reshape(b * N_MEM, D_MODEL)
    for i in range(DEPTH):
        m, j = i % 4, i // 4
        g = p['norm_mix'][i]
        if m == 0:
            x, o, w_out = _mixer_a(x, b, s, g, p['a_w_in'][j], p['a_w_out'][j], pos, tm)
        elif m == 1:
            x, o, w_out = _mixer_b(x, b, s, g, p['b_w_in'][j], p['b_q_norm'][j], p['b_k_norm'][j],
                                   p['b_w_out'][j], rows, cols_pos, tm, tq, tk)
        elif m == 2:
            x, o, w_out = _mixer_c(x, b, s, g, p['c_w_in'][j], p['c_lambda_q1'][j],
                                   p['c_lambda_k1'][j], p['c_lambda_q2'][j], p['c_lambda_k2'][j],
                                   p['c_sub_norm'][j], p['c_w_out'][j], pos,
                                   0.8 - 0.6 * math.exp(-0.3 * i), tm, tq, tk)
        else:
            x, o, w_out = _mixer_d(x, b, s, g, p['d_w_in'][j], p['d_q_norm'][j], p['d_kv_norm'][j],
                                   p['d_w_uq'][j], p['d_w_ukv'][j], p['d_w_out'][j], pos, tm,
                                   tq_mla, tk)
        kv = _proj(memf, 0, D_MODEL, D_MODEL, p['norm_mem'][i], p['w_xkv'][i].astype(BF16),
                   tn=2 * X_HEADS * X_HEAD_DIM, out_dtype=BF16, seq=N_MEM, tm=N_MEM)
        x = _post_mixer(x.reshape(b, s, D_MODEL), o, w_out, p['norm_x'][i],
                        p['w_xq'][i].astype(BF16), kv.reshape(b, N_MEM, 2 * X_HEADS * X_HEAD_DIM),
                        p['w_xo'][i].astype(BF16), p['norm_mlp'][i], p['w_mlp_in'][i].astype(BF16),
                        p['w_mlp_out'][i].astype(BF16), p['final_norm'], i == DEPTH - 1,
                        tm_mlp, 1024).reshape(t, D_MODEL)
    return x.reshape(b, s, D_MODEL)


def kernel(x_prompt, x_sample, mem_prompt, mem_sample, norm_mix, norm_x, norm_mem, w_xq, w_xkv, w_xo, norm_mlp, w_mlp_in, w_mlp_out, a_w_in, a_w_out, b_w_in, b_q_norm, b_k_norm, b_w_out, c_w_in, c_lambda_q1, c_lambda_k1, c_lambda_q2, c_lambda_k2, c_sub_norm, c_w_out, d_w_in, d_q_norm, d_kv_norm, d_w_uq, d_w_ukv, d_w_out, final_norm):
    p = dict(norm_mix=norm_mix, norm_x=norm_x, norm_mem=norm_mem, w_xq=w_xq, w_xkv=w_xkv,
             w_xo=w_xo, norm_mlp=norm_mlp, w_mlp_in=w_mlp_in, w_mlp_out=w_mlp_out,
             a_w_in=a_w_in, a_w_out=a_w_out, b_w_in=b_w_in, b_q_norm=b_q_norm,
             b_k_norm=b_k_norm, b_w_out=b_w_out, c_w_in=c_w_in, c_lambda_q1=c_lambda_q1,
             c_lambda_k1=c_lambda_k1, c_lambda_q2=c_lambda_q2, c_lambda_k2=c_lambda_k2,
             c_sub_norm=c_sub_norm, c_w_out=c_w_out, d_w_in=d_w_in, d_q_norm=d_q_norm,
             d_kv_norm=d_kv_norm, d_w_uq=d_w_uq, d_w_ukv=d_w_ukv, d_w_out=d_w_out,
             final_norm=final_norm)
    return (_run_trunk(x_prompt, mem_prompt, p), _run_trunk(x_sample, mem_sample, p))
```

```python
import functools
import math

import numpy as np
import jax
import jax.numpy as jnp
from jax import lax
from jax.experimental import pallas as pl
from jax.experimental.pallas import tpu as pltpu

F32 = jnp.float32
BF16 = jnp.bfloat16

D_MODEL = 1024
DEPTH = 4
N_MEM = 256
GRID_W = 64
D_FF = 4 * D_MODEL
NORM_EPS = 1e-6
ROPE_THETA = 500000.0
AXIAL_THETA = 10000.0
NEG_INF = -1e30

A_PATTERNS = ((128, 1), (512, 4), (2048, 16))
A_GROUPS = 3
A_HEADS = 8
A_IN = A_GROUPS * 3 * A_HEADS * 64
A_HALF_WINDOW = 64
B_HEADS = 16
B_KV_HEADS = 4
C_HEADS = 8
D_HEADS = 16
D_Q_RANK = 384
D_KV_RANK = 256
D_NOPE = 64
D_ROPE = 32
X_HEADS = 4
X_HEAD_DIM = 128

LOG2E = 1.4426950408889634
BF16_ROWS = 16
SCORE_SLOTS = 4
KEY_LOOP_TRIPS = 2
LANES = 128
BLK = 2 * LANES
VMEM_LIMIT = 56 * 1024 * 1024


def _ones_rows(nr):
    return LANES - nr if nr < LANES else BF16_ROWS


def _cparams(sem):
    return pltpu.CompilerParams(dimension_semantics=sem, vmem_limit_bytes=VMEM_LIMIT)


def _proj_kernel(tt_ref, src_ref, g_ref, w_ref, *rest, dnorm, rope, headnorm, add, nsub, vt, dil):
    del tt_ref
    rest = list(rest)
    c_ref = s_ref = hg_ref = bd_ref = add_ref = y_scr = None
    if dil > 1:
        y_scr = rest.pop()
    if rope:
        c_ref, s_ref = rest[0], rest[1]
        rest = rest[2:]
    if headnorm:
        hg_ref, bd_ref = rest[0], rest[1]
        rest = rest[2:]
    if add:
        add_ref = rest[0]
        rest = rest[1:]
    o_ref, h_scr = rest

    @pl.when(pl.program_id(1) == 0)
    def _():
        xf = src_ref[...].astype(F32)
        ms = jnp.sum(xf * xf, axis=-1, keepdims=True) * (1.0 / dnorm)
        h_scr[...] = (xf * lax.rsqrt(ms + NORM_EPS) * g_ref[...]).astype(BF16)

    y = jnp.dot(h_scr[...], w_ref[...], preferred_element_type=F32)
    if add:
        ad = add_ref[...]
        y = y + (ad if nsub == 1 else jnp.concatenate([ad] * nsub, axis=1))
    if headnorm:
        y2 = y * y
        hi = y2.astype(BF16)
        lo = (y2 - hi.astype(F32)).astype(BF16)
        parts = []
        for n in range(nsub):
            sl = slice(n * BLK, (n + 1) * BLK)
            ss = (jnp.dot(hi[:, sl], bd_ref[...], preferred_element_type=F32)
                  + jnp.dot(lo[:, sl], bd_ref[...], preferred_element_type=F32))
            parts.append(y[:, sl] * lax.rsqrt(ss * (1.0 / 64.0) + NORM_EPS))
        y = (parts[0] if nsub == 1 else jnp.concatenate(parts, axis=1)) * hg_ref[...]
    if rope:
        c = c_ref[0]
        s = s_ref[0]
        for n in range(nsub):
            y1 = y[:, n * BLK:n * BLK + LANES]
            y2 = y[:, n * BLK + LANES:(n + 1) * BLK]
            r1 = y1 * c - y2 * s
            r2 = y2 * c + y1 * s
            if dil > 1:
                y_scr[2 * n] = r1
                y_scr[2 * n + 1] = r2
            else:
                o_ref[:, n * BLK:n * BLK + LANES] = r1.astype(o_ref.dtype)
                o_ref[:, n * BLK + LANES:(n + 1) * BLK] = r2.astype(o_ref.dtype)
        if dil > 1:
            rows = y.shape[0] // dil
            for r in range(dil):
                for cb in range(2 * nsub):
                    o_ref[r, :, cb * LANES:(cb + 1) * LANES] = (
                        y_scr[cb, pl.ds(r, rows, stride=dil), :].astype(o_ref.dtype))
    elif vt:
        ngroups, nvg, nr, tk = vt
        nones = _ones_rows(nr)
        ones = jnp.ones((nones, tk), o_ref.dtype)
        for cc in range(y.shape[0] // tk):
            yt = y[cc * tk:(cc + 1) * tk, :].T
            for g in range(ngroups):
                for vg in range(nvg):
                    src0 = (g * nvg + vg) * nr
                    r0 = vg * (nr + nones)
                    o_ref[g, cc, r0:r0 + nr, :] = yt[src0:src0 + nr, :].astype(o_ref.dtype)
                    o_ref[g, cc, r0 + nr:r0 + nr + nones, :] = ones
    else:
        o_ref[...] = y.astype(o_ref.dtype)


def _proj(src, src_cb, kdim, dnorm, gain, w, *, tn, out_dtype, seq, tm,
          tables=None, tt=None, headnorm=None, add=None, vt=None, dil=1):
    t = src.shape[0]
    n = w.shape[1]
    nj = n // tn
    out_shape = jax.ShapeDtypeStruct((t, n), out_dtype)
    out_spec = pl.BlockSpec((tm, tn), lambda i, j, tt_: (i, j))
    if vt is not None:
        ngroups, nvg, nr, tk = vt
        assert nj == 1 and n == ngroups * nvg * nr and tm % tk == 0
        wv = nvg * (nr + _ones_rows(nr))
        out_shape = jax.ShapeDtypeStruct((t // seq, ngroups, seq // tk, wv, tk), out_dtype)
        out_spec = pl.BlockSpec((None, ngroups, tm // tk, wv, tk),
                                lambda i, j, tt_: (i // (seq // tm), 0, i % (seq // tm), 0, 0))
    scratch = [pltpu.VMEM((tm, kdim), BF16)]
    if dil > 1:
        assert tables is not None and tm % (16 * dil) == 0
        out_shape = jax.ShapeDtypeStruct((t // seq, dil, seq // dil, n), out_dtype)
        out_spec = pl.BlockSpec((None, dil, tm // dil, tn),
                                lambda i, j, tt_: (i // (seq // tm), 0, i % (seq // tm), j))
        scratch.append(pltpu.VMEM((tn // LANES, tm, LANES), F32))
    nsub = tn // BLK if (tables is not None or headnorm is not None) else 1
    ns = seq // tm
    if tt is None:
        tt = np.zeros((nj,), np.int32)
    in_specs = [
        pl.BlockSpec((tm, kdim), lambda i, j, tt_: (i, src_cb)),
        pl.BlockSpec((1, kdim), lambda i, j, tt_: (0, 0)),
        pl.BlockSpec((kdim, tn), lambda i, j, tt_: (0, j)),
    ]
    args = [src, gain.reshape(1, kdim).astype(F32), w]
    if tables is not None:
        for tb in tables:
            in_specs.append(pl.BlockSpec((1, tm, LANES), lambda i, j, tt_: (tt_[j], i % ns, 0)))
            args.append(tb)
    if headnorm is not None:
        in_specs.append(pl.BlockSpec((1, tn), lambda i, j, tt_: (0, j)))
        in_specs.append(pl.BlockSpec((BLK, BLK), lambda i, j, tt_: (0, 0)))
        args += [headnorm[0], headnorm[1]]
    if add is not None:
        add_arr, add_cb = add
        in_specs.append(pl.BlockSpec((tm, BLK), lambda i, j, tt_: (i, add_cb)))
        args.append(add_arr)
    kern = functools.partial(_proj_kernel, dnorm=dnorm, rope=tables is not None,
                             headnorm=headnorm is not None, add=add is not None, nsub=nsub, vt=vt,
                             dil=dil)
    return pl.pallas_call(
        kern,
        out_shape=out_shape,
        grid_spec=pltpu.PrefetchScalarGridSpec(
            num_scalar_prefetch=1,
            grid=(t // tm, nj),
            in_specs=in_specs,
            out_specs=out_spec,
            scratch_shapes=scratch,
        ),
        compiler_params=_cparams(("parallel", "arbitrary")),
    )(jnp.asarray(tt, jnp.int32), *args)


def _flash_kernel(*refs, nh, vgroups, tq, tk, nk, diff):
    if diff:
        (qmask_ref, q_ref, k_ref, vt_ref, lq1, lk1, lq2, lk2, sg_ref,
         o_ref, qm_scr, s_scr, mx_scr, m_scr, acc_scr) = refs
        lambda_init = diff
    else:
        qmask_ref, q_ref, k_ref, vt_ref, o_ref, qm_scr, s_scr, mx_scr, m_scr, acc_scr = refs

    qt = q_ref[...].astype(F32).T
    for j in range(nh):
        rowmask = jnp.concatenate([qmask_ref[j]] * (tq // LANES), axis=1)
        qm_scr[:, j * tq:(j + 1) * tq] = (qt * rowmask).astype(BF16)
    m_scr[...] = jnp.full(m_scr.shape, NEG_INF, F32)
    acc_scr[...] = jnp.zeros(acc_scr.shape, F32)

    ucols = min(2 * BLK, vgroups[0][3] * tq)
    units = []
    for gi, (r0, nr, h0, hn) in enumerate(vgroups):
        for off in range(0, hn * tq, ucols):
            units.append((gi, r0, nr, h0 * tq + off, off))

    def scores(c, slot, col0):
        cols = slice(col0, col0 + ucols)
        ks = pl.multiple_of(c * tk, tk)
        st = jnp.dot(k_ref[pl.ds(ks, tk), :], qm_scr[:, cols], preferred_element_type=F32)
        s_scr[slot, :, cols] = st
        mx_scr[slot, :, cols] = jnp.max(st, axis=0, keepdims=True)

    def step(c, slot, c_next, slot_next):
        vtc = vt_ref[c]
        m_prev = m_scr[...]
        m_new = jnp.maximum(m_prev, mx_scr[slot])
        alpha = jnp.exp2(m_prev - m_new)
        m_scr[...] = m_new
        for gi, r0, nr, col0, off in units:
            cols = slice(col0, col0 + ucols)
            scores(c_next, slot_next, col0)
            pt = jnp.exp2(s_scr[slot, :, cols] - m_new[:, cols]).astype(BF16)
            acols = slice(off, off + ucols)
            acc_scr[gi, :, acols] = acc_scr[gi, :, acols] * alpha[:, cols] + jnp.dot(
                vtc[r0:r0 + nr + _ones_rows(nr), :], pt, preferred_element_type=F32)

    for unit in units:
        scores(0, 0, unit[3])

    per_trip = nk // KEY_LOOP_TRIPS

    def body(i, carry):
        c = per_trip * i
        for u in range(per_trip):
            step(c + u, u % SCORE_SLOTS, jnp.minimum(c + u + 1, nk - 1), (u + 1) % SCORE_SLOTS)
        return carry

    lax.fori_loop(0, KEY_LOOP_TRIPS, body, 0)

    pieces = []
    if diff:
        lam = (jnp.exp(jnp.sum(lq1[...] * lk1[...], axis=-1, keepdims=True))
               - jnp.exp(jnp.sum(lq2[...] * lk2[...], axis=-1, keepdims=True)) + lambda_init)
        for gi, (r0, nr, h0, hn) in enumerate(vgroups):
            acc = acc_scr[gi]
            linv = 1.0 / acc[nr:nr + 1, :]
            oh = acc[:nr, :tq] * linv[:, :tq] - lam * (acc[:nr, tq:] * linv[:, tq:])
            ms = jnp.mean(oh * oh, axis=0, keepdims=True)
            pieces.append(oh * lax.rsqrt(ms + NORM_EPS))
    else:
        for gi, (r0, nr, h0, hn) in enumerate(vgroups):
            acc = acc_scr[gi]
            on = acc[:nr, :] * (1.0 / acc[nr:nr + 1, :])
            for jj in range(hn):
                pieces.append(on[:, jj * tq:(jj + 1) * tq])
    ot = pieces[0] if len(pieces) == 1 else jnp.concatenate(pieces, axis=0)
    o = ot.T
    if diff:
        o = o * sg_ref[...] * (1.0 - lambda_init)
    o_ref[...] = o.astype(o_ref.dtype)


def _flash(q, k, vt, *, qcol0, kcol0, ngroups, wq, wv, wo, nh, vgroups, qmask, tq, tk,
           diff=None, diff_params=None):
    b, s = q.shape[0], q.shape[1]
    nk = s // tk
    assert s % tk == 0 and nk % (KEY_LOOP_TRIPS * SCORE_SLOTS) == 0 and s % tq == 0
    qmask = jnp.broadcast_to(qmask.astype(F32)[:, :, None], (nh, wq, LANES))
    in_specs = [
        pl.BlockSpec((nh, wq, LANES), lambda bi, g, i: (0, 0, 0)),
        pl.BlockSpec((None, tq, wq), lambda bi, g, i: (bi, i, qcol0 + g)),
        pl.BlockSpec((None, s, wq), lambda bi, g, i: (bi, 0, kcol0 + g)),
        pl.BlockSpec((None, None, nk, wv, tk), lambda bi, g, i: (bi, g, 0, 0, 0)),
    ]
    args = [qmask, q, k, vt]
    if diff is not None:
        for prm in diff_params[:4]:
            in_specs.append(pl.BlockSpec((1, 64), lambda bi, g, i: (0, 0)))
            args.append(prm.reshape(1, 64).astype(F32))
        in_specs.append(pl.BlockSpec((1, wo), lambda bi, g, i: (0, 0)))
        args.append(jnp.tile(diff_params[4].astype(F32), wo // LANES).reshape(1, wo))
    kern = functools.partial(_flash_kernel, nh=nh, vgroups=tuple(vgroups), tq=tq, tk=tk, nk=nk,
                             diff=diff)
    nr, hn = vgroups[0][1], vgroups[0][3]
    return pl.pallas_call(
        kern,
        out_shape=jax.ShapeDtypeStruct((b, s, ngroups * wo), BF16),
        grid=(b, ngroups, s // tq),
        in_specs=in_specs,
        out_specs=pl.BlockSpec((None, tq, wo), lambda bi, g, i: (bi, i, g)),
        scratch_shapes=[
            pltpu.VMEM((wq, nh * tq), BF16),
            pltpu.VMEM((SCORE_SLOTS, tk, nh * tq), F32),
            pltpu.VMEM((SCORE_SLOTS, 1, nh * tq), F32),
            pltpu.VMEM((1, nh * tq), F32),
            pltpu.VMEM((len(vgroups), nr + _ones_rows(nr), hn * tq), F32),
        ],
        compiler_params=_cparams(("parallel", "parallel", "arbitrary")),
    )(*args)


def _band_kernel(qmask_ref, vmask_ref, q_ref, k_ref, v_ref, o_ref, lse_ref, *, tq, win, length):
    nh = 4
    i = pl.program_id(2)
    ks = jnp.clip(i * tq - A_HALF_WINDOW, 0, length - win)
    ks = pl.multiple_of(ks, A_HALF_WINDOW)
    kc = k_ref[pl.ds(ks, win), :]
    vc = v_ref[pl.ds(ks, win), :]
    q = q_ref[...]
    qm = jnp.concatenate([q * qmask_ref[j:j + 1, :] for j in range(nh)], axis=0)
    s = lax.dot_general(qm, kc, (((1,), (1,)), ((), ())), preferred_element_type=F32)
    qpos = i * tq + lax.broadcasted_iota(jnp.int32, (tq, win), 0)
    kpos = ks + lax.broadcasted_iota(jnp.int32, (tq, win), 1)
    valid = jnp.abs(qpos - kpos) <= A_HALF_WINDOW
    vmask = vmask_ref[...]
    vmask_b = vmask.astype(BF16)
    ps = []
    inv = None
    lse = None
    for j in range(nh):
        sj = jnp.where(valid, s[j * tq:(j + 1) * tq], NEG_INF)
        mj = jnp.max(sj, axis=-1, keepdims=True)
        pj = jnp.exp2(sj - mj)
        lj = jnp.sum(pj, axis=-1, keepdims=True)
        ps.append(pj.astype(BF16))
        t_inv = (1.0 / lj) * vmask[j:j + 1, :]
        t_lse = ((mj + jnp.log2(lj)) * (1.0 / LOG2E)) * vmask[j:j + 1, :]
        inv = t_inv if inv is None else inv + t_inv
        lse = t_lse if lse is None else lse + t_lse
    lhs = jnp.concatenate(ps, axis=1)
    rhs = jnp.concatenate([vc * vmask_b[j:j + 1, :] for j in range(nh)], axis=0)
    pv = jnp.dot(lhs, rhs, preferred_element_type=F32)
    o_ref[...] = (pv * inv).astype(o_ref.dtype)
    lse_ref[...] = lse


def _band_attention(qkv, qmask, vmask):
    b, dil, length, _ = qkv.shape
    tq = min(256, length)
    win = min(tq + 2 * A_HALF_WINDOW, length)

    def col(which):
        return lambda bi, a, i: (bi, a // 2, 0, which * 2 + a % 2)

    in_specs = [
        pl.BlockSpec((4, BLK), lambda bi, a, i: (0, 0)),
        pl.BlockSpec((4, BLK), lambda bi, a, i: (0, 0)),
        pl.BlockSpec((None, None, tq, BLK), lambda bi, a, i: (bi, a // 2, i, a % 2)),
        pl.BlockSpec((None, None, length, BLK), col(1)),
        pl.BlockSpec((None, None, length, BLK), col(2)),
    ]
    kern = functools.partial(_band_kernel, tq=tq, win=win, length=length)
    out_block = pl.BlockSpec((None, None, tq, BLK), lambda bi, a, i: (bi, a // 2, i, a % 2))
    return pl.pallas_call(
        kern,
        out_shape=(jax.ShapeDtypeStruct((b, dil, length, 2 * BLK), BF16),
                   jax.ShapeDtypeStruct((b, dil, length, 2 * BLK), F32)),
        grid=(b, dil * 2, length // tq),
        in_specs=in_specs,
        out_specs=(out_block, out_block),
        compiler_params=_cparams(("parallel", "parallel", "arbitrary")),
    )(qmask, vmask, qkv, qkv, qkv)


def _a_out_kernel(*refs, dils, tm):
    ng = len(dils)
    o_refs, l_refs = refs[:ng], refs[ng:2 * ng]
    w_ref, x_ref, out_ref = refs[2 * ng:2 * ng + 3]
    scr = list(refs[2 * ng + 3:])

    def token_order(ref, dil):
        if dil == 1:
            return ref[0].astype(F32)
        buf = scr.pop(0)
        ncb = buf.shape[0]
        for r in range(dil):
            v = ref[r].astype(F32)
            for cb in range(ncb):
                buf[cb, pl.ds(r, tm // dil, stride=dil), :] = v[:, cb * LANES:(cb + 1) * LANES]
        return jnp.concatenate([buf[cb] for cb in range(ncb)], axis=1)

    ls = [token_order(l_refs[g], dils[g]) for g in range(ng)]
    os_ = [token_order(o_refs[g], dils[g]) for g in range(ng)]
    mx = functools.reduce(jnp.maximum, ls)
    es = [jnp.exp(l - mx) for l in ls]
    inv = 1.0 / functools.reduce(jnp.add, es)
    o = functools.reduce(jnp.add, [e * og for e, og in zip(es, os_)]) * inv
    out_ref[...] = x_ref[...] + jnp.dot(o.astype(BF16), w_ref[...], preferred_element_type=F32)


def _a_out(os_, lses, w, x3, tm):
    b, s, _ = x3.shape
    kd = w.shape[0]
    dils = tuple(o.shape[1] for o in os_)
    grp = [pl.BlockSpec((None, d, tm // d, kd), lambda bi, i: (bi, 0, i, 0)) for d in dils]
    row = pl.BlockSpec((None, tm, D_MODEL), lambda bi, i: (bi, i, 0))
    nscr = 2 * sum(1 for d in dils if d > 1)
    return pl.pallas_call(
        functools.partial(_a_out_kernel, dils=dils, tm=tm),
        out_shape=jax.ShapeDtypeStruct((b, s, D_MODEL), F32),
        grid=(b, s // tm),
        in_specs=grp + grp + [pl.BlockSpec((kd, D_MODEL), lambda bi, i: (0, 0)), row],
        out_specs=row,
        scratch_shapes=[pltpu.VMEM((kd // LANES, tm, LANES), F32)] * nscr,
        compiler_params=_cparams(("parallel", "parallel")),
    )(*os_, *lses, w, x3)


def _post_kernel(*refs, has_proj, final):
    if has_proj:
        (x_ref, o_ref, wout_ref, gx_ref, wq_ref, kv_ref, wo_ref, gm_ref, w1_ref, w2_ref, fg_ref,
         out_ref, x2_scr, h_scr, acc_scr) = refs
    else:
        (x_ref, gx_ref, wq_ref, kv_ref, wo_ref, gm_ref, w1_ref, w2_ref, fg_ref,
         out_ref, x2_scr, h_scr, acc_scr) = refs
    f = pl.program_id(2)

    @pl.when(f == 0)
    def _():
        x1 = x_ref[...]
        if has_proj:
            x1 = x1 + jnp.dot(o_ref[...], wout_ref[...], preferred_element_type=F32)
        ms = jnp.mean(x1 * x1, axis=-1, keepdims=True)
        h = (x1 * lax.rsqrt(ms + NORM_EPS) * gx_ref[...]).astype(BF16)
        qb = (jnp.dot(h, wq_ref[...], preferred_element_type=F32)
              * (X_HEAD_DIM ** -0.5 * LOG2E)).astype(BF16)
        hd = X_HEADS * X_HEAD_DIM
        outs = []
        for hh in range(X_HEADS):
            qh = qb[:, hh * X_HEAD_DIM:(hh + 1) * X_HEAD_DIM]
            kh = kv_ref[:, hh * X_HEAD_DIM:(hh + 1) * X_HEAD_DIM]
            vh = kv_ref[:, hd + hh * X_HEAD_DIM:hd + (hh + 1) * X_HEAD_DIM]
            s = lax.dot_general(qh, kh, (((1,), (1,)), ((), ())), preferred_element_type=F32)
            m = jnp.max(s, axis=-1, keepdims=True)
            p = jnp.exp2(s - m)
            l = jnp.sum(p, axis=-1, keepdims=True)
            oh = jnp.dot(p.astype(BF16), vh, preferred_element_type=F32) * (1.0 / l)
            outs.append(oh.astype(BF16))
        x2 = x1 + jnp.dot(jnp.concatenate(outs, axis=1), wo_ref[...], preferred_element_type=F32)
        x2_scr[...] = x2
        ms2 = jnp.mean(x2 * x2, axis=-1, keepdims=True)
        h_scr[...] = (x2 * lax.rsqrt(ms2 + NORM_EPS) * gm_ref[...]).astype(BF16)
        acc_scr[...] = jnp.zeros(acc_scr.shape, F32)

    a = jnp.maximum(jnp.dot(h_scr[...], w1_ref[...], preferred_element_type=F32), 0.0)
    acc_scr[...] += jnp.dot((a * a).astype(BF16), w2_ref[...], preferred_element_type=F32)

    @pl.when(f == pl.num_programs(2) - 1)
    def _():
        y = x2_scr[...] + acc_scr[...]
        if final:
            ms = jnp.mean(y * y, axis=-1, keepdims=True)
            y = y * lax.rsqrt(ms + NORM_EPS) * fg_ref[...]
        out_ref[...] = y


def _post_mixer(x3, o3, w_out, gx, wq, kv3, wo, gm, w1, w2, final_gain, final, tm, tf):
    b, s, _ = x3.shape
    hd = X_HEADS * X_HEAD_DIM
    has_proj = o3 is not None
    const = lambda bi, i, f: (0, 0)
    row = lambda bi, i, f: (bi, i, 0)
    in_specs = [pl.BlockSpec((None, tm, D_MODEL), row)]
    args = [x3]
    if has_proj:
        kd = o3.shape[-1]
        in_specs += [pl.BlockSpec((None, tm, kd), row), pl.BlockSpec((kd, D_MODEL), const)]
        args += [o3, w_out]
    in_specs += [pl.BlockSpec((1, D_MODEL), const),
                 pl.BlockSpec((D_MODEL, hd), const),
                 pl.BlockSpec((None, N_MEM, 2 * hd), lambda bi, i, f: (bi, 0, 0)),
                 pl.BlockSpec((hd, D_MODEL), const),
                 pl.BlockSpec((1, D_MODEL), const),
                 pl.BlockSpec((D_MODEL, tf), lambda bi, i, f: (0, f)),
                 pl.BlockSpec((tf, D_MODEL), lambda bi, i, f: (f, 0)),
                 pl.BlockSpec((1, D_MODEL), const)]
    args += [gx.reshape(1, D_MODEL), wq, kv3, wo, gm.reshape(1, D_MODEL), w1, w2,
             final_gain.reshape(1, D_MODEL)]
    return pl.pallas_call(
        functools.partial(_post_kernel, has_proj=has_proj, final=final),
        out_shape=jax.ShapeDtypeStruct((b, s, D_MODEL), F32),
        grid=(b, s // tm, D_FF // tf),
        in_specs=in_specs,
        out_specs=pl.BlockSpec((None, tm, D_MODEL), row),
        scratch_shapes=[pltpu.VMEM((tm, D_MODEL), F32), pltpu.VMEM((tm, D_MODEL), BF16),
                        pltpu.VMEM((tm, D_MODEL), F32)],
        compiler_params=_cparams(("parallel", "parallel", "arbitrary")),
    )(*args)


_F_ROT16 = np.array(list(range(0, 8)) + list(range(16, 40)))
_P_ROT16 = np.array(list(range(8, 16)) + list(range(40, 64)))
_F_AXIAL = np.array(list(range(0, 16)) + list(range(32, 48)))
_P_AXIAL = np.array(list(range(16, 32)) + list(range(48, 64)))


def _block_dims(first, partner):
    lane = np.arange(BLK)
    half, slot, u = lane // LANES, (lane % LANES) // 32, lane % 32
    return np.where(half == 0, first[u], partner[u]), slot


def _slot_masks(nslot, slot_of_lane):
    return np.stack([(slot_of_lane == j) for j in range(nslot)]).astype(np.float32)


def _rope_tables(pos_list, theta, rot, scale_list, npad):
    half = rot // 2
    inv_freq = jnp.exp(jnp.arange(half, dtype=F32) * (-2.0 * math.log(theta) / rot))
    cs, ss = [], []
    for pos in pos_list:
        ang = pos.astype(F32)[:, None] * inv_freq[None, :]
        cs.append(jnp.cos(ang))
        ss.append(jnp.sin(ang))
    c = jnp.concatenate(cs, axis=1)
    s = jnp.concatenate(ss, axis=1)
    n = c.shape[0]
    if npad:
        c = jnp.concatenate([c, jnp.ones((n, npad), F32)], axis=1)
        s = jnp.concatenate([s, jnp.zeros((n, npad), F32)], axis=1)
    reps = LANES // c.shape[1]
    c = jnp.tile(c, (1, reps))
    s = jnp.tile(s, (1, reps))
    ctab = [c * sc for sc in scale_list] + [jnp.ones_like(c)]
    stab = [s * sc for sc in scale_list] + [jnp.zeros_like(s)]
    return jnp.stack(ctab), jnp.stack(stab)


def _mixer_a(x, b, s, gain, w_in, w_out, pos, tm):
    dims, slot = _block_dims(_F_ROT16, _P_ROT16)
    ctab, stab = _rope_tables([pos], ROPE_THETA, 16, [0.125 * LOG2E, 1.0], 24)
    qmask = jnp.asarray(_slot_masks(4, slot), BF16)
    vmask = jnp.asarray(_slot_masks(4, np.arange(BLK) // 64), F32)
    w_bf = w_in.astype(BF16)
    outs, lses = [], []
    for wg, (window, dil) in enumerate(A_PATTERNS):
        assert window // (2 * dil) == A_HALF_WINDOW
        cols = []
        for which in range(3):
            base = (wg * 3 + which) * A_HEADS * 64
            for hg in range(2):
                cols.append(base + hg * BLK + (slot * 64 + dims if which < 2 else np.arange(BLK)))
        w = w_bf[:, np.concatenate(cols)]
        qkv = _proj(x, 0, D_MODEL, D_MODEL, gain, w, tn=2 * BLK, out_dtype=BF16, seq=s, tm=tm,
                    tables=(ctab, stab), tt=np.array([0, 1, 2], np.int32), dil=dil)
        o, lse = _band_attention(qkv.reshape(b, dil, s // dil, 3 * 2 * BLK), qmask, vmask)
        outs.append(o)
        lses.append(lse)
    x3 = _a_out(outs, lses, w_out.astype(BF16), x.reshape(b, s, D_MODEL), min(tm, 512))
    return x3.reshape(b * s, D_MODEL), None, None


def _mixer_b(x, b, s, gain, w_in, q_gain, k_gain, w_out, rows, cols_pos, tm, tq, tk):
    dims, slot = _block_dims(_F_AXIAL, _P_AXIAL)
    qcols = np.concatenate([g * BLK + slot * 64 + dims for g in range(B_KV_HEADS)])
    kcols = np.concatenate([B_HEADS * 64 + g * 64 + dims for g in range(B_KV_HEADS)])
    w_bf = w_in.astype(BF16)
    wqk = w_bf[:, np.concatenate([qcols, kcols])]
    wv = w_bf[:, (B_HEADS + B_KV_HEADS) * 64:]
    hgain = jnp.concatenate([jnp.tile(q_gain[dims] * (0.125 * LOG2E), B_KV_HEADS),
                             jnp.tile(k_gain[dims], B_KV_HEADS)]).reshape(1, -1).astype(F32)
    bd = jnp.asarray(slot[:, None] == slot[None, :], BF16)
    ctab, stab = _rope_tables([rows, cols_pos], AXIAL_THETA, 32, [1.0], 0)
    n = B_KV_HEADS * BLK
    qk = _proj(x, 0, D_MODEL, D_MODEL, gain, wqk, tn=2 * BLK, out_dtype=BF16, seq=s, tm=tm,
               tables=(ctab, stab), tt=np.zeros((n // BLK,), np.int32), headnorm=(hgain, bd))
    vt = _proj(x, 0, D_MODEL, D_MODEL, gain, wv, tn=B_KV_HEADS * 64, out_dtype=BF16, seq=s, tm=tm,
               vt=(B_KV_HEADS, 1, 64, tk))
    qk = qk.reshape(b, s, 2 * n)
    qmask = jnp.asarray(_slot_masks(4, slot), BF16)
    o = _flash(qk, qk, vt, qcol0=0, kcol0=B_KV_HEADS, ngroups=B_KV_HEADS, wq=BLK,
               wv=64 + _ones_rows(64), wo=BLK, nh=4, vgroups=((0, 64, 0, 4),), qmask=qmask,
               tq=tq, tk=tk)
    return x, o, w_out.astype(BF16)


def _mixer_c(x, b, s, gain, w_in, lq1, lk1, lq2, lk2, sub_gain, w_out, pos, lambda_init, tm, tq, tk):
    dims, slot = _block_dims(_F_ROT16, _P_ROT16)
    ngr = C_HEADS // 2
    cols = []
    for which in range(2):
        for g in range(ngr):
            cols.append(which * C_HEADS * 128 + g * BLK + slot * 64 + dims)
    w_bf = w_in.astype(BF16)
    ctab, stab = _rope_tables([pos], ROPE_THETA, 16, [0.125 * LOG2E, 1.0], 24)
    n = ngr * BLK
    qk = _proj(x, 0, D_MODEL, D_MODEL, gain, w_bf[:, np.concatenate(cols)], tn=2 * BLK, out_dtype=BF16,
               seq=s, tm=tm, tables=(ctab, stab), tt=np.array([0, 0, 1, 1], np.int32))
    qkv = qk.reshape(b, s, 2 * n)
    vt = _proj(x, 0, D_MODEL, D_MODEL, gain, w_bf[:, 2 * C_HEADS * 128:], tn=n, out_dtype=BF16,
               seq=s, tm=tm, vt=(ngr, 2, LANES, tk))
    qmask = jnp.asarray(_slot_masks(4, slot), BF16)
    hrows = LANES + _ones_rows(LANES)
    o = _flash(qkv, qkv, vt, qcol0=0, kcol0=ngr, ngroups=ngr, wq=BLK, wv=2 * hrows,
               wo=BLK, nh=4, vgroups=((0, LANES, 0, 2), (hrows, LANES, 2, 2)),
               qmask=qmask, tq=tq, tk=tk,
               diff=lambda_init, diff_params=(lq1, lk1, lq2, lk2, sub_gain))
    return x, o, w_out.astype(BF16)


def _mixer_d(x, b, s, gain, w_in, q_gain, kv_gain, w_uq, w_ukv, w_out, pos, tm, tq, tk):
    lane = np.arange(LANES)
    slot_h = np.where(lane < 32, 0, np.where(lane < 64, 1, np.where(lane < 80, 0, np.where(lane < 96, 1, -1))))
    slot = np.concatenate([slot_h, slot_h])
    nope_lane = lane < 64
    rope_lane = (lane >= 64) & (lane < 96)
    ngr = D_HEADS // 2

    w1 = jnp.zeros((D_MODEL, 4 * BLK), F32)
    w1 = w1.at[:, :D_Q_RANK].set(w_in[:, :D_Q_RANK])
    w1 = w1.at[:, 2 * BLK:3 * BLK].set(w_in[:, D_Q_RANK:D_Q_RANK + D_KV_RANK])
    kr_src = np.zeros((BLK,), np.int64)
    kr_on = np.zeros((BLK,), bool)
    for hf in range(2):
        for l in range(LANES):
            if rope_lane[l]:
                kr_src[hf * LANES + l] = D_Q_RANK + D_KV_RANK + hf * 16 + (l - 64) % 16
                kr_on[hf * LANES + l] = True
    w1 = w1.at[:, 3 * BLK:].set(jnp.where(jnp.asarray(kr_on)[None, :], w_in[:, kr_src], 0.0))
    cmb = _proj(x, 0, D_MODEL, D_MODEL, gain, w1.astype(BF16), tn=4 * BLK, out_dtype=F32, seq=s, tm=tm)

    qsrc = np.zeros((ngr * BLK,), np.int64)
    qon = np.zeros((ngr * BLK,), bool)
    ksrc = np.zeros((ngr * BLK,), np.int64)
    kon = np.zeros((ngr * BLK,), bool)
    for g in range(ngr):
        for hf in range(2):
            for l in range(LANES):
                idx = g * BLK + hf * LANES + l
                if slot_h[l] < 0:
                    continue
                head = 2 * g + slot_h[l]
                if nope_lane[l]:
                    d = hf * 32 + l % 32
                    qsrc[idx], qon[idx] = head * 96 + d, True
                    ksrc[idx], kon[idx] = head * 128 + d, True
                else:
                    d = hf * 16 + (l - 64) % 16
                    qsrc[idx], qon[idx] = head * 96 + D_NOPE + d, True
    wq2 = jnp.where(jnp.asarray(qon)[None, :], w_uq[:, qsrc], 0.0)
    wq2 = jnp.concatenate([wq2, jnp.zeros((2 * BLK - D_Q_RANK, ngr * BLK), F32)], axis=0).astype(BF16)
    wk2 = jnp.where(jnp.asarray(kon)[None, :], w_ukv[:, ksrc], 0.0).astype(BF16)
    vsrc = np.concatenate([h * 128 + D_NOPE + np.arange(64) for h in range(D_HEADS)])
    wv2 = w_ukv[:, vsrc].astype(BF16)
    qg = jnp.concatenate([q_gain, jnp.zeros((2 * BLK - D_Q_RANK,), F32)])

    half = D_ROPE // 2
    inv_freq = jnp.exp(jnp.arange(half, dtype=F32) * (-2.0 * math.log(ROPE_THETA) / D_ROPE))
    ang = pos.astype(F32)[:, None] * inv_freq[None, :]
    ones64 = jnp.ones((s, 64), F32)
    pad32 = jnp.ones((s, 32), F32)
    c = jnp.concatenate([ones64, jnp.cos(ang), jnp.cos(ang), pad32], axis=1)
    sn = jnp.concatenate([0.0 * ones64, jnp.sin(ang), jnp.sin(ang), 0.0 * pad32], axis=1)
    qs = (D_NOPE + D_ROPE) ** -0.5 * LOG2E
    ctab = jnp.stack([c * qs, c])
    stab = jnp.stack([sn * qs, sn])

    q = _proj(cmb, 0, 2 * BLK, D_Q_RANK, qg, wq2, tn=2 * BLK, out_dtype=BF16, seq=s, tm=tm,
              tables=(ctab, stab), tt=np.zeros((ngr // 2,), np.int32))
    k = _proj(cmb, 2, BLK, D_KV_RANK, kv_gain, wk2, tn=4 * BLK, out_dtype=BF16, seq=s, tm=tm,
              tables=(ctab, stab), tt=np.ones((ngr // 4,), np.int32), add=(cmb, 3))
    vt = _proj(cmb, 2, BLK, D_KV_RANK, kv_gain, wv2, tn=4 * BLK, out_dtype=BF16, seq=s, tm=tm,
               vt=(ngr, 2, 64, tk))
    q = q.reshape(b, s, ngr * BLK)
    k = k.reshape(b, s, ngr * BLK)
    qmask = jnp.asarray(_slot_masks(2, slot), BF16)
    hrows = 64 + _ones_rows(64)
    o = _flash(q, k, vt, qcol0=0, kcol0=0, ngroups=ngr, wq=BLK, wv=2 * hrows, wo=LANES,
               nh=2, vgroups=((0, 64, 0, 1), (hrows, 64, 1, 1)), qmask=qmask, tq=tq, tk=tk)
    return x, o, w_out.astype(BF16)


def _run_trunk(x3, mem3, p):
    b, s, _ = x3.shape
    t = b * s
    tm = 1024
    tm_mlp = 512
    tq, tq_mla, tk = 512, 1024, 512
    x = x3.reshape(t, D_MODEL)
    pos = jnp.arange(s, dtype=F32)
    rows = jnp.repeat(jnp.arange(s // GRID_W, dtype=F32), GRID_W)
    cols_pos = jnp.tile(jnp.arange(GRID_W, dtype=F32), s // GRID_W)
    memf = mem3.reshape(b * N_MEM, D_MODEL)
    for i in range(DEPTH):
        m, j = i % 4, i // 4
        g = p['norm_mix'][i]
        if m == 0:
            x, o, w_out = _mixer_a(x, b, s, g, p['a_w_in'][j], p['a_w_out'][j], pos, tm)
        elif m == 1:
            x, o, w_out = _mixer_b(x, b, s, g, p['b_w_in'][j], p['b_q_norm'][j], p['b_k_norm'][j],
                                   p['b_w_out'][j], rows, cols_pos, tm, tq, tk)
        elif m == 2:
            x, o, w_out = _mixer_c(x, b, s, g, p['c_w_in'][j], p['c_lambda_q1'][j],
                                   p['c_lambda_k1'][j], p['c_lambda_q2'][j], p['c_lambda_k2'][j],
                                   p['c_sub_norm'][j], p['c_w_out'][j], pos,
                                   0.8 - 0.6 * math.exp(-0.3 * i), tm, tq, tk)
        else:
            x, o, w_out = _mixer_d(x, b, s, g, p['d_w_in'][j], p['d_q_norm'][j], p['d_kv_norm'][j],
                                   p['d_w_uq'][j], p['d_w_ukv'][j], p['d_w_out'][j], pos, tm,
                                   tq_mla, tk)
        kv = _proj(memf, 0, D_MODEL, D_MODEL, p['norm_mem'][i], p['w_xkv'][i].astype(BF16),
                   tn=2 * X_HEADS * X_HEAD_DIM, out_dtype=BF16, seq=N_MEM, tm=N_MEM)
        x = _post_mixer(x.reshape(b, s, D_MODEL), o, w_out, p['norm_x'][i],
                        p['w_xq'][i].astype(BF16), kv.reshape(b, N_MEM, 2 * X_HEADS * X_HEAD_DIM),
                        p['w_xo'][i].astype(BF16), p['norm_mlp'][i], p['w_mlp_in'][i].astype(BF16),
                        p['w_mlp_out'][i].astype(BF16), p['final_norm'], i == DEPTH - 1,
                        tm_mlp, 1024).reshape(t, D_MODEL)
    return x.reshape(b, s, D_MODEL)


def kernel(x_prompt, x_sample, mem_prompt, mem_sample, norm_mix, norm_x, norm_mem, w_xq, w_xkv, w_xo, norm_mlp, w_mlp_in, w_mlp_out, a_w_in, a_w_out, b_w_in, b_q_norm, b_k_norm, b_w_out, c_w_in, c_lambda_q1, c_lambda_k1, c_lambda_q2, c_lambda_k2, c_sub_norm, c_w_out, d_w_in, d_q_norm, d_kv_norm, d_w_uq, d_w_ukv, d_w_out, final_norm):
    p = dict(norm_mix=norm_mix, norm_x=norm_x, norm_mem=norm_mem, w_xq=w_xq, w_xkv=w_xkv,
             w_xo=w_xo, norm_mlp=norm_mlp, w_mlp_in=w_mlp_in, w_mlp_out=w_mlp_out,
             a_w_in=a_w_in, a_w_out=a_w_out, b_w_in=b_w_in, b_q_norm=b_q_norm,
             b_k_norm=b_k_norm, b_w_out=b_w_out, c_w_in=c_w_in, c_lambda_q1=c_lambda_q1,
             c_lambda_k1=c_lambda_k1, c_lambda_q2=c_lambda_q2, c_lambda_k2=c_lambda_k2,
             c_sub_norm=c_sub_norm, c_w_out=c_w_out, d_w_in=d_w_in, d_q_norm=d_q_norm,
             d_kv_norm=d_kv_norm, d_w_uq=d_w_uq, d_w_ukv=d_w_ukv, d_w_out=d_w_out,
             final_norm=final_norm)
    return (_run_trunk(x_prompt, mem_prompt, p), _run_trunk(x_sample, mem_sample, p))
```

```python
import functools
import math

import numpy as np
import jax
import jax.numpy as jnp
from jax import lax
from jax.experimental import pallas as pl
from jax.experimental.pallas import tpu as pltpu

F32 = jnp.float32
BF16 = jnp.bfloat16

D_MODEL = 1024
DEPTH = 4
N_MEM = 256
GRID_W = 64
D_FF = 4 * D_MODEL
NORM_EPS = 1e-6
ROPE_THETA = 500000.0
AXIAL_THETA = 10000.0
NEG_INF = -1e30

A_PATTERNS = ((128, 1), (512, 4), (2048, 16))
A_GROUPS = 3
A_HEADS = 8
A_IN = A_GROUPS * 3 * A_HEADS * 64
A_HALF_WINDOW = 64
B_HEADS = 16
B_KV_HEADS = 4
C_HEADS = 8
D_HEADS = 16
D_Q_RANK = 384
D_KV_RANK = 256
D_NOPE = 64
D_ROPE = 32
X_HEADS = 4
X_HEAD_DIM = 128

LOG2E = 1.4426950408889634
BF16_ROWS = 16
SCORE_SLOTS = 2
KEY_LOOP_TRIPS = 2
LANES = 128
BLK = 2 * LANES
VMEM_LIMIT = 56 * 1024 * 1024


def _ones_rows(nr):
    return LANES - nr if nr < LANES else BF16_ROWS


def _cparams(sem):
    return pltpu.CompilerParams(dimension_semantics=sem, vmem_limit_bytes=VMEM_LIMIT)


def _proj_kernel(tt_ref, src_ref, g_ref, w_ref, *rest, dnorm, rope, headnorm, add, nsub, vt, dil):
    del tt_ref
    rest = list(rest)
    c_ref = s_ref = hg_ref = bd_ref = add_ref = y_scr = None
    if dil > 1:
        y_scr = rest.pop()
    if rope:
        c_ref, s_ref = rest[0], rest[1]
        rest = rest[2:]
    if headnorm:
        hg_ref, bd_ref = rest[0], rest[1]
        rest = rest[2:]
    if add:
        add_ref = rest[0]
        rest = rest[1:]
    o_ref, h_scr = rest

    @pl.when(pl.program_id(1) == 0)
    def _():
        xf = src_ref[...].astype(F32)
        ms = jnp.sum(xf * xf, axis=-1, keepdims=True) * (1.0 / dnorm)
        h_scr[...] = (xf * lax.rsqrt(ms + NORM_EPS) * g_ref[...]).astype(BF16)

    y = jnp.dot(h_scr[...], w_ref[...], preferred_element_type=F32)
    if add:
        ad = add_ref[...]
        y = y + (ad if nsub == 1 else jnp.concatenate([ad] * nsub, axis=1))
    if headnorm:
        y2 = y * y
        hi = y2.astype(BF16)
        lo = (y2 - hi.astype(F32)).astype(BF16)
        parts = []
        for n in range(nsub):
            sl = slice(n * BLK, (n + 1) * BLK)
            ss = (jnp.dot(hi[:, sl], bd_ref[...], preferred_element_type=F32)
                  + jnp.dot(lo[:, sl], bd_ref[...], preferred_element_type=F32))
            parts.append(y[:, sl] * lax.rsqrt(ss * (1.0 / 64.0) + NORM_EPS))
        y = (parts[0] if nsub == 1 else jnp.concatenate(parts, axis=1)) * hg_ref[...]
    if rope:
        c = c_ref[0]
        s = s_ref[0]
        for n in range(nsub):
            y1 = y[:, n * BLK:n * BLK + LANES]
            y2 = y[:, n * BLK + LANES:(n + 1) * BLK]
            r1 = y1 * c - y2 * s
            r2 = y2 * c + y1 * s
            if dil > 1:
                y_scr[2 * n] = r1
                y_scr[2 * n + 1] = r2
            else:
                o_ref[:, n * BLK:n * BLK + LANES] = r1.astype(o_ref.dtype)
                o_ref[:, n * BLK + LANES:(n + 1) * BLK] = r2.astype(o_ref.dtype)
        if dil > 1:
            rows = y.shape[0] // dil
            for r in range(dil):
                for cb in range(2 * nsub):
                    o_ref[r, :, cb * LANES:(cb + 1) * LANES] = (
                        y_scr[cb, pl.ds(r, rows, stride=dil), :].astype(o_ref.dtype))
    elif vt:
        ngroups, nvg, nr, tk = vt
        nones = _ones_rows(nr)
        ones = jnp.ones((nones, tk), o_ref.dtype)
        for cc in range(y.shape[0] // tk):
            yt = y[cc * tk:(cc + 1) * tk, :].T
            for g in range(ngroups):
                for vg in range(nvg):
                    src0 = (g * nvg + vg) * nr
                    r0 = vg * (nr + nones)
                    o_ref[g, cc, r0:r0 + nr, :] = yt[src0:src0 + nr, :].astype(o_ref.dtype)
                    o_ref[g, cc, r0 + nr:r0 + nr + nones, :] = ones
    else:
        o_ref[...] = y.astype(o_ref.dtype)


def _proj(src, src_cb, kdim, dnorm, gain, w, *, tn, out_dtype, seq, tm,
          tables=None, tt=None, headnorm=None, add=None, vt=None, dil=1):
    t = src.shape[0]
    n = w.shape[1]
    nj = n // tn
    out_shape = jax.ShapeDtypeStruct((t, n), out_dtype)
    out_spec = pl.BlockSpec((tm, tn), lambda i, j, tt_: (i, j))
    if vt is not None:
        ngroups, nvg, nr, tk = vt
        assert nj == 1 and n == ngroups * nvg * nr and tm % tk == 0
        wv = nvg * (nr + _ones_rows(nr))
        out_shape = jax.ShapeDtypeStruct((t // seq, ngroups, seq // tk, wv, tk), out_dtype)
        out_spec = pl.BlockSpec((None, ngroups, tm // tk, wv, tk),
                                lambda i, j, tt_: (i // (seq // tm), 0, i % (seq // tm), 0, 0))
    scratch = [pltpu.VMEM((tm, kdim), BF16)]
    if dil > 1:
        assert tables is not None and tm % (16 * dil) == 0
        out_shape = jax.ShapeDtypeStruct((t // seq, dil, seq // dil, n), out_dtype)
        out_spec = pl.BlockSpec((None, dil, tm // dil, tn),
                                lambda i, j, tt_: (i // (seq // tm), 0, i % (seq // tm), j))
        scratch.append(pltpu.VMEM((tn // LANES, tm, LANES), F32))
    nsub = tn // BLK if (tables is not None or headnorm is not None) else 1
    ns = seq // tm
    if tt is None:
        tt = np.zeros((nj,), np.int32)
    in_specs = [
        pl.BlockSpec((tm, kdim), lambda i, j, tt_: (i, src_cb)),
        pl.BlockSpec((1, kdim), lambda i, j, tt_: (0, 0)),
        pl.BlockSpec((kdim, tn), lambda i, j, tt_: (0, j)),
    ]
    args = [src, gain.reshape(1, kdim).astype(F32), w]
    if tables is not None:
        for tb in tables:
            in_specs.append(pl.BlockSpec((1, tm, LANES), lambda i, j, tt_: (tt_[j], i % ns, 0)))
            args.append(tb)
    if headnorm is not None:
        in_specs.append(pl.BlockSpec((1, tn), lambda i, j, tt_: (0, j)))
        in_specs.append(pl.BlockSpec((BLK, BLK), lambda i, j, tt_: (0, 0)))
        args += [headnorm[0], headnorm[1]]
    if add is not None:
        add_arr, add_cb = add
        in_specs.append(pl.BlockSpec((tm, BLK), lambda i, j, tt_: (i, add_cb)))
        args.append(add_arr)
    kern = functools.partial(_proj_kernel, dnorm=dnorm, rope=tables is not None,
                             headnorm=headnorm is not None, add=add is not None, nsub=nsub, vt=vt,
                             dil=dil)
    return pl.pallas_call(
        kern,
        out_shape=out_shape,
        grid_spec=pltpu.PrefetchScalarGridSpec(
            num_scalar_prefetch=1,
            grid=(t // tm, nj),
            in_specs=in_specs,
            out_specs=out_spec,
            scratch_shapes=scratch,
        ),
        compiler_params=_cparams(("parallel", "arbitrary")),
    )(jnp.asarray(tt, jnp.int32), *args)


def _flash_kernel(*refs, nh, vgroups, tq, tk, nk, diff):
    if diff:
        (qmask_ref, q_ref, k_ref, vt_ref, lq1, lk1, lq2, lk2, sg_ref,
         o_ref, qm_scr, s_scr, mx_scr, m_scr, acc_scr) = refs
        lambda_init = diff
    else:
        qmask_ref, q_ref, k_ref, vt_ref, o_ref, qm_scr, s_scr, mx_scr, m_scr, acc_scr = refs

    qt = q_ref[...].astype(F32).T
    for j in range(nh):
        rowmask = jnp.concatenate([qmask_ref[j]] * (tq // LANES), axis=1)
        qm_scr[:, j * tq:(j + 1) * tq] = (qt * rowmask).astype(BF16)
    m_scr[...] = jnp.full(m_scr.shape, NEG_INF, F32)
    acc_scr[...] = jnp.zeros(acc_scr.shape, F32)

    ucols = min(2 * BLK, vgroups[0][3] * tq)
    units = []
    for gi, (r0, nr, h0, hn) in enumerate(vgroups):
        for off in range(0, hn * tq, ucols):
            units.append((gi, r0, nr, h0 * tq + off, off))

    def scores(c, slot, col0):
        cols = slice(col0, col0 + ucols)
        ks = pl.multiple_of(c * tk, tk)
        st = jnp.dot(k_ref[pl.ds(ks, tk), :], qm_scr[:, cols], preferred_element_type=F32)
        s_scr[slot, :, cols] = st
        mx_scr[slot, :, cols] = jnp.max(st, axis=0, keepdims=True)

    def step(c, slot, c_next, slot_next):
        vtc = vt_ref[c]
        m_prev = m_scr[...]
        m_new = jnp.maximum(m_prev, mx_scr[slot])
        alpha = jnp.exp2(m_prev - m_new)
        m_scr[...] = m_new
        for gi, r0, nr, col0, off in units:
            cols = slice(col0, col0 + ucols)
            scores(c_next, slot_next, col0)
            pt = jnp.exp2(s_scr[slot, :, cols] - m_new[:, cols]).astype(BF16)
            acols = slice(off, off + ucols)
            acc_scr[gi, :, acols] = acc_scr[gi, :, acols] * alpha[:, cols] + jnp.dot(
                vtc[r0:r0 + nr + _ones_rows(nr), :], pt, preferred_element_type=F32)

    for unit in units:
        scores(0, 0, unit[3])

    per_trip = nk // KEY_LOOP_TRIPS

    def body(i, carry):
        c = per_trip * i
        for u in range(per_trip):
            step(c + u, u % SCORE_SLOTS, jnp.minimum(c + u + 1, nk - 1), (u + 1) % SCORE_SLOTS)
        return carry

    lax.fori_loop(0, KEY_LOOP_TRIPS, body, 0)

    pieces = []
    if diff:
        lam = (jnp.exp(jnp.sum(lq1[...] * lk1[...], axis=-1, keepdims=True))
               - jnp.exp(jnp.sum(lq2[...] * lk2[...], axis=-1, keepdims=True)) + lambda_init)
        for gi, (r0, nr, h0, hn) in enumerate(vgroups):
            acc = acc_scr[gi]
            linv = 1.0 / acc[nr:nr + 1, :]
            oh = acc[:nr, :tq] * linv[:, :tq] - lam * (acc[:nr, tq:] * linv[:, tq:])
            ms = jnp.mean(oh * oh, axis=0, keepdims=True)
            pieces.append(oh * lax.rsqrt(ms + NORM_EPS))
    else:
        for gi, (r0, nr, h0, hn) in enumerate(vgroups):
            acc = acc_scr[gi]
            on = acc[:nr, :] * (1.0 / acc[nr:nr + 1, :])
            for jj in range(hn):
                pieces.append(on[:, jj * tq:(jj + 1) * tq])
    ot = pieces[0] if len(pieces) == 1 else jnp.concatenate(pieces, axis=0)
    o = ot.T
    if diff:
        o = o * sg_ref[...] * (1.0 - lambda_init)
    o_ref[...] = o.astype(o_ref.dtype)


def _flash(q, k, vt, *, qcol0, kcol0, ngroups, wq, wv, wo, nh, vgroups, qmask, tq, tk,
           diff=None, diff_params=None):
    b, s = q.shape[0], q.shape[1]
    nk = s // tk
    assert s % tk == 0 and nk % (KEY_LOOP_TRIPS * SCORE_SLOTS) == 0 and s % tq == 0
    qmask = jnp.broadcast_to(qmask.astype(F32)[:, :, None], (nh, wq, LANES))
    in_specs = [
        pl.BlockSpec((nh, wq, LANES), lambda bi, g, i: (0, 0, 0)),
        pl.BlockSpec((None, tq, wq), lambda bi, g, i: (bi, i, qcol0 + g)),
        pl.BlockSpec((None, s, wq), lambda bi, g, i: (bi, 0, kcol0 + g)),
        pl.BlockSpec((None, None, nk, wv, tk), lambda bi, g, i: (bi, g, 0, 0, 0)),
    ]
    args = [qmask, q, k, vt]
    if diff is not None:
        for prm in diff_params[:4]:
            in_specs.append(pl.BlockSpec((1, 64), lambda bi, g, i: (0, 0)))
            args.append(prm.reshape(1, 64).astype(F32))
        in_specs.append(pl.BlockSpec((1, wo), lambda bi, g, i: (0, 0)))
        args.append(jnp.tile(diff_params[4].astype(F32), wo // LANES).reshape(1, wo))
    kern = functools.partial(_flash_kernel, nh=nh, vgroups=tuple(vgroups), tq=tq, tk=tk, nk=nk,
                             diff=diff)
    nr, hn = vgroups[0][1], vgroups[0][3]
    return pl.pallas_call(
        kern,
        out_shape=jax.ShapeDtypeStruct((b, s, ngroups * wo), BF16),
        grid=(b, ngroups, s // tq),
        in_specs=in_specs,
        out_specs=pl.BlockSpec((None, tq, wo), lambda bi, g, i: (bi, i, g)),
        scratch_shapes=[
            pltpu.VMEM((wq, nh * tq), BF16),
            pltpu.VMEM((SCORE_SLOTS, tk, nh * tq), F32),
            pltpu.VMEM((SCORE_SLOTS, 1, nh * tq), F32),
            pltpu.VMEM((1, nh * tq), F32),
            pltpu.VMEM((len(vgroups), nr + _ones_rows(nr), hn * tq), F32),
        ],
        compiler_params=_cparams(("parallel", "parallel", "arbitrary")),
    )(*args)


def _band_kernel(qmask_ref, vmask_ref, q_ref, k_ref, v_ref, o_ref, lse_ref, *, tq, win, length, ntile):
    nh = 4
    vmask = vmask_ref[...]
    vmask_b = vmask.astype(BF16)

    def tile(t, carry):
        r0 = pl.multiple_of(t * tq, tq)
        i = pl.program_id(2) * ntile + t
        ks = jnp.clip(i * tq - A_HALF_WINDOW, 0, length - win)
        ks = pl.multiple_of(ks, A_HALF_WINDOW)
        kc = k_ref[pl.ds(ks, win), :]
        vc = v_ref[pl.ds(ks, win), :]
        q = q_ref[pl.ds(r0, tq), :]
        qm = jnp.concatenate([q * qmask_ref[j:j + 1, :] for j in range(nh)], axis=0)
        s = lax.dot_general(qm, kc, (((1,), (1,)), ((), ())), preferred_element_type=F32)
        qpos = i * tq + lax.broadcasted_iota(jnp.int32, (tq, win), 0)
        kpos = ks + lax.broadcasted_iota(jnp.int32, (tq, win), 1)
        valid = jnp.abs(qpos - kpos) <= A_HALF_WINDOW
        ps = []
        inv = None
        lse = None
        for j in range(nh):
            sj = jnp.where(valid, s[j * tq:(j + 1) * tq], NEG_INF)
            mj = jnp.max(sj, axis=-1, keepdims=True)
            pj = jnp.exp2(sj - mj)
            lj = jnp.sum(pj, axis=-1, keepdims=True)
            ps.append(pj.astype(BF16))
            t_inv = (1.0 / lj) * vmask[j:j + 1, :]
            t_lse = ((mj + jnp.log2(lj)) * (1.0 / LOG2E)) * vmask[j:j + 1, :]
            inv = t_inv if inv is None else inv + t_inv
            lse = t_lse if lse is None else lse + t_lse
        lhs = jnp.concatenate(ps, axis=1)
        rhs = jnp.concatenate([vc * vmask_b[j:j + 1, :] for j in range(nh)], axis=0)
        pv = jnp.dot(lhs, rhs, preferred_element_type=F32)
        o_ref[pl.ds(r0, tq), :] = (pv * inv).astype(o_ref.dtype)
        lse_ref[pl.ds(r0, tq), :] = lse
        return carry

    lax.fori_loop(0, ntile, tile, 0)


def _band_attention(qkv, qmask, vmask):
    b, dil, length, _ = qkv.shape
    tq = min(256, length)
    win = min(tq + 2 * A_HALF_WINDOW, length)
    rows = min(4 * tq, length)

    def col(which):
        return lambda bi, a, i: (bi, a // 2, 0, which * 2 + a % 2)

    in_specs = [
        pl.BlockSpec((4, BLK), lambda bi, a, i: (0, 0)),
        pl.BlockSpec((4, BLK), lambda bi, a, i: (0, 0)),
        pl.BlockSpec((None, None, rows, BLK), lambda bi, a, i: (bi, a // 2, i, a % 2)),
        pl.BlockSpec((None, None, length, BLK), col(1)),
        pl.BlockSpec((None, None, length, BLK), col(2)),
    ]
    kern = functools.partial(_band_kernel, tq=tq, win=win, length=length, ntile=rows // tq)
    out_block = pl.BlockSpec((None, None, rows, BLK), lambda bi, a, i: (bi, a // 2, i, a % 2))
    return pl.pallas_call(
        kern,
        out_shape=(jax.ShapeDtypeStruct((b, dil, length, 2 * BLK), BF16),
                   jax.ShapeDtypeStruct((b, dil, length, 2 * BLK), F32)),
        grid=(b, dil * 2, length // rows),
        in_specs=in_specs,
        out_specs=(out_block, out_block),
        compiler_params=_cparams(("parallel", "parallel", "arbitrary")),
    )(qmask, vmask, qkv, qkv, qkv)


def _a_out_kernel(*refs, dils, tm):
    ng = len(dils)
    o_refs, l_refs = refs[:ng], refs[ng:2 * ng]
    w_ref, x_ref, out_ref = refs[2 * ng:2 * ng + 3]
    scr = list(refs[2 * ng + 3:])

    def token_order(ref, dil):
        if dil == 1:
            return ref[0].astype(F32)
        buf = scr.pop(0)
        ncb = buf.shape[0]
        for r in range(dil):
            v = ref[r].astype(F32)
            for cb in range(ncb):
                buf[cb, pl.ds(r, tm // dil, stride=dil), :] = v[:, cb * LANES:(cb + 1) * LANES]
        return jnp.concatenate([buf[cb] for cb in range(ncb)], axis=1)

    ls = [token_order(l_refs[g], dils[g]) for g in range(ng)]
    os_ = [token_order(o_refs[g], dils[g]) for g in range(ng)]
    mx = functools.reduce(jnp.maximum, ls)
    es = [jnp.exp(l - mx) for l in ls]
    inv = 1.0 / functools.reduce(jnp.add, es)
    o = functools.reduce(jnp.add, [e * og for e, og in zip(es, os_)]) * inv
    out_ref[...] = x_ref[...] + jnp.dot(o.astype(BF16), w_ref[...], preferred_element_type=F32)


def _a_out(os_, lses, w, x3, tm):
    b, s, _ = x3.shape
    kd = w.shape[0]
    dils = tuple(o.shape[1] for o in os_)
    grp = [pl.BlockSpec((None, d, tm // d, kd), lambda bi, i: (bi, 0, i, 0)) for d in dils]
    row = pl.BlockSpec((None, tm, D_MODEL), lambda bi, i: (bi, i, 0))
    nscr = 2 * sum(1 for d in dils if d > 1)
    return pl.pallas_call(
        functools.partial(_a_out_kernel, dils=dils, tm=tm),
        out_shape=jax.ShapeDtypeStruct((b, s, D_MODEL), F32),
        grid=(b, s // tm),
        in_specs=grp + grp + [pl.BlockSpec((kd, D_MODEL), lambda bi, i: (0, 0)), row],
        out_specs=row,
        scratch_shapes=[pltpu.VMEM((kd // LANES, tm, LANES), F32)] * nscr,
        compiler_params=_cparams(("parallel", "parallel")),
    )(*os_, *lses, w, x3)


def _post_kernel(*refs, has_proj, final):
    if has_proj:
        (x_ref, o_ref, wout_ref, gx_ref, wq_ref, kv_ref, wo_ref, gm_ref, w1_ref, w2_ref, fg_ref,
         out_ref, x2_scr, h_scr, acc_scr) = refs
    else:
        (x_ref, gx_ref, wq_ref, kv_ref, wo_ref, gm_ref, w1_ref, w2_ref, fg_ref,
         out_ref, x2_scr, h_scr, acc_scr) = refs
    f = pl.program_id(2)

    @pl.when(f == 0)
    def _():
        x1 = x_ref[...]
        if has_proj:
            x1 = x1 + jnp.dot(o_ref[...], wout_ref[...], preferred_element_type=F32)
        ms = jnp.mean(x1 * x1, axis=-1, keepdims=True)
        h = (x1 * lax.rsqrt(ms + NORM_EPS) * gx_ref[...]).astype(BF16)
        qb = (jnp.dot(h, wq_ref[...], preferred_element_type=F32)
              * (X_HEAD_DIM ** -0.5 * LOG2E)).astype(BF16)
        hd = X_HEADS * X_HEAD_DIM
        outs = []
        for hh in range(X_HEADS):
            qh = qb[:, hh * X_HEAD_DIM:(hh + 1) * X_HEAD_DIM]
            kh = kv_ref[:, hh * X_HEAD_DIM:(hh + 1) * X_HEAD_DIM]
            vh = kv_ref[:, hd + hh * X_HEAD_DIM:hd + (hh + 1) * X_HEAD_DIM]
            s = lax.dot_general(qh, kh, (((1,), (1,)), ((), ())), preferred_element_type=F32)
            m = jnp.max(s, axis=-1, keepdims=True)
            p = jnp.exp2(s - m)
            l = jnp.sum(p, axis=-1, keepdims=True)
            oh = jnp.dot(p.astype(BF16), vh, preferred_element_type=F32) * (1.0 / l)
            outs.append(oh.astype(BF16))
        x2 = x1 + jnp.dot(jnp.concatenate(outs, axis=1), wo_ref[...], preferred_element_type=F32)
        x2_scr[...] = x2
        ms2 = jnp.mean(x2 * x2, axis=-1, keepdims=True)
        h_scr[...] = (x2 * lax.rsqrt(ms2 + NORM_EPS) * gm_ref[...]).astype(BF16)
        acc_scr[...] = jnp.zeros(acc_scr.shape, F32)

    a = jnp.maximum(jnp.dot(h_scr[...], w1_ref[...], preferred_element_type=F32), 0.0)
    acc_scr[...] += jnp.dot((a * a).astype(BF16), w2_ref[...], preferred_element_type=F32)

    @pl.when(f == pl.num_programs(2) - 1)
    def _():
        y = x2_scr[...] + acc_scr[...]
        if final:
            ms = jnp.mean(y * y, axis=-1, keepdims=True)
            y = y * lax.rsqrt(ms + NORM_EPS) * fg_ref[...]
        out_ref[...] = y


def _post_mixer(x3, o3, w_out, gx, wq, kv3, wo, gm, w1, w2, final_gain, final, tm, tf):
    b, s, _ = x3.shape
    hd = X_HEADS * X_HEAD_DIM
    has_proj = o3 is not None
    const = lambda bi, i, f: (0, 0)
    row = lambda bi, i, f: (bi, i, 0)
    in_specs = [pl.BlockSpec((None, tm, D_MODEL), row)]
    args = [x3]
    if has_proj:
        kd = o3.shape[-1]
        in_specs += [pl.BlockSpec((None, tm, kd), row), pl.BlockSpec((kd, D_MODEL), const)]
        args += [o3, w_out]
    in_specs += [pl.BlockSpec((1, D_MODEL), const),
                 pl.BlockSpec((D_MODEL, hd), const),
                 pl.BlockSpec((None, N_MEM, 2 * hd), lambda bi, i, f: (bi, 0, 0)),
                 pl.BlockSpec((hd, D_MODEL), const),
                 pl.BlockSpec((1, D_MODEL), const),
                 pl.BlockSpec((D_MODEL, tf), lambda bi, i, f: (0, f)),
                 pl.BlockSpec((tf, D_MODEL), lambda bi, i, f: (f, 0)),
                 pl.BlockSpec((1, D_MODEL), const)]
    args += [gx.reshape(1, D_MODEL), wq, kv3, wo, gm.reshape(1, D_MODEL), w1, w2,
             final_gain.reshape(1, D_MODEL)]
    return pl.pallas_call(
        functools.partial(_post_kernel, has_proj=has_proj, final=final),
        out_shape=jax.ShapeDtypeStruct((b, s, D_MODEL), F32),
        grid=(b, s // tm, D_FF // tf),
        in_specs=in_specs,
        out_specs=pl.BlockSpec((None, tm, D_MODEL), row),
        scratch_shapes=[pltpu.VMEM((tm, D_MODEL), F32), pltpu.VMEM((tm, D_MODEL), BF16),
                        pltpu.VMEM((tm, D_MODEL), F32)],
        compiler_params=_cparams(("parallel", "parallel", "arbitrary")),
    )(*args)


_F_ROT16 = np.array(list(range(0, 8)) + list(range(16, 40)))
_P_ROT16 = np.array(list(range(8, 16)) + list(range(40, 64)))
_F_AXIAL = np.array(list(range(0, 16)) + list(range(32, 48)))
_P_AXIAL = np.array(list(range(16, 32)) + list(range(48, 64)))


def _block_dims(first, partner):
    lane = np.arange(BLK)
    half, slot, u = lane // LANES, (lane % LANES) // 32, lane % 32
    return np.where(half == 0, first[u], partner[u]), slot


def _slot_masks(nslot, slot_of_lane):
    return np.stack([(slot_of_lane == j) for j in range(nslot)]).astype(np.float32)


def _rope_tables(pos_list, theta, rot, scale_list, npad):
    half = rot // 2
    inv_freq = jnp.exp(jnp.arange(half, dtype=F32) * (-2.0 * math.log(theta) / rot))
    cs, ss = [], []
    for pos in pos_list:
        ang = pos.astype(F32)[:, None] * inv_freq[None, :]
        cs.append(jnp.cos(ang))
        ss.append(jnp.sin(ang))
    c = jnp.concatenate(cs, axis=1)
    s = jnp.concatenate(ss, axis=1)
    n = c.shape[0]
    if npad:
        c = jnp.concatenate([c, jnp.ones((n, npad), F32)], axis=1)
        s = jnp.concatenate([s, jnp.zeros((n, npad), F32)], axis=1)
    reps = LANES // c.shape[1]
    c = jnp.tile(c, (1, reps))
    s = jnp.tile(s, (1, reps))
    ctab = [c * sc for sc in scale_list] + [jnp.ones_like(c)]
    stab = [s * sc for sc in scale_list] + [jnp.zeros_like(s)]
    return jnp.stack(ctab), jnp.stack(stab)


def _mixer_a(x, b, s, gain, w_in, w_out, pos, tm):
    dims, slot = _block_dims(_F_ROT16, _P_ROT16)
    ctab, stab = _rope_tables([pos], ROPE_THETA, 16, [0.125 * LOG2E, 1.0], 24)
    qmask = jnp.asarray(_slot_masks(4, slot), BF16)
    vmask = jnp.asarray(_slot_masks(4, np.arange(BLK) // 64), F32)
    w_bf = w_in.astype(BF16)
    outs, lses = [], []
    for wg, (window, dil) in enumerate(A_PATTERNS):
        assert window // (2 * dil) == A_HALF_WINDOW
        cols = []
        for which in range(3):
            base = (wg * 3 + which) * A_HEADS * 64
            for hg in range(2):
                cols.append(base + hg * BLK + (slot * 64 + dims if which < 2 else np.arange(BLK)))
        w = w_bf[:, np.concatenate(cols)]
        qkv = _proj(x, 0, D_MODEL, D_MODEL, gain, w, tn=2 * BLK, out_dtype=BF16, seq=s, tm=tm,
                    tables=(ctab, stab), tt=np.array([0, 1, 2], np.int32), dil=dil)
        o, lse = _band_attention(qkv.reshape(b, dil, s // dil, 3 * 2 * BLK), qmask, vmask)
        outs.append(o)
        lses.append(lse)
    x3 = _a_out(outs, lses, w_out.astype(BF16), x.reshape(b, s, D_MODEL), min(tm, 512))
    return x3.reshape(b * s, D_MODEL), None, None


def _mixer_b(x, b, s, gain, w_in, q_gain, k_gain, w_out, rows, cols_pos, tm, tq, tk):
    dims, slot = _block_dims(_F_AXIAL, _P_AXIAL)
    qcols = np.concatenate([g * BLK + slot * 64 + dims for g in range(B_KV_HEADS)])
    kcols = np.concatenate([B_HEADS * 64 + g * 64 + dims for g in range(B_KV_HEADS)])
    w_bf = w_in.astype(BF16)
    wqk = w_bf[:, np.concatenate([qcols, kcols])]
    wv = w_bf[:, (B_HEADS + B_KV_HEADS) * 64:]
    hgain = jnp.concatenate([jnp.tile(q_gain[dims] * (0.125 * LOG2E), B_KV_HEADS),
                             jnp.tile(k_gain[dims], B_KV_HEADS)]).reshape(1, -1).astype(F32)
    bd = jnp.asarray(slot[:, None] == slot[None, :], BF16)
    ctab, stab = _rope_tables([rows, cols_pos], AXIAL_THETA, 32, [1.0], 0)
    n = B_KV_HEADS * BLK
    qk = _proj(x, 0, D_MODEL, D_MODEL, gain, wqk, tn=2 * BLK, out_dtype=BF16, seq=s, tm=tm,
               tables=(ctab, stab), tt=np.zeros((n // BLK,), np.int32), headnorm=(hgain, bd))
    vt = _proj(x, 0, D_MODEL, D_MODEL, gain, wv, tn=B_KV_HEADS * 64, out_dtype=BF16, seq=s, tm=tm,
               vt=(B_KV_HEADS, 1, 64, tk))
    qk = qk.reshape(b, s, 2 * n)
    qmask = jnp.asarray(_slot_masks(4, slot), BF16)
    o = _flash(qk, qk, vt, qcol0=0, kcol0=B_KV_HEADS, ngroups=B_KV_HEADS, wq=BLK,
               wv=64 + _ones_rows(64), wo=BLK, nh=4, vgroups=((0, 64, 0, 4),), qmask=qmask,
               tq=tq, tk=tk)
    return x, o, w_out.astype(BF16)


def _mixer_c(x, b, s, gain, w_in, lq1, lk1, lq2, lk2, sub_gain, w_out, pos, lambda_init, tm, tq, tk):
    dims, slot = _block_dims(_F_ROT16, _P_ROT16)
    ngr = C_HEADS // 2
    cols = []
    for which in range(2):
        for g in range(ngr):
            cols.append(which * C_HEADS * 128 + g * BLK + slot * 64 + dims)
    w_bf = w_in.astype(BF16)
    ctab, stab = _rope_tables([pos], ROPE_THETA, 16, [0.125 * LOG2E, 1.0], 24)
    n = ngr * BLK
    qk = _proj(x, 0, D_MODEL, D_MODEL, gain, w_bf[:, np.concatenate(cols)], tn=2 * BLK, out_dtype=BF16,
               seq=s, tm=tm, tables=(ctab, stab), tt=np.array([0, 0, 1, 1], np.int32))
    qkv = qk.reshape(b, s, 2 * n)
    vt = _proj(x, 0, D_MODEL, D_MODEL, gain, w_bf[:, 2 * C_HEADS * 128:], tn=n, out_dtype=BF16,
               seq=s, tm=tm, vt=(ngr, 2, LANES, tk))
    qmask = jnp.asarray(_slot_masks(4, slot), BF16)
    hrows = LANES + _ones_rows(LANES)
    o = _flash(qkv, qkv, vt, qcol0=0, kcol0=ngr, ngroups=ngr, wq=BLK, wv=2 * hrows,
               wo=BLK, nh=4, vgroups=((0, LANES, 0, 2), (hrows, LANES, 2, 2)),
               qmask=qmask, tq=tq, tk=tk,
               diff=lambda_init, diff_params=(lq1, lk1, lq2, lk2, sub_gain))
    return x, o, w_out.astype(BF16)


def _mixer_d(x, b, s, gain, w_in, q_gain, kv_gain, w_uq, w_ukv, w_out, pos, tm, tq, tk):
    lane = np.arange(LANES)
    slot_h = np.where(lane < 32, 0, np.where(lane < 64, 1, np.where(lane < 80, 0, np.where(lane < 96, 1, -1))))
    slot = np.concatenate([slot_h, slot_h])
    nope_lane = lane < 64
    rope_lane = (lane >= 64) & (lane < 96)
    ngr = D_HEADS // 2

    w1 = jnp.zeros((D_MODEL, 4 * BLK), F32)
    w1 = w1.at[:, :D_Q_RANK].set(w_in[:, :D_Q_RANK])
    w1 = w1.at[:, 2 * BLK:3 * BLK].set(w_in[:, D_Q_RANK:D_Q_RANK + D_KV_RANK])
    kr_src = np.zeros((BLK,), np.int64)
    kr_on = np.zeros((BLK,), bool)
    for hf in range(2):
        for l in range(LANES):
            if rope_lane[l]:
                kr_src[hf * LANES + l] = D_Q_RANK + D_KV_RANK + hf * 16 + (l - 64) % 16
                kr_on[hf * LANES + l] = True
    w1 = w1.at[:, 3 * BLK:].set(jnp.where(jnp.asarray(kr_on)[None, :], w_in[:, kr_src], 0.0))
    cmb = _proj(x, 0, D_MODEL, D_MODEL, gain, w1.astype(BF16), tn=4 * BLK, out_dtype=F32, seq=s, tm=tm)

    qsrc = np.zeros((ngr * BLK,), np.int64)
    qon = np.zeros((ngr * BLK,), bool)
    ksrc = np.zeros((ngr * BLK,), np.int64)
    kon = np.zeros((ngr * BLK,), bool)
    for g in range(ngr):
        for hf in range(2):
            for l in range(LANES):
                idx = g * BLK + hf * LANES + l
                if slot_h[l] < 0:
                    continue
                head = 2 * g + slot_h[l]
                if nope_lane[l]:
                    d = hf * 32 + l % 32
                    qsrc[idx], qon[idx] = head * 96 + d, True
                    ksrc[idx], kon[idx] = head * 128 + d, True
                else:
                    d = hf * 16 + (l - 64) % 16
                    qsrc[idx], qon[idx] = head * 96 + D_NOPE + d, True
    wq2 = jnp.where(jnp.asarray(qon)[None, :], w_uq[:, qsrc], 0.0)
    wq2 = jnp.concatenate([wq2, jnp.zeros((2 * BLK - D_Q_RANK, ngr * BLK), F32)], axis=0).astype(BF16)
    wk2 = jnp.where(jnp.asarray(kon)[None, :], w_ukv[:, ksrc], 0.0).astype(BF16)
    vsrc = np.concatenate([h * 128 + D_NOPE + np.arange(64) for h in range(D_HEADS)])
    wv2 = w_ukv[:, vsrc].astype(BF16)
    qg = jnp.concatenate([q_gain, jnp.zeros((2 * BLK - D_Q_RANK,), F32)])

    half = D_ROPE // 2
    inv_freq = jnp.exp(jnp.arange(half, dtype=F32) * (-2.0 * math.log(ROPE_THETA) / D_ROPE))
    ang = pos.astype(F32)[:, None] * inv_freq[None, :]
    ones64 = jnp.ones((s, 64), F32)
    pad32 = jnp.ones((s, 32), F32)
    c = jnp.concatenate([ones64, jnp.cos(ang), jnp.cos(ang), pad32], axis=1)
    sn = jnp.concatenate([0.0 * ones64, jnp.sin(ang), jnp.sin(ang), 0.0 * pad32], axis=1)
    qs = (D_NOPE + D_ROPE) ** -0.5 * LOG2E
    ctab = jnp.stack([c * qs, c])
    stab = jnp.stack([sn * qs, sn])

    q = _proj(cmb, 0, 2 * BLK, D_Q_RANK, qg, wq2, tn=2 * BLK, out_dtype=BF16, seq=s, tm=tm,
              tables=(ctab, stab), tt=np.zeros((ngr // 2,), np.int32))
    k = _proj(cmb, 2, BLK, D_KV_RANK, kv_gain, wk2, tn=4 * BLK, out_dtype=BF16, seq=s, tm=tm,
              tables=(ctab, stab), tt=np.ones((ngr // 4,), np.int32), add=(cmb, 3))
    vt = _proj(cmb, 2, BLK, D_KV_RANK, kv_gain, wv2, tn=4 * BLK, out_dtype=BF16, seq=s, tm=tm,
               vt=(ngr, 2, 64, tk))
    q = q.reshape(b, s, ngr * BLK)
    k = k.reshape(b, s, ngr * BLK)
    qmask = jnp.asarray(_slot_masks(2, slot), BF16)
    hrows = 64 + _ones_rows(64)
    o = _flash(q, k, vt, qcol0=0, kcol0=0, ngroups=ngr, wq=BLK, wv=2 * hrows, wo=LANES,
               nh=2, vgroups=((0, 64, 0, 1), (hrows, 64, 1, 1)), qmask=qmask, tq=tq, tk=tk)
    return x, o, w_out.astype(BF16)


def _run_trunk(x3, mem3, p):
    b, s, _ = x3.shape
    t = b * s
    tm = 1024
    tm_mlp = 512
    tq, tq_mla, tk = 512, 1024, 512
    x = x3.reshape(t, D_MODEL)
    pos = jnp.arange(s, dtype=F32)
    rows = jnp.repeat(jnp.arange(s // GRID_W, dtype=F32), GRID_W)
    cols_pos = jnp.tile(jnp.arange(GRID_W, dtype=F32), s // GRID_W)
    memf = mem3.reshape(b * N_MEM, D_MODEL)
    for i in range(DEPTH):
        m, j = i % 4, i // 4
        g = p['norm_mix'][i]
        if m == 0:
            x, o, w_out = _mixer_a(x, b, s, g, p['a_w_in'][j], p['a_w_out'][j], pos, tm)
        elif m == 1:
            x, o, w_out = _mixer_b(x, b, s, g, p['b_w_in'][j], p['b_q_norm'][j], p['b_k_norm'][j],
                                   p['b_w_out'][j], rows, cols_pos, tm, tq, tk)
        elif m == 2:
            x, o, w_out = _mixer_c(x, b, s, g, p['c_w_in'][j], p['c_lambda_q1'][j],
                                   p['c_lambda_k1'][j], p['c_lambda_q2'][j], p['c_lambda_k2'][j],
                                   p['c_sub_norm'][j], p['c_w_out'][j], pos,
                                   0.8 - 0.6 * math.exp(-0.3 * i), tm, tq, tk)
        else:
            x, o, w_out = _mixer_d(x, b, s, g, p['d_w_in'][j], p['d_q_norm'][j], p['d_kv_norm'][j],
                                   p['d_w_uq'][j], p['d_w_ukv'][j], p['d_w_out'][j], pos, tm,
                                   tq_mla, tk)
        kv = _proj(memf, 0, D_MODEL, D_MODEL, p['norm_mem'][i], p['w_xkv'][i].astype(BF16),
                   tn=2 * X_HEADS * X_HEAD_DIM, out_dtype=BF16, seq=N_MEM, tm=N_MEM)
        x = _post_mixer(x.reshape(b, s, D_MODEL), o, w_out, p['norm_x'][i],
                        p['w_xq'][i].astype(BF16), kv.reshape(b, N_MEM, 2 * X_HEADS * X_HEAD_DIM),
                        p['w_xo'][i].astype(BF16), p['norm_mlp'][i], p['w_mlp_in'][i].astype(BF16),
                        p['w_mlp_out'][i].astype(BF16), p['final_norm'], i == DEPTH - 1,
                        tm_mlp, 2048).reshape(t, D_MODEL)
    return x.reshape(b, s, D_MODEL)


def kernel(x_prompt, x_sample, mem_prompt, mem_sample, norm_mix, norm_x, norm_mem, w_xq, w_xkv, w_xo, norm_mlp, w_mlp_in, w_mlp_out, a_w_in, a_w_out, b_w_in, b_q_norm, b_k_norm, b_w_out, c_w_in, c_lambda_q1, c_lambda_k1, c_lambda_q2, c_lambda_k2, c_sub_norm, c_w_out, d_w_in, d_q_norm, d_kv_norm, d_w_uq, d_w_ukv, d_w_out, final_norm):
    p = dict(norm_mix=norm_mix, norm_x=norm_x, norm_mem=norm_mem, w_xq=w_xq, w_xkv=w_xkv,
             w_xo=w_xo, norm_mlp=norm_mlp, w_mlp_in=w_mlp_in, w_mlp_out=w_mlp_out,
             a_w_in=a_w_in, a_w_out=a_w_out, b_w_in=b_w_in, b_q_norm=b_q_norm,
             b_k_norm=b_k_norm, b_w_out=b_w_out, c_w_in=c_w_in, c_lambda_q1=c_lambda_q1,
             c_lambda_k1=c_lambda_k1, c_lambda_q2=c_lambda_q2, c_lambda_k2=c_lambda_k2,
             c_sub_norm=c_sub_norm, c_w_out=c_w_out, d_w_in=d_w_in, d_q_norm=d_q_norm,
             d_kv_norm=d_kv_norm, d_w_uq=d_w_uq, d_w_ukv=d_w_ukv, d_w_out=d_w_out,
             final_norm=final_norm)
    return (_run_trunk(x_prompt, mem_prompt, p), _run_trunk(x_sample, mem_sample, p))
```

```python
import functools
import math

import numpy as np
import jax
import jax.numpy as jnp
from jax import lax
from jax.experimental import pallas as pl
from jax.experimental.pallas import tpu as pltpu

F32 = jnp.float32
BF16 = jnp.bfloat16

D_MODEL = 1024
DEPTH = 4
N_MEM = 256
GRID_W = 64
D_FF = 4 * D_MODEL
NORM_EPS = 1e-6
ROPE_THETA = 500000.0
AXIAL_THETA = 10000.0
NEG_INF = -1e30

A_PATTERNS = ((128, 1), (512, 4), (2048, 16))
A_GROUPS = 3
A_HEADS = 8
A_IN = A_GROUPS * 3 * A_HEADS * 64
A_HALF_WINDOW = 64
B_HEADS = 16
B_KV_HEADS = 4
C_HEADS = 8
D_HEADS = 16
D_Q_RANK = 384
D_KV_RANK = 256
D_NOPE = 64
D_ROPE = 32
X_HEADS = 4
X_HEAD_DIM = 128

LOG2E = 1.4426950408889634
BF16_ROWS = 16
SCORE_SLOTS = 2
KEY_LOOP_TRIPS = 2
LANES = 128
BLK = 2 * LANES
VMEM_LIMIT = 56 * 1024 * 1024


def _ones_rows(nr):
    return LANES - nr if nr < LANES else BF16_ROWS


def _cparams(sem):
    return pltpu.CompilerParams(dimension_semantics=sem, vmem_limit_bytes=VMEM_LIMIT)


def _proj_kernel(tt_ref, src_ref, g_ref, w_ref, *rest, dnorm, rope, headnorm, add, nsub, vt, dil):
    del tt_ref
    rest = list(rest)
    c_ref = s_ref = hg_ref = bd_ref = add_ref = y_scr = None
    if dil > 1:
        y_scr = rest.pop()
    if rope:
        c_ref, s_ref = rest[0], rest[1]
        rest = rest[2:]
    if headnorm:
        hg_ref, bd_ref = rest[0], rest[1]
        rest = rest[2:]
    if add:
        add_ref = rest[0]
        rest = rest[1:]
    o_ref, h_scr = rest

    @pl.when(pl.program_id(1) == 0)
    def _():
        xf = src_ref[...].astype(F32)
        ms = jnp.sum(xf * xf, axis=-1, keepdims=True) * (1.0 / dnorm)
        h_scr[...] = (xf * lax.rsqrt(ms + NORM_EPS) * g_ref[...]).astype(BF16)

    y = jnp.dot(h_scr[...], w_ref[...], preferred_element_type=F32)
    if add:
        ad = add_ref[...]
        y = y + (ad if nsub == 1 else jnp.concatenate([ad] * nsub, axis=1))
    if headnorm:
        y2 = y * y
        hi = y2.astype(BF16)
        lo = (y2 - hi.astype(F32)).astype(BF16)
        parts = []
        for n in range(nsub):
            sl = slice(n * BLK, (n + 1) * BLK)
            ss = (jnp.dot(hi[:, sl], bd_ref[...], preferred_element_type=F32)
                  + jnp.dot(lo[:, sl], bd_ref[...], preferred_element_type=F32))
            parts.append(y[:, sl] * lax.rsqrt(ss * (1.0 / 64.0) + NORM_EPS))
        y = (parts[0] if nsub == 1 else jnp.concatenate(parts, axis=1)) * hg_ref[...]
    if rope:
        c = c_ref[0]
        s = s_ref[0]
        for n in range(nsub):
            y1 = y[:, n * BLK:n * BLK + LANES]
            y2 = y[:, n * BLK + LANES:(n + 1) * BLK]
            r1 = y1 * c - y2 * s
            r2 = y2 * c + y1 * s
            if dil > 1:
                y_scr[2 * n] = r1
                y_scr[2 * n + 1] = r2
            else:
                o_ref[:, n * BLK:n * BLK + LANES] = r1.astype(o_ref.dtype)
                o_ref[:, n * BLK + LANES:(n + 1) * BLK] = r2.astype(o_ref.dtype)
        if dil > 1:
            rows = y.shape[0] // dil
            for r in range(dil):
                for cb in range(2 * nsub):
                    o_ref[r, :, cb * LANES:(cb + 1) * LANES] = (
                        y_scr[cb, pl.ds(r, rows, stride=dil), :].astype(o_ref.dtype))
    elif vt:
        ngroups, nvg, nr, tk = vt
        nones = _ones_rows(nr)
        ones = jnp.ones((nones, tk), o_ref.dtype)
        for cc in range(y.shape[0] // tk):
            yt = y[cc * tk:(cc + 1) * tk, :].T
            for g in range(ngroups):
                for vg in range(nvg):
                    src0 = (g * nvg + vg) * nr
                    r0 = vg * (nr + nones)
                    o_ref[g, cc, r0:r0 + nr, :] = yt[src0:src0 + nr, :].astype(o_ref.dtype)
                    o_ref[g, cc, r0 + nr:r0 + nr + nones, :] = ones
    else:
        o_ref[...] = y.astype(o_ref.dtype)


def _proj(src, src_cb, kdim, dnorm, gain, w, *, tn, out_dtype, seq, tm,
          tables=None, tt=None, headnorm=None, add=None, vt=None, dil=1):
    t = src.shape[0]
    n = w.shape[1]
    nj = n // tn
    out_shape = jax.ShapeDtypeStruct((t, n), out_dtype)
    out_spec = pl.BlockSpec((tm, tn), lambda i, j, tt_: (i, j))
    if vt is not None:
        ngroups, nvg, nr, tk = vt
        assert nj == 1 and n == ngroups * nvg * nr and tm % tk == 0
        wv = nvg * (nr + _ones_rows(nr))
        out_shape = jax.ShapeDtypeStruct((t // seq, ngroups, seq // tk, wv, tk), out_dtype)
        out_spec = pl.BlockSpec((None, ngroups, tm // tk, wv, tk),
                                lambda i, j, tt_: (i // (seq // tm), 0, i % (seq // tm), 0, 0))
    scratch = [pltpu.VMEM((tm, kdim), BF16)]
    if dil > 1:
        assert tables is not None and tm % (16 * dil) == 0
        out_shape = jax.ShapeDtypeStruct((t // seq, dil, seq // dil, n), out_dtype)
        out_spec = pl.BlockSpec((None, dil, tm // dil, tn),
                                lambda i, j, tt_: (i // (seq // tm), 0, i % (seq // tm), j))
        scratch.append(pltpu.VMEM((tn // LANES, tm, LANES), F32))
    nsub = tn // BLK if (tables is not None or headnorm is not None) else 1
    ns = seq // tm
    if tt is None:
        tt = np.zeros((nj,), np.int32)
    in_specs = [
        pl.BlockSpec((tm, kdim), lambda i, j, tt_: (i, src_cb)),
        pl.BlockSpec((1, kdim), lambda i, j, tt_: (0, 0)),
        pl.BlockSpec((kdim, tn), lambda i, j, tt_: (0, j)),
    ]
    args = [src, gain.reshape(1, kdim).astype(F32), w]
    if tables is not None:
        for tb in tables:
            in_specs.append(pl.BlockSpec((1, tm, LANES), lambda i, j, tt_: (tt_[j], i % ns, 0)))
            args.append(tb)
    if headnorm is not None:
        in_specs.append(pl.BlockSpec((1, tn), lambda i, j, tt_: (0, j)))
        in_specs.append(pl.BlockSpec((BLK, BLK), lambda i, j, tt_: (0, 0)))
        args += [headnorm[0], headnorm[1]]
    if add is not None:
        add_arr, add_cb = add
        in_specs.append(pl.BlockSpec((tm, BLK), lambda i, j, tt_: (i, add_cb)))
        args.append(add_arr)
    kern = functools.partial(_proj_kernel, dnorm=dnorm, rope=tables is not None,
                             headnorm=headnorm is not None, add=add is not None, nsub=nsub, vt=vt,
                             dil=dil)
    return pl.pallas_call(
        kern,
        out_shape=out_shape,
        grid_spec=pltpu.PrefetchScalarGridSpec(
            num_scalar_prefetch=1,
            grid=(t // tm, nj),
            in_specs=in_specs,
            out_specs=out_spec,
            scratch_shapes=scratch,
        ),
        compiler_params=_cparams(("parallel", "arbitrary")),
    )(jnp.asarray(tt, jnp.int32), *args)


def _flash_kernel(*refs, nh, vgroups, tq, tk, nk, diff):
    if diff:
        (qmask_ref, q_ref, k_ref, vt_ref, lq1, lk1, lq2, lk2, sg_ref,
         o_ref, qm_scr, s_scr, mx_scr, m_scr, acc_scr) = refs
        lambda_init = diff
    else:
        qmask_ref, q_ref, k_ref, vt_ref, o_ref, qm_scr, s_scr, mx_scr, m_scr, acc_scr = refs

    qt = q_ref[...].astype(F32).T
    for j in range(nh):
        rowmask = jnp.concatenate([qmask_ref[j]] * (tq // LANES), axis=1)
        qm_scr[:, j * tq:(j + 1) * tq] = (qt * rowmask).astype(BF16)
    m_scr[...] = jnp.full(m_scr.shape, NEG_INF, F32)
    acc_scr[...] = jnp.zeros(acc_scr.shape, F32)

    ucols = min(2 * BLK, vgroups[0][3] * tq)
    units = []
    for gi, (r0, nr, h0, hn) in enumerate(vgroups):
        for off in range(0, hn * tq, ucols):
            units.append((gi, r0, nr, h0 * tq + off, off))

    def scores(c, slot, col0):
        cols = slice(col0, col0 + ucols)
        ks = pl.multiple_of(c * tk, tk)
        st = jnp.dot(k_ref[pl.ds(ks, tk), :], qm_scr[:, cols], preferred_element_type=F32)
        s_scr[slot, :, cols] = st
        mx_scr[slot, :, cols] = jnp.max(st, axis=0, keepdims=True)

    for unit in units:
        scores(0, 0, unit[3])

    per_trip = nk // KEY_LOOP_TRIPS
    items = [(ci, un) for ci in range(per_trip) for un in range(len(units))]

    def body(i, carry):
        c0 = per_trip * i

        def next_scores(k):
            ci, un = items[k]
            scores(jnp.minimum(c0 + ci + 1, nk - 1), (ci + 1) % SCORE_SLOTS, units[un][3])

        next_scores(0)
        for k, (ci, un) in enumerate(items):
            slot = ci % SCORE_SLOTS
            if un == 0:
                vtc = vt_ref[c0 + ci]
                m_prev = m_scr[...]
                m_new = jnp.maximum(m_prev, mx_scr[slot])
                alpha = jnp.exp2(m_prev - m_new)
                m_scr[...] = m_new
            gi, r0, nr, col0, off = units[un]
            cols = slice(col0, col0 + ucols)
            pt = jnp.exp2(s_scr[slot, :, cols] - m_new[:, cols]).astype(BF16)
            if k + 1 < len(items):
                next_scores(k + 1)
            acols = slice(off, off + ucols)
            acc_scr[gi, :, acols] = acc_scr[gi, :, acols] * alpha[:, cols] + jnp.dot(
                vtc[r0:r0 + nr + _ones_rows(nr), :], pt, preferred_element_type=F32)
        return carry

    lax.fori_loop(0, KEY_LOOP_TRIPS, body, 0)

    pieces = []
    if diff:
        lam = (jnp.exp(jnp.sum(lq1[...] * lk1[...], axis=-1, keepdims=True))
               - jnp.exp(jnp.sum(lq2[...] * lk2[...], axis=-1, keepdims=True)) + lambda_init)
        for gi, (r0, nr, h0, hn) in enumerate(vgroups):
            acc = acc_scr[gi]
            linv = 1.0 / acc[nr:nr + 1, :]
            oh = acc[:nr, :tq] * linv[:, :tq] - lam * (acc[:nr, tq:] * linv[:, tq:])
            ms = jnp.mean(oh * oh, axis=0, keepdims=True)
            pieces.append(oh * lax.rsqrt(ms + NORM_EPS))
    else:
        for gi, (r0, nr, h0, hn) in enumerate(vgroups):
            acc = acc_scr[gi]
            on = acc[:nr, :] * (1.0 / acc[nr:nr + 1, :])
            for jj in range(hn):
                pieces.append(on[:, jj * tq:(jj + 1) * tq])
    ot = pieces[0] if len(pieces) == 1 else jnp.concatenate(pieces, axis=0)
    o = ot.T
    if diff:
        o = o * sg_ref[...] * (1.0 - lambda_init)
    o_ref[...] = o.astype(o_ref.dtype)


def _flash(q, k, vt, *, qcol0, kcol0, ngroups, wq, wv, wo, nh, vgroups, qmask, tq, tk,
           diff=None, diff_params=None):
    b, s = q.shape[0], q.shape[1]
    nk = s // tk
    assert s % tk == 0 and nk % (KEY_LOOP_TRIPS * SCORE_SLOTS) == 0 and s % tq == 0
    qmask = jnp.broadcast_to(qmask.astype(F32)[:, :, None], (nh, wq, LANES))
    in_specs = [
        pl.BlockSpec((nh, wq, LANES), lambda bi, g, i: (0, 0, 0)),
        pl.BlockSpec((None, tq, wq), lambda bi, g, i: (bi, i, qcol0 + g)),
        pl.BlockSpec((None, s, wq), lambda bi, g, i: (bi, 0, kcol0 + g)),
        pl.BlockSpec((None, None, nk, wv, tk), lambda bi, g, i: (bi, g, 0, 0, 0)),
    ]
    args = [qmask, q, k, vt]
    if diff is not None:
        for prm in diff_params[:4]:
            in_specs.append(pl.BlockSpec((1, 64), lambda bi, g, i: (0, 0)))
            args.append(prm.reshape(1, 64).astype(F32))
        in_specs.append(pl.BlockSpec((1, wo), lambda bi, g, i: (0, 0)))
        args.append(jnp.tile(diff_params[4].astype(F32), wo // LANES).reshape(1, wo))
    kern = functools.partial(_flash_kernel, nh=nh, vgroups=tuple(vgroups), tq=tq, tk=tk, nk=nk,
                             diff=diff)
    nr, hn = vgroups[0][1], vgroups[0][3]
    return pl.pallas_call(
        kern,
        out_shape=jax.ShapeDtypeStruct((b, s, ngroups * wo), BF16),
        grid=(b, ngroups, s // tq),
        in_specs=in_specs,
        out_specs=pl.BlockSpec((None, tq, wo), lambda bi, g, i: (bi, i, g)),
        scratch_shapes=[
            pltpu.VMEM((wq, nh * tq), BF16),
            pltpu.VMEM((SCORE_SLOTS, tk, nh * tq), F32),
            pltpu.VMEM((SCORE_SLOTS, 1, nh * tq), F32),
            pltpu.VMEM((1, nh * tq), F32),
            pltpu.VMEM((len(vgroups), nr + _ones_rows(nr), hn * tq), F32),
        ],
        compiler_params=_cparams(("parallel", "parallel", "arbitrary")),
    )(*args)


def _band_kernel(qmask_ref, vmask_ref, q_ref, k_ref, v_ref, o_ref, lse_ref, *, tq, win, length, ntile):
    nh = 4
    vmask = vmask_ref[...]
    vmask_b = vmask.astype(BF16)

    def tile(t, carry):
        r0 = pl.multiple_of(t * tq, tq)
        i = pl.program_id(2) * ntile + t
        ks = jnp.clip(i * tq - A_HALF_WINDOW, 0, length - win)
        ks = pl.multiple_of(ks, A_HALF_WINDOW)
        kc = k_ref[pl.ds(ks, win), :]
        vc = v_ref[pl.ds(ks, win), :]
        q = q_ref[pl.ds(r0, tq), :]
        qm = jnp.concatenate([q * qmask_ref[j:j + 1, :] for j in range(nh)], axis=0)
        s = lax.dot_general(qm, kc, (((1,), (1,)), ((), ())), preferred_element_type=F32)
        qpos = i * tq + lax.broadcasted_iota(jnp.int32, (tq, win), 0)
        kpos = ks + lax.broadcasted_iota(jnp.int32, (tq, win), 1)
        valid = jnp.abs(qpos - kpos) <= A_HALF_WINDOW
        ps = []
        inv = None
        lse = None
        for j in range(nh):
            sj = jnp.where(valid, s[j * tq:(j + 1) * tq], NEG_INF)
            mj = jnp.max(sj, axis=-1, keepdims=True)
            pj = jnp.exp2(sj - mj)
            lj = jnp.sum(pj, axis=-1, keepdims=True)
            ps.append(pj.astype(BF16))
            t_inv = (1.0 / lj) * vmask[j:j + 1, :]
            t_lse = ((mj + jnp.log2(lj)) * (1.0 / LOG2E)) * vmask[j:j + 1, :]
            inv = t_inv if inv is None else inv + t_inv
            lse = t_lse if lse is None else lse + t_lse
        lhs = jnp.concatenate(ps, axis=1)
        rhs = jnp.concatenate([vc * vmask_b[j:j + 1, :] for j in range(nh)], axis=0)
        pv = jnp.dot(lhs, rhs, preferred_element_type=F32)
        o_ref[pl.ds(r0, tq), :] = (pv * inv).astype(o_ref.dtype)
        lse_ref[pl.ds(r0, tq), :] = lse
        return carry

    lax.fori_loop(0, ntile, tile, 0)


def _band_attention(qkv, qmask, vmask):
    b, dil, length, _ = qkv.shape
    tq = min(256, length)
    win = min(tq + 2 * A_HALF_WINDOW, length)
    rows = min(4 * tq, length)

    def col(which):
        return lambda bi, a, i: (bi, a // 2, 0, which * 2 + a % 2)

    in_specs = [
        pl.BlockSpec((4, BLK), lambda bi, a, i: (0, 0)),
        pl.BlockSpec((4, BLK), lambda bi, a, i: (0, 0)),
        pl.BlockSpec((None, None, rows, BLK), lambda bi, a, i: (bi, a // 2, i, a % 2)),
        pl.BlockSpec((None, None, length, BLK), col(1)),
        pl.BlockSpec((None, None, length, BLK), col(2)),
    ]
    kern = functools.partial(_band_kernel, tq=tq, win=win, length=length, ntile=rows // tq)
    out_block = pl.BlockSpec((None, None, rows, BLK), lambda bi, a, i: (bi, a // 2, i, a % 2))
    return pl.pallas_call(
        kern,
        out_shape=(jax.ShapeDtypeStruct((b, dil, length, 2 * BLK), BF16),
                   jax.ShapeDtypeStruct((b, dil, length, 2 * BLK), F32)),
        grid=(b, dil * 2, length // rows),
        in_specs=in_specs,
        out_specs=(out_block, out_block),
        compiler_params=_cparams(("parallel", "parallel", "arbitrary")),
    )(qmask, vmask, qkv, qkv, qkv)


def _a_out_kernel(*refs, dils, tm):
    ng = len(dils)
    o_refs, l_refs = refs[:ng], refs[ng:2 * ng]
    w_ref, x_ref, out_ref = refs[2 * ng:2 * ng + 3]
    scr = list(refs[2 * ng + 3:])

    def token_order(ref, dil):
        if dil == 1:
            return ref[0].astype(F32)
        buf = scr.pop(0)
        ncb = buf.shape[0]
        for r in range(dil):
            v = ref[r].astype(F32)
            for cb in range(ncb):
                buf[cb, pl.ds(r, tm // dil, stride=dil), :] = v[:, cb * LANES:(cb + 1) * LANES]
        return jnp.concatenate([buf[cb] for cb in range(ncb)], axis=1)

    ls = [token_order(l_refs[g], dils[g]) for g in range(ng)]
    os_ = [token_order(o_refs[g], dils[g]) for g in range(ng)]
    mx = functools.reduce(jnp.maximum, ls)
    es = [jnp.exp(l - mx) for l in ls]
    inv = 1.0 / functools.reduce(jnp.add, es)
    o = functools.reduce(jnp.add, [e * og for e, og in zip(es, os_)]) * inv
    out_ref[...] = x_ref[...] + jnp.dot(o.astype(BF16), w_ref[...], preferred_element_type=F32)


def _a_out(os_, lses, w, x3, tm):
    b, s, _ = x3.shape
    kd = w.shape[0]
    dils = tuple(o.shape[1] for o in os_)
    grp = [pl.BlockSpec((None, d, tm // d, kd), lambda bi, i: (bi, 0, i, 0)) for d in dils]
    row = pl.BlockSpec((None, tm, D_MODEL), lambda bi, i: (bi, i, 0))
    nscr = 2 * sum(1 for d in dils if d > 1)
    return pl.pallas_call(
        functools.partial(_a_out_kernel, dils=dils, tm=tm),
        out_shape=jax.ShapeDtypeStruct((b, s, D_MODEL), F32),
        grid=(b, s // tm),
        in_specs=grp + grp + [pl.BlockSpec((kd, D_MODEL), lambda bi, i: (0, 0)), row],
        out_specs=row,
        scratch_shapes=[pltpu.VMEM((kd // LANES, tm, LANES), F32)] * nscr,
        compiler_params=_cparams(("parallel", "parallel")),
    )(*os_, *lses, w, x3)


def _post_kernel(*refs, has_proj, final):
    if has_proj:
        (x_ref, o_ref, wout_ref, gx_ref, wq_ref, kv_ref, wo_ref, gm_ref, w1_ref, w2_ref, fg_ref,
         out_ref, x2_scr, h_scr, acc_scr) = refs
    else:
        (x_ref, gx_ref, wq_ref, kv_ref, wo_ref, gm_ref, w1_ref, w2_ref, fg_ref,
         out_ref, x2_scr, h_scr, acc_scr) = refs
    f = pl.program_id(2)

    @pl.when(f == 0)
    def _():
        x1 = x_ref[...]
        if has_proj:
            x1 = x1 + jnp.dot(o_ref[...], wout_ref[...], preferred_element_type=F32)
        ms = jnp.mean(x1 * x1, axis=-1, keepdims=True)
        h = (x1 * lax.rsqrt(ms + NORM_EPS) * gx_ref[...]).astype(BF16)
        qb = (jnp.dot(h, wq_ref[...], preferred_element_type=F32)
              * (X_HEAD_DIM ** -0.5 * LOG2E)).astype(BF16)
        hd = X_HEADS * X_HEAD_DIM
        outs = []
        for hh in range(X_HEADS):
            qh = qb[:, hh * X_HEAD_DIM:(hh + 1) * X_HEAD_DIM]
            kh = kv_ref[:, hh * X_HEAD_DIM:(hh + 1) * X_HEAD_DIM]
            vh = kv_ref[:, hd + hh * X_HEAD_DIM:hd + (hh + 1) * X_HEAD_DIM]
            s = lax.dot_general(qh, kh, (((1,), (1,)), ((), ())), preferred_element_type=F32)
            m = jnp.max(s, axis=-1, keepdims=True)
            p = jnp.exp2(s - m)
            l = jnp.sum(p, axis=-1, keepdims=True)
            oh = jnp.dot(p.astype(BF16), vh, preferred_element_type=F32) * (1.0 / l)
            outs.append(oh.astype(BF16))
        x2 = x1 + jnp.dot(jnp.concatenate(outs, axis=1), wo_ref[...], preferred_element_type=F32)
        x2_scr[...] = x2
        ms2 = jnp.mean(x2 * x2, axis=-1, keepdims=True)
        h_scr[...] = (x2 * lax.rsqrt(ms2 + NORM_EPS) * gm_ref[...]).astype(BF16)
        acc_scr[...] = jnp.zeros(acc_scr.shape, F32)

    a = jnp.maximum(jnp.dot(h_scr[...], w1_ref[...], preferred_element_type=F32), 0.0)
    acc_scr[...] += jnp.dot((a * a).astype(BF16), w2_ref[...], preferred_element_type=F32)

    @pl.when(f == pl.num_programs(2) - 1)
    def _():
        y = x2_scr[...] + acc_scr[...]
        if final:
            ms = jnp.mean(y * y, axis=-1, keepdims=True)
            y = y * lax.rsqrt(ms + NORM_EPS) * fg_ref[...]
        out_ref[...] = y


def _post_mixer(x3, o3, w_out, gx, wq, kv3, wo, gm, w1, w2, final_gain, final, tm, tf):
    b, s, _ = x3.shape
    hd = X_HEADS * X_HEAD_DIM
    has_proj = o3 is not None
    const = lambda bi, i, f: (0, 0)
    row = lambda bi, i, f: (bi, i, 0)
    in_specs = [pl.BlockSpec((None, tm, D_MODEL), row)]
    args = [x3]
    if has_proj:
        kd = o3.shape[-1]
        in_specs += [pl.BlockSpec((None, tm, kd), row), pl.BlockSpec((kd, D_MODEL), const)]
        args += [o3, w_out]
    in_specs += [pl.BlockSpec((1, D_MODEL), const),
                 pl.BlockSpec((D_MODEL, hd), const),
                 pl.BlockSpec((None, N_MEM, 2 * hd), lambda bi, i, f: (bi, 0, 0)),
                 pl.BlockSpec((hd, D_MODEL), const),
                 pl.BlockSpec((1, D_MODEL), const),
                 pl.BlockSpec((D_MODEL, tf), lambda bi, i, f: (0, f)),
                 pl.BlockSpec((tf, D_MODEL), lambda bi, i, f: (f, 0)),
                 pl.BlockSpec((1, D_MODEL), const)]
    args += [gx.reshape(1, D_MODEL), wq, kv3, wo, gm.reshape(1, D_MODEL), w1, w2,
             final_gain.reshape(1, D_MODEL)]
    return pl.pallas_call(
        functools.partial(_post_kernel, has_proj=has_proj, final=final),
        out_shape=jax.ShapeDtypeStruct((b, s, D_MODEL), F32),
        grid=(b, s // tm, D_FF // tf),
        in_specs=in_specs,
        out_specs=pl.BlockSpec((None, tm, D_MODEL), row),
        scratch_shapes=[pltpu.VMEM((tm, D_MODEL), F32), pltpu.VMEM((tm, D_MODEL), BF16),
                        pltpu.VMEM((tm, D_MODEL), F32)],
        compiler_params=_cparams(("parallel", "parallel", "arbitrary")),
    )(*args)


_F_ROT16 = np.array(list(range(0, 8)) + list(range(16, 40)))
_P_ROT16 = np.array(list(range(8, 16)) + list(range(40, 64)))
_F_AXIAL = np.array(list(range(0, 16)) + list(range(32, 48)))
_P_AXIAL = np.array(list(range(16, 32)) + list(range(48, 64)))


def _block_dims(first, partner):
    lane = np.arange(BLK)
    half, slot, u = lane // LANES, (lane % LANES) // 32, lane % 32
    return np.where(half == 0, first[u], partner[u]), slot


def _slot_masks(nslot, slot_of_lane):
    return np.stack([(slot_of_lane == j) for j in range(nslot)]).astype(np.float32)


def _rope_tables(pos_list, theta, rot, scale_list, npad):
    half = rot // 2
    inv_freq = jnp.exp(jnp.arange(half, dtype=F32) * (-2.0 * math.log(theta) / rot))
    cs, ss = [], []
    for pos in pos_list:
        ang = pos.astype(F32)[:, None] * inv_freq[None, :]
        cs.append(jnp.cos(ang))
        ss.append(jnp.sin(ang))
    c = jnp.concatenate(cs, axis=1)
    s = jnp.concatenate(ss, axis=1)
    n = c.shape[0]
    if npad:
        c = jnp.concatenate([c, jnp.ones((n, npad), F32)], axis=1)
        s = jnp.concatenate([s, jnp.zeros((n, npad), F32)], axis=1)
    reps = LANES // c.shape[1]
    c = jnp.tile(c, (1, reps))
    s = jnp.tile(s, (1, reps))
    ctab = [c * sc for sc in scale_list] + [jnp.ones_like(c)]
    stab = [s * sc for sc in scale_list] + [jnp.zeros_like(s)]
    return jnp.stack(ctab), jnp.stack(stab)


def _mixer_a(x, b, s, gain, w_in, w_out, pos, tm):
    dims, slot = _block_dims(_F_ROT16, _P_ROT16)
    ctab, stab = _rope_tables([pos], ROPE_THETA, 16, [0.125 * LOG2E, 1.0], 24)
    qmask = jnp.asarray(_slot_masks(4, slot), BF16)
    vmask = jnp.asarray(_slot_masks(4, np.arange(BLK) // 64), F32)
    w_bf = w_in.astype(BF16)
    outs, lses = [], []
    for wg, (window, dil) in enumerate(A_PATTERNS):
        assert window // (2 * dil) == A_HALF_WINDOW
        cols = []
        for which in range(3):
            base = (wg * 3 + which) * A_HEADS * 64
            for hg in range(2):
                cols.append(base + hg * BLK + (slot * 64 + dims if which < 2 else np.arange(BLK)))
        w = w_bf[:, np.concatenate(cols)]
        qkv = _proj(x, 0, D_MODEL, D_MODEL, gain, w, tn=2 * BLK, out_dtype=BF16, seq=s, tm=tm,
                    tables=(ctab, stab), tt=np.array([0, 1, 2], np.int32), dil=dil)
        o, lse = _band_attention(qkv.reshape(b, dil, s // dil, 3 * 2 * BLK), qmask, vmask)
        outs.append(o)
        lses.append(lse)
    x3 = _a_out(outs, lses, w_out.astype(BF16), x.reshape(b, s, D_MODEL), min(tm, 512))
    return x3.reshape(b * s, D_MODEL), None, None


def _mixer_b(x, b, s, gain, w_in, q_gain, k_gain, w_out, rows, cols_pos, tm, tq, tk):
    dims, slot = _block_dims(_F_AXIAL, _P_AXIAL)
    qcols = np.concatenate([g * BLK + slot * 64 + dims for g in range(B_KV_HEADS)])
    kcols = np.concatenate([B_HEADS * 64 + g * 64 + dims for g in range(B_KV_HEADS)])
    w_bf = w_in.astype(BF16)
    wqk = w_bf[:, np.concatenate([qcols, kcols])]
    wv = w_bf[:, (B_HEADS + B_KV_HEADS) * 64:]
    hgain = jnp.concatenate([jnp.tile(q_gain[dims] * (0.125 * LOG2E), B_KV_HEADS),
                             jnp.tile(k_gain[dims], B_KV_HEADS)]).reshape(1, -1).astype(F32)
    bd = jnp.asarray(slot[:, None] == slot[None, :], BF16)
    ctab, stab = _rope_tables([rows, cols_pos], AXIAL_THETA, 32, [1.0], 0)
    n = B_KV_HEADS * BLK
    qk = _proj(x, 0, D_MODEL, D_MODEL, gain, wqk, tn=2 * BLK, out_dtype=BF16, seq=s, tm=tm,
               tables=(ctab, stab), tt=np.zeros((n // BLK,), np.int32), headnorm=(hgain, bd))
    vt = _proj(x, 0, D_MODEL, D_MODEL, gain, wv, tn=B_KV_HEADS * 64, out_dtype=BF16, seq=s, tm=tm,
               vt=(B_KV_HEADS, 1, 64, tk))
    qk = qk.reshape(b, s, 2 * n)
    qmask = jnp.asarray(_slot_masks(4, slot), BF16)
    o = _flash(qk, qk, vt, qcol0=0, kcol0=B_KV_HEADS, ngroups=B_KV_HEADS, wq=BLK,
               wv=64 + _ones_rows(64), wo=BLK, nh=4, vgroups=((0, 64, 0, 4),), qmask=qmask,
               tq=tq, tk=tk)
    return x, o, w_out.astype(BF16)


def _mixer_c(x, b, s, gain, w_in, lq1, lk1, lq2, lk2, sub_gain, w_out, pos, lambda_init, tm, tq, tk):
    dims, slot = _block_dims(_F_ROT16, _P_ROT16)
    ngr = C_HEADS // 2
    cols = []
    for which in range(2):
        for g in range(ngr):
            cols.append(which * C_HEADS * 128 + g * BLK + slot * 64 + dims)
    w_bf = w_in.astype(BF16)
    ctab, stab = _rope_tables([pos], ROPE_THETA, 16, [0.125 * LOG2E, 1.0], 24)
    n = ngr * BLK
    qk = _proj(x, 0, D_MODEL, D_MODEL, gain, w_bf[:, np.concatenate(cols)], tn=2 * BLK, out_dtype=BF16,
               seq=s, tm=tm, tables=(ctab, stab), tt=np.array([0, 0, 1, 1], np.int32))
    qkv = qk.reshape(b, s, 2 * n)
    vt = _proj(x, 0, D_MODEL, D_MODEL, gain, w_bf[:, 2 * C_HEADS * 128:], tn=n, out_dtype=BF16,
               seq=s, tm=tm, vt=(ngr, 2, LANES, tk))
    qmask = jnp.asarray(_slot_masks(4, slot), BF16)
    hrows = LANES + _ones_rows(LANES)
    o = _flash(qkv, qkv, vt, qcol0=0, kcol0=ngr, ngroups=ngr, wq=BLK, wv=2 * hrows,
               wo=BLK, nh=4, vgroups=((0, LANES, 0, 2), (hrows, LANES, 2, 2)),
               qmask=qmask, tq=tq, tk=tk,
               diff=lambda_init, diff_params=(lq1, lk1, lq2, lk2, sub_gain))
    return x, o, w_out.astype(BF16)


def _mixer_d(x, b, s, gain, w_in, q_gain, kv_gain, w_uq, w_ukv, w_out, pos, tm, tq, tk):
    lane = np.arange(LANES)
    slot_h = np.where(lane < 32, 0, np.where(lane < 64, 1, np.where(lane < 80, 0, np.where(lane < 96, 1, -1))))
    slot = np.concatenate([slot_h, slot_h])
    nope_lane = lane < 64
    rope_lane = (lane >= 64) & (lane < 96)
    ngr = D_HEADS // 2

    w1 = jnp.zeros((D_MODEL, 4 * BLK), F32)
    w1 = w1.at[:, :D_Q_RANK].set(w_in[:, :D_Q_RANK])
    w1 = w1.at[:, 2 * BLK:3 * BLK].set(w_in[:, D_Q_RANK:D_Q_RANK + D_KV_RANK])
    kr_src = np.zeros((BLK,), np.int64)
    kr_on = np.zeros((BLK,), bool)
    for hf in range(2):
        for l in range(LANES):
            if rope_lane[l]:
                kr_src[hf * LANES + l] = D_Q_RANK + D_KV_RANK + hf * 16 + (l - 64) % 16
                kr_on[hf * LANES + l] = True
    w1 = w1.at[:, 3 * BLK:].set(jnp.where(jnp.asarray(kr_on)[None, :], w_in[:, kr_src], 0.0))
    cmb = _proj(x, 0, D_MODEL, D_MODEL, gain, w1.astype(BF16), tn=4 * BLK, out_dtype=F32, seq=s, tm=tm)

    qsrc = np.zeros((ngr * BLK,), np.int64)
    qon = np.zeros((ngr * BLK,), bool)
    ksrc = np.zeros((ngr * BLK,), np.int64)
    kon = np.zeros((ngr * BLK,), bool)
    for g in range(ngr):
        for hf in range(2):
            for l in range(LANES):
                idx = g * BLK + hf * LANES + l
                if slot_h[l] < 0:
                    continue
                head = 2 * g + slot_h[l]
                if nope_lane[l]:
                    d = hf * 32 + l % 32
                    qsrc[idx], qon[idx] = head * 96 + d, True
                    ksrc[idx], kon[idx] = head * 128 + d, True
                else:
                    d = hf * 16 + (l - 64) % 16
                    qsrc[idx], qon[idx] = head * 96 + D_NOPE + d, True
    wq2 = jnp.where(jnp.asarray(qon)[None, :], w_uq[:, qsrc], 0.0)
    wq2 = jnp.concatenate([wq2, jnp.zeros((2 * BLK - D_Q_RANK, ngr * BLK), F32)], axis=0).astype(BF16)
    wk2 = jnp.where(jnp.asarray(kon)[None, :], w_ukv[:, ksrc], 0.0).astype(BF16)
    vsrc = np.concatenate([h * 128 + D_NOPE + np.arange(64) for h in range(D_HEADS)])
    wv2 = w_ukv[:, vsrc].astype(BF16)
    qg = jnp.concatenate([q_gain, jnp.zeros((2 * BLK - D_Q_RANK,), F32)])

    half = D_ROPE // 2
    inv_freq = jnp.exp(jnp.arange(half, dtype=F32) * (-2.0 * math.log(ROPE_THETA) / D_ROPE))
    ang = pos.astype(F32)[:, None] * inv_freq[None, :]
    ones64 = jnp.ones((s, 64), F32)
    pad32 = jnp.ones((s, 32), F32)
    c = jnp.concatenate([ones64, jnp.cos(ang), jnp.cos(ang), pad32], axis=1)
    sn = jnp.concatenate([0.0 * ones64, jnp.sin(ang), jnp.sin(ang), 0.0 * pad32], axis=1)
    qs = (D_NOPE + D_ROPE) ** -0.5 * LOG2E
    ctab = jnp.stack([c * qs, c])
    stab = jnp.stack([sn * qs, sn])

    q = _proj(cmb, 0, 2 * BLK, D_Q_RANK, qg, wq2, tn=2 * BLK, out_dtype=BF16, seq=s, tm=tm,
              tables=(ctab, stab), tt=np.zeros((ngr // 2,), np.int32))
    k = _proj(cmb, 2, BLK, D_KV_RANK, kv_gain, wk2, tn=4 * BLK, out_dtype=BF16, seq=s, tm=tm,
              tables=(ctab, stab), tt=np.ones((ngr // 4,), np.int32), add=(cmb, 3))
    vt = _proj(cmb, 2, BLK, D_KV_RANK, kv_gain, wv2, tn=4 * BLK, out_dtype=BF16, seq=s, tm=tm,
               vt=(ngr, 2, 64, tk))
    q = q.reshape(b, s, ngr * BLK)
    k = k.reshape(b, s, ngr * BLK)
    qmask = jnp.asarray(_slot_masks(2, slot), BF16)
    hrows = 64 + _ones_rows(64)
    o = _flash(q, k, vt, qcol0=0, kcol0=0, ngroups=ngr, wq=BLK, wv=2 * hrows, wo=LANES,
               nh=2, vgroups=((0, 64, 0, 1), (hrows, 64, 1, 1)), qmask=qmask, tq=tq, tk=tk)
    return x, o, w_out.astype(BF16)


def _run_trunk(x3, mem3, p):
    b, s, _ = x3.shape
    t = b * s
    tm = 1024
    tm_mlp = 512
    tq, tq_mla, tk = 512, 1024, 512
    x = x3.reshape(t, D_MODEL)
    pos = jnp.arange(s, dtype=F32)
    rows = jnp.repeat(jnp.arange(s // GRID_W, dtype=F32), GRID_W)
    cols_pos = jnp.tile(jnp.arange(GRID_W, dtype=F32), s // GRID_W)
    memf = mem3.reshape(b * N_MEM, D_MODEL)
    for i in range(DEPTH):
        m, j = i % 4, i // 4
        g = p['norm_mix'][i]
        if m == 0:
            x, o, w_out = _mixer_a(x, b, s, g, p['a_w_in'][j], p['a_w_out'][j], pos, tm)
        elif m == 1:
            x, o, w_out = _mixer_b(x, b, s, g, p['b_w_in'][j], p['b_q_norm'][j], p['b_k_norm'][j],
                                   p['b_w_out'][j], rows, cols_pos, tm, tq, tk)
        elif m == 2:
            x, o, w_out = _mixer_c(x, b, s, g, p['c_w_in'][j], p['c_lambda_q1'][j],
                                   p['c_lambda_k1'][j], p['c_lambda_q2'][j], p['c_lambda_k2'][j],
                                   p['c_sub_norm'][j], p['c_w_out'][j], pos,
                                   0.8 - 0.6 * math.exp(-0.3 * i), tm, tq, tk)
        else:
            x, o, w_out = _mixer_d(x, b, s, g, p['d_w_in'][j], p['d_q_norm'][j], p['d_kv_norm'][j],
                                   p['d_w_uq'][j], p['d_w_ukv'][j], p['d_w_out'][j], pos, tm,
                                   tq_mla, tk)
        kv = _proj(memf, 0, D_MODEL, D_MODEL, p['norm_mem'][i], p['w_xkv'][i].astype(BF16),
                   tn=2 * X_HEADS * X_HEAD_DIM, out_dtype=BF16, seq=N_MEM, tm=N_MEM)
        x = _post_mixer(x.reshape(b, s, D_MODEL), o, w_out, p['norm_x'][i],
                        p['w_xq'][i].astype(BF16), kv.reshape(b, N_MEM, 2 * X_HEADS * X_HEAD_DIM),
                        p['w_xo'][i].astype(BF16), p['norm_mlp'][i], p['w_mlp_in'][i].astype(BF16),
                        p['w_mlp_out'][i].astype(BF16), p['final_norm'], i == DEPTH - 1,
                        tm_mlp, 2048).reshape(t, D_MODEL)
    return x.reshape(b, s, D_MODEL)


def kernel(x_prompt, x_sample, mem_prompt, mem_sample, norm_mix, norm_x, norm_mem, w_xq, w_xkv, w_xo, norm_mlp, w_mlp_in, w_mlp_out, a_w_in, a_w_out, b_w_in, b_q_norm, b_k_norm, b_w_out, c_w_in, c_lambda_q1, c_lambda_k1, c_lambda_q2, c_lambda_k2, c_sub_norm, c_w_out, d_w_in, d_q_norm, d_kv_norm, d_w_uq, d_w_ukv, d_w_out, final_norm):
    p = dict(norm_mix=norm_mix, norm_x=norm_x, norm_mem=norm_mem, w_xq=w_xq, w_xkv=w_xkv,
             w_xo=w_xo, norm_mlp=norm_mlp, w_mlp_in=w_mlp_in, w_mlp_out=w_mlp_out,
             a_w_in=a_w_in, a_w_out=a_w_out, b_w_in=b_w_in, b_q_norm=b_q_norm,
             b_k_norm=b_k_norm, b_w_out=b_w_out, c_w_in=c_w_in, c_lambda_q1=c_lambda_q1,
             c_lambda_k1=c_lambda_k1, c_lambda_q2=c_lambda_q2, c_lambda_k2=c_lambda_k2,
             c_sub_norm=c_sub_norm, c_w_out=c_w_out, d_w_in=d_w_in, d_q_norm=d_q_norm,
             d_kv_norm=d_kv_norm, d_w_uq=d_w_uq, d_w_ukv=d_w_ukv, d_w_out=d_w_out,
             final_norm=final_norm)
    return (_run_trunk(x_prompt, mem_prompt, p), _run_trunk(x_sample, mem_sample, p))
```

```python
import functools
import math

import numpy as np
import jax
import jax.numpy as jnp
from jax import lax
from jax.experimental import pallas as pl
from jax.experimental.pallas import tpu as pltpu

F32 = jnp.float32
BF16 = jnp.bfloat16

D_MODEL = 1024
DEPTH = 4
N_MEM = 256
GRID_W = 64
D_FF = 4 * D_MODEL
NORM_EPS = 1e-6
ROPE_THETA = 500000.0
AXIAL_THETA = 10000.0
NEG_INF = -1e30

A_PATTERNS = ((128, 1), (512, 4), (2048, 16))
A_GROUPS = 3
A_HEADS = 8
A_IN = A_GROUPS * 3 * A_HEADS * 64
A_HALF_WINDOW = 64
B_HEADS = 16
B_KV_HEADS = 4
C_HEADS = 8
D_HEADS = 16
D_Q_RANK = 384
D_KV_RANK = 256
D_NOPE = 64
D_ROPE = 32
X_HEADS = 4
X_HEAD_DIM = 128

LOG2E = 1.4426950408889634
BF16_ROWS = 16
SCORE_SLOTS = 2
KEY_LOOP_TRIPS = 2
LANES = 128
BLK = 2 * LANES
VMEM_LIMIT = 56 * 1024 * 1024


def _ones_rows(nr):
    return LANES - nr if nr < LANES else BF16_ROWS


def _cparams(sem):
    return pltpu.CompilerParams(dimension_semantics=sem, vmem_limit_bytes=VMEM_LIMIT)


def _proj_kernel(tt_ref, src_ref, g_ref, w_ref, *rest, dnorm, rope, headnorm, add, nsub, vt, dil):
    del tt_ref
    rest = list(rest)
    c_ref = s_ref = hg_ref = bd_ref = add_ref = y_scr = None
    if dil > 1:
        y_scr = rest.pop()
    if rope:
        c_ref, s_ref = rest[0], rest[1]
        rest = rest[2:]
    if headnorm:
        hg_ref, bd_ref = rest[0], rest[1]
        rest = rest[2:]
    if add:
        add_ref = rest[0]
        rest = rest[1:]
    o_ref, h_scr = rest

    @pl.when(pl.program_id(1) == 0)
    def _():
        xf = src_ref[...].astype(F32)
        ms = jnp.sum(xf * xf, axis=-1, keepdims=True) * (1.0 / dnorm)
        h_scr[...] = (xf * lax.rsqrt(ms + NORM_EPS) * g_ref[...]).astype(BF16)

    y = jnp.dot(h_scr[...], w_ref[...], preferred_element_type=F32)
    if add:
        ad = add_ref[...]
        y = y + (ad if nsub == 1 else jnp.concatenate([ad] * nsub, axis=1))
    if headnorm:
        y2 = y * y
        hi = y2.astype(BF16)
        lo = (y2 - hi.astype(F32)).astype(BF16)
        parts = []
        for n in range(nsub):
            sl = slice(n * BLK, (n + 1) * BLK)
            ss = (jnp.dot(hi[:, sl], bd_ref[...], preferred_element_type=F32)
                  + jnp.dot(lo[:, sl], bd_ref[...], preferred_element_type=F32))
            parts.append(y[:, sl] * lax.rsqrt(ss * (1.0 / 64.0) + NORM_EPS))
        y = (parts[0] if nsub == 1 else jnp.concatenate(parts, axis=1)) * hg_ref[...]
    if rope:
        c = c_ref[0]
        s = s_ref[0]
        for n in range(nsub):
            y1 = y[:, n * BLK:n * BLK + LANES]
            y2 = y[:, n * BLK + LANES:(n + 1) * BLK]
            r1 = y1 * c - y2 * s
            r2 = y2 * c + y1 * s
            if dil > 1:
                y_scr[2 * n] = r1
                y_scr[2 * n + 1] = r2
            else:
                o_ref[:, n * BLK:n * BLK + LANES] = r1.astype(o_ref.dtype)
                o_ref[:, n * BLK + LANES:(n + 1) * BLK] = r2.astype(o_ref.dtype)
        if dil > 1:
            rows = y.shape[0] // dil
            for r in range(dil):
                for cb in range(2 * nsub):
                    o_ref[r, :, cb * LANES:(cb + 1) * LANES] = (
                        y_scr[cb, pl.ds(r, rows, stride=dil), :].astype(o_ref.dtype))
    elif vt:
        ngroups, nvg, nr, tk = vt
        nones = _ones_rows(nr)
        ones = jnp.ones((nones, tk), o_ref.dtype)
        for cc in range(y.shape[0] // tk):
            yt = y[cc * tk:(cc + 1) * tk, :].T
            for g in range(ngroups):
                for vg in range(nvg):
                    src0 = (g * nvg + vg) * nr
                    r0 = vg * (nr + nones)
                    o_ref[g, cc, r0:r0 + nr, :] = yt[src0:src0 + nr, :].astype(o_ref.dtype)
                    o_ref[g, cc, r0 + nr:r0 + nr + nones, :] = ones
    else:
        o_ref[...] = y.astype(o_ref.dtype)


def _proj(src, src_cb, kdim, dnorm, gain, w, *, tn, out_dtype, seq, tm,
          tables=None, tt=None, headnorm=None, add=None, vt=None, dil=1):
    t = src.shape[0]
    n = w.shape[1]
    nj = n // tn
    out_shape = jax.ShapeDtypeStruct((t, n), out_dtype)
    out_spec = pl.BlockSpec((tm, tn), lambda i, j, tt_: (i, j))
    if vt is not None:
        ngroups, nvg, nr, tk = vt
        assert nj == 1 and n == ngroups * nvg * nr and tm % tk == 0
        wv = nvg * (nr + _ones_rows(nr))
        out_shape = jax.ShapeDtypeStruct((t // seq, ngroups, seq // tk, wv, tk), out_dtype)
        out_spec = pl.BlockSpec((None, ngroups, tm // tk, wv, tk),
                                lambda i, j, tt_: (i // (seq // tm), 0, i % (seq // tm), 0, 0))
    scratch = [pltpu.VMEM((tm, kdim), BF16)]
    if dil > 1:
        assert tables is not None and tm % (16 * dil) == 0
        out_shape = jax.ShapeDtypeStruct((t // seq, dil, seq // dil, n), out_dtype)
        out_spec = pl.BlockSpec((None, dil, tm // dil, tn),
                                lambda i, j, tt_: (i // (seq // tm), 0, i % (seq // tm), j))
        scratch.append(pltpu.VMEM((tn // LANES, tm, LANES), F32))
    nsub = tn // BLK if (tables is not None or headnorm is not None) else 1
    ns = seq // tm
    if tt is None:
        tt = np.zeros((nj,), np.int32)
    in_specs = [
        pl.BlockSpec((tm, kdim), lambda i, j, tt_: (i, src_cb)),
        pl.BlockSpec((1, kdim), lambda i, j, tt_: (0, 0)),
        pl.BlockSpec((kdim, tn), lambda i, j, tt_: (0, j)),
    ]
    args = [src, gain.reshape(1, kdim).astype(F32), w]
    if tables is not None:
        for tb in tables:
            in_specs.append(pl.BlockSpec((1, tm, LANES), lambda i, j, tt_: (tt_[j], i % ns, 0)))
            args.append(tb)
    if headnorm is not None:
        in_specs.append(pl.BlockSpec((1, tn), lambda i, j, tt_: (0, j)))
        in_specs.append(pl.BlockSpec((BLK, BLK), lambda i, j, tt_: (0, 0)))
        args += [headnorm[0], headnorm[1]]
    if add is not None:
        add_arr, add_cb = add
        in_specs.append(pl.BlockSpec((tm, BLK), lambda i, j, tt_: (i, add_cb)))
        args.append(add_arr)
    kern = functools.partial(_proj_kernel, dnorm=dnorm, rope=tables is not None,
                             headnorm=headnorm is not None, add=add is not None, nsub=nsub, vt=vt,
                             dil=dil)
    return pl.pallas_call(
        kern,
        out_shape=out_shape,
        grid_spec=pltpu.PrefetchScalarGridSpec(
            num_scalar_prefetch=1,
            grid=(t // tm, nj),
            in_specs=in_specs,
            out_specs=out_spec,
            scratch_shapes=scratch,
        ),
        compiler_params=_cparams(("parallel", "arbitrary")),
    )(jnp.asarray(tt, jnp.int32), *args)


def _flash_kernel(*refs, nh, vgroups, tq, tk, nk, diff):
    if diff:
        (qmask_ref, q_ref, k_ref, vt_ref, lq1, lk1, lq2, lk2, sg_ref,
         o_ref, qm_scr, s_scr, mx_scr, m_scr, acc_scr) = refs
        lambda_init = diff
    else:
        qmask_ref, q_ref, k_ref, vt_ref, o_ref, qm_scr, s_scr, mx_scr, m_scr, acc_scr = refs

    qt = q_ref[...].astype(F32).T
    for j in range(nh):
        rowmask = jnp.concatenate([qmask_ref[j]] * (tq // LANES), axis=1)
        qm_scr[:, j * tq:(j + 1) * tq] = (qt * rowmask).astype(BF16)
    m_scr[...] = jnp.full(m_scr.shape, NEG_INF, F32)
    acc_scr[...] = jnp.zeros(acc_scr.shape, F32)

    ucols = min(2 * BLK, vgroups[0][3] * tq)
    units = []
    for gi, (r0, nr, h0, hn) in enumerate(vgroups):
        for off in range(0, hn * tq, ucols):
            units.append((gi, r0, nr, h0 * tq + off, off))

    def scores(c, slot, col0):
        cols = slice(col0, col0 + ucols)
        ks = pl.multiple_of(c * tk, tk)
        st = jnp.dot(k_ref[pl.ds(ks, tk), :], qm_scr[:, cols], preferred_element_type=F32)
        s_scr[slot, :, cols] = st
        mx_scr[slot, :, cols] = jnp.max(st, axis=0, keepdims=True)

    def step(c, slot, c_next, slot_next):
        vtc = vt_ref[c]
        m_prev = m_scr[...]
        m_new = jnp.maximum(m_prev, mx_scr[slot])
        alpha = jnp.exp2(m_prev - m_new)
        m_scr[...] = m_new
        for gi, r0, nr, col0, off in units:
            cols = slice(col0, col0 + ucols)
            scores(c_next, slot_next, col0)
            pt = jnp.exp2(s_scr[slot, :, cols] - m_new[:, cols]).astype(BF16)
            acols = slice(off, off + ucols)
            acc_scr[gi, :, acols] = acc_scr[gi, :, acols] * alpha[:, cols] + jnp.dot(
                vtc[r0:r0 + nr + _ones_rows(nr), :], pt, preferred_element_type=F32)

    for unit in units:
        scores(0, 0, unit[3])

    per_trip = nk // KEY_LOOP_TRIPS

    def body(i, carry):
        c = per_trip * i
        for u in range(per_trip):
            step(c + u, u % SCORE_SLOTS, jnp.minimum(c + u + 1, nk - 1), (u + 1) % SCORE_SLOTS)
        return carry

    lax.fori_loop(0, KEY_LOOP_TRIPS, body, 0)

    pieces = []
    if diff:
        lam = (jnp.exp(jnp.sum(lq1[...] * lk1[...], axis=-1, keepdims=True))
               - jnp.exp(jnp.sum(lq2[...] * lk2[...], axis=-1, keepdims=True)) + lambda_init)
        for gi, (r0, nr, h0, hn) in enumerate(vgroups):
            acc = acc_scr[gi]
            linv = 1.0 / acc[nr:nr + 1, :]
            oh = acc[:nr, :tq] * linv[:, :tq] - lam * (acc[:nr, tq:] * linv[:, tq:])
            ms = jnp.mean(oh * oh, axis=0, keepdims=True)
            pieces.append(oh * lax.rsqrt(ms + NORM_EPS))
    else:
        for gi, (r0, nr, h0, hn) in enumerate(vgroups):
            acc = acc_scr[gi]
            on = acc[:nr, :] * (1.0 / acc[nr:nr + 1, :])
            for jj in range(hn):
                pieces.append(on[:, jj * tq:(jj + 1) * tq])
    ot = pieces[0] if len(pieces) == 1 else jnp.concatenate(pieces, axis=0)
    o = ot.T
    if diff:
        o = o * sg_ref[...] * (1.0 - lambda_init)
    o_ref[...] = o.astype(o_ref.dtype)


def _flash(q, k, vt, *, qcol0, kcol0, ngroups, wq, wv, wo, nh, vgroups, qmask, tq, tk,
           diff=None, diff_params=None):
    b, s = q.shape[0], q.shape[1]
    nk = s // tk
    assert s % tk == 0 and nk % (KEY_LOOP_TRIPS * SCORE_SLOTS) == 0 and s % tq == 0
    qmask = jnp.broadcast_to(qmask.astype(F32)[:, :, None], (nh, wq, LANES))
    in_specs = [
        pl.BlockSpec((nh, wq, LANES), lambda bi, g, i: (0, 0, 0)),
        pl.BlockSpec((None, tq, wq), lambda bi, g, i: (bi, i, qcol0 + g)),
        pl.BlockSpec((None, s, wq), lambda bi, g, i: (bi, 0, kcol0 + g)),
        pl.BlockSpec((None, None, nk, wv, tk), lambda bi, g, i: (bi, g, 0, 0, 0)),
    ]
    args = [qmask, q, k, vt]
    if diff is not None:
        for prm in diff_params[:4]:
            in_specs.append(pl.BlockSpec((1, 64), lambda bi, g, i: (0, 0)))
            args.append(prm.reshape(1, 64).astype(F32))
        in_specs.append(pl.BlockSpec((1, wo), lambda bi, g, i: (0, 0)))
        args.append(jnp.tile(diff_params[4].astype(F32), wo // LANES).reshape(1, wo))
    kern = functools.partial(_flash_kernel, nh=nh, vgroups=tuple(vgroups), tq=tq, tk=tk, nk=nk,
                             diff=diff)
    nr, hn = vgroups[0][1], vgroups[0][3]
    return pl.pallas_call(
        kern,
        out_shape=jax.ShapeDtypeStruct((b, s, ngroups * wo), BF16),
        grid=(b, ngroups, s // tq),
        in_specs=in_specs,
        out_specs=pl.BlockSpec((None, tq, wo), lambda bi, g, i: (bi, i, g)),
        scratch_shapes=[
            pltpu.VMEM((wq, nh * tq), BF16),
            pltpu.VMEM((SCORE_SLOTS, tk, nh * tq), F32),
            pltpu.VMEM((SCORE_SLOTS, 1, nh * tq), F32),
            pltpu.VMEM((1, nh * tq), F32),
            pltpu.VMEM((len(vgroups), nr + _ones_rows(nr), hn * tq), F32),
        ],
        compiler_params=_cparams(("parallel", "parallel", "arbitrary")),
    )(*args)


def _band_kernel(qmask_ref, vmask_ref, q_ref, k_ref, v_ref, o_ref, lse_ref, *, tq, win, length, ntile):
    nh = 4
    vmask = vmask_ref[...]
    vmask_b = vmask.astype(BF16)

    def tile(t, carry):
        r0 = pl.multiple_of(t * tq, tq)
        i = pl.program_id(2) * ntile + t
        ks = jnp.clip(i * tq - A_HALF_WINDOW, 0, length - win)
        ks = pl.multiple_of(ks, A_HALF_WINDOW)
        kc = k_ref[pl.ds(ks, win), :]
        vc = v_ref[pl.ds(ks, win), :]
        q = q_ref[pl.ds(r0, tq), :]
        qm = jnp.concatenate([q * qmask_ref[j:j + 1, :] for j in range(nh)], axis=0)
        s = lax.dot_general(qm, kc, (((1,), (1,)), ((), ())), preferred_element_type=F32)
        qpos = i * tq + lax.broadcasted_iota(jnp.int32, (tq, win), 0)
        kpos = ks + lax.broadcasted_iota(jnp.int32, (tq, win), 1)
        valid = jnp.abs(qpos - kpos) <= A_HALF_WINDOW
        ps = []
        inv = None
        lse = None
        for j in range(nh):
            sj = jnp.where(valid, s[j * tq:(j + 1) * tq], NEG_INF)
            mj = jnp.max(sj, axis=-1, keepdims=True)
            pj = jnp.exp2(sj - mj)
            lj = jnp.sum(pj, axis=-1, keepdims=True)
            ps.append(pj.astype(BF16))
            t_inv = (1.0 / lj) * vmask[j:j + 1, :]
            t_lse = ((mj + jnp.log2(lj)) * (1.0 / LOG2E)) * vmask[j:j + 1, :]
            inv = t_inv if inv is None else inv + t_inv
            lse = t_lse if lse is None else lse + t_lse
        lhs = jnp.concatenate(ps, axis=1)
        rhs = jnp.concatenate([vc * vmask_b[j:j + 1, :] for j in range(nh)], axis=0)
        pv = jnp.dot(lhs, rhs, preferred_element_type=F32)
        o_ref[pl.ds(r0, tq), :] = (pv * inv).astype(o_ref.dtype)
        lse_ref[pl.ds(r0, tq), :] = lse
        return carry

    lax.fori_loop(0, ntile, tile, 0)


def _band_attention(qkv, qmask, vmask):
    b, dil, length, _ = qkv.shape
    tq = min(256, length)
    win = min(tq + 2 * A_HALF_WINDOW, length)
    rows = min(4 * tq, length)

    def col(which):
        return lambda bi, a, i: (bi, a // 2, 0, which * 2 + a % 2)

    in_specs = [
        pl.BlockSpec((4, BLK), lambda bi, a, i: (0, 0)),
        pl.BlockSpec((4, BLK), lambda bi, a, i: (0, 0)),
        pl.BlockSpec((None, None, rows, BLK), lambda bi, a, i: (bi, a // 2, i, a % 2)),
        pl.BlockSpec((None, None, length, BLK), col(1)),
        pl.BlockSpec((None, None, length, BLK), col(2)),
    ]
    kern = functools.partial(_band_kernel, tq=tq, win=win, length=length, ntile=rows // tq)
    out_block = pl.BlockSpec((None, None, rows, BLK), lambda bi, a, i: (bi, a // 2, i, a % 2))
    return pl.pallas_call(
        kern,
        out_shape=(jax.ShapeDtypeStruct((b, dil, length, 2 * BLK), BF16),
                   jax.ShapeDtypeStruct((b, dil, length, 2 * BLK), F32)),
        grid=(b, dil * 2, length // rows),
        in_specs=in_specs,
        out_specs=(out_block, out_block),
        compiler_params=_cparams(("parallel", "parallel", "arbitrary")),
    )(qmask, vmask, qkv, qkv, qkv)


def _a_out_kernel(*refs, dils, tm):
    ng = len(dils)
    o_refs, l_refs = refs[:ng], refs[ng:2 * ng]
    w_ref, x_ref, out_ref = refs[2 * ng:2 * ng + 3]
    scr = list(refs[2 * ng + 3:])

    def token_order(ref, dil):
        if dil == 1:
            return ref[0].astype(F32)
        buf = scr.pop(0)
        ncb = buf.shape[0]
        for r in range(dil):
            v = ref[r].astype(F32)
            for cb in range(ncb):
                buf[cb, pl.ds(r, tm // dil, stride=dil), :] = v[:, cb * LANES:(cb + 1) * LANES]
        return jnp.concatenate([buf[cb] for cb in range(ncb)], axis=1)

    ls = [token_order(l_refs[g], dils[g]) for g in range(ng)]
    os_ = [token_order(o_refs[g], dils[g]) for g in range(ng)]
    mx = functools.reduce(jnp.maximum, ls)
    es = [jnp.exp(l - mx) for l in ls]
    inv = 1.0 / functools.reduce(jnp.add, es)
    o = functools.reduce(jnp.add, [e * og for e, og in zip(es, os_)]) * inv
    out_ref[...] = x_ref[...] + jnp.dot(o.astype(BF16), w_ref[...], preferred_element_type=F32)


def _a_out(os_, lses, w, x3, tm):
    b, s, _ = x3.shape
    kd = w.shape[0]
    dils = tuple(o.shape[1] for o in os_)
    grp = [pl.BlockSpec((None, d, tm // d, kd), lambda bi, i: (bi, 0, i, 0)) for d in dils]
    row = pl.BlockSpec((None, tm, D_MODEL), lambda bi, i: (bi, i, 0))
    nscr = 2 * sum(1 for d in dils if d > 1)
    return pl.pallas_call(
        functools.partial(_a_out_kernel, dils=dils, tm=tm),
        out_shape=jax.ShapeDtypeStruct((b, s, D_MODEL), F32),
        grid=(b, s // tm),
        in_specs=grp + grp + [pl.BlockSpec((kd, D_MODEL), lambda bi, i: (0, 0)), row],
        out_specs=row,
        scratch_shapes=[pltpu.VMEM((kd // LANES, tm, LANES), F32)] * nscr,
        compiler_params=_cparams(("parallel", "parallel")),
    )(*os_, *lses, w, x3)


def _post_kernel(*refs, has_proj, final):
    if has_proj:
        (x_ref, o_ref, wout_ref, gx_ref, wq_ref, kv_ref, wo_ref, gm_ref, w1_ref, w2_ref, fg_ref,
         out_ref, x2_scr, h_scr, acc_scr) = refs
    else:
        (x_ref, gx_ref, wq_ref, kv_ref, wo_ref, gm_ref, w1_ref, w2_ref, fg_ref,
         out_ref, x2_scr, h_scr, acc_scr) = refs
    f = pl.program_id(2)

    @pl.when(f == 0)
    def _():
        x1 = x_ref[...]
        if has_proj:
            x1 = x1 + jnp.dot(o_ref[...], wout_ref[...], preferred_element_type=F32)
        ms = jnp.mean(x1 * x1, axis=-1, keepdims=True)
        h = (x1 * lax.rsqrt(ms + NORM_EPS) * gx_ref[...]).astype(BF16)
        qb = (jnp.dot(h, wq_ref[...], preferred_element_type=F32)
              * (X_HEAD_DIM ** -0.5 * LOG2E)).astype(BF16)
        hd = X_HEADS * X_HEAD_DIM
        outs = []
        for hh in range(X_HEADS):
            qh = qb[:, hh * X_HEAD_DIM:(hh + 1) * X_HEAD_DIM]
            kh = kv_ref[:, hh * X_HEAD_DIM:(hh + 1) * X_HEAD_DIM]
            vh = kv_ref[:, hd + hh * X_HEAD_DIM:hd + (hh + 1) * X_HEAD_DIM]
            s = lax.dot_general(qh, kh, (((1,), (1,)), ((), ())), preferred_element_type=F32)
            m = jnp.max(s, axis=-1, keepdims=True)
            p = jnp.exp2(s - m)
            l = jnp.sum(p, axis=-1, keepdims=True)
            oh = jnp.dot(p.astype(BF16), vh, preferred_element_type=F32) * (1.0 / l)
            outs.append(oh.astype(BF16))
        x2 = x1 + jnp.dot(jnp.concatenate(outs, axis=1), wo_ref[...], preferred_element_type=F32)
        x2_scr[...] = x2
        ms2 = jnp.mean(x2 * x2, axis=-1, keepdims=True)
        h_scr[...] = (x2 * lax.rsqrt(ms2 + NORM_EPS) * gm_ref[...]).astype(BF16)
        acc_scr[...] = jnp.zeros(acc_scr.shape, F32)

    a = jnp.maximum(jnp.dot(h_scr[...], w1_ref[...], preferred_element_type=F32), 0.0)
    acc_scr[...] += jnp.dot((a * a).astype(BF16), w2_ref[...], preferred_element_type=F32)

    @pl.when(f == pl.num_programs(2) - 1)
    def _():
        y = x2_scr[...] + acc_scr[...]
        if final:
            ms = jnp.mean(y * y, axis=-1, keepdims=True)
            y = y * lax.rsqrt(ms + NORM_EPS) * fg_ref[...]
        out_ref[...] = y


def _post_mixer(x3, o3, w_out, gx, wq, kv3, wo, gm, w1, w2, final_gain, final, tm, tf):
    b, s, _ = x3.shape
    hd = X_HEADS * X_HEAD_DIM
    has_proj = o3 is not None
    const = lambda bi, i, f: (0, 0)
    row = lambda bi, i, f: (bi, i, 0)
    in_specs = [pl.BlockSpec((None, tm, D_MODEL), row)]
    args = [x3]
    if has_proj:
        kd = o3.shape[-1]
        in_specs += [pl.BlockSpec((None, tm, kd), row), pl.BlockSpec((kd, D_MODEL), const)]
        args += [o3, w_out]
    in_specs += [pl.BlockSpec((1, D_MODEL), const),
                 pl.BlockSpec((D_MODEL, hd), const),
                 pl.BlockSpec((None, N_MEM, 2 * hd), lambda bi, i, f: (bi, 0, 0)),
                 pl.BlockSpec((hd, D_MODEL), const),
                 pl.BlockSpec((1, D_MODEL), const),
                 pl.BlockSpec((D_MODEL, tf), lambda bi, i, f: (0, f)),
                 pl.BlockSpec((tf, D_MODEL), lambda bi, i, f: (f, 0)),
                 pl.BlockSpec((1, D_MODEL), const)]
    args += [gx.reshape(1, D_MODEL), wq, kv3, wo, gm.reshape(1, D_MODEL), w1, w2,
             final_gain.reshape(1, D_MODEL)]
    return pl.pallas_call(
        functools.partial(_post_kernel, has_proj=has_proj, final=final),
        out_shape=jax.ShapeDtypeStruct((b, s, D_MODEL), F32),
        grid=(b, s // tm, D_FF // tf),
        in_specs=in_specs,
        out_specs=pl.BlockSpec((None, tm, D_MODEL), row),
        scratch_shapes=[pltpu.VMEM((tm, D_MODEL), F32), pltpu.VMEM((tm, D_MODEL), BF16),
                        pltpu.VMEM((tm, D_MODEL), F32)],
        compiler_params=_cparams(("parallel", "parallel", "arbitrary")),
    )(*args)


_F_ROT16 = np.array(list(range(0, 8)) + list(range(16, 40)))
_P_ROT16 = np.array(list(range(8, 16)) + list(range(40, 64)))
_F_AXIAL = np.array(list(range(0, 16)) + list(range(32, 48)))
_P_AXIAL = np.array(list(range(16, 32)) + list(range(48, 64)))


def _block_dims(first, partner):
    lane = np.arange(BLK)
    half, slot, u = lane // LANES, (lane % LANES) // 32, lane % 32
    return np.where(half == 0, first[u], partner[u]), slot


def _slot_masks(nslot, slot_of_lane):
    return np.stack([(slot_of_lane == j) for j in range(nslot)]).astype(np.float32)


def _rope_tables(pos_list, theta, rot, scale_list, npad):
    half = rot // 2
    inv_freq = jnp.exp(jnp.arange(half, dtype=F32) * (-2.0 * math.log(theta) / rot))
    cs, ss = [], []
    for pos in pos_list:
        ang = pos.astype(F32)[:, None] * inv_freq[None, :]
        cs.append(jnp.cos(ang))
        ss.append(jnp.sin(ang))
    c = jnp.concatenate(cs, axis=1)
    s = jnp.concatenate(ss, axis=1)
    n = c.shape[0]
    if npad:
        c = jnp.concatenate([c, jnp.ones((n, npad), F32)], axis=1)
        s = jnp.concatenate([s, jnp.zeros((n, npad), F32)], axis=1)
    reps = LANES // c.shape[1]
    c = jnp.tile(c, (1, reps))
    s = jnp.tile(s, (1, reps))
    ctab = [c * sc for sc in scale_list] + [jnp.ones_like(c)]
    stab = [s * sc for sc in scale_list] + [jnp.zeros_like(s)]
    return jnp.stack(ctab), jnp.stack(stab)


def _mixer_a(x, b, s, gain, w_in, w_out, pos, tm):
    dims, slot = _block_dims(_F_ROT16, _P_ROT16)
    ctab, stab = _rope_tables([pos], ROPE_THETA, 16, [0.125 * LOG2E, 1.0], 24)
    qmask = jnp.asarray(_slot_masks(4, slot), BF16)
    vmask = jnp.asarray(_slot_masks(4, np.arange(BLK) // 64), F32)
    w_bf = w_in.astype(BF16)
    outs, lses = [], []
    for wg, (window, dil) in enumerate(A_PATTERNS):
        assert window // (2 * dil) == A_HALF_WINDOW
        cols = []
        for which in range(3):
            base = (wg * 3 + which) * A_HEADS * 64
            for hg in range(2):
                cols.append(base + hg * BLK + (slot * 64 + dims if which < 2 else np.arange(BLK)))
        w = w_bf[:, np.concatenate(cols)]
        qkv = _proj(x, 0, D_MODEL, D_MODEL, gain, w, tn=2 * BLK, out_dtype=BF16, seq=s, tm=tm,
                    tables=(ctab, stab), tt=np.array([0, 1, 2], np.int32), dil=dil)
        o, lse = _band_attention(qkv.reshape(b, dil, s // dil, 3 * 2 * BLK), qmask, vmask)
        outs.append(o)
        lses.append(lse)
    x3 = _a_out(outs, lses, w_out.astype(BF16), x.reshape(b, s, D_MODEL), min(tm, 512))
    return x3.reshape(b * s, D_MODEL), None, None


def _mixer_b(x, b, s, gain, w_in, q_gain, k_gain, w_out, rows, cols_pos, tm, tq, tk):
    dims, slot = _block_dims(_F_AXIAL, _P_AXIAL)
    qcols = np.concatenate([g * BLK + slot * 64 + dims for g in range(B_KV_HEADS)])
    kcols = np.concatenate([B_HEADS * 64 + g * 64 + dims for g in range(B_KV_HEADS)])
    w_bf = w_in.astype(BF16)
    wqk = w_bf[:, np.concatenate([qcols, kcols])]
    wv = w_bf[:, (B_HEADS + B_KV_HEADS) * 64:]
    hgain = jnp.concatenate([jnp.tile(q_gain[dims] * (0.125 * LOG2E), B_KV_HEADS),
                             jnp.tile(k_gain[dims], B_KV_HEADS)]).reshape(1, -1).astype(F32)
    bd = jnp.asarray(slot[:, None] == slot[None, :], BF16)
    ctab, stab = _rope_tables([rows, cols_pos], AXIAL_THETA, 32, [1.0], 0)
    n = B_KV_HEADS * BLK
    qk = _proj(x, 0, D_MODEL, D_MODEL, gain, wqk, tn=2 * BLK, out_dtype=BF16, seq=s, tm=tm,
               tables=(ctab, stab), tt=np.zeros((n // BLK,), np.int32), headnorm=(hgain, bd))
    vt = _proj(x, 0, D_MODEL, D_MODEL, gain, wv, tn=B_KV_HEADS * 64, out_dtype=BF16, seq=s, tm=tm,
               vt=(B_KV_HEADS, 1, 64, tk))
    qk = qk.reshape(b, s, 2 * n)
    qmask = jnp.asarray(_slot_masks(4, slot), BF16)
    o = _flash(qk, qk, vt, qcol0=0, kcol0=B_KV_HEADS, ngroups=B_KV_HEADS, wq=BLK,
               wv=64 + _ones_rows(64), wo=BLK, nh=4, vgroups=((0, 64, 0, 4),), qmask=qmask,
               tq=tq, tk=tk)
    return x, o, w_out.astype(BF16)


def _mixer_c(x, b, s, gain, w_in, lq1, lk1, lq2, lk2, sub_gain, w_out, pos, lambda_init, tm, tq, tk):
    dims, slot = _block_dims(_F_ROT16, _P_ROT16)
    ngr = C_HEADS // 2
    cols = []
    for which in range(2):
        for g in range(ngr):
            cols.append(which * C_HEADS * 128 + g * BLK + slot * 64 + dims)
    w_bf = w_in.astype(BF16)
    ctab, stab = _rope_tables([pos], ROPE_THETA, 16, [0.125 * LOG2E, 1.0], 24)
    n = ngr * BLK
    qk = _proj(x, 0, D_MODEL, D_MODEL, gain, w_bf[:, np.concatenate(cols)], tn=2 * BLK, out_dtype=BF16,
               seq=s, tm=tm, tables=(ctab, stab), tt=np.array([0, 0, 1, 1], np.int32))
    qkv = qk.reshape(b, s, 2 * n)
    vt = _proj(x, 0, D_MODEL, D_MODEL, gain, w_bf[:, 2 * C_HEADS * 128:], tn=n, out_dtype=BF16,
               seq=s, tm=tm, vt=(ngr, 2, LANES, tk))
    qmask = jnp.asarray(_slot_masks(4, slot), BF16)
    hrows = LANES + _ones_rows(LANES)
    o = _flash(qkv, qkv, vt, qcol0=0, kcol0=ngr, ngroups=ngr, wq=BLK, wv=2 * hrows,
               wo=BLK, nh=4, vgroups=((0, LANES, 0, 2), (hrows, LANES, 2, 2)),
               qmask=qmask, tq=tq, tk=tk,
               diff=lambda_init, diff_params=(lq1, lk1, lq2, lk2, sub_gain))
    return x, o, w_out.astype(BF16)


def _mixer_d(x, b, s, gain, w_in, q_gain, kv_gain, w_uq, w_ukv, w_out, pos, tm, tq, tk):
    lane = np.arange(LANES)
    slot_h = np.where(lane < 32, 0, np.where(lane < 64, 1, np.where(lane < 80, 0, np.where(lane < 96, 1, -1))))
    slot = np.concatenate([slot_h, slot_h])
    nope_lane = lane < 64
    rope_lane = (lane >= 64) & (lane < 96)
    ngr = D_HEADS // 2

    w1 = jnp.zeros((D_MODEL, 4 * BLK), F32)
    w1 = w1.at[:, :D_Q_RANK].set(w_in[:, :D_Q_RANK])
    w1 = w1.at[:, 2 * BLK:3 * BLK].set(w_in[:, D_Q_RANK:D_Q_RANK + D_KV_RANK])
    kr_src = np.zeros((BLK,), np.int64)
    kr_on = np.zeros((BLK,), bool)
    for hf in range(2):
        for l in range(LANES):
            if rope_lane[l]:
                kr_src[hf * LANES + l] = D_Q_RANK + D_KV_RANK + hf * 16 + (l - 64) % 16
                kr_on[hf * LANES + l] = True
    w1 = w1.at[:, 3 * BLK:].set(jnp.where(jnp.asarray(kr_on)[None, :], w_in[:, kr_src], 0.0))
    cmb = _proj(x, 0, D_MODEL, D_MODEL, gain, w1.astype(BF16), tn=4 * BLK, out_dtype=F32, seq=s, tm=tm)

    qsrc = np.zeros((ngr * BLK,), np.int64)
    qon = np.zeros((ngr * BLK,), bool)
    ksrc = np.zeros((ngr * BLK,), np.int64)
    kon = np.zeros((ngr * BLK,), bool)
    for g in range(ngr):
        for hf in range(2):
            for l in range(LANES):
                idx = g * BLK + hf * LANES + l
                if slot_h[l] < 0:
                    continue
                head = 2 * g + slot_h[l]
                if nope_lane[l]:
                    d = hf * 32 + l % 32
                    qsrc[idx], qon[idx] = head * 96 + d, True
                    ksrc[idx], kon[idx] = head * 128 + d, True
                else:
                    d = hf * 16 + (l - 64) % 16
                    qsrc[idx], qon[idx] = head * 96 + D_NOPE + d, True
    wq2 = jnp.where(jnp.asarray(qon)[None, :], w_uq[:, qsrc], 0.0)
    wq2 = jnp.concatenate([wq2, jnp.zeros((2 * BLK - D_Q_RANK, ngr * BLK), F32)], axis=0).astype(BF16)
    wk2 = jnp.where(jnp.asarray(kon)[None, :], w_ukv[:, ksrc], 0.0).astype(BF16)
    vsrc = np.concatenate([h * 128 + D_NOPE + np.arange(64) for h in range(D_HEADS)])
    wv2 = w_ukv[:, vsrc].astype(BF16)
    qg = jnp.concatenate([q_gain, jnp.zeros((2 * BLK - D_Q_RANK,), F32)])

    half = D_ROPE // 2
    inv_freq = jnp.exp(jnp.arange(half, dtype=F32) * (-2.0 * math.log(ROPE_THETA) / D_ROPE))
    ang = pos.astype(F32)[:, None] * inv_freq[None, :]
    ones64 = jnp.ones((s, 64), F32)
    pad32 = jnp.ones((s, 32), F32)
    c = jnp.concatenate([ones64, jnp.cos(ang), jnp.cos(ang), pad32], axis=1)
    sn = jnp.concatenate([0.0 * ones64, jnp.sin(ang), jnp.sin(ang), 0.0 * pad32], axis=1)
    qs = (D_NOPE + D_ROPE) ** -0.5 * LOG2E
    ctab = jnp.stack([c * qs, c])
    stab = jnp.stack([sn * qs, sn])

    q = _proj(cmb, 0, 2 * BLK, D_Q_RANK, qg, wq2, tn=2 * BLK, out_dtype=BF16, seq=s, tm=tm,
              tables=(ctab, stab), tt=np.zeros((ngr // 2,), np.int32))
    k = _proj(cmb, 2, BLK, D_KV_RANK, kv_gain, wk2, tn=4 * BLK, out_dtype=BF16, seq=s, tm=tm,
              tables=(ctab, stab), tt=np.ones((ngr // 4,), np.int32), add=(cmb, 3))
    vt = _proj(cmb, 2, BLK, D_KV_RANK, kv_gain, wv2, tn=4 * BLK, out_dtype=BF16, seq=s, tm=tm,
               vt=(ngr, 2, 64, tk))
    q = q.reshape(b, s, ngr * BLK)
    k = k.reshape(b, s, ngr * BLK)
    qmask = jnp.asarray(_slot_masks(2, slot), BF16)
    hrows = 64 + _ones_rows(64)
    o = _flash(q, k, vt, qcol0=0, kcol0=0, ngroups=ngr, wq=BLK, wv=2 * hrows, wo=LANES,
               nh=2, vgroups=((0, 64, 0, 1), (hrows, 64, 1, 1)), qmask=qmask, tq=tq, tk=tk)
    return x, o, w_out.astype(BF16)


def _run_trunk(x3, mem3, p):
    b, s, _ = x3.shape
    t = b * s
    tm = 1024
    tm_mlp = 512
    tk = 512
    tq = 512 if s // tk > 8 else 1024
    tq_mla = 2 * tq
    x = x3.reshape(t, D_MODEL)
    pos = jnp.arange(s, dtype=F32)
    rows = jnp.repeat(jnp.arange(s // GRID_W, dtype=F32), GRID_W)
    cols_pos = jnp.tile(jnp.arange(GRID_W, dtype=F32), s // GRID_W)
    memf = mem3.reshape(b * N_MEM, D_MODEL)
    for i in range(DEPTH):
        m, j = i % 4, i // 4
        g = p['norm_mix'][i]
        if m == 0:
            x, o, w_out = _mixer_a(x, b, s, g, p['a_w_in'][j], p['a_w_out'][j], pos, tm)
        elif m == 1:
            x, o, w_out = _mixer_b(x, b, s, g, p['b_w_in'][j], p['b_q_norm'][j], p['b_k_norm'][j],
                                   p['b_w_out'][j], rows, cols_pos, tm, tq, tk)
        elif m == 2:
            x, o, w_out = _mixer_c(x, b, s, g, p['c_w_in'][j], p['c_lambda_q1'][j],
                                   p['c_lambda_k1'][j], p['c_lambda_q2'][j], p['c_lambda_k2'][j],
                                   p['c_sub_norm'][j], p['c_w_out'][j], pos,
                                   0.8 - 0.6 * math.exp(-0.3 * i), tm, tq, tk)
        else:
            x, o, w_out = _mixer_d(x, b, s, g, p['d_w_in'][j], p['d_q_norm'][j], p['d_kv_norm'][j],
                                   p['d_w_uq'][j], p['d_w_ukv'][j], p['d_w_out'][j], pos, tm,
                                   tq_mla, tk)
        kv = _proj(memf, 0, D_MODEL, D_MODEL, p['norm_mem'][i], p['w_xkv'][i].astype(BF16),
                   tn=2 * X_HEADS * X_HEAD_DIM, out_dtype=BF16, seq=N_MEM, tm=N_MEM)
        x = _post_mixer(x.reshape(b, s, D_MODEL), o, w_out, p['norm_x'][i],
                        p['w_xq'][i].astype(BF16), kv.reshape(b, N_MEM, 2 * X_HEADS * X_HEAD_DIM),
                        p['w_xo'][i].astype(BF16), p['norm_mlp'][i], p['w_mlp_in'][i].astype(BF16),
                        p['w_mlp_out'][i].astype(BF16), p['final_norm'], i == DEPTH - 1,
                        tm_mlp, 2048).reshape(t, D_MODEL)
    return x.reshape(b, s, D_MODEL)


def kernel(x_prompt, x_sample, mem_prompt, mem_sample, norm_mix, norm_x, norm_mem, w_xq, w_xkv, w_xo, norm_mlp, w_mlp_in, w_mlp_out, a_w_in, a_w_out, b_w_in, b_q_norm, b_k_norm, b_w_out, c_w_in, c_lambda_q1, c_lambda_k1, c_lambda_q2, c_lambda_k2, c_sub_norm, c_w_out, d_w_in, d_q_norm, d_kv_norm, d_w_uq, d_w_ukv, d_w_out, final_norm):
    p = dict(norm_mix=norm_mix, norm_x=norm_x, norm_mem=norm_mem, w_xq=w_xq, w_xkv=w_xkv,
             w_xo=w_xo, norm_mlp=norm_mlp, w_mlp_in=w_mlp_in, w_mlp_out=w_mlp_out,
             a_w_in=a_w_in, a_w_out=a_w_out, b_w_in=b_w_in, b_q_norm=b_q_norm,
             b_k_norm=b_k_norm, b_w_out=b_w_out, c_w_in=c_w_in, c_lambda_q1=c_lambda_q1,
             c_lambda_k1=c_lambda_k1, c_lambda_q2=c_lambda_q2, c_lambda_k2=c_lambda_k2,
             c_sub_norm=c_sub_norm, c_w_out=c_w_out, d_w_in=d_w_in, d_q_norm=d_q_norm,
             d_kv_norm=d_kv_norm, d_w_uq=d_w_uq, d_w_ukv=d_w_ukv, d_w_out=d_w_out,
             final_norm=final_norm)
    return (_run_trunk(x_prompt, mem_prompt, p), _run_trunk(x_sample, mem_sample, p))
```

```python
import functools
import math

import numpy as np
import jax
import jax.numpy as jnp
from jax import lax
from jax.experimental import pallas as pl
from jax.experimental.pallas import tpu as pltpu

F32 = jnp.float32
BF16 = jnp.bfloat16

D_MODEL = 1024
DEPTH = 4
N_MEM = 256
GRID_W = 64
D_FF = 4 * D_MODEL
NORM_EPS = 1e-6
ROPE_THETA = 500000.0
AXIAL_THETA = 10000.0
NEG_INF = -1e30

A_PATTERNS = ((128, 1), (512, 4), (2048, 16))
A_GROUPS = 3
A_HEADS = 8
A_IN = A_GROUPS * 3 * A_HEADS * 64
A_HALF_WINDOW = 64
B_HEADS = 16
B_KV_HEADS = 4
C_HEADS = 8
D_HEADS = 16
D_Q_RANK = 384
D_KV_RANK = 256
D_NOPE = 64
D_ROPE = 32
X_HEADS = 4
X_HEAD_DIM = 128

LOG2E = 1.4426950408889634
BF16_ROWS = 16
SCORE_SLOTS = 2
KEY_LOOP_TRIPS = 2
LANES = 128
BLK = 2 * LANES
VMEM_LIMIT = 56 * 1024 * 1024


def _ones_rows(nr):
    return LANES - nr if nr < LANES else BF16_ROWS


def _cparams(sem):
    return pltpu.CompilerParams(dimension_semantics=sem, vmem_limit_bytes=VMEM_LIMIT)


def _proj_kernel(tt_ref, src_ref, *rest, dnorm, rope, headnorm, add, nsub, vt, dil):
    del tt_ref
    rest = list(rest)
    c_ref = s_ref = hg_ref = bd_ref = add_ref = y_scr = None
    prenormed = dnorm is None
    g_ref = None if prenormed else rest.pop(0)
    w_ref = rest.pop(0)
    if dil > 1:
        y_scr = rest.pop()
    if rope:
        c_ref, s_ref = rest[0], rest[1]
        rest = rest[2:]
    if headnorm:
        hg_ref, bd_ref = rest[0], rest[1]
        rest = rest[2:]
    if add:
        add_ref = rest[0]
        rest = rest[1:]
    if prenormed:
        (o_ref,) = rest
        h_ref = src_ref
    else:
        o_ref, h_ref = rest

        @pl.when(pl.program_id(1) == 0)
        def _():
            xf = src_ref[...].astype(F32)
            ms = jnp.sum(xf * xf, axis=-1, keepdims=True) * (1.0 / dnorm)
            h_ref[...] = (xf * lax.rsqrt(ms + NORM_EPS) * g_ref[...]).astype(BF16)

    y = jnp.dot(h_ref[...], w_ref[...], preferred_element_type=F32)
    if add:
        ad = add_ref[...]
        y = y + (ad if nsub == 1 else jnp.concatenate([ad] * nsub, axis=1))
    if headnorm:
        y2 = y * y
        hi = y2.astype(BF16)
        lo = (y2 - hi.astype(F32)).astype(BF16)
        parts = []
        for n in range(nsub):
            sl = slice(n * BLK, (n + 1) * BLK)
            ss = (jnp.dot(hi[:, sl], bd_ref[...], preferred_element_type=F32)
                  + jnp.dot(lo[:, sl], bd_ref[...], preferred_element_type=F32))
            parts.append(y[:, sl] * lax.rsqrt(ss * (1.0 / 64.0) + NORM_EPS))
        y = (parts[0] if nsub == 1 else jnp.concatenate(parts, axis=1)) * hg_ref[...]
    if rope:
        c = c_ref[0]
        s = s_ref[0]
        for n in range(nsub):
            y1 = y[:, n * BLK:n * BLK + LANES]
            y2 = y[:, n * BLK + LANES:(n + 1) * BLK]
            r1 = y1 * c - y2 * s
            r2 = y2 * c + y1 * s
            if dil > 1:
                y_scr[2 * n] = r1
                y_scr[2 * n + 1] = r2
            else:
                o_ref[:, n * BLK:n * BLK + LANES] = r1.astype(o_ref.dtype)
                o_ref[:, n * BLK + LANES:(n + 1) * BLK] = r2.astype(o_ref.dtype)
        if dil > 1:
            rows = y.shape[0] // dil
            for r in range(dil):
                for cb in range(2 * nsub):
                    o_ref[r, :, cb * LANES:(cb + 1) * LANES] = (
                        y_scr[cb, pl.ds(r, rows, stride=dil), :].astype(o_ref.dtype))
    elif vt:
        ngroups, nvg, nr, tk = vt
        nones = _ones_rows(nr)
        ones = jnp.ones((nones, tk), o_ref.dtype)
        for cc in range(y.shape[0] // tk):
            yt = y[cc * tk:(cc + 1) * tk, :].T
            for g in range(ngroups):
                for vg in range(nvg):
                    src0 = (g * nvg + vg) * nr
                    r0 = vg * (nr + nones)
                    o_ref[g, cc, r0:r0 + nr, :] = yt[src0:src0 + nr, :].astype(o_ref.dtype)
                    o_ref[g, cc, r0 + nr:r0 + nr + nones, :] = ones
    else:
        o_ref[...] = y.astype(o_ref.dtype)


def _proj(src, src_cb, kdim, dnorm, gain, w, *, tn, out_dtype, seq, tm,
          tables=None, tt=None, headnorm=None, add=None, vt=None, dil=1):
    t = src.shape[0]
    n = w.shape[1]
    nj = n // tn
    out_shape = jax.ShapeDtypeStruct((t, n), out_dtype)
    out_spec = pl.BlockSpec((tm, tn), lambda i, j, tt_: (i, j))
    if vt is not None:
        ngroups, nvg, nr, tk = vt
        assert nj == 1 and n == ngroups * nvg * nr and tm % tk == 0
        wv = nvg * (nr + _ones_rows(nr))
        out_shape = jax.ShapeDtypeStruct((t // seq, ngroups, seq // tk, wv, tk), out_dtype)
        out_spec = pl.BlockSpec((None, ngroups, tm // tk, wv, tk),
                                lambda i, j, tt_: (i // (seq // tm), 0, i % (seq // tm), 0, 0))
    prenormed = dnorm is None
    scratch = [] if prenormed else [pltpu.VMEM((tm, kdim), BF16)]
    if dil > 1:
        assert tables is not None and tm % (16 * dil) == 0
        out_shape = jax.ShapeDtypeStruct((t // seq, dil, seq // dil, n), out_dtype)
        out_spec = pl.BlockSpec((None, dil, tm // dil, tn),
                                lambda i, j, tt_: (i // (seq // tm), 0, i % (seq // tm), j))
        scratch.append(pltpu.VMEM((tn // LANES, tm, LANES), F32))
    nsub = tn // BLK if (tables is not None or headnorm is not None) else 1
    ns = seq // tm
    if tt is None:
        tt = np.zeros((nj,), np.int32)
    in_specs = [pl.BlockSpec((tm, kdim), lambda i, j, tt_: (i, src_cb))]
    args = [src]
    if not prenormed:
        in_specs.append(pl.BlockSpec((1, kdim), lambda i, j, tt_: (0, 0)))
        args.append(gain.reshape(1, kdim).astype(F32))
    in_specs.append(pl.BlockSpec((kdim, tn), lambda i, j, tt_: (0, j)))
    args.append(w)
    if tables is not None:
        for tb in tables:
            in_specs.append(pl.BlockSpec((1, tm, LANES), lambda i, j, tt_: (tt_[j], i % ns, 0)))
            args.append(tb)
    if headnorm is not None:
        in_specs.append(pl.BlockSpec((1, tn), lambda i, j, tt_: (0, j)))
        in_specs.append(pl.BlockSpec((BLK, BLK), lambda i, j, tt_: (0, 0)))
        args += [headnorm[0], headnorm[1]]
    if add is not None:
        add_arr, add_cb = add
        in_specs.append(pl.BlockSpec((tm, BLK), lambda i, j, tt_: (i, add_cb)))
        args.append(add_arr)
    kern = functools.partial(_proj_kernel, dnorm=dnorm, rope=tables is not None,
                             headnorm=headnorm is not None, add=add is not None, nsub=nsub, vt=vt,
                             dil=dil)
    return pl.pallas_call(
        kern,
        out_shape=out_shape,
        grid_spec=pltpu.PrefetchScalarGridSpec(
            num_scalar_prefetch=1,
            grid=(t // tm, nj),
            in_specs=in_specs,
            out_specs=out_spec,
            scratch_shapes=scratch,
        ),
        compiler_params=_cparams(("parallel", "arbitrary")),
    )(jnp.asarray(tt, jnp.int32), *args)


def _flash_kernel(*refs, nh, vgroups, tq, tk, nk, diff):
    if diff:
        (qmask_ref, q_ref, k_ref, vt_ref, lq1, lk1, lq2, lk2, sg_ref,
         o_ref, qm_scr, s_scr, mx_scr, m_scr, acc_scr) = refs
        lambda_init = diff
    else:
        qmask_ref, q_ref, k_ref, vt_ref, o_ref, qm_scr, s_scr, mx_scr, m_scr, acc_scr = refs

    qt = q_ref[...].astype(F32).T
    for j in range(nh):
        rowmask = jnp.concatenate([qmask_ref[j]] * (tq // LANES), axis=1)
        qm_scr[:, j * tq:(j + 1) * tq] = (qt * rowmask).astype(BF16)
    m_scr[...] = jnp.full(m_scr.shape, NEG_INF, F32)
    acc_scr[...] = jnp.zeros(acc_scr.shape, F32)

    ucols = min(2 * BLK, vgroups[0][3] * tq)
    units = []
    for gi, (r0, nr, h0, hn) in enumerate(vgroups):
        for off in range(0, hn * tq, ucols):
            units.append((gi, r0, nr, h0 * tq + off, off))

    def scores(c, slot, col0):
        cols = slice(col0, col0 + ucols)
        ks = pl.multiple_of(c * tk, tk)
        st = jnp.dot(k_ref[pl.ds(ks, tk), :], qm_scr[:, cols], preferred_element_type=F32)
        s_scr[slot, :, cols] = st
        mx_scr[slot, :, cols] = jnp.max(st, axis=0, keepdims=True)

    def step(c, slot, c_next, slot_next):
        vtc = vt_ref[c]
        m_prev = m_scr[...]
        m_new = jnp.maximum(m_prev, mx_scr[slot])
        alpha = jnp.exp2(m_prev - m_new)
        m_scr[...] = m_new
        for gi, r0, nr, col0, off in units:
            cols = slice(col0, col0 + ucols)
            scores(c_next, slot_next, col0)
            pt = jnp.exp2(s_scr[slot, :, cols] - m_new[:, cols]).astype(BF16)
            acols = slice(off, off + ucols)
            acc_scr[gi, :, acols] = acc_scr[gi, :, acols] * alpha[:, cols] + jnp.dot(
                vtc[r0:r0 + nr + _ones_rows(nr), :], pt, preferred_element_type=F32)

    for unit in units:
        scores(0, 0, unit[3])

    per_trip = nk // KEY_LOOP_TRIPS

    def body(i, carry):
        c = per_trip * i
        for u in range(per_trip):
            step(c + u, u % SCORE_SLOTS, jnp.minimum(c + u + 1, nk - 1), (u + 1) % SCORE_SLOTS)
        return carry

    lax.fori_loop(0, KEY_LOOP_TRIPS, body, 0)

    pieces = []
    if diff:
        lam = (jnp.exp(jnp.sum(lq1[...] * lk1[...], axis=-1, keepdims=True))
               - jnp.exp(jnp.sum(lq2[...] * lk2[...], axis=-1, keepdims=True)) + lambda_init)
        for gi, (r0, nr, h0, hn) in enumerate(vgroups):
            acc = acc_scr[gi]
            linv = 1.0 / acc[nr:nr + 1, :]
            oh = acc[:nr, :tq] * linv[:, :tq] - lam * (acc[:nr, tq:] * linv[:, tq:])
            ms = jnp.mean(oh * oh, axis=0, keepdims=True)
            pieces.append(oh * lax.rsqrt(ms + NORM_EPS))
    else:
        for gi, (r0, nr, h0, hn) in enumerate(vgroups):
            acc = acc_scr[gi]
            on = acc[:nr, :] * (1.0 / acc[nr:nr + 1, :])
            for jj in range(hn):
                pieces.append(on[:, jj * tq:(jj + 1) * tq])
    ot = pieces[0] if len(pieces) == 1 else jnp.concatenate(pieces, axis=0)
    o = ot.T
    if diff:
        o = o * sg_ref[...] * (1.0 - lambda_init)
    o_ref[...] = o.astype(o_ref.dtype)


def _flash(q, k, vt, *, qcol0, kcol0, ngroups, wq, wv, wo, nh, vgroups, qmask, tq, tk,
           diff=None, diff_params=None):
    b, s = q.shape[0], q.shape[1]
    nk = s // tk
    assert s % tk == 0 and nk % (KEY_LOOP_TRIPS * SCORE_SLOTS) == 0 and s % tq == 0
    qmask = jnp.broadcast_to(qmask.astype(F32)[:, :, None], (nh, wq, LANES))
    in_specs = [
        pl.BlockSpec((nh, wq, LANES), lambda bi, g, i: (0, 0, 0)),
        pl.BlockSpec((None, tq, wq), lambda bi, g, i: (bi, i, qcol0 + g)),
        pl.BlockSpec((None, s, wq), lambda bi, g, i: (bi, 0, kcol0 + g)),
        pl.BlockSpec((None, None, nk, wv, tk), lambda bi, g, i: (bi, g, 0, 0, 0)),
    ]
    args = [qmask, q, k, vt]
    if diff is not None:
        for prm in diff_params[:4]:
            in_specs.append(pl.BlockSpec((1, 64), lambda bi, g, i: (0, 0)))
            args.append(prm.reshape(1, 64).astype(F32))
        in_specs.append(pl.BlockSpec((1, wo), lambda bi, g, i: (0, 0)))
        args.append(jnp.tile(diff_params[4].astype(F32), wo // LANES).reshape(1, wo))
    kern = functools.partial(_flash_kernel, nh=nh, vgroups=tuple(vgroups), tq=tq, tk=tk, nk=nk,
                             diff=diff)
    nr, hn = vgroups[0][1], vgroups[0][3]
    return pl.pallas_call(
        kern,
        out_shape=jax.ShapeDtypeStruct((b, s, ngroups * wo), BF16),
        grid=(b, ngroups, s // tq),
        in_specs=in_specs,
        out_specs=pl.BlockSpec((None, tq, wo), lambda bi, g, i: (bi, i, g)),
        scratch_shapes=[
            pltpu.VMEM((wq, nh * tq), BF16),
            pltpu.VMEM((SCORE_SLOTS, tk, nh * tq), F32),
            pltpu.VMEM((SCORE_SLOTS, 1, nh * tq), F32),
            pltpu.VMEM((1, nh * tq), F32),
            pltpu.VMEM((len(vgroups), nr + _ones_rows(nr), hn * tq), F32),
        ],
        compiler_params=_cparams(("parallel", "parallel", "arbitrary")),
    )(*args)


def _band_kernel(qmask_ref, vmask_ref, q_ref, k_ref, v_ref, o_ref, lse_ref, *, tq, win, length, ntile):
    nh = 4
    vmask = vmask_ref[...]
    vmask_b = vmask.astype(BF16)

    def tile(t, carry):
        r0 = pl.multiple_of(t * tq, tq)
        i = pl.program_id(2) * ntile + t
        ks = jnp.clip(i * tq - A_HALF_WINDOW, 0, length - win)
        ks = pl.multiple_of(ks, A_HALF_WINDOW)
        kc = k_ref[pl.ds(ks, win), :]
        vc = v_ref[pl.ds(ks, win), :]
        q = q_ref[pl.ds(r0, tq), :]
        qm = jnp.concatenate([q * qmask_ref[j:j + 1, :] for j in range(nh)], axis=0)
        s = lax.dot_general(qm, kc, (((1,), (1,)), ((), ())), preferred_element_type=F32)
        qpos = i * tq + lax.broadcasted_iota(jnp.int32, (tq, win), 0)
        kpos = ks + lax.broadcasted_iota(jnp.int32, (tq, win), 1)
        valid = jnp.abs(qpos - kpos) <= A_HALF_WINDOW
        ps = []
        inv = None
        lse = None
        for j in range(nh):
            sj = jnp.where(valid, s[j * tq:(j + 1) * tq], NEG_INF)
            mj = jnp.max(sj, axis=-1, keepdims=True)
            pj = jnp.exp2(sj - mj)
            lj = jnp.sum(pj, axis=-1, keepdims=True)
            ps.append(pj.astype(BF16))
            t_inv = (1.0 / lj) * vmask[j:j + 1, :]
            t_lse = ((mj + jnp.log2(lj)) * (1.0 / LOG2E)) * vmask[j:j + 1, :]
            inv = t_inv if inv is None else inv + t_inv
            lse = t_lse if lse is None else lse + t_lse
        lhs = jnp.concatenate(ps, axis=1)
        rhs = jnp.concatenate([vc * vmask_b[j:j + 1, :] for j in range(nh)], axis=0)
        pv = jnp.dot(lhs, rhs, preferred_element_type=F32)
        o_ref[pl.ds(r0, tq), :] = (pv * inv).astype(o_ref.dtype)
        lse_ref[pl.ds(r0, tq), :] = lse
        return carry

    lax.fori_loop(0, ntile, tile, 0)


def _band_attention(qkv, qmask, vmask):
    b, dil, length, _ = qkv.shape
    tq = min(256, length)
    win = min(tq + 2 * A_HALF_WINDOW, length)
    rows = min(4 * tq, length)

    def col(which):
        return lambda bi, a, i: (bi, a // 2, 0, which * 2 + a % 2)

    in_specs = [
        pl.BlockSpec((4, BLK), lambda bi, a, i: (0, 0)),
        pl.BlockSpec((4, BLK), lambda bi, a, i: (0, 0)),
        pl.BlockSpec((None, None, rows, BLK), lambda bi, a, i: (bi, a // 2, i, a % 2)),
        pl.BlockSpec((None, None, length, BLK), col(1)),
        pl.BlockSpec((None, None, length, BLK), col(2)),
    ]
    kern = functools.partial(_band_kernel, tq=tq, win=win, length=length, ntile=rows // tq)
    out_block = pl.BlockSpec((None, None, rows, BLK), lambda bi, a, i: (bi, a // 2, i, a % 2))
    return pl.pallas_call(
        kern,
        out_shape=(jax.ShapeDtypeStruct((b, dil, length, 2 * BLK), BF16),
                   jax.ShapeDtypeStruct((b, dil, length, 2 * BLK), F32)),
        grid=(b, dil * 2, length // rows),
        in_specs=in_specs,
        out_specs=(out_block, out_block),
        compiler_params=_cparams(("parallel", "parallel", "arbitrary")),
    )(qmask, vmask, qkv, qkv, qkv)


def _a_out_kernel(*refs, dils, tm):
    ng = len(dils)
    o_refs, l_refs = refs[:ng], refs[ng:2 * ng]
    w_ref, x_ref, out_ref = refs[2 * ng:2 * ng + 3]
    scr = list(refs[2 * ng + 3:])

    def token_order(ref, dil):
        if dil == 1:
            return ref[0].astype(F32)
        buf = scr.pop(0)
        ncb = buf.shape[0]
        for r in range(dil):
            v = ref[r].astype(F32)
            for cb in range(ncb):
                buf[cb, pl.ds(r, tm // dil, stride=dil), :] = v[:, cb * LANES:(cb + 1) * LANES]
        return jnp.concatenate([buf[cb] for cb in range(ncb)], axis=1)

    ls = [token_order(l_refs[g], dils[g]) for g in range(ng)]
    os_ = [token_order(o_refs[g], dils[g]) for g in range(ng)]
    mx = functools.reduce(jnp.maximum, ls)
    es = [jnp.exp(l - mx) for l in ls]
    inv = 1.0 / functools.reduce(jnp.add, es)
    o = functools.reduce(jnp.add, [e * og for e, og in zip(es, os_)]) * inv
    out_ref[...] = x_ref[...] + jnp.dot(o.astype(BF16), w_ref[...], preferred_element_type=F32)


def _a_out(os_, lses, w, x3, tm):
    b, s, _ = x3.shape
    kd = w.shape[0]
    dils = tuple(o.shape[1] for o in os_)
    grp = [pl.BlockSpec((None, d, tm // d, kd), lambda bi, i: (bi, 0, i, 0)) for d in dils]
    row = pl.BlockSpec((None, tm, D_MODEL), lambda bi, i: (bi, i, 0))
    nscr = 2 * sum(1 for d in dils if d > 1)
    return pl.pallas_call(
        functools.partial(_a_out_kernel, dils=dils, tm=tm),
        out_shape=jax.ShapeDtypeStruct((b, s, D_MODEL), F32),
        grid=(b, s // tm),
        in_specs=grp + grp + [pl.BlockSpec((kd, D_MODEL), lambda bi, i: (0, 0)), row],
        out_specs=row,
        scratch_shapes=[pltpu.VMEM((kd // LANES, tm, LANES), F32)] * nscr,
        compiler_params=_cparams(("parallel", "parallel")),
    )(*os_, *lses, w, x3)


def _post_kernel(*refs, has_proj, final):
    refs = list(refs)
    x2_scr, h_scr, acc_scr = refs[-3:]
    hn_ref = None if final else refs[-4]
    out_ref = refs[-4] if final else refs[-5]
    if has_proj:
        (x_ref, o_ref, wout_ref, gx_ref, wq_ref, kv_ref, wo_ref, gm_ref, w1_ref, w2_ref,
         fg_ref) = refs[:11]
    else:
        x_ref, gx_ref, wq_ref, kv_ref, wo_ref, gm_ref, w1_ref, w2_ref, fg_ref = refs[:9]
    f = pl.program_id(2)

    @pl.when(f == 0)
    def _():
        x1 = x_ref[...]
        if has_proj:
            x1 = x1 + jnp.dot(o_ref[...], wout_ref[...], preferred_element_type=F32)
        ms = jnp.mean(x1 * x1, axis=-1, keepdims=True)
        h = (x1 * lax.rsqrt(ms + NORM_EPS) * gx_ref[...]).astype(BF16)
        qb = (jnp.dot(h, wq_ref[...], preferred_element_type=F32)
              * (X_HEAD_DIM ** -0.5 * LOG2E)).astype(BF16)
        hd = X_HEADS * X_HEAD_DIM
        outs = []
        for hh in range(X_HEADS):
            qh = qb[:, hh * X_HEAD_DIM:(hh + 1) * X_HEAD_DIM]
            kh = kv_ref[:, hh * X_HEAD_DIM:(hh + 1) * X_HEAD_DIM]
            vh = kv_ref[:, hd + hh * X_HEAD_DIM:hd + (hh + 1) * X_HEAD_DIM]
            s = lax.dot_general(qh, kh, (((1,), (1,)), ((), ())), preferred_element_type=F32)
            m = jnp.max(s, axis=-1, keepdims=True)
            p = jnp.exp2(s - m)
            l = jnp.sum(p, axis=-1, keepdims=True)
            oh = jnp.dot(p.astype(BF16), vh, preferred_element_type=F32) * (1.0 / l)
            outs.append(oh.astype(BF16))
        x2 = x1 + jnp.dot(jnp.concatenate(outs, axis=1), wo_ref[...], preferred_element_type=F32)
        x2_scr[...] = x2
        ms2 = jnp.mean(x2 * x2, axis=-1, keepdims=True)
        h_scr[...] = (x2 * lax.rsqrt(ms2 + NORM_EPS) * gm_ref[...]).astype(BF16)
        acc_scr[...] = jnp.zeros(acc_scr.shape, F32)

    a = jnp.maximum(jnp.dot(h_scr[...], w1_ref[...], preferred_element_type=F32), 0.0)
    acc_scr[...] += jnp.dot((a * a).astype(BF16), w2_ref[...], preferred_element_type=F32)

    @pl.when(f == pl.num_programs(2) - 1)
    def _():
        y = x2_scr[...] + acc_scr[...]
        ms = jnp.mean(y * y, axis=-1, keepdims=True)
        yn = y * lax.rsqrt(ms + NORM_EPS) * fg_ref[...]
        if final:
            out_ref[...] = yn
        else:
            out_ref[...] = y
            hn_ref[...] = yn.astype(hn_ref.dtype)


def _post_mixer(x3, o3, w_out, gx, wq, kv3, wo, gm, w1, w2, next_gain, final, tm, tf):
    b, s, _ = x3.shape
    hd = X_HEADS * X_HEAD_DIM
    has_proj = o3 is not None
    const = lambda bi, i, f: (0, 0)
    row = lambda bi, i, f: (bi, i, 0)
    in_specs = [pl.BlockSpec((None, tm, D_MODEL), row)]
    args = [x3]
    if has_proj:
        kd = o3.shape[-1]
        in_specs += [pl.BlockSpec((None, tm, kd), row), pl.BlockSpec((kd, D_MODEL), const)]
        args += [o3, w_out]
    in_specs += [pl.BlockSpec((1, D_MODEL), const),
                 pl.BlockSpec((D_MODEL, hd), const),
                 pl.BlockSpec((None, N_MEM, 2 * hd), lambda bi, i, f: (bi, 0, 0)),
                 pl.BlockSpec((hd, D_MODEL), const),
                 pl.BlockSpec((1, D_MODEL), const),
                 pl.BlockSpec((D_MODEL, tf), lambda bi, i, f: (0, f)),
                 pl.BlockSpec((tf, D_MODEL), lambda bi, i, f: (f, 0)),
                 pl.BlockSpec((1, D_MODEL), const)]
    args += [gx.reshape(1, D_MODEL), wq, kv3, wo, gm.reshape(1, D_MODEL), w1, w2,
             next_gain.reshape(1, D_MODEL)]
    out_shape = jax.ShapeDtypeStruct((b, s, D_MODEL), F32)
    out_specs = pl.BlockSpec((None, tm, D_MODEL), row)
    if not final:
        out_shape = (out_shape, jax.ShapeDtypeStruct((b, s, D_MODEL), BF16))
        out_specs = (out_specs, pl.BlockSpec((None, tm, D_MODEL), row))
    return pl.pallas_call(
        functools.partial(_post_kernel, has_proj=has_proj, final=final),
        out_shape=out_shape,
        grid=(b, s // tm, D_FF // tf),
        in_specs=in_specs,
        out_specs=out_specs,
        scratch_shapes=[pltpu.VMEM((tm, D_MODEL), F32), pltpu.VMEM((tm, D_MODEL), BF16),
                        pltpu.VMEM((tm, D_MODEL), F32)],
        compiler_params=_cparams(("parallel", "parallel", "arbitrary")),
    )(*args)


_F_ROT16 = np.array(list(range(0, 8)) + list(range(16, 40)))
_P_ROT16 = np.array(list(range(8, 16)) + list(range(40, 64)))
_F_AXIAL = np.array(list(range(0, 16)) + list(range(32, 48)))
_P_AXIAL = np.array(list(range(16, 32)) + list(range(48, 64)))


def _block_dims(first, partner):
    lane = np.arange(BLK)
    half, slot, u = lane // LANES, (lane % LANES) // 32, lane % 32
    return np.where(half == 0, first[u], partner[u]), slot


def _slot_masks(nslot, slot_of_lane):
    return np.stack([(slot_of_lane == j) for j in range(nslot)]).astype(np.float32)


def _rope_tables(pos_list, theta, rot, scale_list, npad):
    half = rot // 2
    inv_freq = jnp.exp(jnp.arange(half, dtype=F32) * (-2.0 * math.log(theta) / rot))
    cs, ss = [], []
    for pos in pos_list:
        ang = pos.astype(F32)[:, None] * inv_freq[None, :]
        cs.append(jnp.cos(ang))
        ss.append(jnp.sin(ang))
    c = jnp.concatenate(cs, axis=1)
    s = jnp.concatenate(ss, axis=1)
    n = c.shape[0]
    if npad:
        c = jnp.concatenate([c, jnp.ones((n, npad), F32)], axis=1)
        s = jnp.concatenate([s, jnp.zeros((n, npad), F32)], axis=1)
    reps = LANES // c.shape[1]
    c = jnp.tile(c, (1, reps))
    s = jnp.tile(s, (1, reps))
    ctab = [c * sc for sc in scale_list] + [jnp.ones_like(c)]
    stab = [s * sc for sc in scale_list] + [jnp.zeros_like(s)]
    return jnp.stack(ctab), jnp.stack(stab)


def _mixer_a(x, b, s, gain, w_in, w_out, pos, tm):
    dims, slot = _block_dims(_F_ROT16, _P_ROT16)
    ctab, stab = _rope_tables([pos], ROPE_THETA, 16, [0.125 * LOG2E, 1.0], 24)
    qmask = jnp.asarray(_slot_masks(4, slot), BF16)
    vmask = jnp.asarray(_slot_masks(4, np.arange(BLK) // 64), F32)
    w_bf = w_in.astype(BF16)
    outs, lses = [], []
    for wg, (window, dil) in enumerate(A_PATTERNS):
        assert window // (2 * dil) == A_HALF_WINDOW
        cols = []
        for which in range(3):
            base = (wg * 3 + which) * A_HEADS * 64
            for hg in range(2):
                cols.append(base + hg * BLK + (slot * 64 + dims if which < 2 else np.arange(BLK)))
        w = w_bf[:, np.concatenate(cols)]
        qkv = _proj(x, 0, D_MODEL, D_MODEL, gain, w, tn=2 * BLK, out_dtype=BF16, seq=s, tm=tm,
                    tables=(ctab, stab), tt=np.array([0, 1, 2], np.int32), dil=dil)
        o, lse = _band_attention(qkv.reshape(b, dil, s // dil, 3 * 2 * BLK), qmask, vmask)
        outs.append(o)
        lses.append(lse)
    x3 = _a_out(outs, lses, w_out.astype(BF16), x.reshape(b, s, D_MODEL), min(tm, 512))
    return x3.reshape(b * s, D_MODEL), None, None


def _mixer_b(x, h, b, s, w_in, q_gain, k_gain, w_out, rows, cols_pos, tm, tq, tk):
    dims, slot = _block_dims(_F_AXIAL, _P_AXIAL)
    qcols = np.concatenate([g * BLK + slot * 64 + dims for g in range(B_KV_HEADS)])
    kcols = np.concatenate([B_HEADS * 64 + g * 64 + dims for g in range(B_KV_HEADS)])
    w_bf = w_in.astype(BF16)
    wqk = w_bf[:, np.concatenate([qcols, kcols])]
    wv = w_bf[:, (B_HEADS + B_KV_HEADS) * 64:]
    hgain = jnp.concatenate([jnp.tile(q_gain[dims] * (0.125 * LOG2E), B_KV_HEADS),
                             jnp.tile(k_gain[dims], B_KV_HEADS)]).reshape(1, -1).astype(F32)
    bd = jnp.asarray(slot[:, None] == slot[None, :], BF16)
    ctab, stab = _rope_tables([rows, cols_pos], AXIAL_THETA, 32, [1.0], 0)
    n = B_KV_HEADS * BLK
    qk = _proj(h, 0, D_MODEL, None, None, wqk, tn=2 * BLK, out_dtype=BF16, seq=s, tm=tm,
               tables=(ctab, stab), tt=np.zeros((n // BLK,), np.int32), headnorm=(hgain, bd))
    vt = _proj(h, 0, D_MODEL, None, None, wv, tn=B_KV_HEADS * 64, out_dtype=BF16, seq=s, tm=tm,
               vt=(B_KV_HEADS, 1, 64, tk))
    qk = qk.reshape(b, s, 2 * n)
    qmask = jnp.asarray(_slot_masks(4, slot), BF16)
    o = _flash(qk, qk, vt, qcol0=0, kcol0=B_KV_HEADS, ngroups=B_KV_HEADS, wq=BLK,
               wv=64 + _ones_rows(64), wo=BLK, nh=4, vgroups=((0, 64, 0, 4),), qmask=qmask,
               tq=tq, tk=tk)
    return x, o, w_out.astype(BF16)


def _mixer_c(x, h, b, s, w_in, lq1, lk1, lq2, lk2, sub_gain, w_out, pos, lambda_init, tm, tq, tk):
    dims, slot = _block_dims(_F_ROT16, _P_ROT16)
    ngr = C_HEADS // 2
    cols = []
    for which in range(2):
        for g in range(ngr):
            cols.append(which * C_HEADS * 128 + g * BLK + slot * 64 + dims)
    w_bf = w_in.astype(BF16)
    ctab, stab = _rope_tables([pos], ROPE_THETA, 16, [0.125 * LOG2E, 1.0], 24)
    n = ngr * BLK
    qk = _proj(h, 0, D_MODEL, None, None, w_bf[:, np.concatenate(cols)], tn=2 * BLK, out_dtype=BF16,
               seq=s, tm=tm, tables=(ctab, stab), tt=np.array([0, 0, 1, 1], np.int32))
    qkv = qk.reshape(b, s, 2 * n)
    vt = _proj(h, 0, D_MODEL, None, None, w_bf[:, 2 * C_HEADS * 128:], tn=n, out_dtype=BF16,
               seq=s, tm=tm, vt=(ngr, 2, LANES, tk))
    qmask = jnp.asarray(_slot_masks(4, slot), BF16)
    hrows = LANES + _ones_rows(LANES)
    o = _flash(qkv, qkv, vt, qcol0=0, kcol0=ngr, ngroups=ngr, wq=BLK, wv=2 * hrows,
               wo=BLK, nh=4, vgroups=((0, LANES, 0, 2), (hrows, LANES, 2, 2)),
               qmask=qmask, tq=tq, tk=tk,
               diff=lambda_init, diff_params=(lq1, lk1, lq2, lk2, sub_gain))
    return x, o, w_out.astype(BF16)


def _mixer_d(x, h, b, s, w_in, q_gain, kv_gain, w_uq, w_ukv, w_out, pos, tm, tq, tk):
    lane = np.arange(LANES)
    slot_h = np.where(lane < 32, 0, np.where(lane < 64, 1, np.where(lane < 80, 0, np.where(lane < 96, 1, -1))))
    slot = np.concatenate([slot_h, slot_h])
    nope_lane = lane < 64
    rope_lane = (lane >= 64) & (lane < 96)
    ngr = D_HEADS // 2

    w1 = jnp.zeros((D_MODEL, 4 * BLK), F32)
    w1 = w1.at[:, :D_Q_RANK].set(w_in[:, :D_Q_RANK])
    w1 = w1.at[:, 2 * BLK:3 * BLK].set(w_in[:, D_Q_RANK:D_Q_RANK + D_KV_RANK])
    kr_src = np.zeros((BLK,), np.int64)
    kr_on = np.zeros((BLK,), bool)
    for hf in range(2):
        for l in range(LANES):
            if rope_lane[l]:
                kr_src[hf * LANES + l] = D_Q_RANK + D_KV_RANK + hf * 16 + (l - 64) % 16
                kr_on[hf * LANES + l] = True
    w1 = w1.at[:, 3 * BLK:].set(jnp.where(jnp.asarray(kr_on)[None, :], w_in[:, kr_src], 0.0))
    cmb = _proj(h, 0, D_MODEL, None, None, w1.astype(BF16), tn=4 * BLK, out_dtype=F32, seq=s, tm=tm)

    qsrc = np.zeros((ngr * BLK,), np.int64)
    qon = np.zeros((ngr * BLK,), bool)
    ksrc = np.zeros((ngr * BLK,), np.int64)
    kon = np.zeros((ngr * BLK,), bool)
    for g in range(ngr):
        for hf in range(2):
            for l in range(LANES):
                idx = g * BLK + hf * LANES + l
                if slot_h[l] < 0:
                    continue
                head = 2 * g + slot_h[l]
                if nope_lane[l]:
                    d = hf * 32 + l % 32
                    qsrc[idx], qon[idx] = head * 96 + d, True
                    ksrc[idx], kon[idx] = head * 128 + d, True
                else:
                    d = hf * 16 + (l - 64) % 16
                    qsrc[idx], qon[idx] = head * 96 + D_NOPE + d, True
    wq2 = jnp.where(jnp.asarray(qon)[None, :], w_uq[:, qsrc], 0.0)
    wq2 = jnp.concatenate([wq2, jnp.zeros((2 * BLK - D_Q_RANK, ngr * BLK), F32)], axis=0).astype(BF16)
    wk2 = jnp.where(jnp.asarray(kon)[None, :], w_ukv[:, ksrc], 0.0).astype(BF16)
    vsrc = np.concatenate([h * 128 + D_NOPE + np.arange(64) for h in range(D_HEADS)])
    wv2 = w_ukv[:, vsrc].astype(BF16)
    qg = jnp.concatenate([q_gain, jnp.zeros((2 * BLK - D_Q_RANK,), F32)])

    half = D_ROPE // 2
    inv_freq = jnp.exp(jnp.arange(half, dtype=F32) * (-2.0 * math.log(ROPE_THETA) / D_ROPE))
    ang = pos.astype(F32)[:, None] * inv_freq[None, :]
    ones64 = jnp.ones((s, 64), F32)
    pad32 = jnp.ones((s, 32), F32)
    c = jnp.concatenate([ones64, jnp.cos(ang), jnp.cos(ang), pad32], axis=1)
    sn = jnp.concatenate([0.0 * ones64, jnp.sin(ang), jnp.sin(ang), 0.0 * pad32], axis=1)
    qs = (D_NOPE + D_ROPE) ** -0.5 * LOG2E
    ctab = jnp.stack([c * qs, c])
    stab = jnp.stack([sn * qs, sn])

    q = _proj(cmb, 0, 2 * BLK, D_Q_RANK, qg, wq2, tn=2 * BLK, out_dtype=BF16, seq=s, tm=tm,
              tables=(ctab, stab), tt=np.zeros((ngr // 2,), np.int32))
    k = _proj(cmb, 2, BLK, D_KV_RANK, kv_gain, wk2, tn=4 * BLK, out_dtype=BF16, seq=s, tm=tm,
              tables=(ctab, stab), tt=np.ones((ngr // 4,), np.int32), add=(cmb, 3))
    vt = _proj(cmb, 2, BLK, D_KV_RANK, kv_gain, wv2, tn=4 * BLK, out_dtype=BF16, seq=s, tm=tm,
               vt=(ngr, 2, 64, tk))
    q = q.reshape(b, s, ngr * BLK)
    k = k.reshape(b, s, ngr * BLK)
    qmask = jnp.asarray(_slot_masks(2, slot), BF16)
    hrows = 64 + _ones_rows(64)
    o = _flash(q, k, vt, qcol0=0, kcol0=0, ngroups=ngr, wq=BLK, wv=2 * hrows, wo=LANES,
               nh=2, vgroups=((0, 64, 0, 1), (hrows, 64, 1, 1)), qmask=qmask, tq=tq, tk=tk)
    return x, o, w_out.astype(BF16)


def _run_trunk(x3, mem3, p):
    b, s, _ = x3.shape
    t = b * s
    tm = 1024
    tm_mlp = 512
    tk = 512
    tq = 512 if s // tk > 8 else 1024
    tq_mla = 2 * tq
    x = x3.reshape(t, D_MODEL)
    pos = jnp.arange(s, dtype=F32)
    rows = jnp.repeat(jnp.arange(s // GRID_W, dtype=F32), GRID_W)
    cols_pos = jnp.tile(jnp.arange(GRID_W, dtype=F32), s // GRID_W)
    memf = mem3.reshape(b * N_MEM, D_MODEL)
    h = None
    for i in range(DEPTH):
        m, j = i % 4, i // 4
        if m == 0:
            x, o, w_out = _mixer_a(x, b, s, p['norm_mix'][i], p['a_w_in'][j], p['a_w_out'][j], pos, tm)
        elif m == 1:
            x, o, w_out = _mixer_b(x, h, b, s, p['b_w_in'][j], p['b_q_norm'][j], p['b_k_norm'][j],
                                   p['b_w_out'][j], rows, cols_pos, tm, tq, tk)
        elif m == 2:
            x, o, w_out = _mixer_c(x, h, b, s, p['c_w_in'][j], p['c_lambda_q1'][j],
                                   p['c_lambda_k1'][j], p['c_lambda_q2'][j], p['c_lambda_k2'][j],
                                   p['c_sub_norm'][j], p['c_w_out'][j], pos,
                                   0.8 - 0.6 * math.exp(-0.3 * i), tm, tq, tk)
        else:
            x, o, w_out = _mixer_d(x, h, b, s, p['d_w_in'][j], p['d_q_norm'][j], p['d_kv_norm'][j],
                                   p['d_w_uq'][j], p['d_w_ukv'][j], p['d_w_out'][j], pos, tm,
                                   tq_mla, tk)
        kv = _proj(memf, 0, D_MODEL, D_MODEL, p['norm_mem'][i], p['w_xkv'][i].astype(BF16),
                   tn=2 * X_HEADS * X_HEAD_DIM, out_dtype=BF16, seq=N_MEM, tm=N_MEM)
        final = i == DEPTH - 1
        res = _post_mixer(x.reshape(b, s, D_MODEL), o, w_out, p['norm_x'][i],
                          p['w_xq'][i].astype(BF16), kv.reshape(b, N_MEM, 2 * X_HEADS * X_HEAD_DIM),
                          p['w_xo'][i].astype(BF16), p['norm_mlp'][i], p['w_mlp_in'][i].astype(BF16),
                          p['w_mlp_out'][i].astype(BF16),
                          p['final_norm'] if final else p['norm_mix'][i + 1], final, tm_mlp, 2048)
        if final:
            x = res.reshape(t, D_MODEL)
        else:
            x, h = res[0].reshape(t, D_MODEL), res[1].reshape(t, D_MODEL)
    return x.reshape(b, s, D_MODEL)


def kernel(x_prompt, x_sample, mem_prompt, mem_sample, norm_mix, norm_x, norm_mem, w_xq, w_xkv, w_xo, norm_mlp, w_mlp_in, w_mlp_out, a_w_in, a_w_out, b_w_in, b_q_norm, b_k_norm, b_w_out, c_w_in, c_lambda_q1, c_lambda_k1, c_lambda_q2, c_lambda_k2, c_sub_norm, c_w_out, d_w_in, d_q_norm, d_kv_norm, d_w_uq, d_w_ukv, d_w_out, final_norm):
    p = dict(norm_mix=norm_mix, norm_x=norm_x, norm_mem=norm_mem, w_xq=w_xq, w_xkv=w_xkv,
             w_xo=w_xo, norm_mlp=norm_mlp, w_mlp_in=w_mlp_in, w_mlp_out=w_mlp_out,
             a_w_in=a_w_in, a_w_out=a_w_out, b_w_in=b_w_in, b_q_norm=b_q_norm,
             b_k_norm=b_k_norm, b_w_out=b_w_out, c_w_in=c_w_in, c_lambda_q1=c_lambda_q1,
             c_lambda_k1=c_lambda_k1, c_lambda_q2=c_lambda_q2, c_lambda_k2=c_lambda_k2,
             c_sub_norm=c_sub_norm, c_w_out=c_w_out, d_w_in=d_w_in, d_q_norm=d_q_norm,
             d_kv_norm=d_kv_norm, d_w_uq=d_w_uq, d_w_ukv=d_w_ukv, d_w_out=d_w_out,
             final_norm=final_norm)
    return (_run_trunk(x_prompt, mem_prompt, p), _run_trunk(x_sample, mem_sample, p))
```

```python
import functools
import math

import numpy as np
import jax
import jax.numpy as jnp
from jax import lax
from jax.experimental import pallas as pl
from jax.experimental.pallas import tpu as pltpu

F32 = jnp.float32
BF16 = jnp.bfloat16

D_MODEL = 1024
DEPTH = 4
N_MEM = 256
GRID_W = 64
D_FF = 4 * D_MODEL
NORM_EPS = 1e-6
ROPE_THETA = 500000.0
AXIAL_THETA = 10000.0
NEG_INF = -1e30

A_PATTERNS = ((128, 1), (512, 4), (2048, 16))
A_GROUPS = 3
A_HEADS = 8
A_HALF_WINDOW = 64
B_HEADS = 16
B_KV_HEADS = 4
C_HEADS = 8
D_HEADS = 16
D_Q_RANK = 384
D_KV_RANK = 256
D_NOPE = 64
D_ROPE = 32
X_HEADS = 4
X_HEAD_DIM = 128

LOG2E = 1.4426950408889634
BF16_ROWS = 16
SCORE_SLOTS = 2
KEY_LOOP_TRIPS = 2
LANES = 128
BLK = 2 * LANES
VMEM_LIMIT = 56 * 1024 * 1024


def _ones_rows(nr):
    return LANES - nr if nr < LANES else BF16_ROWS


def _cparams(sem):
    return pltpu.CompilerParams(dimension_semantics=sem, vmem_limit_bytes=VMEM_LIMIT)


def _proj_kernel(tt_ref, src_ref, *rest, dnorm, rope, headnorm, add, nsub, vt, dils, spg):
    del tt_ref
    rest = list(rest)
    c_ref = s_ref = hg_ref = bd_ref = add_ref = y_scr = None
    prenormed = dnorm is None
    g_ref = None if prenormed else rest.pop(0)
    w_ref = rest.pop(0)
    staged = dils != (1,)
    if staged:
        y_scr = rest.pop()
    if rope:
        c_ref, s_ref = rest[0], rest[1]
        rest = rest[2:]
    if headnorm:
        hg_ref, bd_ref = rest[0], rest[1]
        rest = rest[2:]
    if add:
        add_ref = rest[0]
        rest = rest[1:]
    o_refs = rest[:len(dils)]
    o_ref = o_refs[0]
    if prenormed:
        h_ref = src_ref
    else:
        h_ref = rest[len(dils)]

        @pl.when(pl.program_id(1) == 0)
        def _():
            xf = src_ref[...].astype(F32)
            ms = jnp.sum(xf * xf, axis=-1, keepdims=True) * (1.0 / dnorm)
            h_ref[...] = (xf * lax.rsqrt(ms + NORM_EPS) * g_ref[...]).astype(BF16)

    y = jnp.dot(h_ref[...], w_ref[...], preferred_element_type=F32)
    if add:
        ad = add_ref[...]
        y = y + (ad if nsub == 1 else jnp.concatenate([ad] * nsub, axis=1))
    if headnorm:
        y2 = y * y
        hi = y2.astype(BF16)
        lo = (y2 - hi.astype(F32)).astype(BF16)
        parts = []
        for n in range(nsub):
            sl = slice(n * BLK, (n + 1) * BLK)
            ss = (jnp.dot(hi[:, sl], bd_ref[...], preferred_element_type=F32)
                  + jnp.dot(lo[:, sl], bd_ref[...], preferred_element_type=F32))
            parts.append(y[:, sl] * lax.rsqrt(ss * (1.0 / 64.0) + NORM_EPS))
        y = (parts[0] if nsub == 1 else jnp.concatenate(parts, axis=1)) * hg_ref[...]
    if rope:
        c = c_ref[0]
        s = s_ref[0]
        for n in range(nsub):
            y1 = y[:, n * BLK:n * BLK + LANES]
            y2 = y[:, n * BLK + LANES:(n + 1) * BLK]
            r1 = y1 * c - y2 * s
            r2 = y2 * c + y1 * s
            if staged:
                y_scr[2 * n] = r1
                y_scr[2 * n + 1] = r2
            else:
                o_ref[:, n * BLK:n * BLK + LANES] = r1.astype(o_ref.dtype)
                o_ref[:, n * BLK + LANES:(n + 1) * BLK] = r2.astype(o_ref.dtype)
        if staged:
            def copy_out(out, dil):
                rows = y.shape[0] // dil
                for r in range(dil):
                    for cb in range(2 * nsub):
                        out[r, :, cb * LANES:(cb + 1) * LANES] = (
                            y_scr[cb, pl.ds(r, rows, stride=dil), :].astype(out.dtype))

            for gi, dil in enumerate(dils):
                pl.when(pl.program_id(1) // spg == gi)(functools.partial(copy_out, o_refs[gi], dil))
    elif vt:
        ngroups, nvg, nr, tk = vt
        nones = _ones_rows(nr)
        ones = jnp.ones((nones, tk), o_ref.dtype)
        for cc in range(y.shape[0] // tk):
            yt = y[cc * tk:(cc + 1) * tk, :].T
            for g in range(ngroups):
                for vg in range(nvg):
                    src0 = (g * nvg + vg) * nr
                    r0 = vg * (nr + nones)
                    o_ref[g, cc, r0:r0 + nr, :] = yt[src0:src0 + nr, :].astype(o_ref.dtype)
                    o_ref[g, cc, r0 + nr:r0 + nr + nones, :] = ones
    else:
        o_ref[...] = y.astype(o_ref.dtype)


def _proj(src, src_cb, kdim, dnorm, gain, w, *, tn, out_dtype, seq, tm,
          tables=None, tt=None, headnorm=None, add=None, vt=None, dils=(1,)):
    t = src.shape[0]
    n = w.shape[1]
    nj = n // tn
    dils = tuple(dils)
    out_shape = jax.ShapeDtypeStruct((t, n), out_dtype)
    out_spec = pl.BlockSpec((tm, tn), lambda i, j, tt_: (i, j))
    if vt is not None:
        ngroups, nvg, nr, tk = vt
        assert nj == 1 and n == ngroups * nvg * nr and tm % tk == 0
        wv = nvg * (nr + _ones_rows(nr))
        out_shape = jax.ShapeDtypeStruct((t // seq, ngroups, seq // tk, wv, tk), out_dtype)
        out_spec = pl.BlockSpec((None, ngroups, tm // tk, wv, tk),
                                lambda i, j, tt_: (i // (seq // tm), 0, i % (seq // tm), 0, 0))
    prenormed = dnorm is None
    scratch = [] if prenormed else [pltpu.VMEM((tm, kdim), BF16)]
    spg = nj // len(dils)
    if dils != (1,):
        assert tables is not None and all(tm % (BF16_ROWS * d) == 0 for d in dils)
        out_shape = tuple(jax.ShapeDtypeStruct((t // seq, d, seq // d, n // len(dils)), out_dtype)
                          for d in dils)
        out_spec = tuple(
            pl.BlockSpec((None, d, tm // d, tn),
                         lambda i, j, tt_, gi=gi: (i // (seq // tm), 0, i % (seq // tm),
                                                   jnp.clip(j - gi * spg, 0, spg - 1)))
            for gi, d in enumerate(dils))
        scratch.append(pltpu.VMEM((tn // LANES, tm, LANES), F32))
    nsub = tn // BLK if (tables is not None or headnorm is not None) else 1
    ns = seq // tm
    if tt is None:
        tt = np.zeros((nj,), np.int32)
    in_specs = [pl.BlockSpec((tm, kdim), lambda i, j, tt_: (i, src_cb))]
    args = [src]
    if not prenormed:
        in_specs.append(pl.BlockSpec((1, kdim), lambda i, j, tt_: (0, 0)))
        args.append(gain.reshape(1, kdim).astype(F32))
    in_specs.append(pl.BlockSpec((kdim, tn), lambda i, j, tt_: (0, j)))
    args.append(w)
    if tables is not None:
        for tb in tables:
            in_specs.append(pl.BlockSpec((1, tm, LANES), lambda i, j, tt_: (tt_[j], i % ns, 0)))
            args.append(tb)
    if headnorm is not None:
        in_specs.append(pl.BlockSpec((1, tn), lambda i, j, tt_: (0, j)))
        in_specs.append(pl.BlockSpec((BLK, BLK), lambda i, j, tt_: (0, 0)))
        args += [headnorm[0], headnorm[1]]
    if add is not None:
        add_arr, add_cb = add
        in_specs.append(pl.BlockSpec((tm, BLK), lambda i, j, tt_: (i, add_cb)))
        args.append(add_arr)
    kern = functools.partial(_proj_kernel, dnorm=dnorm, rope=tables is not None,
                             headnorm=headnorm is not None, add=add is not None, nsub=nsub, vt=vt,
                             dils=tuple(dils), spg=spg)
    return pl.pallas_call(
        kern,
        out_shape=out_shape,
        grid_spec=pltpu.PrefetchScalarGridSpec(
            num_scalar_prefetch=1,
            grid=(t // tm, nj),
            in_specs=in_specs,
            out_specs=out_spec,
            scratch_shapes=scratch,
        ),
        compiler_params=_cparams(("parallel", "arbitrary")),
    )(jnp.asarray(tt, jnp.int32), *args)


def _flash_kernel(*refs, nh, vgroups, tq, tk, nk, diff):
    if diff:
        (qmask_ref, q_ref, k_ref, vt_ref, lq1, lk1, lq2, lk2, sg_ref,
         o_ref, qm_scr, s_scr, mx_scr, m_scr, acc_scr) = refs
        lambda_init = diff
    else:
        qmask_ref, q_ref, k_ref, vt_ref, o_ref, qm_scr, s_scr, mx_scr, m_scr, acc_scr = refs

    qt = q_ref[...].astype(F32).T
    for j in range(nh):
        rowmask = jnp.concatenate([qmask_ref[j]] * (tq // LANES), axis=1)
        qm_scr[:, j * tq:(j + 1) * tq] = (qt * rowmask).astype(BF16)
    m_scr[...] = jnp.full(m_scr.shape, NEG_INF, F32)
    acc_scr[...] = jnp.zeros(acc_scr.shape, F32)

    ucols = min(2 * BLK, vgroups[0][3] * tq)
    units = []
    for gi, (r0, nr, h0, hn) in enumerate(vgroups):
        for off in range(0, hn * tq, ucols):
            units.append((gi, r0, nr, h0 * tq + off, off))

    def scores(c, slot, col0):
        cols = slice(col0, col0 + ucols)
        ks = pl.multiple_of(c * tk, tk)
        st = jnp.dot(k_ref[pl.ds(ks, tk), :], qm_scr[:, cols], preferred_element_type=F32)
        s_scr[slot, :, cols] = st
        mx_scr[slot, :, cols] = jnp.max(st, axis=0, keepdims=True)

    def step(c, slot, c_next, slot_next):
        vtc = vt_ref[c]
        m_prev = m_scr[...]
        m_new = jnp.maximum(m_prev, mx_scr[slot])
        alpha = jnp.exp2(m_prev - m_new)
        m_scr[...] = m_new
        for gi, r0, nr, col0, off in units:
            cols = slice(col0, col0 + ucols)
            scores(c_next, slot_next, col0)
            pt = jnp.exp2(s_scr[slot, :, cols] - m_new[:, cols]).astype(BF16)
            acols = slice(off, off + ucols)
            acc_scr[gi, :, acols] = acc_scr[gi, :, acols] * alpha[:, cols] + jnp.dot(
                vtc[r0:r0 + nr + _ones_rows(nr), :], pt, preferred_element_type=F32)

    for unit in units:
        scores(0, 0, unit[3])

    per_trip = nk // KEY_LOOP_TRIPS

    def body(i, carry):
        c = per_trip * i
        for u in range(per_trip):
            step(c + u, u % SCORE_SLOTS, jnp.minimum(c + u + 1, nk - 1), (u + 1) % SCORE_SLOTS)
        return carry

    lax.fori_loop(0, KEY_LOOP_TRIPS, body, 0)

    pieces = []
    if diff:
        lam = (jnp.exp(jnp.sum(lq1[...] * lk1[...], axis=-1, keepdims=True))
               - jnp.exp(jnp.sum(lq2[...] * lk2[...], axis=-1, keepdims=True)) + lambda_init)
        for gi, (r0, nr, h0, hn) in enumerate(vgroups):
            acc = acc_scr[gi]
            linv = 1.0 / acc[nr:nr + 1, :]
            oh = acc[:nr, :tq] * linv[:, :tq] - lam * (acc[:nr, tq:] * linv[:, tq:])
            ms = jnp.mean(oh * oh, axis=0, keepdims=True)
            pieces.append(oh * lax.rsqrt(ms + NORM_EPS))
    else:
        for gi, (r0, nr, h0, hn) in enumerate(vgroups):
            acc = acc_scr[gi]
            on = acc[:nr, :] * (1.0 / acc[nr:nr + 1, :])
            for jj in range(hn):
                pieces.append(on[:, jj * tq:(jj + 1) * tq])
    ot = pieces[0] if len(pieces) == 1 else jnp.concatenate(pieces, axis=0)
    o = ot.T
    if diff:
        o = o * sg_ref[...] * (1.0 - lambda_init)
    o_ref[...] = o.astype(o_ref.dtype)


def _flash(q, k, vt, *, qcol0, kcol0, ngroups, wq, wv, wo, nh, vgroups, qmask, tq, tk,
           diff=None, diff_params=None):
    b, s = q.shape[0], q.shape[1]
    nk = s // tk
    assert s % tk == 0 and nk % (KEY_LOOP_TRIPS * SCORE_SLOTS) == 0 and s % tq == 0
    qmask = jnp.broadcast_to(qmask.astype(F32)[:, :, None], (nh, wq, LANES))
    in_specs = [
        pl.BlockSpec((nh, wq, LANES), lambda bi, g, i: (0, 0, 0)),
        pl.BlockSpec((None, tq, wq), lambda bi, g, i: (bi, i, qcol0 + g)),
        pl.BlockSpec((None, s, wq), lambda bi, g, i: (bi, 0, kcol0 + g)),
        pl.BlockSpec((None, None, nk, wv, tk), lambda bi, g, i: (bi, g, 0, 0, 0)),
    ]
    args = [qmask, q, k, vt]
    if diff is not None:
        for prm in diff_params[:4]:
            in_specs.append(pl.BlockSpec((1, 64), lambda bi, g, i: (0, 0)))
            args.append(prm.reshape(1, 64).astype(F32))
        in_specs.append(pl.BlockSpec((1, wo), lambda bi, g, i: (0, 0)))
        args.append(jnp.tile(diff_params[4].astype(F32), wo // LANES).reshape(1, wo))
    kern = functools.partial(_flash_kernel, nh=nh, vgroups=tuple(vgroups), tq=tq, tk=tk, nk=nk,
                             diff=diff)
    nr, hn = vgroups[0][1], vgroups[0][3]
    return pl.pallas_call(
        kern,
        out_shape=jax.ShapeDtypeStruct((b, s, ngroups * wo), BF16),
        grid=(b, ngroups, s // tq),
        in_specs=in_specs,
        out_specs=pl.BlockSpec((None, tq, wo), lambda bi, g, i: (bi, i, g)),
        scratch_shapes=[
            pltpu.VMEM((wq, nh * tq), BF16),
            pltpu.VMEM((SCORE_SLOTS, tk, nh * tq), F32),
            pltpu.VMEM((SCORE_SLOTS, 1, nh * tq), F32),
            pltpu.VMEM((1, nh * tq), F32),
            pltpu.VMEM((len(vgroups), nr + _ones_rows(nr), hn * tq), F32),
        ],
        compiler_params=_cparams(("parallel", "parallel", "arbitrary")),
    )(*args)


def _band_kernel(qmask_ref, vmask_ref, q_ref, k_ref, v_ref, o_ref, lse_ref, *, tq, win, length, ntile):
    nh = 4
    vmask = vmask_ref[...]
    vmask_b = vmask.astype(BF16)

    def tile(t, carry):
        r0 = pl.multiple_of(t * tq, tq)
        i = pl.program_id(2) * ntile + t
        ks = jnp.clip(i * tq - A_HALF_WINDOW, 0, length - win)
        ks = pl.multiple_of(ks, A_HALF_WINDOW)
        kc = k_ref[pl.ds(ks, win), :]
        vc = v_ref[pl.ds(ks, win), :]
        q = q_ref[pl.ds(r0, tq), :]
        qm = jnp.concatenate([q * qmask_ref[j:j + 1, :] for j in range(nh)], axis=0)
        s = lax.dot_general(qm, kc, (((1,), (1,)), ((), ())), preferred_element_type=F32)
        qpos = i * tq + lax.broadcasted_iota(jnp.int32, (tq, win), 0)
        kpos = ks + lax.broadcasted_iota(jnp.int32, (tq, win), 1)
        valid = jnp.abs(qpos - kpos) <= A_HALF_WINDOW
        ps = []
        inv = None
        lse = None
        for j in range(nh):
            sj = jnp.where(valid, s[j * tq:(j + 1) * tq], NEG_INF)
            mj = jnp.max(sj, axis=-1, keepdims=True)
            pj = jnp.exp2(sj - mj)
            lj = jnp.sum(pj, axis=-1, keepdims=True)
            ps.append(pj.astype(BF16))
            t_inv = (1.0 / lj) * vmask[j:j + 1, :]
            t_lse = ((mj + jnp.log2(lj)) * (1.0 / LOG2E)) * vmask[j:j + 1, :]
            inv = t_inv if inv is None else inv + t_inv
            lse = t_lse if lse is None else lse + t_lse
        lhs = jnp.concatenate(ps, axis=1)
        rhs = jnp.concatenate([vc * vmask_b[j:j + 1, :] for j in range(nh)], axis=0)
        pv = jnp.dot(lhs, rhs, preferred_element_type=F32)
        o_ref[pl.ds(r0, tq), :] = (pv * inv).astype(o_ref.dtype)
        lse_ref[pl.ds(r0, tq), :] = lse
        return carry

    lax.fori_loop(0, ntile, tile, 0)


def _band_attention(qkv, qmask, vmask):
    b, dil, length, _ = qkv.shape
    tq = min(256, length)
    win = min(tq + 2 * A_HALF_WINDOW, length)
    rows = min(4 * tq, length)

    def col(which):
        return lambda bi, a, i: (bi, a // 2, 0, which * 2 + a % 2)

    in_specs = [
        pl.BlockSpec((4, BLK), lambda bi, a, i: (0, 0)),
        pl.BlockSpec((4, BLK), lambda bi, a, i: (0, 0)),
        pl.BlockSpec((None, None, rows, BLK), lambda bi, a, i: (bi, a // 2, i, a % 2)),
        pl.BlockSpec((None, None, length, BLK), col(1)),
        pl.BlockSpec((None, None, length, BLK), col(2)),
    ]
    kern = functools.partial(_band_kernel, tq=tq, win=win, length=length, ntile=rows // tq)
    out_block = pl.BlockSpec((None, None, rows, BLK), lambda bi, a, i: (bi, a // 2, i, a % 2))
    return pl.pallas_call(
        kern,
        out_shape=(jax.ShapeDtypeStruct((b, dil, length, 2 * BLK), BF16),
                   jax.ShapeDtypeStruct((b, dil, length, 2 * BLK), F32)),
        grid=(b, dil * 2, length // rows),
        in_specs=in_specs,
        out_specs=(out_block, out_block),
        compiler_params=_cparams(("parallel", "parallel", "arbitrary")),
    )(qmask, vmask, qkv, qkv, qkv)


def _a_out_kernel(*refs, dils, tm):
    ng = len(dils)
    o_refs, l_refs = refs[:ng], refs[ng:2 * ng]
    w_ref, x_ref, out_ref = refs[2 * ng:2 * ng + 3]
    scr = list(refs[2 * ng + 3:])

    def token_order(ref, dil):
        if dil == 1:
            return ref[0].astype(F32)
        buf = scr.pop(0)
        ncb = buf.shape[0]
        for r in range(dil):
            v = ref[r].astype(F32)
            for cb in range(ncb):
                buf[cb, pl.ds(r, tm // dil, stride=dil), :] = v[:, cb * LANES:(cb + 1) * LANES]
        return jnp.concatenate([buf[cb] for cb in range(ncb)], axis=1)

    ls = [token_order(l_refs[g], dils[g]) for g in range(ng)]
    os_ = [token_order(o_refs[g], dils[g]) for g in range(ng)]
    mx = functools.reduce(jnp.maximum, ls)
    es = [jnp.exp(l - mx) for l in ls]
    inv = 1.0 / functools.reduce(jnp.add, es)
    o = functools.reduce(jnp.add, [e * og for e, og in zip(es, os_)]) * inv
    out_ref[...] = x_ref[...] + jnp.dot(o.astype(BF16), w_ref[...], preferred_element_type=F32)


def _a_out(os_, lses, w, x3, tm):
    b, s, _ = x3.shape
    kd = w.shape[0]
    dils = tuple(o.shape[1] for o in os_)
    grp = [pl.BlockSpec((None, d, tm // d, kd), lambda bi, i: (bi, 0, i, 0)) for d in dils]
    row = pl.BlockSpec((None, tm, D_MODEL), lambda bi, i: (bi, i, 0))
    nscr = 2 * sum(1 for d in dils if d > 1)
    return pl.pallas_call(
        functools.partial(_a_out_kernel, dils=dils, tm=tm),
        out_shape=jax.ShapeDtypeStruct((b, s, D_MODEL), F32),
        grid=(b, s // tm),
        in_specs=grp + grp + [pl.BlockSpec((kd, D_MODEL), lambda bi, i: (0, 0)), row],
        out_specs=row,
        scratch_shapes=[pltpu.VMEM((kd // LANES, tm, LANES), F32)] * nscr,
        compiler_params=_cparams(("parallel", "parallel")),
    )(*os_, *lses, w, x3)


def _post_kernel(*refs, has_proj, final):
    refs = list(refs)
    x2_scr, h_scr, acc_scr = refs[-3:]
    hn_ref = None if final else refs[-4]
    out_ref = refs[-4] if final else refs[-5]
    if has_proj:
        (x_ref, o_ref, wout_ref, gx_ref, wq_ref, kv_ref, wo_ref, gm_ref, w1_ref, w2_ref,
         fg_ref) = refs[:11]
    else:
        x_ref, gx_ref, wq_ref, kv_ref, wo_ref, gm_ref, w1_ref, w2_ref, fg_ref = refs[:9]
    f = pl.program_id(2)

    @pl.when(f == 0)
    def _():
        x1 = x_ref[...]
        if has_proj:
            x1 = x1 + jnp.dot(o_ref[...], wout_ref[...], preferred_element_type=F32)
        ms = jnp.mean(x1 * x1, axis=-1, keepdims=True)
        h = (x1 * lax.rsqrt(ms + NORM_EPS) * gx_ref[...]).astype(BF16)
        qb = (jnp.dot(h, wq_ref[...], preferred_element_type=F32)
              * (X_HEAD_DIM ** -0.5 * LOG2E)).astype(BF16)
        hd = X_HEADS * X_HEAD_DIM
        outs = []
        for hh in range(X_HEADS):
            qh = qb[:, hh * X_HEAD_DIM:(hh + 1) * X_HEAD_DIM]
            kh = kv_ref[:, hh * X_HEAD_DIM:(hh + 1) * X_HEAD_DIM]
            vh = kv_ref[:, hd + hh * X_HEAD_DIM:hd + (hh + 1) * X_HEAD_DIM]
            s = lax.dot_general(qh, kh, (((1,), (1,)), ((), ())), preferred_element_type=F32)
            m = jnp.max(s, axis=-1, keepdims=True)
            p = jnp.exp2(s - m)
            l = jnp.sum(p, axis=-1, keepdims=True)
            oh = jnp.dot(p.astype(BF16), vh, preferred_element_type=F32) * (1.0 / l)
            outs.append(oh.astype(BF16))
        x2 = x1 + jnp.dot(jnp.concatenate(outs, axis=1), wo_ref[...], preferred_element_type=F32)
        x2_scr[...] = x2
        ms2 = jnp.mean(x2 * x2, axis=-1, keepdims=True)
        h_scr[...] = (x2 * lax.rsqrt(ms2 + NORM_EPS) * gm_ref[...]).astype(BF16)
        acc_scr[...] = jnp.zeros(acc_scr.shape, F32)

    a = jnp.maximum(jnp.dot(h_scr[...], w1_ref[...], preferred_element_type=F32), 0.0)
    acc_scr[...] += jnp.dot((a * a).astype(BF16), w2_ref[...], preferred_element_type=F32)

    @pl.when(f == pl.num_programs(2) - 1)
    def _():
        y = x2_scr[...] + acc_scr[...]
        ms = jnp.mean(y * y, axis=-1, keepdims=True)
        yn = y * lax.rsqrt(ms + NORM_EPS) * fg_ref[...]
        if final:
            out_ref[...] = yn
        else:
            out_ref[...] = y
            hn_ref[...] = yn.astype(hn_ref.dtype)


def _post_mixer(x3, o3, w_out, gx, wq, kv3, wo, gm, w1, w2, next_gain, final, tm, tf):
    b, s, _ = x3.shape
    hd = X_HEADS * X_HEAD_DIM
    has_proj = o3 is not None
    const = lambda bi, i, f: (0, 0)
    row = lambda bi, i, f: (bi, i, 0)
    in_specs = [pl.BlockSpec((None, tm, D_MODEL), row)]
    args = [x3]
    if has_proj:
        kd = o3.shape[-1]
        in_specs += [pl.BlockSpec((None, tm, kd), row), pl.BlockSpec((kd, D_MODEL), const)]
        args += [o3, w_out]
    in_specs += [pl.BlockSpec((1, D_MODEL), const),
                 pl.BlockSpec((D_MODEL, hd), const),
                 pl.BlockSpec((None, N_MEM, 2 * hd), lambda bi, i, f: (bi, 0, 0)),
                 pl.BlockSpec((hd, D_MODEL), const),
                 pl.BlockSpec((1, D_MODEL), const),
                 pl.BlockSpec((D_MODEL, tf), lambda bi, i, f: (0, f)),
                 pl.BlockSpec((tf, D_MODEL), lambda bi, i, f: (f, 0)),
                 pl.BlockSpec((1, D_MODEL), const)]
    args += [gx.reshape(1, D_MODEL), wq, kv3, wo, gm.reshape(1, D_MODEL), w1, w2,
             next_gain.reshape(1, D_MODEL)]
    out_shape = jax.ShapeDtypeStruct((b, s, D_MODEL), F32)
    out_specs = pl.BlockSpec((None, tm, D_MODEL), row)
    if not final:
        out_shape = (out_shape, jax.ShapeDtypeStruct((b, s, D_MODEL), BF16))
        out_specs = (out_specs, pl.BlockSpec((None, tm, D_MODEL), row))
    return pl.pallas_call(
        functools.partial(_post_kernel, has_proj=has_proj, final=final),
        out_shape=out_shape,
        grid=(b, s // tm, D_FF // tf),
        in_specs=in_specs,
        out_specs=out_specs,
        scratch_shapes=[pltpu.VMEM((tm, D_MODEL), F32), pltpu.VMEM((tm, D_MODEL), BF16),
                        pltpu.VMEM((tm, D_MODEL), F32)],
        compiler_params=_cparams(("parallel", "parallel", "arbitrary")),
    )(*args)


_F_ROT16 = np.array(list(range(0, 8)) + list(range(16, 40)))
_P_ROT16 = np.array(list(range(8, 16)) + list(range(40, 64)))
_F_AXIAL = np.array(list(range(0, 16)) + list(range(32, 48)))
_P_AXIAL = np.array(list(range(16, 32)) + list(range(48, 64)))


def _block_dims(first, partner):
    lane = np.arange(BLK)
    half, slot, u = lane // LANES, (lane % LANES) // 32, lane % 32
    return np.where(half == 0, first[u], partner[u]), slot


def _slot_masks(nslot, slot_of_lane):
    return np.stack([(slot_of_lane == j) for j in range(nslot)]).astype(np.float32)


def _rope_tables(pos_list, theta, rot, scale_list, npad):
    half = rot // 2
    inv_freq = jnp.exp(jnp.arange(half, dtype=F32) * (-2.0 * math.log(theta) / rot))
    cs, ss = [], []
    for pos in pos_list:
        ang = pos.astype(F32)[:, None] * inv_freq[None, :]
        cs.append(jnp.cos(ang))
        ss.append(jnp.sin(ang))
    c = jnp.concatenate(cs, axis=1)
    s = jnp.concatenate(ss, axis=1)
    n = c.shape[0]
    if npad:
        c = jnp.concatenate([c, jnp.ones((n, npad), F32)], axis=1)
        s = jnp.concatenate([s, jnp.zeros((n, npad), F32)], axis=1)
    reps = LANES // c.shape[1]
    c = jnp.tile(c, (1, reps))
    s = jnp.tile(s, (1, reps))
    ctab = [c * sc for sc in scale_list] + [jnp.ones_like(c)]
    stab = [s * sc for sc in scale_list] + [jnp.zeros_like(s)]
    return jnp.stack(ctab), jnp.stack(stab)


def _mixer_a(x, b, s, gain, w_in, w_out, pos, tm):
    dims, slot = _block_dims(_F_ROT16, _P_ROT16)
    ctab, stab = _rope_tables([pos], ROPE_THETA, 16, [0.125 * LOG2E, 1.0], 24)
    qmask = jnp.asarray(_slot_masks(4, slot), BF16)
    vmask = jnp.asarray(_slot_masks(4, np.arange(BLK) // 64), F32)
    cols = []
    for wg, (window, dil) in enumerate(A_PATTERNS):
        assert window // (2 * dil) == A_HALF_WINDOW
        for which in range(3):
            base = (wg * 3 + which) * A_HEADS * 64
            for hg in range(2):
                cols.append(base + hg * BLK + (slot * 64 + dims if which < 2 else np.arange(BLK)))
    w = w_in.astype(BF16)[:, np.concatenate(cols)]
    dils = tuple(dil for _, dil in A_PATTERNS)
    qkvs = _proj(x, 0, D_MODEL, D_MODEL, gain, w, tn=2 * BLK, out_dtype=BF16, seq=s, tm=tm,
                 tables=(ctab, stab), tt=np.array([0, 1, 2] * A_GROUPS, np.int32), dils=dils)
    outs, lses = [], []
    for qkv in qkvs:
        o, lse = _band_attention(qkv, qmask, vmask)
        outs.append(o)
        lses.append(lse)
    x3 = _a_out(outs, lses, w_out.astype(BF16), x.reshape(b, s, D_MODEL), min(tm, 512))
    return x3.reshape(b * s, D_MODEL), None, None


def _mixer_b(x, h, b, s, w_in, q_gain, k_gain, w_out, rows, cols_pos, tm, tq, tk):
    dims, slot = _block_dims(_F_AXIAL, _P_AXIAL)
    qcols = np.concatenate([g * BLK + slot * 64 + dims for g in range(B_KV_HEADS)])
    kcols = np.concatenate([B_HEADS * 64 + g * 64 + dims for g in range(B_KV_HEADS)])
    w_bf = w_in.astype(BF16)
    wqk = w_bf[:, np.concatenate([qcols, kcols])]
    wv = w_bf[:, (B_HEADS + B_KV_HEADS) * 64:]
    hgain = jnp.concatenate([jnp.tile(q_gain[dims] * (0.125 * LOG2E), B_KV_HEADS),
                             jnp.tile(k_gain[dims], B_KV_HEADS)]).reshape(1, -1).astype(F32)
    bd = jnp.asarray(slot[:, None] == slot[None, :], BF16)
    ctab, stab = _rope_tables([rows, cols_pos], AXIAL_THETA, 32, [1.0], 0)
    n = B_KV_HEADS * BLK
    qk = _proj(h, 0, D_MODEL, None, None, wqk, tn=2 * BLK, out_dtype=BF16, seq=s, tm=tm,
               tables=(ctab, stab), tt=np.zeros((n // BLK,), np.int32), headnorm=(hgain, bd))
    vt = _proj(h, 0, D_MODEL, None, None, wv, tn=B_KV_HEADS * 64, out_dtype=BF16, seq=s, tm=tm,
               vt=(B_KV_HEADS, 1, 64, tk))
    qk = qk.reshape(b, s, 2 * n)
    qmask = jnp.asarray(_slot_masks(4, slot), BF16)
    o = _flash(qk, qk, vt, qcol0=0, kcol0=B_KV_HEADS, ngroups=B_KV_HEADS, wq=BLK,
               wv=64 + _ones_rows(64), wo=BLK, nh=4, vgroups=((0, 64, 0, 4),), qmask=qmask,
               tq=tq, tk=tk)
    return x, o, w_out.astype(BF16)


def _mixer_c(x, h, b, s, w_in, lq1, lk1, lq2, lk2, sub_gain, w_out, pos, lambda_init, tm, tq, tk):
    dims, slot = _block_dims(_F_ROT16, _P_ROT16)
    ngr = C_HEADS // 2
    cols = []
    for which in range(2):
        for g in range(ngr):
            cols.append(which * C_HEADS * 128 + g * BLK + slot * 64 + dims)
    w_bf = w_in.astype(BF16)
    ctab, stab = _rope_tables([pos], ROPE_THETA, 16, [0.125 * LOG2E, 1.0], 24)
    n = ngr * BLK
    qk = _proj(h, 0, D_MODEL, None, None, w_bf[:, np.concatenate(cols)], tn=2 * BLK, out_dtype=BF16,
               seq=s, tm=tm, tables=(ctab, stab), tt=np.array([0, 0, 1, 1], np.int32))
    qkv = qk.reshape(b, s, 2 * n)
    vt = _proj(h, 0, D_MODEL, None, None, w_bf[:, 2 * C_HEADS * 128:], tn=n, out_dtype=BF16,
               seq=s, tm=tm, vt=(ngr, 2, LANES, tk))
    qmask = jnp.asarray(_slot_masks(4, slot), BF16)
    hrows = LANES + _ones_rows(LANES)
    o = _flash(qkv, qkv, vt, qcol0=0, kcol0=ngr, ngroups=ngr, wq=BLK, wv=2 * hrows,
               wo=BLK, nh=4, vgroups=((0, LANES, 0, 2), (hrows, LANES, 2, 2)),
               qmask=qmask, tq=tq, tk=tk,
               diff=lambda_init, diff_params=(lq1, lk1, lq2, lk2, sub_gain))
    return x, o, w_out.astype(BF16)


def _mixer_d(x, h, b, s, w_in, q_gain, kv_gain, w_uq, w_ukv, w_out, pos, tm, tq, tk):
    lane = np.arange(LANES)
    slot_h = np.where(lane < 32, 0, np.where(lane < 64, 1, np.where(lane < 80, 0, np.where(lane < 96, 1, -1))))
    slot = np.concatenate([slot_h, slot_h])
    nope_lane = lane < 64
    rope_lane = (lane >= 64) & (lane < 96)
    ngr = D_HEADS // 2

    w1 = jnp.zeros((D_MODEL, 4 * BLK), F32)
    w1 = w1.at[:, :D_Q_RANK].set(w_in[:, :D_Q_RANK])
    w1 = w1.at[:, 2 * BLK:3 * BLK].set(w_in[:, D_Q_RANK:D_Q_RANK + D_KV_RANK])
    kr_src = np.zeros((BLK,), np.int64)
    kr_on = np.zeros((BLK,), bool)
    for hf in range(2):
        for l in range(LANES):
            if rope_lane[l]:
                kr_src[hf * LANES + l] = D_Q_RANK + D_KV_RANK + hf * 16 + (l - 64) % 16
                kr_on[hf * LANES + l] = True
    w1 = w1.at[:, 3 * BLK:].set(jnp.where(jnp.asarray(kr_on)[None, :], w_in[:, kr_src], 0.0))
    cmb = _proj(h, 0, D_MODEL, None, None, w1.astype(BF16), tn=4 * BLK, out_dtype=F32, seq=s, tm=tm)

    qsrc = np.zeros((ngr * BLK,), np.int64)
    qon = np.zeros((ngr * BLK,), bool)
    ksrc = np.zeros((ngr * BLK,), np.int64)
    kon = np.zeros((ngr * BLK,), bool)
    for g in range(ngr):
        for hf in range(2):
            for l in range(LANES):
                idx = g * BLK + hf * LANES + l
                if slot_h[l] < 0:
                    continue
                head = 2 * g + slot_h[l]
                if nope_lane[l]:
                    d = hf * 32 + l % 32
                    qsrc[idx], qon[idx] = head * 96 + d, True
                    ksrc[idx], kon[idx] = head * 128 + d, True
                else:
                    d = hf * 16 + (l - 64) % 16
                    qsrc[idx], qon[idx] = head * 96 + D_NOPE + d, True
    wq2 = jnp.where(jnp.asarray(qon)[None, :], w_uq[:, qsrc], 0.0)
    wq2 = jnp.concatenate([wq2, jnp.zeros((2 * BLK - D_Q_RANK, ngr * BLK), F32)], axis=0).astype(BF16)
    wk2 = jnp.where(jnp.asarray(kon)[None, :], w_ukv[:, ksrc], 0.0).astype(BF16)
    vsrc = np.concatenate([h * 128 + D_NOPE + np.arange(64) for h in range(D_HEADS)])
    wv2 = w_ukv[:, vsrc].astype(BF16)
    qg = jnp.concatenate([q_gain, jnp.zeros((2 * BLK - D_Q_RANK,), F32)])

    half = D_ROPE // 2
    inv_freq = jnp.exp(jnp.arange(half, dtype=F32) * (-2.0 * math.log(ROPE_THETA) / D_ROPE))
    ang = pos.astype(F32)[:, None] * inv_freq[None, :]
    ones64 = jnp.ones((s, 64), F32)
    pad32 = jnp.ones((s, 32), F32)
    c = jnp.concatenate([ones64, jnp.cos(ang), jnp.cos(ang), pad32], axis=1)
    sn = jnp.concatenate([0.0 * ones64, jnp.sin(ang), jnp.sin(ang), 0.0 * pad32], axis=1)
    qs = (D_NOPE + D_ROPE) ** -0.5 * LOG2E
    ctab = jnp.stack([c * qs, c])
    stab = jnp.stack([sn * qs, sn])

    q = _proj(cmb, 0, 2 * BLK, D_Q_RANK, qg, wq2, tn=2 * BLK, out_dtype=BF16, seq=s, tm=tm,
              tables=(ctab, stab), tt=np.zeros((ngr // 2,), np.int32))
    k = _proj(cmb, 2, BLK, D_KV_RANK, kv_gain, wk2, tn=4 * BLK, out_dtype=BF16, seq=s, tm=tm,
              tables=(ctab, stab), tt=np.ones((ngr // 4,), np.int32), add=(cmb, 3))
    vt = _proj(cmb, 2, BLK, D_KV_RANK, kv_gain, wv2, tn=4 * BLK, out_dtype=BF16, seq=s, tm=tm,
               vt=(ngr, 2, 64, tk))
    q = q.reshape(b, s, ngr * BLK)
    k = k.reshape(b, s, ngr * BLK)
    qmask = jnp.asarray(_slot_masks(2, slot), BF16)
    hrows = 64 + _ones_rows(64)
    o = _flash(q, k, vt, qcol0=0, kcol0=0, ngroups=ngr, wq=BLK, wv=2 * hrows, wo=LANES,
               nh=2, vgroups=((0, 64, 0, 1), (hrows, 64, 1, 1)), qmask=qmask, tq=tq, tk=tk)
    return x, o, w_out.astype(BF16)


def _run_trunk(x3, mem3, p):
    b, s, _ = x3.shape
    t = b * s
    tm = 1024
    tm_mlp = 512
    tk = 512
    tq = 1024
    tq_mla = 2 * tq
    x = x3.reshape(t, D_MODEL)
    pos = jnp.arange(s, dtype=F32)
    rows = jnp.repeat(jnp.arange(s // GRID_W, dtype=F32), GRID_W)
    cols_pos = jnp.tile(jnp.arange(GRID_W, dtype=F32), s // GRID_W)
    memf = mem3.reshape(b * N_MEM, D_MODEL)
    h = None
    for i in range(DEPTH):
        m, j = i % 4, i // 4
        if m == 0:
            x, o, w_out = _mixer_a(x, b, s, p['norm_mix'][i], p['a_w_in'][j], p['a_w_out'][j], pos, tm)
        elif m == 1:
            x, o, w_out = _mixer_b(x, h, b, s, p['b_w_in'][j], p['b_q_norm'][j], p['b_k_norm'][j],
                                   p['b_w_out'][j], rows, cols_pos, tm, tq, tk)
        elif m == 2:
            x, o, w_out = _mixer_c(x, h, b, s, p['c_w_in'][j], p['c_lambda_q1'][j],
                                   p['c_lambda_k1'][j], p['c_lambda_q2'][j], p['c_lambda_k2'][j],
                                   p['c_sub_norm'][j], p['c_w_out'][j], pos,
                                   0.8 - 0.6 * math.exp(-0.3 * i), tm, tq, tk)
        else:
            x, o, w_out = _mixer_d(x, h, b, s, p['d_w_in'][j], p['d_q_norm'][j], p['d_kv_norm'][j],
                                   p['d_w_uq'][j], p['d_w_ukv'][j], p['d_w_out'][j], pos, tm,
                                   tq_mla, tk)
        kv = _proj(memf, 0, D_MODEL, D_MODEL, p['norm_mem'][i], p['w_xkv'][i].astype(BF16),
                   tn=2 * X_HEADS * X_HEAD_DIM, out_dtype=BF16, seq=N_MEM, tm=N_MEM)
        final = i == DEPTH - 1
        res = _post_mixer(x.reshape(b, s, D_MODEL), o, w_out, p['norm_x'][i],
                          p['w_xq'][i].astype(BF16), kv.reshape(b, N_MEM, 2 * X_HEADS * X_HEAD_DIM),
                          p['w_xo'][i].astype(BF16), p['norm_mlp'][i], p['w_mlp_in'][i].astype(BF16),
                          p['w_mlp_out'][i].astype(BF16),
                          p['final_norm'] if final else p['norm_mix'][i + 1], final, tm_mlp, 2048)
        if final:
            x = res.reshape(t, D_MODEL)
        else:
            x, h = res[0].reshape(t, D_MODEL), res[1].reshape(t, D_MODEL)
    return x.reshape(b, s, D_MODEL)


def kernel(x_prompt, x_sample, mem_prompt, mem_sample, norm_mix, norm_x, norm_mem, w_xq, w_xkv, w_xo, norm_mlp, w_mlp_in, w_mlp_out, a_w_in, a_w_out, b_w_in, b_q_norm, b_k_norm, b_w_out, c_w_in, c_lambda_q1, c_lambda_k1, c_lambda_q2, c_lambda_k2, c_sub_norm, c_w_out, d_w_in, d_q_norm, d_kv_norm, d_w_uq, d_w_ukv, d_w_out, final_norm):
    p = dict(norm_mix=norm_mix, norm_x=norm_x, norm_mem=norm_mem, w_xq=w_xq, w_xkv=w_xkv,
             w_xo=w_xo, norm_mlp=norm_mlp, w_mlp_in=w_mlp_in, w_mlp_out=w_mlp_out,
             a_w_in=a_w_in, a_w_out=a_w_out, b_w_in=b_w_in, b_q_norm=b_q_norm,
             b_k_norm=b_k_norm, b_w_out=b_w_out, c_w_in=c_w_in, c_lambda_q1=c_lambda_q1,
             c_lambda_k1=c_lambda_k1, c_lambda_q2=c_lambda_q2, c_lambda_k2=c_lambda_k2,
             c_sub_norm=c_sub_norm, c_w_out=c_w_out, d_w_in=d_w_in, d_q_norm=d_q_norm,
             d_kv_norm=d_kv_norm, d_w_uq=d_w_uq, d_w_ukv=d_w_ukv, d_w_out=d_w_out,
             final_norm=final_norm)
    return (_run_trunk(x_prompt, mem_prompt, p), _run_trunk(x_sample, mem_sample, p))
```

```python
import functools
import math

import numpy as np
import jax
import jax.numpy as jnp
from jax import lax
from jax.experimental import pallas as pl
from jax.experimental.pallas import tpu as pltpu

F32 = jnp.float32
BF16 = jnp.bfloat16

D_MODEL = 1024
DEPTH = 4
N_MEM = 256
GRID_W = 64
D_FF = 4 * D_MODEL
NORM_EPS = 1e-6
ROPE_THETA = 500000.0
AXIAL_THETA = 10000.0
NEG_INF = -1e30

A_PATTERNS = ((128, 1), (512, 4), (2048, 16))
A_GROUPS = 3
A_HEADS = 8
A_HALF_WINDOW = 64
B_HEADS = 16
B_KV_HEADS = 4
C_HEADS = 8
D_HEADS = 16
D_Q_RANK = 384
D_KV_RANK = 256
D_NOPE = 64
D_ROPE = 32
X_HEADS = 4
X_HEAD_DIM = 128

LOG2E = 1.4426950408889634
BF16_ROWS = 16
SCORE_SLOTS = 2
KEY_LOOP_TRIPS = 2
LANES = 128
BLK = 2 * LANES
VMEM_LIMIT = 56 * 1024 * 1024


def _ones_rows(nr):
    return LANES - nr if nr < LANES else BF16_ROWS


def _cparams(sem):
    return pltpu.CompilerParams(dimension_semantics=sem, vmem_limit_bytes=VMEM_LIMIT)


def _proj_kernel(tt_ref, src_ref, *rest, dnorm, rope, headnorm, add, nsub, vt, dils, spg):
    del tt_ref
    rest = list(rest)
    c_ref = s_ref = hg_ref = bd_ref = add_ref = y_scr = None
    prenormed = dnorm is None
    g_ref = None if prenormed else rest.pop(0)
    w_ref = rest.pop(0)
    staged = dils != (1,)
    if staged:
        y_scr = rest.pop()
    if rope:
        c_ref, s_ref = rest[0], rest[1]
        rest = rest[2:]
    if headnorm:
        hg_ref, bd_ref = rest[0], rest[1]
        rest = rest[2:]
    if add:
        add_ref = rest[0]
        rest = rest[1:]
    o_refs = rest[:len(dils)]
    o_ref = o_refs[0]
    if prenormed:
        h_ref = src_ref
    else:
        h_ref = rest[len(dils)]

        @pl.when(pl.program_id(1) == 0)
        def _():
            xf = src_ref[...].astype(F32)
            ms = jnp.sum(xf * xf, axis=-1, keepdims=True) * (1.0 / dnorm)
            h_ref[...] = (xf * lax.rsqrt(ms + NORM_EPS) * g_ref[...]).astype(BF16)

    y = jnp.dot(h_ref[...], w_ref[...], preferred_element_type=F32)
    if add:
        ad = add_ref[...]
        y = y + (ad if nsub == 1 else jnp.concatenate([ad] * nsub, axis=1))
    if headnorm:
        y2 = y * y
        hi = y2.astype(BF16)
        lo = (y2 - hi.astype(F32)).astype(BF16)
        parts = []
        for n in range(nsub):
            sl = slice(n * BLK, (n + 1) * BLK)
            ss = (jnp.dot(hi[:, sl], bd_ref[...], preferred_element_type=F32)
                  + jnp.dot(lo[:, sl], bd_ref[...], preferred_element_type=F32))
            parts.append(y[:, sl] * lax.rsqrt(ss * (1.0 / 64.0) + NORM_EPS))
        y = (parts[0] if nsub == 1 else jnp.concatenate(parts, axis=1)) * hg_ref[...]
    if rope:
        c = c_ref[0]
        s = s_ref[0]
        for n in range(nsub):
            y1 = y[:, n * BLK:n * BLK + LANES]
            y2 = y[:, n * BLK + LANES:(n + 1) * BLK]
            r1 = y1 * c - y2 * s
            r2 = y2 * c + y1 * s
            if staged:
                y_scr[2 * n] = r1
                y_scr[2 * n + 1] = r2
            else:
                o_ref[:, n * BLK:n * BLK + LANES] = r1.astype(o_ref.dtype)
                o_ref[:, n * BLK + LANES:(n + 1) * BLK] = r2.astype(o_ref.dtype)
        if staged:
            def copy_out(out, dil):
                rows = y.shape[0] // dil
                for r in range(dil):
                    for cb in range(2 * nsub):
                        out[r, :, cb * LANES:(cb + 1) * LANES] = (
                            y_scr[cb, pl.ds(r, rows, stride=dil), :].astype(out.dtype))

            for gi, dil in enumerate(dils):
                pl.when(pl.program_id(1) // spg == gi)(functools.partial(copy_out, o_refs[gi], dil))
    elif vt:
        ngroups, nvg, nr, tk = vt
        nones = _ones_rows(nr)
        ones = jnp.ones((nones, tk), o_ref.dtype)
        for cc in range(y.shape[0] // tk):
            yt = y[cc * tk:(cc + 1) * tk, :].T
            for g in range(ngroups):
                for vg in range(nvg):
                    src0 = (g * nvg + vg) * nr
                    r0 = vg * (nr + nones)
                    o_ref[g, cc, r0:r0 + nr, :] = yt[src0:src0 + nr, :].astype(o_ref.dtype)
                    o_ref[g, cc, r0 + nr:r0 + nr + nones, :] = ones
    else:
        o_ref[...] = y.astype(o_ref.dtype)


def _proj(src, src_cb, kdim, dnorm, gain, w, *, tn, out_dtype, seq, tm,
          tables=None, tt=None, headnorm=None, add=None, vt=None, dils=(1,)):
    t = src.shape[0]
    n = w.shape[1]
    nj = n // tn
    dils = tuple(dils)
    out_shape = jax.ShapeDtypeStruct((t, n), out_dtype)
    out_spec = pl.BlockSpec((tm, tn), lambda i, j, tt_: (i, j))
    if vt is not None:
        ngroups, nvg, nr, tk = vt
        assert nj == 1 and n == ngroups * nvg * nr and tm % tk == 0
        wv = nvg * (nr + _ones_rows(nr))
        out_shape = jax.ShapeDtypeStruct((t // seq, ngroups, seq // tk, wv, tk), out_dtype)
        out_spec = pl.BlockSpec((None, ngroups, tm // tk, wv, tk),
                                lambda i, j, tt_: (i // (seq // tm), 0, i % (seq // tm), 0, 0))
    prenormed = dnorm is None
    scratch = [] if prenormed else [pltpu.VMEM((tm, kdim), BF16)]
    spg = nj // len(dils)
    if dils != (1,):
        assert tables is not None and all(tm % (BF16_ROWS * d) == 0 for d in dils)
        out_shape = tuple(jax.ShapeDtypeStruct((t // seq, d, seq // d, n // len(dils)), out_dtype)
                          for d in dils)
        out_spec = tuple(
            pl.BlockSpec((None, d, tm // d, tn),
                         lambda i, j, tt_, gi=gi: (i // (seq // tm), 0, i % (seq // tm),
                                                   jnp.clip(j - gi * spg, 0, spg - 1)))
            for gi, d in enumerate(dils))
        scratch.append(pltpu.VMEM((tn // LANES, tm, LANES), F32))
    nsub = tn // BLK if (tables is not None or headnorm is not None) else 1
    ns = seq // tm
    if tt is None:
        tt = np.zeros((nj,), np.int32)
    in_specs = [pl.BlockSpec((tm, kdim), lambda i, j, tt_: (i, src_cb))]
    args = [src]
    if not prenormed:
        in_specs.append(pl.BlockSpec((1, kdim), lambda i, j, tt_: (0, 0)))
        args.append(gain.reshape(1, kdim).astype(F32))
    in_specs.append(pl.BlockSpec((kdim, tn), lambda i, j, tt_: (0, j)))
    args.append(w)
    if tables is not None:
        for tb in tables:
            in_specs.append(pl.BlockSpec((1, tm, LANES), lambda i, j, tt_: (tt_[j], i % ns, 0)))
            args.append(tb)
    if headnorm is not None:
        in_specs.append(pl.BlockSpec((1, tn), lambda i, j, tt_: (0, j)))
        in_specs.append(pl.BlockSpec((BLK, BLK), lambda i, j, tt_: (0, 0)))
        args += [headnorm[0], headnorm[1]]
    if add is not None:
        add_arr, add_cb = add
        in_specs.append(pl.BlockSpec((tm, BLK), lambda i, j, tt_: (i, add_cb)))
        args.append(add_arr)
    kern = functools.partial(_proj_kernel, dnorm=dnorm, rope=tables is not None,
                             headnorm=headnorm is not None, add=add is not None, nsub=nsub, vt=vt,
                             dils=tuple(dils), spg=spg)
    return pl.pallas_call(
        kern,
        out_shape=out_shape,
        grid_spec=pltpu.PrefetchScalarGridSpec(
            num_scalar_prefetch=1,
            grid=(t // tm, nj),
            in_specs=in_specs,
            out_specs=out_spec,
            scratch_shapes=scratch,
        ),
        compiler_params=_cparams(("parallel", "arbitrary")),
    )(jnp.asarray(tt, jnp.int32), *args)


def _flash_kernel(*refs, nh, vgroups, tq, tk, nk, diff):
    if diff:
        (qmask_ref, q_ref, k_ref, vt_ref, lq1, lk1, lq2, lk2, sg_ref,
         o_ref, qm_scr, s_scr, mx_scr, m_scr, acc_scr) = refs
        lambda_init = diff
    else:
        qmask_ref, q_ref, k_ref, vt_ref, o_ref, qm_scr, s_scr, mx_scr, m_scr, acc_scr = refs

    qt = q_ref[...].astype(F32).T
    for j in range(nh):
        rowmask = jnp.concatenate([qmask_ref[j]] * (tq // LANES), axis=1)
        qm_scr[:, j * tq:(j + 1) * tq] = (qt * rowmask).astype(BF16)
    m_scr[...] = jnp.full(m_scr.shape, NEG_INF, F32)
    acc_scr[...] = jnp.zeros(acc_scr.shape, F32)

    ucols = min(2 * BLK, vgroups[0][3] * tq)
    units = []
    for gi, (r0, nr, h0, hn) in enumerate(vgroups):
        for off in range(0, hn * tq, ucols):
            units.append((gi, r0, nr, h0 * tq + off, off))

    def scores(c, slot, col0):
        cols = slice(col0, col0 + ucols)
        ks = pl.multiple_of(c * tk, tk)
        st = jnp.dot(k_ref[pl.ds(ks, tk), :], qm_scr[:, cols], preferred_element_type=F32)
        s_scr[slot, :, cols] = st
        mx_scr[slot, :, cols] = jnp.max(st, axis=0, keepdims=True)

    def step(c, slot, c_next, slot_next):
        vtc = vt_ref[c]
        m_prev = m_scr[...]
        m_new = jnp.maximum(m_prev, mx_scr[slot])
        alpha = jnp.exp2(m_prev - m_new)
        m_scr[...] = m_new
        for gi, r0, nr, col0, off in units:
            cols = slice(col0, col0 + ucols)
            scores(c_next, slot_next, col0)
            pt = jnp.exp2(s_scr[slot, :, cols] - m_new[:, cols]).astype(BF16)
            acols = slice(off, off + ucols)
            acc_scr[gi, :, acols] = acc_scr[gi, :, acols] * alpha[:, cols] + jnp.dot(
                vtc[r0:r0 + nr + _ones_rows(nr), :], pt, preferred_element_type=F32)

    for unit in units:
        scores(0, 0, unit[3])

    per_trip = nk // KEY_LOOP_TRIPS

    def body(i, carry):
        c = per_trip * i
        for u in range(per_trip):
            step(c + u, u % SCORE_SLOTS, jnp.minimum(c + u + 1, nk - 1), (u + 1) % SCORE_SLOTS)
        return carry

    lax.fori_loop(0, KEY_LOOP_TRIPS, body, 0)

    pieces = []
    if diff:
        lam = (jnp.exp(jnp.sum(lq1[...] * lk1[...], axis=-1, keepdims=True))
               - jnp.exp(jnp.sum(lq2[...] * lk2[...], axis=-1, keepdims=True)) + lambda_init)
        for gi, (r0, nr, h0, hn) in enumerate(vgroups):
            acc = acc_scr[gi]
            linv = 1.0 / acc[nr:nr + 1, :]
            oh = acc[:nr, :tq] * linv[:, :tq] - lam * (acc[:nr, tq:] * linv[:, tq:])
            ms = jnp.mean(oh * oh, axis=0, keepdims=True)
            pieces.append(oh * lax.rsqrt(ms + NORM_EPS))
    else:
        for gi, (r0, nr, h0, hn) in enumerate(vgroups):
            acc = acc_scr[gi]
            on = acc[:nr, :] * (1.0 / acc[nr:nr + 1, :])
            for jj in range(hn):
                pieces.append(on[:, jj * tq:(jj + 1) * tq])
    ot = pieces[0] if len(pieces) == 1 else jnp.concatenate(pieces, axis=0)
    o = ot.T
    if diff:
        o = o * sg_ref[...] * (1.0 - lambda_init)
    o_ref[...] = o.astype(o_ref.dtype)


def _flash(q, k, vt, *, qcol0, kcol0, ngroups, wq, wv, wo, nh, vgroups, qmask, tq, tk,
           diff=None, diff_params=None):
    b, s = q.shape[0], q.shape[1]
    nk = s // tk
    assert s % tk == 0 and nk % (KEY_LOOP_TRIPS * SCORE_SLOTS) == 0 and s % tq == 0
    qmask = jnp.broadcast_to(qmask.astype(F32)[:, :, None], (nh, wq, LANES))
    in_specs = [
        pl.BlockSpec((nh, wq, LANES), lambda bi, g, i: (0, 0, 0)),
        pl.BlockSpec((None, tq, wq), lambda bi, g, i: (bi, i, qcol0 + g)),
        pl.BlockSpec((None, s, wq), lambda bi, g, i: (bi, 0, kcol0 + g)),
        pl.BlockSpec((None, None, nk, wv, tk), lambda bi, g, i: (bi, g, 0, 0, 0)),
    ]
    args = [qmask, q, k, vt]
    if diff is not None:
        for prm in diff_params[:4]:
            in_specs.append(pl.BlockSpec((1, 64), lambda bi, g, i: (0, 0)))
            args.append(prm.reshape(1, 64).astype(F32))
        in_specs.append(pl.BlockSpec((1, wo), lambda bi, g, i: (0, 0)))
        args.append(jnp.tile(diff_params[4].astype(F32), wo // LANES).reshape(1, wo))
    kern = functools.partial(_flash_kernel, nh=nh, vgroups=tuple(vgroups), tq=tq, tk=tk, nk=nk,
                             diff=diff)
    nr, hn = vgroups[0][1], vgroups[0][3]
    return pl.pallas_call(
        kern,
        out_shape=jax.ShapeDtypeStruct((b, s, ngroups * wo), BF16),
        grid=(b, ngroups, s // tq),
        in_specs=in_specs,
        out_specs=pl.BlockSpec((None, tq, wo), lambda bi, g, i: (bi, i, g)),
        scratch_shapes=[
            pltpu.VMEM((wq, nh * tq), BF16),
            pltpu.VMEM((SCORE_SLOTS, tk, nh * tq), F32),
            pltpu.VMEM((SCORE_SLOTS, 1, nh * tq), F32),
            pltpu.VMEM((1, nh * tq), F32),
            pltpu.VMEM((len(vgroups), nr + _ones_rows(nr), hn * tq), F32),
        ],
        compiler_params=_cparams(("parallel", "parallel", "arbitrary")),
    )(*args)


def _band_kernel(qmask_ref, vmask_ref, q_ref, k_ref, v_ref, o_ref, lse_ref, *, tq, win, length, ntile):
    nh = 4
    vmask = vmask_ref[...]
    vmask_b = vmask.astype(BF16)

    def tile(t, carry):
        r0 = pl.multiple_of(t * tq, tq)
        i = pl.program_id(2) * ntile + t
        ks = jnp.clip(i * tq - A_HALF_WINDOW, 0, length - win)
        ks = pl.multiple_of(ks, A_HALF_WINDOW)
        kc = k_ref[pl.ds(ks, win), :]
        vc = v_ref[pl.ds(ks, win), :]
        q = q_ref[pl.ds(r0, tq), :]
        qm = jnp.concatenate([q * qmask_ref[j:j + 1, :] for j in range(nh)], axis=0)
        s = lax.dot_general(qm, kc, (((1,), (1,)), ((), ())), preferred_element_type=F32)
        qpos = i * tq + lax.broadcasted_iota(jnp.int32, (tq, win), 0)
        kpos = ks + lax.broadcasted_iota(jnp.int32, (tq, win), 1)
        valid = jnp.abs(qpos - kpos) <= A_HALF_WINDOW
        ps = []
        inv = None
        lse = None
        for j in range(nh):
            sj = jnp.where(valid, s[j * tq:(j + 1) * tq], NEG_INF)
            mj = jnp.max(sj, axis=-1, keepdims=True)
            pj = jnp.exp2(sj - mj)
            lj = jnp.sum(pj, axis=-1, keepdims=True)
            ps.append(pj.astype(BF16))
            t_inv = (1.0 / lj) * vmask[j:j + 1, :]
            t_lse = ((mj + jnp.log2(lj)) * (1.0 / LOG2E)) * vmask[j:j + 1, :]
            inv = t_inv if inv is None else inv + t_inv
            lse = t_lse if lse is None else lse + t_lse
        lhs = jnp.concatenate(ps, axis=1)
        rhs = jnp.concatenate([vc * vmask_b[j:j + 1, :] for j in range(nh)], axis=0)
        pv = jnp.dot(lhs, rhs, preferred_element_type=F32)
        o_ref[pl.ds(r0, tq), :] = (pv * inv).astype(o_ref.dtype)
        lse_ref[pl.ds(r0, tq), :] = lse
        return carry

    lax.fori_loop(0, ntile, tile, 0)


def _band_attention(qkv, qmask, vmask):
    b, dil, length, _ = qkv.shape
    tq = min(256, length)
    win = min(tq + 2 * A_HALF_WINDOW, length)
    rows = min(4 * tq, length)

    def col(which):
        return lambda bi, a, i: (bi, a // 2, 0, which * 2 + a % 2)

    in_specs = [
        pl.BlockSpec((4, BLK), lambda bi, a, i: (0, 0)),
        pl.BlockSpec((4, BLK), lambda bi, a, i: (0, 0)),
        pl.BlockSpec((None, None, rows, BLK), lambda bi, a, i: (bi, a // 2, i, a % 2)),
        pl.BlockSpec((None, None, length, BLK), col(1)),
        pl.BlockSpec((None, None, length, BLK), col(2)),
    ]
    kern = functools.partial(_band_kernel, tq=tq, win=win, length=length, ntile=rows // tq)
    out_block = pl.BlockSpec((None, None, rows, BLK), lambda bi, a, i: (bi, a // 2, i, a % 2))
    return pl.pallas_call(
        kern,
        out_shape=(jax.ShapeDtypeStruct((b, dil, length, 2 * BLK), BF16),
                   jax.ShapeDtypeStruct((b, dil, length, 2 * BLK), F32)),
        grid=(b, dil * 2, length // rows),
        in_specs=in_specs,
        out_specs=(out_block, out_block),
        compiler_params=_cparams(("parallel", "parallel", "arbitrary")),
    )(qmask, vmask, qkv, qkv, qkv)


def _a_out_kernel(*refs, dils, tm):
    ng = len(dils)
    o_refs, l_refs = refs[:ng], refs[ng:2 * ng]
    w_ref, x_ref, out_ref = refs[2 * ng:2 * ng + 3]
    scr = list(refs[2 * ng + 3:])

    def token_order(ref, dil):
        if dil == 1:
            return ref[0].astype(F32)
        buf = scr.pop(0)
        ncb = buf.shape[0]
        for r in range(dil):
            v = ref[r].astype(F32)
            for cb in range(ncb):
                buf[cb, pl.ds(r, tm // dil, stride=dil), :] = v[:, cb * LANES:(cb + 1) * LANES]
        return jnp.concatenate([buf[cb] for cb in range(ncb)], axis=1)

    ls = [token_order(l_refs[g], dils[g]) for g in range(ng)]
    os_ = [token_order(o_refs[g], dils[g]) for g in range(ng)]
    mx = functools.reduce(jnp.maximum, ls)
    es = [jnp.exp(l - mx) for l in ls]
    inv = 1.0 / functools.reduce(jnp.add, es)
    o = functools.reduce(jnp.add, [e * og for e, og in zip(es, os_)]) * inv
    out_ref[...] = x_ref[...] + jnp.dot(o.astype(BF16), w_ref[...], preferred_element_type=F32)


def _a_out(os_, lses, w, x3, tm):
    b, s, _ = x3.shape
    kd = w.shape[0]
    dils = tuple(o.shape[1] for o in os_)
    grp = [pl.BlockSpec((None, d, tm // d, kd), lambda bi, i: (bi, 0, i, 0)) for d in dils]
    row = pl.BlockSpec((None, tm, D_MODEL), lambda bi, i: (bi, i, 0))
    nscr = 2 * sum(1 for d in dils if d > 1)
    return pl.pallas_call(
        functools.partial(_a_out_kernel, dils=dils, tm=tm),
        out_shape=jax.ShapeDtypeStruct((b, s, D_MODEL), F32),
        grid=(b, s // tm),
        in_specs=grp + grp + [pl.BlockSpec((kd, D_MODEL), lambda bi, i: (0, 0)), row],
        out_specs=row,
        scratch_shapes=[pltpu.VMEM((kd // LANES, tm, LANES), F32)] * nscr,
        compiler_params=_cparams(("parallel", "parallel")),
    )(*os_, *lses, w, x3)


def _post_kernel(*refs, has_proj, final):
    refs = list(refs)
    x2_scr, h_scr, acc_scr = refs[-3:]
    hn_ref = None if final else refs[-4]
    out_ref = refs[-4] if final else refs[-5]
    if has_proj:
        (x_ref, o_ref, wout_ref, gx_ref, wq_ref, kv_ref, wo_ref, gm_ref, w1_ref, w2_ref,
         fg_ref) = refs[:11]
    else:
        x_ref, gx_ref, wq_ref, kv_ref, wo_ref, gm_ref, w1_ref, w2_ref, fg_ref = refs[:9]
    f = pl.program_id(2)

    @pl.when(f == 0)
    def _():
        x1 = x_ref[...]
        if has_proj:
            x1 = x1 + jnp.dot(o_ref[...], wout_ref[...], preferred_element_type=F32)
        ms = jnp.mean(x1 * x1, axis=-1, keepdims=True)
        h = (x1 * lax.rsqrt(ms + NORM_EPS) * gx_ref[...]).astype(BF16)
        qb = (jnp.dot(h, wq_ref[...], preferred_element_type=F32)
              * (X_HEAD_DIM ** -0.5 * LOG2E)).astype(BF16)
        hd = X_HEADS * X_HEAD_DIM
        outs = []
        for hh in range(X_HEADS):
            qh = qb[:, hh * X_HEAD_DIM:(hh + 1) * X_HEAD_DIM]
            kh = kv_ref[:, hh * X_HEAD_DIM:(hh + 1) * X_HEAD_DIM]
            vh = kv_ref[:, hd + hh * X_HEAD_DIM:hd + (hh + 1) * X_HEAD_DIM]
            s = lax.dot_general(qh, kh, (((1,), (1,)), ((), ())), preferred_element_type=F32)
            m = jnp.max(s, axis=-1, keepdims=True)
            p = jnp.exp2(s - m)
            l = jnp.sum(p, axis=-1, keepdims=True)
            oh = jnp.dot(p.astype(BF16), vh, preferred_element_type=F32) * (1.0 / l)
            outs.append(oh.astype(BF16))
        x2 = x1 + jnp.dot(jnp.concatenate(outs, axis=1), wo_ref[...], preferred_element_type=F32)
        x2_scr[...] = x2
        ms2 = jnp.mean(x2 * x2, axis=-1, keepdims=True)
        h_scr[...] = (x2 * lax.rsqrt(ms2 + NORM_EPS) * gm_ref[...]).astype(BF16)
        acc_scr[...] = jnp.zeros(acc_scr.shape, F32)

    a = jnp.maximum(jnp.dot(h_scr[...], w1_ref[...], preferred_element_type=F32), 0.0)
    acc_scr[...] += jnp.dot((a * a).astype(BF16), w2_ref[...], preferred_element_type=F32)

    @pl.when(f == pl.num_programs(2) - 1)
    def _():
        y = x2_scr[...] + acc_scr[...]
        ms = jnp.mean(y * y, axis=-1, keepdims=True)
        yn = y * lax.rsqrt(ms + NORM_EPS) * fg_ref[...]
        if final:
            out_ref[...] = yn
        else:
            out_ref[...] = y
            hn_ref[...] = yn.astype(hn_ref.dtype)


def _post_mixer(x3, o3, w_out, gx, wq, kv3, wo, gm, w1, w2, next_gain, final, tm, tf):
    b, s, _ = x3.shape
    hd = X_HEADS * X_HEAD_DIM
    has_proj = o3 is not None
    const = lambda bi, i, f: (0, 0)
    row = lambda bi, i, f: (bi, i, 0)
    in_specs = [pl.BlockSpec((None, tm, D_MODEL), row)]
    args = [x3]
    if has_proj:
        kd = o3.shape[-1]
        in_specs += [pl.BlockSpec((None, tm, kd), row), pl.BlockSpec((kd, D_MODEL), const)]
        args += [o3, w_out]
    in_specs += [pl.BlockSpec((1, D_MODEL), const),
                 pl.BlockSpec((D_MODEL, hd), const),
                 pl.BlockSpec((None, N_MEM, 2 * hd), lambda bi, i, f: (bi, 0, 0)),
                 pl.BlockSpec((hd, D_MODEL), const),
                 pl.BlockSpec((1, D_MODEL), const),
                 pl.BlockSpec((D_MODEL, tf), lambda bi, i, f: (0, f)),
                 pl.BlockSpec((tf, D_MODEL), lambda bi, i, f: (f, 0)),
                 pl.BlockSpec((1, D_MODEL), const)]
    args += [gx.reshape(1, D_MODEL), wq, kv3, wo, gm.reshape(1, D_MODEL), w1, w2,
             next_gain.reshape(1, D_MODEL)]
    out_shape = jax.ShapeDtypeStruct((b, s, D_MODEL), F32)
    out_specs = pl.BlockSpec((None, tm, D_MODEL), row)
    if not final:
        out_shape = (out_shape, jax.ShapeDtypeStruct((b, s, D_MODEL), BF16))
        out_specs = (out_specs, pl.BlockSpec((None, tm, D_MODEL), row))
    return pl.pallas_call(
        functools.partial(_post_kernel, has_proj=has_proj, final=final),
        out_shape=out_shape,
        grid=(b, s // tm, D_FF // tf),
        in_specs=in_specs,
        out_specs=out_specs,
        scratch_shapes=[pltpu.VMEM((tm, D_MODEL), F32), pltpu.VMEM((tm, D_MODEL), BF16),
                        pltpu.VMEM((tm, D_MODEL), F32)],
        compiler_params=_cparams(("parallel", "parallel", "arbitrary")),
    )(*args)


_F_ROT16 = np.array(list(range(0, 8)) + list(range(16, 40)))
_P_ROT16 = np.array(list(range(8, 16)) + list(range(40, 64)))
_F_AXIAL = np.array(list(range(0, 16)) + list(range(32, 48)))
_P_AXIAL = np.array(list(range(16, 32)) + list(range(48, 64)))


def _block_dims(first, partner):
    lane = np.arange(BLK)
    half, slot, u = lane // LANES, (lane % LANES) // 32, lane % 32
    return np.where(half == 0, first[u], partner[u]), slot


def _slot_masks(nslot, slot_of_lane):
    return np.stack([(slot_of_lane == j) for j in range(nslot)]).astype(np.float32)


def _rope_tables(pos_list, theta, rot, scale_list, npad):
    half = rot // 2
    inv_freq = jnp.exp(jnp.arange(half, dtype=F32) * (-2.0 * math.log(theta) / rot))
    cs, ss = [], []
    for pos in pos_list:
        ang = pos.astype(F32)[:, None] * inv_freq[None, :]
        cs.append(jnp.cos(ang))
        ss.append(jnp.sin(ang))
    c = jnp.concatenate(cs, axis=1)
    s = jnp.concatenate(ss, axis=1)
    n = c.shape[0]
    if npad:
        c = jnp.concatenate([c, jnp.ones((n, npad), F32)], axis=1)
        s = jnp.concatenate([s, jnp.zeros((n, npad), F32)], axis=1)
    reps = LANES // c.shape[1]
    c = jnp.tile(c, (1, reps))
    s = jnp.tile(s, (1, reps))
    ctab = [c * sc for sc in scale_list] + [jnp.ones_like(c)]
    stab = [s * sc for sc in scale_list] + [jnp.zeros_like(s)]
    return jnp.stack(ctab), jnp.stack(stab)


def _mixer_a(x, b, s, gain, w_in, w_out, pos, tm):
    dims, slot = _block_dims(_F_ROT16, _P_ROT16)
    ctab, stab = _rope_tables([pos], ROPE_THETA, 16, [0.125 * LOG2E, 1.0], 24)
    qmask = jnp.asarray(_slot_masks(4, slot), BF16)
    vmask = jnp.asarray(_slot_masks(4, np.arange(BLK) // 64), F32)
    cols = []
    for wg, (window, dil) in enumerate(A_PATTERNS):
        assert window // (2 * dil) == A_HALF_WINDOW
        for which in range(3):
            base = (wg * 3 + which) * A_HEADS * 64
            for hg in range(2):
                cols.append(base + hg * BLK + (slot * 64 + dims if which < 2 else np.arange(BLK)))
    w = w_in.astype(BF16)[:, np.concatenate(cols)]
    dils = tuple(dil for _, dil in A_PATTERNS)
    qkvs = _proj(x, 0, D_MODEL, D_MODEL, gain, w, tn=2 * BLK, out_dtype=BF16, seq=s, tm=tm,
                 tables=(ctab, stab), tt=np.array([0, 1, 2] * A_GROUPS, np.int32), dils=dils)
    outs, lses = [], []
    for qkv in qkvs:
        o, lse = _band_attention(qkv, qmask, vmask)
        outs.append(o)
        lses.append(lse)
    x3 = _a_out(outs, lses, w_out.astype(BF16), x.reshape(b, s, D_MODEL), min(tm, 512))
    return x3.reshape(b * s, D_MODEL), None, None


def _mixer_b(x, h, b, s, w_in, q_gain, k_gain, w_out, rows, cols_pos, tm, tq, tk):
    dims, slot = _block_dims(_F_AXIAL, _P_AXIAL)
    qcols = np.concatenate([g * BLK + slot * 64 + dims for g in range(B_KV_HEADS)])
    kcols = np.concatenate([B_HEADS * 64 + g * 64 + dims for g in range(B_KV_HEADS)])
    w_bf = w_in.astype(BF16)
    wqk = w_bf[:, np.concatenate([qcols, kcols])]
    wv = w_bf[:, (B_HEADS + B_KV_HEADS) * 64:]
    hgain = jnp.concatenate([jnp.tile(q_gain[dims] * (0.125 * LOG2E), B_KV_HEADS),
                             jnp.tile(k_gain[dims], B_KV_HEADS)]).reshape(1, -1).astype(F32)
    bd = jnp.asarray(slot[:, None] == slot[None, :], BF16)
    ctab, stab = _rope_tables([rows, cols_pos], AXIAL_THETA, 32, [1.0], 0)
    n = B_KV_HEADS * BLK
    qk = _proj(h, 0, D_MODEL, None, None, wqk, tn=2 * BLK, out_dtype=BF16, seq=s, tm=tm,
               tables=(ctab, stab), tt=np.zeros((n // BLK,), np.int32), headnorm=(hgain, bd))
    vt = _proj(h, 0, D_MODEL, None, None, wv, tn=B_KV_HEADS * 64, out_dtype=BF16, seq=s, tm=tm,
               vt=(B_KV_HEADS, 1, 64, tk))
    qk = qk.reshape(b, s, 2 * n)
    qmask = jnp.asarray(_slot_masks(4, slot), BF16)
    o = _flash(qk, qk, vt, qcol0=0, kcol0=B_KV_HEADS, ngroups=B_KV_HEADS, wq=BLK,
               wv=64 + _ones_rows(64), wo=BLK, nh=4, vgroups=((0, 64, 0, 4),), qmask=qmask,
               tq=tq, tk=tk)
    return x, o, w_out.astype(BF16)


def _mixer_c(x, h, b, s, w_in, lq1, lk1, lq2, lk2, sub_gain, w_out, pos, lambda_init, tm, tq, tk):
    dims, slot = _block_dims(_F_ROT16, _P_ROT16)
    ngr = C_HEADS // 2
    cols = []
    for which in range(2):
        for g in range(ngr):
            cols.append(which * C_HEADS * 128 + g * BLK + slot * 64 + dims)
    w_bf = w_in.astype(BF16)
    ctab, stab = _rope_tables([pos], ROPE_THETA, 16, [0.125 * LOG2E, 1.0], 24)
    n = ngr * BLK
    qk = _proj(h, 0, D_MODEL, None, None, w_bf[:, np.concatenate(cols)], tn=2 * BLK, out_dtype=BF16,
               seq=s, tm=tm, tables=(ctab, stab), tt=np.array([0, 0, 1, 1], np.int32))
    qkv = qk.reshape(b, s, 2 * n)
    vt = _proj(h, 0, D_MODEL, None, None, w_bf[:, 2 * C_HEADS * 128:], tn=n, out_dtype=BF16,
               seq=s, tm=tm, vt=(ngr, 2, LANES, tk))
    qmask = jnp.asarray(_slot_masks(4, slot), BF16)
    hrows = LANES + _ones_rows(LANES)
    o = _flash(qkv, qkv, vt, qcol0=0, kcol0=ngr, ngroups=ngr, wq=BLK, wv=2 * hrows,
               wo=BLK, nh=4, vgroups=((0, LANES, 0, 2), (hrows, LANES, 2, 2)),
               qmask=qmask, tq=tq, tk=tk,
               diff=lambda_init, diff_params=(lq1, lk1, lq2, lk2, sub_gain))
    return x, o, w_out.astype(BF16)


def _mixer_d(x, h, b, s, w_in, q_gain, kv_gain, w_uq, w_ukv, w_out, pos, tm, tq, tk):
    lane = np.arange(LANES)
    slot_h = np.where(lane < 32, 0, np.where(lane < 64, 1, np.where(lane < 80, 0, np.where(lane < 96, 1, -1))))
    slot = np.concatenate([slot_h, slot_h])
    nope_lane = lane < 64
    rope_lane = (lane >= 64) & (lane < 96)
    ngr = D_HEADS // 2

    w1 = jnp.zeros((D_MODEL, 4 * BLK), F32)
    w1 = w1.at[:, :D_Q_RANK].set(w_in[:, :D_Q_RANK])
    w1 = w1.at[:, 2 * BLK:3 * BLK].set(w_in[:, D_Q_RANK:D_Q_RANK + D_KV_RANK])
    kr_src = np.zeros((BLK,), np.int64)
    kr_on = np.zeros((BLK,), bool)
    for hf in range(2):
        for l in range(LANES):
            if rope_lane[l]:
                kr_src[hf * LANES + l] = D_Q_RANK + D_KV_RANK + hf * 16 + (l - 64) % 16
                kr_on[hf * LANES + l] = True
    w1 = w1.at[:, 3 * BLK:].set(jnp.where(jnp.asarray(kr_on)[None, :], w_in[:, kr_src], 0.0))
    cmb = _proj(h, 0, D_MODEL, None, None, w1.astype(BF16), tn=4 * BLK, out_dtype=F32, seq=s, tm=tm)

    qsrc = np.zeros((ngr * BLK,), np.int64)
    qon = np.zeros((ngr * BLK,), bool)
    ksrc = np.zeros((ngr * BLK,), np.int64)
    kon = np.zeros((ngr * BLK,), bool)
    for g in range(ngr):
        for hf in range(2):
            for l in range(LANES):
                idx = g * BLK + hf * LANES + l
                if slot_h[l] < 0:
                    continue
                head = 2 * g + slot_h[l]
                if nope_lane[l]:
                    d = hf * 32 + l % 32
                    qsrc[idx], qon[idx] = head * 96 + d, True
                    ksrc[idx], kon[idx] = head * 128 + d, True
                else:
                    d = hf * 16 + (l - 64) % 16
                    qsrc[idx], qon[idx] = head * 96 + D_NOPE + d, True
    wq2 = jnp.where(jnp.asarray(qon)[None, :], w_uq[:, qsrc], 0.0)
    wq2 = jnp.concatenate([wq2, jnp.zeros((2 * BLK - D_Q_RANK, ngr * BLK), F32)], axis=0).astype(BF16)
    wk2 = jnp.where(jnp.asarray(kon)[None, :], w_ukv[:, ksrc], 0.0).astype(BF16)
    vsrc = np.concatenate([h * 128 + D_NOPE + np.arange(64) for h in range(D_HEADS)])
    wv2 = w_ukv[:, vsrc].astype(BF16)
    qg = jnp.concatenate([q_gain, jnp.zeros((2 * BLK - D_Q_RANK,), F32)])

    half = D_ROPE // 2
    inv_freq = jnp.exp(jnp.arange(half, dtype=F32) * (-2.0 * math.log(ROPE_THETA) / D_ROPE))
    ang = pos.astype(F32)[:, None] * inv_freq[None, :]
    ones64 = jnp.ones((s, 64), F32)
    pad32 = jnp.ones((s, 32), F32)
    c = jnp.concatenate([ones64, jnp.cos(ang), jnp.cos(ang), pad32], axis=1)
    sn = jnp.concatenate([0.0 * ones64, jnp.sin(ang), jnp.sin(ang), 0.0 * pad32], axis=1)
    qs = (D_NOPE + D_ROPE) ** -0.5 * LOG2E
    ctab = jnp.stack([c * qs, c])
    stab = jnp.stack([sn * qs, sn])

    q = _proj(cmb, 0, 2 * BLK, D_Q_RANK, qg, wq2, tn=2 * BLK, out_dtype=BF16, seq=s, tm=tm,
              tables=(ctab, stab), tt=np.zeros((ngr // 2,), np.int32))
    k = _proj(cmb, 2, BLK, D_KV_RANK, kv_gain, wk2, tn=4 * BLK, out_dtype=BF16, seq=s, tm=tm,
              tables=(ctab, stab), tt=np.ones((ngr // 4,), np.int32), add=(cmb, 3))
    vt = _proj(cmb, 2, BLK, D_KV_RANK, kv_gain, wv2, tn=4 * BLK, out_dtype=BF16, seq=s, tm=tm,
               vt=(ngr, 2, 64, tk))
    q = q.reshape(b, s, ngr * BLK)
    k = k.reshape(b, s, ngr * BLK)
    qmask = jnp.asarray(_slot_masks(2, slot), BF16)
    hrows = 64 + _ones_rows(64)
    o = _flash(q, k, vt, qcol0=0, kcol0=0, ngroups=ngr, wq=BLK, wv=2 * hrows, wo=LANES,
               nh=2, vgroups=((0, 64, 0, 1), (hrows, 64, 1, 1)), qmask=qmask, tq=tq, tk=tk)
    return x, o, w_out.astype(BF16)


def _run_trunk(x3, mem3, p):
    b, s, _ = x3.shape
    t = b * s
    tm = 1024
    tm_h = 2048
    tm_mlp = 512
    tk = 512
    tq = 1024
    tq_mla = 2 * tq
    x = x3.reshape(t, D_MODEL)
    pos = jnp.arange(s, dtype=F32)
    rows = jnp.repeat(jnp.arange(s // GRID_W, dtype=F32), GRID_W)
    cols_pos = jnp.tile(jnp.arange(GRID_W, dtype=F32), s // GRID_W)
    memf = mem3.reshape(b * N_MEM, D_MODEL)
    h = None
    for i in range(DEPTH):
        m, j = i % 4, i // 4
        if m == 0:
            x, o, w_out = _mixer_a(x, b, s, p['norm_mix'][i], p['a_w_in'][j], p['a_w_out'][j], pos, tm)
        elif m == 1:
            x, o, w_out = _mixer_b(x, h, b, s, p['b_w_in'][j], p['b_q_norm'][j], p['b_k_norm'][j],
                                   p['b_w_out'][j], rows, cols_pos, tm_h, tq, tk)
        elif m == 2:
            x, o, w_out = _mixer_c(x, h, b, s, p['c_w_in'][j], p['c_lambda_q1'][j],
                                   p['c_lambda_k1'][j], p['c_lambda_q2'][j], p['c_lambda_k2'][j],
                                   p['c_sub_norm'][j], p['c_w_out'][j], pos,
                                   0.8 - 0.6 * math.exp(-0.3 * i), tm_h, tq, tk)
        else:
            x, o, w_out = _mixer_d(x, h, b, s, p['d_w_in'][j], p['d_q_norm'][j], p['d_kv_norm'][j],
                                   p['d_w_uq'][j], p['d_w_ukv'][j], p['d_w_out'][j], pos, tm_h,
                                   tq_mla, tk)
        kv = _proj(memf, 0, D_MODEL, D_MODEL, p['norm_mem'][i], p['w_xkv'][i].astype(BF16),
                   tn=2 * X_HEADS * X_HEAD_DIM, out_dtype=BF16, seq=N_MEM, tm=N_MEM)
        final = i == DEPTH - 1
        res = _post_mixer(x.reshape(b, s, D_MODEL), o, w_out, p['norm_x'][i],
                          p['w_xq'][i].astype(BF16), kv.reshape(b, N_MEM, 2 * X_HEADS * X_HEAD_DIM),
                          p['w_xo'][i].astype(BF16), p['norm_mlp'][i], p['w_mlp_in'][i].astype(BF16),
                          p['w_mlp_out'][i].astype(BF16),
                          p['final_norm'] if final else p['norm_mix'][i + 1], final, tm_mlp, 2048)
        if final:
            x = res.reshape(t, D_MODEL)
        else:
            x, h = res[0].reshape(t, D_MODEL), res[1].reshape(t, D_MODEL)
    return x.reshape(b, s, D_MODEL)


def kernel(x_prompt, x_sample, mem_prompt, mem_sample, norm_mix, norm_x, norm_mem, w_xq, w_xkv, w_xo, norm_mlp, w_mlp_in, w_mlp_out, a_w_in, a_w_out, b_w_in, b_q_norm, b_k_norm, b_w_out, c_w_in, c_lambda_q1, c_lambda_k1, c_lambda_q2, c_lambda_k2, c_sub_norm, c_w_out, d_w_in, d_q_norm, d_kv_norm, d_w_uq, d_w_ukv, d_w_out, final_norm):
    p = dict(norm_mix=norm_mix, norm_x=norm_x, norm_mem=norm_mem, w_xq=w_xq, w_xkv=w_xkv,
             w_xo=w_xo, norm_mlp=norm_mlp, w_mlp_in=w_mlp_in, w_mlp_out=w_mlp_out,
             a_w_in=a_w_in, a_w_out=a_w_out, b_w_in=b_w_in, b_q_norm=b_q_norm,
             b_k_norm=b_k_norm, b_w_out=b_w_out, c_w_in=c_w_in, c_lambda_q1=c_lambda_q1,
             c_lambda_k1=c_lambda_k1, c_lambda_q2=c_lambda_q2, c_lambda_k2=c_lambda_k2,
             c_sub_norm=c_sub_norm, c_w_out=c_w_out, d_w_in=d_w_in, d_q_norm=d_q_norm,
             d_kv_norm=d_kv_norm, d_w_uq=d_w_uq, d_w_ukv=d_w_ukv, d_w_out=d_w_out,
             final_norm=final_norm)
    return (_run_trunk(x_prompt, mem_prompt, p), _run_trunk(x_sample, mem_sample, p))
```

```python
import functools
import math

import numpy as np
import jax
import jax.numpy as jnp
from jax import lax
from jax.experimental import pallas as pl
from jax.experimental.pallas import tpu as pltpu

F32 = jnp.float32
BF16 = jnp.bfloat16

D_MODEL = 1024
DEPTH = 4
N_MEM = 256
GRID_W = 64
D_FF = 4 * D_MODEL
NORM_EPS = 1e-6
ROPE_THETA = 500000.0
AXIAL_THETA = 10000.0
NEG_INF = -1e30

A_PATTERNS = ((128, 1), (512, 4), (2048, 16))
A_GROUPS = 3
A_HEADS = 8
A_HALF_WINDOW = 64
B_HEADS = 16
B_KV_HEADS = 4
C_HEADS = 8
D_HEADS = 16
D_Q_RANK = 384
D_KV_RANK = 256
D_NOPE = 64
D_ROPE = 32
X_HEADS = 4
X_HEAD_DIM = 128

LOG2E = 1.4426950408889634
BF16_ROWS = 16
SCORE_SLOTS = 2
KEY_LOOP_TRIPS = 2
LANES = 128
BLK = 2 * LANES
VMEM_LIMIT = 56 * 1024 * 1024


def _ones_rows(nr):
    return LANES - nr if nr < LANES else BF16_ROWS


def _cparams(sem):
    return pltpu.CompilerParams(dimension_semantics=sem, vmem_limit_bytes=VMEM_LIMIT)


def _proj_kernel(tt_ref, src_ref, *rest, dnorm, rope, headnorm, add, nsub, vt, dils, spg):
    del tt_ref
    rest = list(rest)
    c_ref = s_ref = hg_ref = bd_ref = add_ref = y_scr = None
    prenormed = dnorm is None
    g_ref = None if prenormed else rest.pop(0)
    w_ref = rest.pop(0)
    staged = dils != (1,)
    if staged:
        y_scr = rest.pop()
    if rope:
        c_ref, s_ref = rest[0], rest[1]
        rest = rest[2:]
    if headnorm:
        hg_ref, bd_ref = rest[0], rest[1]
        rest = rest[2:]
    if add:
        add_ref = rest[0]
        rest = rest[1:]
    o_refs = rest[:len(dils)]
    o_ref = o_refs[0]
    if prenormed:
        h_ref = src_ref
    else:
        h_ref = rest[len(dils)]

        @pl.when(pl.program_id(1) == 0)
        def _():
            xf = src_ref[...].astype(F32)
            ms = jnp.sum(xf * xf, axis=-1, keepdims=True) * (1.0 / dnorm)
            h_ref[...] = (xf * lax.rsqrt(ms + NORM_EPS) * g_ref[...]).astype(BF16)

    y = jnp.dot(h_ref[...], w_ref[...], preferred_element_type=F32)
    if add:
        ad = add_ref[...]
        y = y + (ad if nsub == 1 else jnp.concatenate([ad] * nsub, axis=1))
    if headnorm:
        y2 = y * y
        hi = y2.astype(BF16)
        lo = (y2 - hi.astype(F32)).astype(BF16)
        parts = []
        for n in range(nsub):
            sl = slice(n * BLK, (n + 1) * BLK)
            ss = (jnp.dot(hi[:, sl], bd_ref[...], preferred_element_type=F32)
                  + jnp.dot(lo[:, sl], bd_ref[...], preferred_element_type=F32))
            parts.append(y[:, sl] * lax.rsqrt(ss * (1.0 / 64.0) + NORM_EPS))
        y = (parts[0] if nsub == 1 else jnp.concatenate(parts, axis=1)) * hg_ref[...]
    if rope:
        c = c_ref[0]
        s = s_ref[0]
        for n in range(nsub):
            y1 = y[:, n * BLK:n * BLK + LANES]
            y2 = y[:, n * BLK + LANES:(n + 1) * BLK]
            r1 = y1 * c - y2 * s
            r2 = y2 * c + y1 * s
            if staged:
                y_scr[2 * n] = r1
                y_scr[2 * n + 1] = r2
            else:
                o_ref[:, n * BLK:n * BLK + LANES] = r1.astype(o_ref.dtype)
                o_ref[:, n * BLK + LANES:(n + 1) * BLK] = r2.astype(o_ref.dtype)
        if staged:
            def copy_out(out, dil):
                rows = y.shape[0] // dil
                for r in range(dil):
                    for cb in range(2 * nsub):
                        out[r, :, cb * LANES:(cb + 1) * LANES] = (
                            y_scr[cb, pl.ds(r, rows, stride=dil), :].astype(out.dtype))

            for gi, dil in enumerate(dils):
                pl.when(pl.program_id(1) // spg == gi)(functools.partial(copy_out, o_refs[gi], dil))
    elif vt:
        ngroups, nvg, nr, tk = vt
        nones = _ones_rows(nr)
        ones = jnp.ones((nones, tk), o_ref.dtype)
        for cc in range(y.shape[0] // tk):
            yt = y[cc * tk:(cc + 1) * tk, :].T
            for g in range(ngroups):
                for vg in range(nvg):
                    src0 = (g * nvg + vg) * nr
                    r0 = vg * (nr + nones)
                    o_ref[g, cc, r0:r0 + nr, :] = yt[src0:src0 + nr, :].astype(o_ref.dtype)
                    o_ref[g, cc, r0 + nr:r0 + nr + nones, :] = ones
    else:
        o_ref[...] = y.astype(o_ref.dtype)


def _proj(src, src_cb, kdim, dnorm, gain, w, *, tn, out_dtype, seq, tm,
          tables=None, tt=None, headnorm=None, add=None, vt=None, dils=(1,)):
    t = src.shape[0]
    n = w.shape[1]
    nj = n // tn
    dils = tuple(dils)
    out_shape = jax.ShapeDtypeStruct((t, n), out_dtype)
    out_spec = pl.BlockSpec((tm, tn), lambda i, j, tt_: (i, j))
    if vt is not None:
        ngroups, nvg, nr, tk = vt
        assert nj == 1 and n == ngroups * nvg * nr and tm % tk == 0
        wv = nvg * (nr + _ones_rows(nr))
        out_shape = jax.ShapeDtypeStruct((t // seq, ngroups, seq // tk, wv, tk), out_dtype)
        out_spec = pl.BlockSpec((None, ngroups, tm // tk, wv, tk),
                                lambda i, j, tt_: (i // (seq // tm), 0, i % (seq // tm), 0, 0))
    prenormed = dnorm is None
    scratch = [] if prenormed else [pltpu.VMEM((tm, kdim), BF16)]
    spg = nj // len(dils)
    if dils != (1,):
        assert tables is not None and all(tm % (BF16_ROWS * d) == 0 for d in dils)
        out_shape = tuple(jax.ShapeDtypeStruct((t // seq, d, seq // d, n // len(dils)), out_dtype)
                          for d in dils)
        out_spec = tuple(
            pl.BlockSpec((None, d, tm // d, tn),
                         lambda i, j, tt_, gi=gi: (i // (seq // tm), 0, i % (seq // tm),
                                                   jnp.clip(j - gi * spg, 0, spg - 1)))
            for gi, d in enumerate(dils))
        scratch.append(pltpu.VMEM((tn // LANES, tm, LANES), F32))
    nsub = tn // BLK if (tables is not None or headnorm is not None) else 1
    ns = seq // tm
    if tt is None:
        tt = np.zeros((nj,), np.int32)
    in_specs = [pl.BlockSpec((tm, kdim), lambda i, j, tt_: (i, src_cb))]
    args = [src]
    if not prenormed:
        in_specs.append(pl.BlockSpec((1, kdim), lambda i, j, tt_: (0, 0)))
        args.append(gain.reshape(1, kdim).astype(F32))
    in_specs.append(pl.BlockSpec((kdim, tn), lambda i, j, tt_: (0, j)))
    args.append(w)
    if tables is not None:
        for tb in tables:
            in_specs.append(pl.BlockSpec((1, tm, LANES), lambda i, j, tt_: (tt_[j], i % ns, 0)))
            args.append(tb)
    if headnorm is not None:
        in_specs.append(pl.BlockSpec((1, tn), lambda i, j, tt_: (0, j)))
        in_specs.append(pl.BlockSpec((BLK, BLK), lambda i, j, tt_: (0, 0)))
        args += [headnorm[0], headnorm[1]]
    if add is not None:
        add_arr, add_cb = add
        in_specs.append(pl.BlockSpec((tm, BLK), lambda i, j, tt_: (i, add_cb)))
        args.append(add_arr)
    kern = functools.partial(_proj_kernel, dnorm=dnorm, rope=tables is not None,
                             headnorm=headnorm is not None, add=add is not None, nsub=nsub, vt=vt,
                             dils=tuple(dils), spg=spg)
    return pl.pallas_call(
        kern,
        out_shape=out_shape,
        grid_spec=pltpu.PrefetchScalarGridSpec(
            num_scalar_prefetch=1,
            grid=(t // tm, nj),
            in_specs=in_specs,
            out_specs=out_spec,
            scratch_shapes=scratch,
        ),
        compiler_params=_cparams(("parallel", "arbitrary")),
    )(jnp.asarray(tt, jnp.int32), *args)


def _flash_kernel(*refs, nh, vgroups, tq, tk, nk, diff):
    if diff:
        (qmask_ref, q_ref, k_ref, vt_ref, lq1, lk1, lq2, lk2, sg_ref,
         o_ref, qm_scr, s_scr, mx_scr, m_scr, acc_scr) = refs
        lambda_init = diff
    else:
        qmask_ref, q_ref, k_ref, vt_ref, o_ref, qm_scr, s_scr, mx_scr, m_scr, acc_scr = refs

    qt = q_ref[...].astype(F32).T
    for j in range(nh):
        rowmask = jnp.concatenate([qmask_ref[j]] * (tq // LANES), axis=1)
        qm_scr[:, j * tq:(j + 1) * tq] = (qt * rowmask).astype(BF16)
    m_scr[...] = jnp.full(m_scr.shape, NEG_INF, F32)
    acc_scr[...] = jnp.zeros(acc_scr.shape, F32)

    ucols = min(2 * BLK, vgroups[0][3] * tq)
    units = []
    for gi, (r0, nr, h0, hn) in enumerate(vgroups):
        for off in range(0, hn * tq, ucols):
            units.append((gi, r0, nr, h0 * tq + off, off))

    def scores(c, slot, col0):
        cols = slice(col0, col0 + ucols)
        ks = pl.multiple_of(c * tk, tk)
        st = jnp.dot(k_ref[pl.ds(ks, tk), :], qm_scr[:, cols], preferred_element_type=F32)
        s_scr[slot, :, cols] = st
        mx_scr[slot, :, cols] = jnp.max(st, axis=0, keepdims=True)

    def step(c, slot, c_next, slot_next):
        vtc = vt_ref[c]
        m_prev = m_scr[...]
        m_new = jnp.maximum(m_prev, mx_scr[slot])
        alpha = jnp.exp2(m_prev - m_new)
        m_scr[...] = m_new
        for gi, r0, nr, col0, off in units:
            cols = slice(col0, col0 + ucols)
            scores(c_next, slot_next, col0)
            pt = jnp.exp2(s_scr[slot, :, cols] - m_new[:, cols]).astype(BF16)
            acols = slice(off, off + ucols)
            acc_scr[gi, :, acols] = acc_scr[gi, :, acols] * alpha[:, cols] + jnp.dot(
                vtc[r0:r0 + nr + _ones_rows(nr), :], pt, preferred_element_type=F32)

    for unit in units:
        scores(0, 0, unit[3])

    per_trip = nk // KEY_LOOP_TRIPS

    def body(i, carry):
        c = per_trip * i
        for u in range(per_trip):
            step(c + u, u % SCORE_SLOTS, jnp.minimum(c + u + 1, nk - 1), (u + 1) % SCORE_SLOTS)
        return carry

    lax.fori_loop(0, KEY_LOOP_TRIPS, body, 0)

    pieces = []
    if diff:
        lam = (jnp.exp(jnp.sum(lq1[...] * lk1[...], axis=-1, keepdims=True))
               - jnp.exp(jnp.sum(lq2[...] * lk2[...], axis=-1, keepdims=True)) + lambda_init)
        for gi, (r0, nr, h0, hn) in enumerate(vgroups):
            acc = acc_scr[gi]
            linv = 1.0 / acc[nr:nr + 1, :]
            oh = acc[:nr, :tq] * linv[:, :tq] - lam * (acc[:nr, tq:] * linv[:, tq:])
            ms = jnp.mean(oh * oh, axis=0, keepdims=True)
            pieces.append(oh * lax.rsqrt(ms + NORM_EPS))
    else:
        for gi, (r0, nr, h0, hn) in enumerate(vgroups):
            acc = acc_scr[gi]
            on = acc[:nr, :] * (1.0 / acc[nr:nr + 1, :])
            for jj in range(hn):
                pieces.append(on[:, jj * tq:(jj + 1) * tq])
    ot = pieces[0] if len(pieces) == 1 else jnp.concatenate(pieces, axis=0)
    o = ot.T
    if diff:
        o = o * sg_ref[...] * (1.0 - lambda_init)
    o_ref[...] = o.astype(o_ref.dtype)


def _flash(q, k, vt, *, qcol0, kcol0, ngroups, wq, wv, wo, nh, vgroups, qmask, tq, tk,
           diff=None, diff_params=None):
    b, s = q.shape[0], q.shape[1]
    nk = s // tk
    assert s % tk == 0 and nk % (KEY_LOOP_TRIPS * SCORE_SLOTS) == 0 and s % tq == 0
    qmask = jnp.broadcast_to(qmask.astype(F32)[:, :, None], (nh, wq, LANES))
    in_specs = [
        pl.BlockSpec((nh, wq, LANES), lambda bi, g, i: (0, 0, 0)),
        pl.BlockSpec((None, tq, wq), lambda bi, g, i: (bi, i, qcol0 + g)),
        pl.BlockSpec((None, s, wq), lambda bi, g, i: (bi, 0, kcol0 + g)),
        pl.BlockSpec((None, None, nk, wv, tk), lambda bi, g, i: (bi, g, 0, 0, 0)),
    ]
    args = [qmask, q, k, vt]
    if diff is not None:
        for prm in diff_params[:4]:
            in_specs.append(pl.BlockSpec((1, 64), lambda bi, g, i: (0, 0)))
            args.append(prm.reshape(1, 64).astype(F32))
        in_specs.append(pl.BlockSpec((1, wo), lambda bi, g, i: (0, 0)))
        args.append(jnp.tile(diff_params[4].astype(F32), wo // LANES).reshape(1, wo))
    kern = functools.partial(_flash_kernel, nh=nh, vgroups=tuple(vgroups), tq=tq, tk=tk, nk=nk,
                             diff=diff)
    nr, hn = vgroups[0][1], vgroups[0][3]
    return pl.pallas_call(
        kern,
        out_shape=jax.ShapeDtypeStruct((b, s, ngroups * wo), BF16),
        grid=(b, ngroups, s // tq),
        in_specs=in_specs,
        out_specs=pl.BlockSpec((None, tq, wo), lambda bi, g, i: (bi, i, g)),
        scratch_shapes=[
            pltpu.VMEM((wq, nh * tq), BF16),
            pltpu.VMEM((SCORE_SLOTS, tk, nh * tq), F32),
            pltpu.VMEM((SCORE_SLOTS, 1, nh * tq), F32),
            pltpu.VMEM((1, nh * tq), F32),
            pltpu.VMEM((len(vgroups), nr + _ones_rows(nr), hn * tq), F32),
        ],
        compiler_params=_cparams(("parallel", "parallel", "arbitrary")),
    )(*args)


def _band_kernel(qmask_ref, vmask_ref, q_ref, k_ref, v_ref, o_ref, lse_ref, *, tq, win, length, ntile):
    nh = 4
    vmask = vmask_ref[...]
    vmask_b = vmask.astype(BF16)

    def tile(t, carry):
        r0 = pl.multiple_of(t * tq, tq)
        i = pl.program_id(2) * ntile + t
        ks = jnp.clip(i * tq - A_HALF_WINDOW, 0, length - win)
        ks = pl.multiple_of(ks, A_HALF_WINDOW)
        kc = k_ref[pl.ds(ks, win), :]
        vc = v_ref[pl.ds(ks, win), :]
        q = q_ref[pl.ds(r0, tq), :]
        qm = jnp.concatenate([q * qmask_ref[j:j + 1, :] for j in range(nh)], axis=0)
        s = lax.dot_general(qm, kc, (((1,), (1,)), ((), ())), preferred_element_type=F32)
        qpos = i * tq + lax.broadcasted_iota(jnp.int32, (tq, win), 0)
        kpos = ks + lax.broadcasted_iota(jnp.int32, (tq, win), 1)
        valid = jnp.abs(qpos - kpos) <= A_HALF_WINDOW
        ps = []
        inv = None
        lse = None
        for j in range(nh):
            sj = jnp.where(valid, s[j * tq:(j + 1) * tq], NEG_INF)
            mj = jnp.max(sj, axis=-1, keepdims=True)
            pj = jnp.exp2(sj - mj)
            lj = jnp.sum(pj, axis=-1, keepdims=True)
            ps.append(pj.astype(BF16))
            t_inv = (1.0 / lj) * vmask[j:j + 1, :]
            t_lse = ((mj + jnp.log2(lj)) * (1.0 / LOG2E)) * vmask[j:j + 1, :]
            inv = t_inv if inv is None else inv + t_inv
            lse = t_lse if lse is None else lse + t_lse
        lhs = jnp.concatenate(ps, axis=1)
        rhs = jnp.concatenate([vc * vmask_b[j:j + 1, :] for j in range(nh)], axis=0)
        pv = jnp.dot(lhs, rhs, preferred_element_type=F32)
        o_ref[pl.ds(r0, tq), :] = (pv * inv).astype(o_ref.dtype)
        lse_ref[pl.ds(r0, tq), :] = lse
        return carry

    lax.fori_loop(0, ntile, tile, 0)


def _band_attention(qkv, qmask, vmask):
    b, dil, length, _ = qkv.shape
    tq = 2 * A_HALF_WINDOW
    win = tq + 2 * A_HALF_WINDOW
    rows = min(8 * tq, length)

    def col(which):
        return lambda bi, a, i: (bi, a // 2, 0, which * 2 + a % 2)

    in_specs = [
        pl.BlockSpec((4, BLK), lambda bi, a, i: (0, 0)),
        pl.BlockSpec((4, BLK), lambda bi, a, i: (0, 0)),
        pl.BlockSpec((None, None, rows, BLK), lambda bi, a, i: (bi, a // 2, i, a % 2)),
        pl.BlockSpec((None, None, length, BLK), col(1)),
        pl.BlockSpec((None, None, length, BLK), col(2)),
    ]
    kern = functools.partial(_band_kernel, tq=tq, win=win, length=length, ntile=rows // tq)
    out_block = pl.BlockSpec((None, None, rows, BLK), lambda bi, a, i: (bi, a // 2, i, a % 2))
    return pl.pallas_call(
        kern,
        out_shape=(jax.ShapeDtypeStruct((b, dil, length, 2 * BLK), BF16),
                   jax.ShapeDtypeStruct((b, dil, length, 2 * BLK), F32)),
        grid=(b, dil * 2, length // rows),
        in_specs=in_specs,
        out_specs=(out_block, out_block),
        compiler_params=_cparams(("parallel", "parallel", "arbitrary")),
    )(qmask, vmask, qkv, qkv, qkv)


def _a_out_kernel(*refs, dils, tm):
    ng = len(dils)
    o_refs, l_refs = refs[:ng], refs[ng:2 * ng]
    w_ref, x_ref, out_ref = refs[2 * ng:2 * ng + 3]
    scr = list(refs[2 * ng + 3:])

    def token_order(ref, dil):
        if dil == 1:
            return ref[0].astype(F32)
        buf = scr.pop(0)
        ncb = buf.shape[0]
        for r in range(dil):
            v = ref[r].astype(F32)
            for cb in range(ncb):
                buf[cb, pl.ds(r, tm // dil, stride=dil), :] = v[:, cb * LANES:(cb + 1) * LANES]
        return jnp.concatenate([buf[cb] for cb in range(ncb)], axis=1)

    ls = [token_order(l_refs[g], dils[g]) for g in range(ng)]
    os_ = [token_order(o_refs[g], dils[g]) for g in range(ng)]
    mx = functools.reduce(jnp.maximum, ls)
    es = [jnp.exp(l - mx) for l in ls]
    inv = 1.0 / functools.reduce(jnp.add, es)
    o = functools.reduce(jnp.add, [e * og for e, og in zip(es, os_)]) * inv
    out_ref[...] = x_ref[...] + jnp.dot(o.astype(BF16), w_ref[...], preferred_element_type=F32)


def _a_out(os_, lses, w, x3, tm):
    b, s, _ = x3.shape
    kd = w.shape[0]
    dils = tuple(o.shape[1] for o in os_)
    grp = [pl.BlockSpec((None, d, tm // d, kd), lambda bi, i: (bi, 0, i, 0)) for d in dils]
    row = pl.BlockSpec((None, tm, D_MODEL), lambda bi, i: (bi, i, 0))
    nscr = 2 * sum(1 for d in dils if d > 1)
    return pl.pallas_call(
        functools.partial(_a_out_kernel, dils=dils, tm=tm),
        out_shape=jax.ShapeDtypeStruct((b, s, D_MODEL), F32),
        grid=(b, s // tm),
        in_specs=grp + grp + [pl.BlockSpec((kd, D_MODEL), lambda bi, i: (0, 0)), row],
        out_specs=row,
        scratch_shapes=[pltpu.VMEM((kd // LANES, tm, LANES), F32)] * nscr,
        compiler_params=_cparams(("parallel", "parallel")),
    )(*os_, *lses, w, x3)


def _post_kernel(*refs, has_proj, final):
    refs = list(refs)
    x2_scr, h_scr, acc_scr = refs[-3:]
    hn_ref = None if final else refs[-4]
    out_ref = refs[-4] if final else refs[-5]
    if has_proj:
        (x_ref, o_ref, wout_ref, gx_ref, wq_ref, kv_ref, wo_ref, gm_ref, w1_ref, w2_ref,
         fg_ref) = refs[:11]
    else:
        x_ref, gx_ref, wq_ref, kv_ref, wo_ref, gm_ref, w1_ref, w2_ref, fg_ref = refs[:9]
    f = pl.program_id(2)

    @pl.when(f == 0)
    def _():
        x1 = x_ref[...]
        if has_proj:
            x1 = x1 + jnp.dot(o_ref[...], wout_ref[...], preferred_element_type=F32)
        ms = jnp.mean(x1 * x1, axis=-1, keepdims=True)
        h = (x1 * lax.rsqrt(ms + NORM_EPS) * gx_ref[...]).astype(BF16)
        qb = (jnp.dot(h, wq_ref[...], preferred_element_type=F32)
              * (X_HEAD_DIM ** -0.5 * LOG2E)).astype(BF16)
        hd = X_HEADS * X_HEAD_DIM
        outs = []
        for hh in range(X_HEADS):
            qh = qb[:, hh * X_HEAD_DIM:(hh + 1) * X_HEAD_DIM]
            kh = kv_ref[:, hh * X_HEAD_DIM:(hh + 1) * X_HEAD_DIM]
            vh = kv_ref[:, hd + hh * X_HEAD_DIM:hd + (hh + 1) * X_HEAD_DIM]
            s = lax.dot_general(qh, kh, (((1,), (1,)), ((), ())), preferred_element_type=F32)
            m = jnp.max(s, axis=-1, keepdims=True)
            p = jnp.exp2(s - m)
            l = jnp.sum(p, axis=-1, keepdims=True)
            oh = jnp.dot(p.astype(BF16), vh, preferred_element_type=F32) * (1.0 / l)
            outs.append(oh.astype(BF16))
        x2 = x1 + jnp.dot(jnp.concatenate(outs, axis=1), wo_ref[...], preferred_element_type=F32)
        x2_scr[...] = x2
        ms2 = jnp.mean(x2 * x2, axis=-1, keepdims=True)
        h_scr[...] = (x2 * lax.rsqrt(ms2 + NORM_EPS) * gm_ref[...]).astype(BF16)
        acc_scr[...] = jnp.zeros(acc_scr.shape, F32)

    a = jnp.maximum(jnp.dot(h_scr[...], w1_ref[...], preferred_element_type=F32), 0.0)
    acc_scr[...] += jnp.dot((a * a).astype(BF16), w2_ref[...], preferred_element_type=F32)

    @pl.when(f == pl.num_programs(2) - 1)
    def _():
        y = x2_scr[...] + acc_scr[...]
        ms = jnp.mean(y * y, axis=-1, keepdims=True)
        yn = y * lax.rsqrt(ms + NORM_EPS) * fg_ref[...]
        if final:
            out_ref[...] = yn
        else:
            out_ref[...] = y
            hn_ref[...] = yn.astype(hn_ref.dtype)


def _post_mixer(x3, o3, w_out, gx, wq, kv3, wo, gm, w1, w2, next_gain, final, tm, tf):
    b, s, _ = x3.shape
    hd = X_HEADS * X_HEAD_DIM
    has_proj = o3 is not None
    const = lambda bi, i, f: (0, 0)
    row = lambda bi, i, f: (bi, i, 0)
    in_specs = [pl.BlockSpec((None, tm, D_MODEL), row)]
    args = [x3]
    if has_proj:
        kd = o3.shape[-1]
        in_specs += [pl.BlockSpec((None, tm, kd), row), pl.BlockSpec((kd, D_MODEL), const)]
        args += [o3, w_out]
    in_specs += [pl.BlockSpec((1, D_MODEL), const),
                 pl.BlockSpec((D_MODEL, hd), const),
                 pl.BlockSpec((None, N_MEM, 2 * hd), lambda bi, i, f: (bi, 0, 0)),
                 pl.BlockSpec((hd, D_MODEL), const),
                 pl.BlockSpec((1, D_MODEL), const),
                 pl.BlockSpec((D_MODEL, tf), lambda bi, i, f: (0, f)),
                 pl.BlockSpec((tf, D_MODEL), lambda bi, i, f: (f, 0)),
                 pl.BlockSpec((1, D_MODEL), const)]
    args += [gx.reshape(1, D_MODEL), wq, kv3, wo, gm.reshape(1, D_MODEL), w1, w2,
             next_gain.reshape(1, D_MODEL)]
    out_shape = jax.ShapeDtypeStruct((b, s, D_MODEL), F32)
    out_specs = pl.BlockSpec((None, tm, D_MODEL), row)
    if not final:
        out_shape = (out_shape, jax.ShapeDtypeStruct((b, s, D_MODEL), BF16))
        out_specs = (out_specs, pl.BlockSpec((None, tm, D_MODEL), row))
    return pl.pallas_call(
        functools.partial(_post_kernel, has_proj=has_proj, final=final),
        out_shape=out_shape,
        grid=(b, s // tm, D_FF // tf),
        in_specs=in_specs,
        out_specs=out_specs,
        scratch_shapes=[pltpu.VMEM((tm, D_MODEL), F32), pltpu.VMEM((tm, D_MODEL), BF16),
                        pltpu.VMEM((tm, D_MODEL), F32)],
        compiler_params=_cparams(("parallel", "parallel", "arbitrary")),
    )(*args)


_F_ROT16 = np.array(list(range(0, 8)) + list(range(16, 40)))
_P_ROT16 = np.array(list(range(8, 16)) + list(range(40, 64)))
_F_AXIAL = np.array(list(range(0, 16)) + list(range(32, 48)))
_P_AXIAL = np.array(list(range(16, 32)) + list(range(48, 64)))


def _block_dims(first, partner):
    lane = np.arange(BLK)
    half, slot, u = lane // LANES, (lane % LANES) // 32, lane % 32
    return np.where(half == 0, first[u], partner[u]), slot


def _slot_masks(nslot, slot_of_lane):
    return np.stack([(slot_of_lane == j) for j in range(nslot)]).astype(np.float32)


def _rope_tables(pos_list, theta, rot, scale_list, npad):
    half = rot // 2
    inv_freq = jnp.exp(jnp.arange(half, dtype=F32) * (-2.0 * math.log(theta) / rot))
    cs, ss = [], []
    for pos in pos_list:
        ang = pos.astype(F32)[:, None] * inv_freq[None, :]
        cs.append(jnp.cos(ang))
        ss.append(jnp.sin(ang))
    c = jnp.concatenate(cs, axis=1)
    s = jnp.concatenate(ss, axis=1)
    n = c.shape[0]
    if npad:
        c = jnp.concatenate([c, jnp.ones((n, npad), F32)], axis=1)
        s = jnp.concatenate([s, jnp.zeros((n, npad), F32)], axis=1)
    reps = LANES // c.shape[1]
    c = jnp.tile(c, (1, reps))
    s = jnp.tile(s, (1, reps))
    ctab = [c * sc for sc in scale_list] + [jnp.ones_like(c)]
    stab = [s * sc for sc in scale_list] + [jnp.zeros_like(s)]
    return jnp.stack(ctab), jnp.stack(stab)


def _mixer_a(x, b, s, gain, w_in, w_out, pos, tm):
    dims, slot = _block_dims(_F_ROT16, _P_ROT16)
    ctab, stab = _rope_tables([pos], ROPE_THETA, 16, [0.125 * LOG2E, 1.0], 24)
    qmask = jnp.asarray(_slot_masks(4, slot), BF16)
    vmask = jnp.asarray(_slot_masks(4, np.arange(BLK) // 64), F32)
    cols = []
    for wg, (window, dil) in enumerate(A_PATTERNS):
        assert window // (2 * dil) == A_HALF_WINDOW
        for which in range(3):
            base = (wg * 3 + which) * A_HEADS * 64
            for hg in range(2):
                cols.append(base + hg * BLK + (slot * 64 + dims if which < 2 else np.arange(BLK)))
    w = w_in.astype(BF16)[:, np.concatenate(cols)]
    dils = tuple(dil for _, dil in A_PATTERNS)
    qkvs = _proj(x, 0, D_MODEL, D_MODEL, gain, w, tn=2 * BLK, out_dtype=BF16, seq=s, tm=tm,
                 tables=(ctab, stab), tt=np.array([0, 1, 2] * A_GROUPS, np.int32), dils=dils)
    outs, lses = [], []
    for qkv in qkvs:
        o, lse = _band_attention(qkv, qmask, vmask)
        outs.append(o)
        lses.append(lse)
    x3 = _a_out(outs, lses, w_out.astype(BF16), x.reshape(b, s, D_MODEL), min(tm, 512))
    return x3.reshape(b * s, D_MODEL), None, None


def _mixer_b(x, h, b, s, w_in, q_gain, k_gain, w_out, rows, cols_pos, tm, tq, tk):
    dims, slot = _block_dims(_F_AXIAL, _P_AXIAL)
    qcols = np.concatenate([g * BLK + slot * 64 + dims for g in range(B_KV_HEADS)])
    kcols = np.concatenate([B_HEADS * 64 + g * 64 + dims for g in range(B_KV_HEADS)])
    w_bf = w_in.astype(BF16)
    wqk = w_bf[:, np.concatenate([qcols, kcols])]
    wv = w_bf[:, (B_HEADS + B_KV_HEADS) * 64:]
    hgain = jnp.concatenate([jnp.tile(q_gain[dims] * (0.125 * LOG2E), B_KV_HEADS),
                             jnp.tile(k_gain[dims], B_KV_HEADS)]).reshape(1, -1).astype(F32)
    bd = jnp.asarray(slot[:, None] == slot[None, :], BF16)
    ctab, stab = _rope_tables([rows, cols_pos], AXIAL_THETA, 32, [1.0], 0)
    n = B_KV_HEADS * BLK
    qk = _proj(h, 0, D_MODEL, None, None, wqk, tn=2 * BLK, out_dtype=BF16, seq=s, tm=tm,
               tables=(ctab, stab), tt=np.zeros((n // BLK,), np.int32), headnorm=(hgain, bd))
    vt = _proj(h, 0, D_MODEL, None, None, wv, tn=B_KV_HEADS * 64, out_dtype=BF16, seq=s, tm=tm,
               vt=(B_KV_HEADS, 1, 64, tk))
    qk = qk.reshape(b, s, 2 * n)
    qmask = jnp.asarray(_slot_masks(4, slot), BF16)
    o = _flash(qk, qk, vt, qcol0=0, kcol0=B_KV_HEADS, ngroups=B_KV_HEADS, wq=BLK,
               wv=64 + _ones_rows(64), wo=BLK, nh=4, vgroups=((0, 64, 0, 4),), qmask=qmask,
               tq=tq, tk=tk)
    return x, o, w_out.astype(BF16)


def _mixer_c(x, h, b, s, w_in, lq1, lk1, lq2, lk2, sub_gain, w_out, pos, lambda_init, tm, tq, tk):
    dims, slot = _block_dims(_F_ROT16, _P_ROT16)
    ngr = C_HEADS // 2
    cols = []
    for which in range(2):
        for g in range(ngr):
            cols.append(which * C_HEADS * 128 + g * BLK + slot * 64 + dims)
    w_bf = w_in.astype(BF16)
    ctab, stab = _rope_tables([pos], ROPE_THETA, 16, [0.125 * LOG2E, 1.0], 24)
    n = ngr * BLK
    qk = _proj(h, 0, D_MODEL, None, None, w_bf[:, np.concatenate(cols)], tn=2 * BLK, out_dtype=BF16,
               seq=s, tm=tm, tables=(ctab, stab), tt=np.array([0, 0, 1, 1], np.int32))
    qkv = qk.reshape(b, s, 2 * n)
    vt = _proj(h, 0, D_MODEL, None, None, w_bf[:, 2 * C_HEADS * 128:], tn=n, out_dtype=BF16,
               seq=s, tm=tm, vt=(ngr, 2, LANES, tk))
    qmask = jnp.asarray(_slot_masks(4, slot), BF16)
    hrows = LANES + _ones_rows(LANES)
    o = _flash(qkv, qkv, vt, qcol0=0, kcol0=ngr, ngroups=ngr, wq=BLK, wv=2 * hrows,
               wo=BLK, nh=4, vgroups=((0, LANES, 0, 2), (hrows, LANES, 2, 2)),
               qmask=qmask, tq=tq, tk=tk,
               diff=lambda_init, diff_params=(lq1, lk1, lq2, lk2, sub_gain))
    return x, o, w_out.astype(BF16)


def _mixer_d(x, h, b, s, w_in, q_gain, kv_gain, w_uq, w_ukv, w_out, pos, tm, tq, tk):
    lane = np.arange(LANES)
    slot_h = np.where(lane < 32, 0, np.where(lane < 64, 1, np.where(lane < 80, 0, np.where(lane < 96, 1, -1))))
    slot = np.concatenate([slot_h, slot_h])
    nope_lane = lane < 64
    rope_lane = (lane >= 64) & (lane < 96)
    ngr = D_HEADS // 2

    w1 = jnp.zeros((D_MODEL, 4 * BLK), F32)
    w1 = w1.at[:, :D_Q_RANK].set(w_in[:, :D_Q_RANK])
    w1 = w1.at[:, 2 * BLK:3 * BLK].set(w_in[:, D_Q_RANK:D_Q_RANK + D_KV_RANK])
    kr_src = np.zeros((BLK,), np.int64)
    kr_on = np.zeros((BLK,), bool)
    for hf in range(2):
        for l in range(LANES):
            if rope_lane[l]:
                kr_src[hf * LANES + l] = D_Q_RANK + D_KV_RANK + hf * 16 + (l - 64) % 16
                kr_on[hf * LANES + l] = True
    w1 = w1.at[:, 3 * BLK:].set(jnp.where(jnp.asarray(kr_on)[None, :], w_in[:, kr_src], 0.0))
    cmb = _proj(h, 0, D_MODEL, None, None, w1.astype(BF16), tn=4 * BLK, out_dtype=F32, seq=s, tm=tm)

    qsrc = np.zeros((ngr * BLK,), np.int64)
    qon = np.zeros((ngr * BLK,), bool)
    ksrc = np.zeros((ngr * BLK,), np.int64)
    kon = np.zeros((ngr * BLK,), bool)
    for g in range(ngr):
        for hf in range(2):
            for l in range(LANES):
                idx = g * BLK + hf * LANES + l
                if slot_h[l] < 0:
                    continue
                head = 2 * g + slot_h[l]
                if nope_lane[l]:
                    d = hf * 32 + l % 32
                    qsrc[idx], qon[idx] = head * 96 + d, True
                    ksrc[idx], kon[idx] = head * 128 + d, True
                else:
                    d = hf * 16 + (l - 64) % 16
                    qsrc[idx], qon[idx] = head * 96 + D_NOPE + d, True
    wq2 = jnp.where(jnp.asarray(qon)[None, :], w_uq[:, qsrc], 0.0)
    wq2 = jnp.concatenate([wq2, jnp.zeros((2 * BLK - D_Q_RANK, ngr * BLK), F32)], axis=0).astype(BF16)
    wk2 = jnp.where(jnp.asarray(kon)[None, :], w_ukv[:, ksrc], 0.0).astype(BF16)
    vsrc = np.concatenate([h * 128 + D_NOPE + np.arange(64) for h in range(D_HEADS)])
    wv2 = w_ukv[:, vsrc].astype(BF16)
    qg = jnp.concatenate([q_gain, jnp.zeros((2 * BLK - D_Q_RANK,), F32)])

    half = D_ROPE // 2
    inv_freq = jnp.exp(jnp.arange(half, dtype=F32) * (-2.0 * math.log(ROPE_THETA) / D_ROPE))
    ang = pos.astype(F32)[:, None] * inv_freq[None, :]
    ones64 = jnp.ones((s, 64), F32)
    pad32 = jnp.ones((s, 32), F32)
    c = jnp.concatenate([ones64, jnp.cos(ang), jnp.cos(ang), pad32], axis=1)
    sn = jnp.concatenate([0.0 * ones64, jnp.sin(ang), jnp.sin(ang), 0.0 * pad32], axis=1)
    qs = (D_NOPE + D_ROPE) ** -0.5 * LOG2E
    ctab = jnp.stack([c * qs, c])
    stab = jnp.stack([sn * qs, sn])

    q = _proj(cmb, 0, 2 * BLK, D_Q_RANK, qg, wq2, tn=2 * BLK, out_dtype=BF16, seq=s, tm=tm,
              tables=(ctab, stab), tt=np.zeros((ngr // 2,), np.int32))
    k = _proj(cmb, 2, BLK, D_KV_RANK, kv_gain, wk2, tn=4 * BLK, out_dtype=BF16, seq=s, tm=tm,
              tables=(ctab, stab), tt=np.ones((ngr // 4,), np.int32), add=(cmb, 3))
    vt = _proj(cmb, 2, BLK, D_KV_RANK, kv_gain, wv2, tn=4 * BLK, out_dtype=BF16, seq=s, tm=tm,
               vt=(ngr, 2, 64, tk))
    q = q.reshape(b, s, ngr * BLK)
    k = k.reshape(b, s, ngr * BLK)
    qmask = jnp.asarray(_slot_masks(2, slot), BF16)
    hrows = 64 + _ones_rows(64)
    o = _flash(q, k, vt, qcol0=0, kcol0=0, ngroups=ngr, wq=BLK, wv=2 * hrows, wo=LANES,
               nh=2, vgroups=((0, 64, 0, 1), (hrows, 64, 1, 1)), qmask=qmask, tq=tq, tk=tk)
    return x, o, w_out.astype(BF16)


def _run_trunk(x3, mem3, p):
    b, s, _ = x3.shape
    t = b * s
    tm = 1024
    tm_h = 2048
    tm_mlp = 512
    tk = 512
    tq = 1024
    tq_mla = 2 * tq
    x = x3.reshape(t, D_MODEL)
    pos = jnp.arange(s, dtype=F32)
    rows = jnp.repeat(jnp.arange(s // GRID_W, dtype=F32), GRID_W)
    cols_pos = jnp.tile(jnp.arange(GRID_W, dtype=F32), s // GRID_W)
    memf = mem3.reshape(b * N_MEM, D_MODEL)
    h = None
    for i in range(DEPTH):
        m, j = i % 4, i // 4
        if m == 0:
            x, o, w_out = _mixer_a(x, b, s, p['norm_mix'][i], p['a_w_in'][j], p['a_w_out'][j], pos, tm)
        elif m == 1:
            x, o, w_out = _mixer_b(x, h, b, s, p['b_w_in'][j], p['b_q_norm'][j], p['b_k_norm'][j],
                                   p['b_w_out'][j], rows, cols_pos, tm_h, tq, tk)
        elif m == 2:
            x, o, w_out = _mixer_c(x, h, b, s, p['c_w_in'][j], p['c_lambda_q1'][j],
                                   p['c_lambda_k1'][j], p['c_lambda_q2'][j], p['c_lambda_k2'][j],
                                   p['c_sub_norm'][j], p['c_w_out'][j], pos,
                                   0.8 - 0.6 * math.exp(-0.3 * i), tm_h, tq, tk)
        else:
            x, o, w_out = _mixer_d(x, h, b, s, p['d_w_in'][j], p['d_q_norm'][j], p['d_kv_norm'][j],
                                   p['d_w_uq'][j], p['d_w_ukv'][j], p['d_w_out'][j], pos, tm_h,
                                   tq_mla, tk)
        kv = _proj(memf, 0, D_MODEL, D_MODEL, p['norm_mem'][i], p['w_xkv'][i].astype(BF16),
                   tn=2 * X_HEADS * X_HEAD_DIM, out_dtype=BF16, seq=N_MEM, tm=N_MEM)
        final = i == DEPTH - 1
        res = _post_mixer(x.reshape(b, s, D_MODEL), o, w_out, p['norm_x'][i],
                          p['w_xq'][i].astype(BF16), kv.reshape(b, N_MEM, 2 * X_HEADS * X_HEAD_DIM),
                          p['w_xo'][i].astype(BF16), p['norm_mlp'][i], p['w_mlp_in'][i].astype(BF16),
                          p['w_mlp_out'][i].astype(BF16),
                          p['final_norm'] if final else p['norm_mix'][i + 1], final, tm_mlp, 2048)
        if final:
            x = res.reshape(t, D_MODEL)
        else:
            x, h = res[0].reshape(t, D_MODEL), res[1].reshape(t, D_MODEL)
    return x.reshape(b, s, D_MODEL)


def kernel(x_prompt, x_sample, mem_prompt, mem_sample, norm_mix, norm_x, norm_mem, w_xq, w_xkv, w_xo, norm_mlp, w_mlp_in, w_mlp_out, a_w_in, a_w_out, b_w_in, b_q_norm, b_k_norm, b_w_out, c_w_in, c_lambda_q1, c_lambda_k1, c_lambda_q2, c_lambda_k2, c_sub_norm, c_w_out, d_w_in, d_q_norm, d_kv_norm, d_w_uq, d_w_ukv, d_w_out, final_norm):
    p = dict(norm_mix=norm_mix, norm_x=norm_x, norm_mem=norm_mem, w_xq=w_xq, w_xkv=w_xkv,
             w_xo=w_xo, norm_mlp=norm_mlp, w_mlp_in=w_mlp_in, w_mlp_out=w_mlp_out,
             a_w_in=a_w_in, a_w_out=a_w_out, b_w_in=b_w_in, b_q_norm=b_q_norm,
             b_k_norm=b_k_norm, b_w_out=b_w_out, c_w_in=c_w_in, c_lambda_q1=c_lambda_q1,
             c_lambda_k1=c_lambda_k1, c_lambda_q2=c_lambda_q2, c_lambda_k2=c_lambda_k2,
             c_sub_norm=c_sub_norm, c_w_out=c_w_out, d_w_in=d_w_in, d_q_norm=d_q_norm,
             d_kv_norm=d_kv_norm, d_w_uq=d_w_uq, d_w_ukv=d_w_ukv, d_w_out=d_w_out,
             final_norm=final_norm)
    return (_run_trunk(x_prompt, mem_prompt, p), _run_trunk(x_sample, mem_sample, p))
```
